```python
import math
import jax, jax.numpy as jnp
from jax import lax
import numpy as np


D_MODEL = 1024
BATCH = 8
SEQ = 8192
DEPTH = 4

HEAD_DIM = 64
HEADS_PER_GROUP = D_MODEL // 2 // HEAD_DIM
ATTN_PATTERN = ((128, 1), (512, 4), (2048, 16))
N_ATTN_GROUPS = len(ATTN_PATTERN)
ATTN_WIDTH = HEADS_PER_GROUP * HEAD_DIM
QKV_GROUP_WIDTH = N_ATTN_GROUPS * ATTN_WIDTH
BLK = 128
SSM_WIDTH = D_MODEL // 2
SSM_GROUP = 16
SSM_GROUPS = SSM_WIDTH // SSM_GROUP
SSM_STATE = 64
DT_MIN = 1e-3
DT_MAX = 1e-1
D_FF = ((8 * D_MODEL + 3 * 256 - 1) // (3 * 256)) * 256
EPS = 1e-6
IN_COLS = 3 * QKV_GROUP_WIDTH + SSM_WIDTH + 2 * D_MODEL
SPLIT_POINTS = (QKV_GROUP_WIDTH, 2 * QKV_GROUP_WIDTH, 3 * QKV_GROUP_WIDTH,
                3 * QKV_GROUP_WIDTH + SSM_WIDTH, 3 * QKV_GROUP_WIDTH + SSM_WIDTH + D_MODEL)

kernel_name = "hybrid_gated_dilated_attn_s5_swiglu"


def rms_norm(t, gain):
    t32 = t.astype(jnp.float32)
    y = t32 * lax.rsqrt(jnp.mean(t32 * t32, axis=-1, keepdims=True) + EPS) * gain.astype(jnp.float32)
    return y.astype(t.dtype)


def head_rms_norm(t, gain):
    t32 = t.astype(jnp.float32)
    return t32 * lax.rsqrt(jnp.mean(t32 * t32, axis=-1, keepdims=True) + EPS) * gain.astype(jnp.float32)


def dilated_window_attention(q, k, v, window, dilation):
    b_, L, H, E = q.shape
    span = window // dilation
    unit = dilation * BLK
    Lp = -(-L // unit) * unit
    M = Lp // dilation
    nb = M // BLK

    def to_blocks(t):
        t = jnp.pad(t, ((0, 0), (0, Lp - L), (0, 0), (0, 0)))
        t = t.reshape(b_, M, dilation, H, E).transpose(0, 2, 3, 1, 4)
        return t.reshape(b_, dilation, H, nb, BLK, E)

    def with_prev(t):
        prev = jnp.pad(t[:, :, :, :-1], ((0, 0), (0, 0), (0, 0), (1, 0), (0, 0), (0, 0)))
        return jnp.concatenate([prev, t], axis=4)

    qb = to_blocks(q)
    kw = with_prev(to_blocks(k))
    vw = with_prev(to_blocks(v))
    s = jnp.einsum('brhnqe,brhnke->brhnqk', qb, kw) * (HEAD_DIM ** -0.5)
    qi = jnp.arange(BLK)[:, None]
    ki = jnp.arange(2 * BLK)[None, :]
    dist = BLK + qi - ki
    blk = jnp.arange(nb)[:, None, None]
    mask = (dist >= 0) & (dist <= span) & (blk * BLK + ki - BLK >= 0)
    s = jnp.where(mask, s, -jnp.inf)
    m = jnp.max(s, axis=-1, keepdims=True)
    p = jnp.exp(s - m)
    denom = jnp.sum(p, axis=-1, keepdims=True)
    o = jnp.einsum('brhnqk,brhnke->brhnqe', p, vw) / denom
    lse = (m + jnp.log(denom))[..., 0]
    o = o.reshape(b_, dilation, H, M, E).transpose(0, 3, 1, 2, 4).reshape(b_, Lp, H, E)[:, :L]
    lse = lse.reshape(b_, dilation, H, M).transpose(0, 3, 1, 2).reshape(b_, Lp, H)[:, :L]
    return o, lse


def dilated_attention(q, k, v, g_q, g_k):
    b_, L, _ = q.shape
    shape = (b_, L, N_ATTN_GROUPS, HEADS_PER_GROUP, HEAD_DIM)
    q = head_rms_norm(q.reshape(shape), g_q)
    k = head_rms_norm(k.reshape(shape), g_k)
    v = v.reshape(shape).astype(jnp.float32)
    outs, lses = [], []
    for gi, (window, dilation) in enumerate(ATTN_PATTERN):
        o, lse = dilated_window_attention(q[:, :, gi], k[:, :, gi], v[:, :, gi], window, dilation)
        outs.append(o)
        lses.append(lse)
    w = jax.nn.softmax(jnp.stack(lses, axis=0), axis=0)
    out = jnp.sum(w[..., None] * jnp.stack(outs, axis=0), axis=0)
    return out.reshape(b_, L, ATTN_WIDTH)


def _complex_linear_combine(e1, e2):
    a1r, a1i, b1r, b1i = e1
    a2r, a2i, b2r, b2i = e2
    ar = a2r * a1r - a2i * a1i
    ai = a2r * a1i + a2i * a1r
    br = a2r * b1r - a2i * b1i + b2r
    bi = a2r * b1i + a2i * b1r + b2i
    return (ar, ai, br, bi)


def s5_ssm(u, lam_re, lam_im, log_dt, b_re, b_im, c_re, c_im, d_skip):
    b_, L, _ = u.shape
    u = u.astype(jnp.float32)
    lr = lam_re.astype(jnp.float32)
    li = lam_im.astype(jnp.float32)
    dt = jnp.exp(log_dt.astype(jnp.float32))[:, None]
    mag = jnp.exp(lr * dt)
    ang = li * dt
    abar_re = mag * jnp.cos(ang)
    abar_im = mag * jnp.sin(ang)
    nr = abar_re - 1.0
    ni = abar_im
    den = lr * lr + li * li
    cr = ((nr * lr + ni * li) / den)[..., None]
    ci = ((ni * lr - nr * li) / den)[..., None]
    br = b_re.astype(jnp.float32)
    bi = b_im.astype(jnp.float32)
    bbar_re = cr * br - ci * bi
    bbar_im = cr * bi + ci * br
    ug = u.reshape(b_, L, SSM_GROUPS, SSM_GROUP)
    bu_re = jnp.einsum('blgc,gpc->lbgp', ug, bbar_re)
    bu_im = jnp.einsum('blgc,gpc->lbgp', ug, bbar_im)
    a_re = jnp.broadcast_to(abar_re[None, None], (L, 1, SSM_GROUPS, SSM_STATE))
    a_im = jnp.broadcast_to(abar_im[None, None], (L, 1, SSM_GROUPS, SSM_STATE))
    _, _, xr, xi = lax.associative_scan(_complex_linear_combine, (a_re, a_im, bu_re, bu_im), axis=0)
    y = (jnp.einsum('lbgp,gcp->blgc', xr, c_re.astype(jnp.float32))
         - jnp.einsum('lbgp,gcp->blgc', xi, c_im.astype(jnp.float32)))
    return y.reshape(b_, L, SSM_WIDTH) + d_skip.astype(jnp.float32) * u


def _fwd_setup_inputs(seed: int = 0) -> dict:
    key = jax.random.key(seed)
    ks = jax.random.split(key, 24)
    f32 = jnp.float32

    def normal(k, shape, scale):
        return jax.random.normal(k, shape, f32) * scale

    n_idx = jnp.arange(SSM_STATE, dtype=f32)
    return {
        "x": normal(ks[0], (BATCH, SEQ, D_MODEL), 1.0),
        "g_mix": 1.0 + normal(ks[1], (DEPTH, D_MODEL), 0.02),
        "w_in": normal(ks[2], (DEPTH, D_MODEL, IN_COLS), D_MODEL ** -0.5),
        "g_q": 1.0 + normal(ks[3], (DEPTH, HEAD_DIM), 0.02),
        "g_k": 1.0 + normal(ks[4], (DEPTH, HEAD_DIM), 0.02),
        "w_attn_proj": normal(ks[5], (DEPTH, ATTN_WIDTH, D_MODEL), ATTN_WIDTH ** -0.5),
        "lambda_re": -0.5 + normal(ks[6], (DEPTH, SSM_GROUPS, SSM_STATE), 0.01),
        "lambda_im": math.pi * n_idx + normal(ks[7], (DEPTH, SSM_GROUPS, SSM_STATE), 0.01),
        "log_dt": jax.random.uniform(ks[8], (DEPTH, SSM_GROUPS), f32, math.log(DT_MIN), math.log(DT_MAX)),
        "b_re": normal(ks[9], (DEPTH, SSM_GROUPS, SSM_STATE, SSM_GROUP), (2 * SSM_GROUP) ** -0.5),
        "b_im": normal(ks[10], (DEPTH, SSM_GROUPS, SSM_STATE, SSM_GROUP), (2 * SSM_GROUP) ** -0.5),
        "c_re": normal(ks[11], (DEPTH, SSM_GROUPS, SSM_GROUP, SSM_STATE), (2 * SSM_STATE) ** -0.5),
        "c_im": normal(ks[12], (DEPTH, SSM_GROUPS, SSM_GROUP, SSM_STATE), (2 * SSM_STATE) ** -0.5),
        "d_skip": normal(ks[13], (DEPTH, SSM_WIDTH), 1.0),
        "w_glu_a": normal(ks[14], (DEPTH, SSM_WIDTH, D_MODEL), SSM_WIDTH ** -0.5),
        "w_glu_b": normal(ks[15], (DEPTH, SSM_WIDTH, D_MODEL), SSM_WIDTH ** -0.5),
        "w_out": normal(ks[16], (DEPTH, D_MODEL, D_MODEL), D_MODEL ** -0.5),
        "g_ffn": 1.0 + normal(ks[17], (DEPTH, D_MODEL), 0.02),
        "w_ffn_gate": normal(ks[18], (DEPTH, D_MODEL, D_FF), D_MODEL ** -0.5),
        "w_ffn_up": normal(ks[19], (DEPTH, D_MODEL, D_FF), D_MODEL ** -0.5),
        "w_ffn_down": normal(ks[20], (DEPTH, D_FF, D_MODEL), D_FF ** -0.5),
    }


def _fwd_reference(x, g_mix, w_in, g_q, g_k, w_attn_proj, lambda_re, lambda_im, log_dt,
              b_re, b_im, c_re, c_im, d_skip, w_glu_a, w_glu_b, w_out,
              g_ffn, w_ffn_gate, w_ffn_up, w_ffn_down):
    for l in range(DEPTH):
        h = rms_norm(x, g_mix[l])
        z = h @ w_in[l]
        q, k, v, u, gate_a, gate_s = jnp.split(z, SPLIT_POINTS, axis=-1)
        a = dilated_attention(q, k, v, g_q[l], g_k[l]).astype(x.dtype)
        a_out = a @ w_attn_proj[l]
        y = s5_ssm(u, lambda_re[l], lambda_im[l], log_dt[l], b_re[l], b_im[l],
                   c_re[l], c_im[l], d_skip[l])
        y = jax.nn.gelu(y).astype(x.dtype)
        s_out = (y @ w_glu_a[l]) * jax.nn.sigmoid(y @ w_glu_b[l])
        mix = jax.nn.sigmoid(gate_a) * a_out + jax.nn.sigmoid(gate_s) * s_out
        x = x + mix @ w_out[l]
        h2 = rms_norm(x, g_ffn[l])
        x = x + (jax.nn.silu(h2 @ w_ffn_gate[l]) * (h2 @ w_ffn_up[l])) @ w_ffn_down[l]
    return x


import jax as _jax
import jax.numpy as _jnp

TWIN_FORMAT = 'train_step'
FWD_PARAMS = ['x', 'g_mix', 'w_in', 'g_q', 'g_k', 'w_attn_proj', 'lambda_re', 'lambda_im', 'log_dt', 'b_re', 'b_im', 'c_re', 'c_im', 'd_skip', 'w_glu_a', 'w_glu_b', 'w_out', 'g_ffn', 'w_ffn_gate', 'w_ffn_up', 'w_ffn_down']
TWIN_WEIGHTS = ['g_mix', 'w_in', 'g_q', 'g_k', 'w_attn_proj', 'lambda_re', 'lambda_im', 'log_dt', 'b_re', 'b_im', 'c_re', 'c_im', 'd_skip', 'w_glu_a', 'w_glu_b', 'w_out', 'g_ffn', 'w_ffn_gate', 'w_ffn_up', 'w_ffn_down']
TWIN_DIFF_INPUT = 'x'
TWIN_INPUTS = ['x', 'g_mix', 'w_in', 'g_q', 'g_k', 'w_attn_proj', 'lambda_re', 'lambda_im', 'log_dt', 'b_re', 'b_im', 'c_re', 'c_im', 'd_skip', 'w_glu_a', 'w_glu_b', 'w_out', 'g_ffn', 'w_ffn_gate', 'w_ffn_up', 'w_ffn_down', 'loss_target', 'm_g_mix', 'm_w_in', 'm_g_q', 'm_g_k', 'm_w_attn_proj', 'm_lambda_re', 'm_lambda_im', 'm_log_dt', 'm_b_re', 'm_b_im', 'm_c_re', 'm_c_im', 'm_d_skip', 'm_w_glu_a', 'm_w_glu_b', 'm_w_out', 'm_g_ffn', 'm_w_ffn_gate', 'm_w_ffn_up', 'm_w_ffn_down', 'v_g_mix', 'v_w_in', 'v_g_q', 'v_g_k', 'v_w_attn_proj', 'v_lambda_re', 'v_lambda_im', 'v_log_dt', 'v_b_re', 'v_b_im', 'v_c_re', 'v_c_im', 'v_d_skip', 'v_w_glu_a', 'v_w_glu_b', 'v_w_out', 'v_g_ffn', 'v_w_ffn_gate', 'v_w_ffn_up', 'v_w_ffn_down']
TWIN_OUTPUTS = ['loss', 'grad_x', 'grad_g_mix', 'grad_w_in', 'grad_g_q', 'grad_g_k', 'grad_w_attn_proj', 'grad_lambda_re', 'grad_lambda_im', 'grad_log_dt', 'grad_b_re', 'grad_b_im', 'grad_c_re', 'grad_c_im', 'grad_d_skip', 'grad_w_glu_a', 'grad_w_glu_b', 'grad_w_out', 'grad_g_ffn', 'grad_w_ffn_gate', 'grad_w_ffn_up', 'grad_w_ffn_down', 'delta_g_mix', 'delta_w_in', 'delta_g_q', 'delta_g_k', 'delta_w_attn_proj', 'delta_lambda_re', 'delta_lambda_im', 'delta_log_dt', 'delta_b_re', 'delta_b_im', 'delta_c_re', 'delta_c_im', 'delta_d_skip', 'delta_w_glu_a', 'delta_w_glu_b', 'delta_w_out', 'delta_g_ffn', 'delta_w_ffn_gate', 'delta_w_ffn_up', 'delta_w_ffn_down', 'new_m_g_mix', 'new_m_w_in', 'new_m_g_q', 'new_m_g_k', 'new_m_w_attn_proj', 'new_m_lambda_re', 'new_m_lambda_im', 'new_m_log_dt', 'new_m_b_re', 'new_m_b_im', 'new_m_c_re', 'new_m_c_im', 'new_m_d_skip', 'new_m_w_glu_a', 'new_m_w_glu_b', 'new_m_w_out', 'new_m_g_ffn', 'new_m_w_ffn_gate', 'new_m_w_ffn_up', 'new_m_w_ffn_down', 'new_v_g_mix', 'new_v_w_in', 'new_v_g_q', 'new_v_g_k', 'new_v_w_attn_proj', 'new_v_lambda_re', 'new_v_lambda_im', 'new_v_log_dt', 'new_v_b_re', 'new_v_b_im', 'new_v_c_re', 'new_v_c_im', 'new_v_d_skip', 'new_v_w_glu_a', 'new_v_w_glu_b', 'new_v_w_out', 'new_v_g_ffn', 'new_v_w_ffn_gate', 'new_v_w_ffn_up', 'new_v_w_ffn_down']
TWIN_LEAF_KINDS = {'loss': 'loss', 'grad_x': 'grad_x', 'grad_g_mix': 'grad_w', 'grad_w_in': 'grad_w', 'grad_g_q': 'grad_w', 'grad_g_k': 'grad_w', 'grad_w_attn_proj': 'grad_w', 'grad_lambda_re': 'grad_w', 'grad_lambda_im': 'grad_w', 'grad_log_dt': 'grad_w', 'grad_b_re': 'grad_w', 'grad_b_im': 'grad_w', 'grad_c_re': 'grad_w', 'grad_c_im': 'grad_w', 'grad_d_skip': 'grad_w', 'grad_w_glu_a': 'grad_w', 'grad_w_glu_b': 'grad_w', 'grad_w_out': 'grad_w', 'grad_g_ffn': 'grad_w', 'grad_w_ffn_gate': 'grad_w', 'grad_w_ffn_up': 'grad_w', 'grad_w_ffn_down': 'grad_w', 'delta_g_mix': 'delta_w', 'delta_w_in': 'delta_w', 'delta_g_q': 'delta_w', 'delta_g_k': 'delta_w', 'delta_w_attn_proj': 'delta_w', 'delta_lambda_re': 'delta_w', 'delta_lambda_im': 'delta_w', 'delta_log_dt': 'delta_w', 'delta_b_re': 'delta_w', 'delta_b_im': 'delta_w', 'delta_c_re': 'delta_w', 'delta_c_im': 'delta_w', 'delta_d_skip': 'delta_w', 'delta_w_glu_a': 'delta_w', 'delta_w_glu_b': 'delta_w', 'delta_w_out': 'delta_w', 'delta_g_ffn': 'delta_w', 'delta_w_ffn_gate': 'delta_w', 'delta_w_ffn_up': 'delta_w', 'delta_w_ffn_down': 'delta_w', 'new_m_g_mix': 'new_m', 'new_m_w_in': 'new_m', 'new_m_g_q': 'new_m', 'new_m_g_k': 'new_m', 'new_m_w_attn_proj': 'new_m', 'new_m_lambda_re': 'new_m', 'new_m_lambda_im': 'new_m', 'new_m_log_dt': 'new_m', 'new_m_b_re': 'new_m', 'new_m_b_im': 'new_m', 'new_m_c_re': 'new_m', 'new_m_c_im': 'new_m', 'new_m_d_skip': 'new_m', 'new_m_w_glu_a': 'new_m', 'new_m_w_glu_b': 'new_m', 'new_m_w_out': 'new_m', 'new_m_g_ffn': 'new_m', 'new_m_w_ffn_gate': 'new_m', 'new_m_w_ffn_up': 'new_m', 'new_m_w_ffn_down': 'new_m', 'new_v_g_mix': 'new_v', 'new_v_w_in': 'new_v', 'new_v_g_q': 'new_v', 'new_v_g_k': 'new_v', 'new_v_w_attn_proj': 'new_v', 'new_v_lambda_re': 'new_v', 'new_v_lambda_im': 'new_v', 'new_v_log_dt': 'new_v', 'new_v_b_re': 'new_v', 'new_v_b_im': 'new_v', 'new_v_c_re': 'new_v', 'new_v_c_im': 'new_v', 'new_v_d_skip': 'new_v', 'new_v_w_glu_a': 'new_v', 'new_v_w_glu_b': 'new_v', 'new_v_w_out': 'new_v', 'new_v_g_ffn': 'new_v', 'new_v_w_ffn_gate': 'new_v', 'new_v_w_ffn_up': 'new_v', 'new_v_w_ffn_down': 'new_v'}


def _forward(args):
    return _fwd_reference(*[args[k] for k in FWD_PARAMS])


def _output_shape():
    def fwd():
        inp = _fwd_setup_inputs(0)
        return _fwd_reference(*[inp[k] for k in FWD_PARAMS])
    out = _jax.eval_shape(fwd)
    return out.shape, out.dtype

N_MICROBATCH = 1
ADAM_LR = 0.001
ADAM_B1 = 0.9
ADAM_B2 = 0.999
ADAM_EPS = 1e-08
ADAM_WD = 0.01
ADAM_STEP = 10
PER_EXAMPLE_BATCH_AXIS = {'x': 0, 'loss_target': 0}
SHARED_INPUTS = []
_WEIGHT_DTYPES = {'g_mix': _jnp.float32, 'w_in': _jnp.float32, 'g_q': _jnp.float32, 'g_k': _jnp.float32, 'w_attn_proj': _jnp.float32, 'lambda_re': _jnp.float32, 'lambda_im': _jnp.float32, 'log_dt': _jnp.float32, 'b_re': _jnp.float32, 'b_im': _jnp.float32, 'c_re': _jnp.float32, 'c_im': _jnp.float32, 'd_skip': _jnp.float32, 'w_glu_a': _jnp.float32, 'w_glu_b': _jnp.float32, 'w_out': _jnp.float32, 'g_ffn': _jnp.float32, 'w_ffn_gate': _jnp.float32, 'w_ffn_up': _jnp.float32, 'w_ffn_down': _jnp.float32}
MOMENT_SCALE = {'g_mix': 2.917088e+00, 'w_in': 1.981399e-01, 'g_q': 1.232463e+00, 'g_k': 1.230485e+00, 'w_attn_proj': 2.502684e-01, 'lambda_re': 5.711260e-02, 'lambda_im': 5.434731e-02, 'log_dt': 7.216403e+00, 'b_re': 3.643089e-02, 'b_im': 3.127924e-02, 'c_re': 6.344197e-02, 'c_im': 7.329262e-02, 'd_skip': 6.218143e+00, 'w_glu_a': 1.658580e+00, 'w_glu_b': 4.075839e-01, 'w_out': 1.515072e+00, 'g_ffn': 4.954604e+01, 'w_ffn_gate': 5.737070e-01, 'w_ffn_up': 4.242136e-01, 'w_ffn_down': 6.866483e-01}


def _to_microbatches(a, axis):
    t = _jnp.moveaxis(a, axis, 0)
    t = t.reshape((N_MICROBATCH, t.shape[0] // N_MICROBATCH) + t.shape[1:])
    return _jnp.moveaxis(t, 1, axis + 1)


def setup_inputs(seed: int = 0) -> dict:
    inp = _fwd_setup_inputs(seed)
    key = _jax.random.fold_in(_jax.random.key(seed), 7919)
    shape, _ = _output_shape()
    out = dict(inp)
    out["loss_target"] = _jax.random.normal(_jax.random.fold_in(key, 0), shape, _jnp.float32)
    for i, name in enumerate(TWIN_WEIGHTS):
        w = inp[name].astype(_jnp.float32)
        if MOMENT_SCALE is None:
            s = _jnp.sqrt(_jnp.mean(_jnp.square(w)) + 1e-30)
        else:
            s = MOMENT_SCALE[name]
        km, kv = _jax.random.split(_jax.random.fold_in(key, i + 1))
        out[name] = w
        out["m_" + name] = s * _jax.random.normal(km, w.shape, _jnp.float32)
        out["v_" + name] = (s * s) * _jax.random.uniform(kv, w.shape, _jnp.float32, 0.5, 1.5)
    if N_MICROBATCH > 1:
        for name, axis in PER_EXAMPLE_BATCH_AXIS.items():
            out[name] = _to_microbatches(out[name], axis)
    return {'x': out['x'], 'g_mix': out['g_mix'], 'w_in': out['w_in'], 'g_q': out['g_q'], 'g_k': out['g_k'], 'w_attn_proj': out['w_attn_proj'], 'lambda_re': out['lambda_re'], 'lambda_im': out['lambda_im'], 'log_dt': out['log_dt'], 'b_re': out['b_re'], 'b_im': out['b_im'], 'c_re': out['c_re'], 'c_im': out['c_im'], 'd_skip': out['d_skip'], 'w_glu_a': out['w_glu_a'], 'w_glu_b': out['w_glu_b'], 'w_out': out['w_out'], 'g_ffn': out['g_ffn'], 'w_ffn_gate': out['w_ffn_gate'], 'w_ffn_up': out['w_ffn_up'], 'w_ffn_down': out['w_ffn_down'], 'loss_target': out['loss_target'], 'm_g_mix': out['m_g_mix'], 'm_w_in': out['m_w_in'], 'm_g_q': out['m_g_q'], 'm_g_k': out['m_g_k'], 'm_w_attn_proj': out['m_w_attn_proj'], 'm_lambda_re': out['m_lambda_re'], 'm_lambda_im': out['m_lambda_im'], 'm_log_dt': out['m_log_dt'], 'm_b_re': out['m_b_re'], 'm_b_im': out['m_b_im'], 'm_c_re': out['m_c_re'], 'm_c_im': out['m_c_im'], 'm_d_skip': out['m_d_skip'], 'm_w_glu_a': out['m_w_glu_a'], 'm_w_glu_b': out['m_w_glu_b'], 'm_w_out': out['m_w_out'], 'm_g_ffn': out['m_g_ffn'], 'm_w_ffn_gate': out['m_w_ffn_gate'], 'm_w_ffn_up': out['m_w_ffn_up'], 'm_w_ffn_down': out['m_w_ffn_down'], 'v_g_mix': out['v_g_mix'], 'v_w_in': out['v_w_in'], 'v_g_q': out['v_g_q'], 'v_g_k': out['v_g_k'], 'v_w_attn_proj': out['v_w_attn_proj'], 'v_lambda_re': out['v_lambda_re'], 'v_lambda_im': out['v_lambda_im'], 'v_log_dt': out['v_log_dt'], 'v_b_re': out['v_b_re'], 'v_b_im': out['v_b_im'], 'v_c_re': out['v_c_re'], 'v_c_im': out['v_c_im'], 'v_d_skip': out['v_d_skip'], 'v_w_glu_a': out['v_w_glu_a'], 'v_w_glu_b': out['v_w_glu_b'], 'v_w_out': out['v_w_out'], 'v_g_ffn': out['v_g_ffn'], 'v_w_ffn_gate': out['v_w_ffn_gate'], 'v_w_ffn_up': out['v_w_ffn_up'], 'v_w_ffn_down': out['v_w_ffn_down']}


def _loss(weights, diff, rest, loss_target):
    with _jax.named_scope("forward"):
        args = {**rest, TWIN_DIFF_INPUT: diff, **{k: w.astype(_WEIGHT_DTYPES[k]) for k, w in weights.items()}}
        y = _forward(args)
    with _jax.named_scope("loss_head"):
        err = _jnp.square(y.astype(_jnp.float32) - loss_target)
        return 0.5 * _jnp.sum(_jnp.mean(err, axis=-1)) if err.ndim else 0.5 * err


def _adamw(w, g, m, v):
    m = ADAM_B1 * m + (1.0 - ADAM_B1) * g
    v = ADAM_B2 * v + (1.0 - ADAM_B2) * _jnp.square(g)
    m_hat = m / (1.0 - ADAM_B1 ** ADAM_STEP)
    v_hat = v / (1.0 - ADAM_B2 ** ADAM_STEP)
    delta = -ADAM_LR * (m_hat / (_jnp.sqrt(v_hat) + ADAM_EPS) + ADAM_WD * w)
    return delta, m, v


def reference(x, g_mix, w_in, g_q, g_k, w_attn_proj, lambda_re, lambda_im, log_dt, b_re, b_im, c_re, c_im, d_skip, w_glu_a, w_glu_b, w_out, g_ffn, w_ffn_gate, w_ffn_up, w_ffn_down, loss_target, m_g_mix, m_w_in, m_g_q, m_g_k, m_w_attn_proj, m_lambda_re, m_lambda_im, m_log_dt, m_b_re, m_b_im, m_c_re, m_c_im, m_d_skip, m_w_glu_a, m_w_glu_b, m_w_out, m_g_ffn, m_w_ffn_gate, m_w_ffn_up, m_w_ffn_down, v_g_mix, v_w_in, v_g_q, v_g_k, v_w_attn_proj, v_lambda_re, v_lambda_im, v_log_dt, v_b_re, v_b_im, v_c_re, v_c_im, v_d_skip, v_w_glu_a, v_w_glu_b, v_w_out, v_g_ffn, v_w_ffn_gate, v_w_ffn_up, v_w_ffn_down):
    given = dict(x=x, g_mix=g_mix, w_in=w_in, g_q=g_q, g_k=g_k, w_attn_proj=w_attn_proj, lambda_re=lambda_re, lambda_im=lambda_im, log_dt=log_dt, b_re=b_re, b_im=b_im, c_re=c_re, c_im=c_im, d_skip=d_skip, w_glu_a=w_glu_a, w_glu_b=w_glu_b, w_out=w_out, g_ffn=g_ffn, w_ffn_gate=w_ffn_gate, w_ffn_up=w_ffn_up, w_ffn_down=w_ffn_down, loss_target=loss_target, m_g_mix=m_g_mix, m_w_in=m_w_in, m_g_q=m_g_q, m_g_k=m_g_k, m_w_attn_proj=m_w_attn_proj, m_lambda_re=m_lambda_re, m_lambda_im=m_lambda_im, m_log_dt=m_log_dt, m_b_re=m_b_re, m_b_im=m_b_im, m_c_re=m_c_re, m_c_im=m_c_im, m_d_skip=m_d_skip, m_w_glu_a=m_w_glu_a, m_w_glu_b=m_w_glu_b, m_w_out=m_w_out, m_g_ffn=m_g_ffn, m_w_ffn_gate=m_w_ffn_gate, m_w_ffn_up=m_w_ffn_up, m_w_ffn_down=m_w_ffn_down, v_g_mix=v_g_mix, v_w_in=v_w_in, v_g_q=v_g_q, v_g_k=v_g_k, v_w_attn_proj=v_w_attn_proj, v_lambda_re=v_lambda_re, v_lambda_im=v_lambda_im, v_log_dt=v_log_dt, v_b_re=v_b_re, v_b_im=v_b_im, v_c_re=v_c_re, v_c_im=v_c_im, v_d_skip=v_d_skip, v_w_glu_a=v_w_glu_a, v_w_glu_b=v_w_glu_b, v_w_out=v_w_out, v_g_ffn=v_g_ffn, v_w_ffn_gate=v_w_ffn_gate, v_w_ffn_up=v_w_ffn_up, v_w_ffn_down=v_w_ffn_down)
    weights = {n: given[n] for n in TWIN_WEIGHTS}
    shared = {n: given[n] for n in SHARED_INPUTS}
    per_example = {n: given[n] for n in ['x']}
    grad_fn = _jax.value_and_grad(_loss, argnums=(0, 1))

    def one_microbatch(ex, loss_target):
        ex = dict(ex)
        diff = ex.pop(TWIN_DIFF_INPUT)
        return grad_fn(weights, diff, {**shared, **ex}, loss_target)

    if N_MICROBATCH == 1:
        loss, (grad_w, grad_x) = one_microbatch(per_example, given["loss_target"])
    else:
        def body(carry, xs):
            loss_sum, grad_sum = carry
            l_k, (gw_k, gx_k) = one_microbatch(xs[0], xs[1])
            with _jax.named_scope("update"):
                return (loss_sum + l_k, _jax.tree.map(_jnp.add, grad_sum, gw_k)), gx_k

        init = (_jnp.zeros((), _jnp.float32), _jax.tree.map(_jnp.zeros_like, weights))
        (loss, grad_w), grad_x = _jax.lax.scan(body, init, (per_example, given["loss_target"]))
    with _jax.named_scope("update"):
        delta_w, new_m, new_v = {}, {}, {}
        for n in TWIN_WEIGHTS:
            delta_w[n], new_m[n], new_v[n] = _adamw(weights[n], grad_w[n], given["m_" + n], given["v_" + n])
    return (loss, grad_x, *[grad_w[n] for n in TWIN_WEIGHTS], *[delta_w[n] for n in TWIN_WEIGHTS],
            *[new_m[n] for n in TWIN_WEIGHTS], *[new_v[n] for n in TWIN_WEIGHTS])
```

```python
import functools
import math

import jax
import jax.numpy as jnp
from jax import lax
from jax.experimental import pallas as pl
from jax.experimental.pallas import tpu as pltpu

F32 = jnp.float32
BF16 = jnp.bfloat16

D_MODEL = 1024
DEPTH = 4
N_DEV = 8
HEAD_DIM = 64
BLK = 128
LANES = 128
ATTN_W = 512
N_GROUPS = 3
DILATIONS = (1, 4, 16)
SSM_W = 512
SSM_STATES = 2048
IN_COLS = 7168
COL_U = 9
D_FF = 2816
FF_SHARD = D_FF // N_DEV
FF_SHARD_PAD = 384
FF_PAD = FF_SHARD_PAD * N_DEV
EPS = 1e-6
SSM_TM = 256
SMALL_TILES = 4

ADAM_LR = 0.001
ADAM_B1 = 0.9
ADAM_B2 = 0.999
ADAM_EPS = 1e-08
ADAM_WD = 0.01
ADAM_STEP = 10

MESH_AXES = ("x", "y", "c")
MIB = 1024 * 1024


def _cp(sem=None, vmem_mib=None):
    kw = {}
    if sem is not None:
        kw["dimension_semantics"] = sem
    if vmem_mib is not None:
        kw["vmem_limit_bytes"] = vmem_mib * MIB
    return pltpu.CompilerParams(**kw)


def _nt(a, b):
    return lax.dot_general(a, b, (((1,), (1,)), ((), ())), preferred_element_type=F32)


def _tn_dot(a, b):
    return lax.dot_general(a, b, (((0,), (0,)), ((), ())), preferred_element_type=F32)


def _nn(a, b):
    return jnp.dot(a, b, preferred_element_type=F32)


def _sigmoid(t):
    return jax.nn.sigmoid(t)


def _prep_weight(w, rows_to, cols_to, name):
    _, k, n = w.shape

    def body(w_ref, o_ref):
        if rows_to != k or cols_to != n:
            o_ref[...] = jnp.zeros(o_ref.shape, BF16)
        o_ref[0, :k, :n] = w_ref[0].astype(BF16)

    return pl.pallas_call(
        body, grid=(DEPTH,),
        in_specs=[pl.BlockSpec((1, k, n), lambda l: (l, 0, 0))],
        out_specs=pl.BlockSpec((1, rows_to, cols_to), lambda l: (l, 0, 0)),
        out_shape=jax.ShapeDtypeStruct((DEPTH, rows_to, cols_to), BF16),
        compiler_params=_cp(("parallel",), 40), name=name)(w)


def _my_index():
    return 4 * lax.axis_index("x") + 2 * lax.axis_index("y") + lax.axis_index("c")


def _peer(k):
    return (jnp.bitwise_xor(lax.axis_index("x"), (k >> 2) & 1),
            jnp.bitwise_xor(lax.axis_index("y"), (k >> 1) & 1),
            jnp.bitwise_xor(lax.axis_index("c"), k & 1))


def _slab(ref, idx, width, axis):
    start = pl.multiple_of(idx * width, width)
    sl = [slice(None)] * len(ref.shape)
    sl[axis] = pl.ds(start, width)
    return ref.at[tuple(sl)]


def _all_gather(shards, axes):
    nt = len(shards)

    def body(*refs):
        ins, outs = refs[:nt], refs[nt:2 * nt]
        ssem, rsem, lsem = refs[2 * nt:]
        me = _my_index()
        copies = []
        for t in range(nt):
            width = shards[t].shape[axes[t]]
            mine = _slab(outs[t], me, width, axes[t])
            loc = pltpu.make_async_copy(ins[t], mine, lsem.at[t])
            loc.start()
            copies.append(loc)
            for k in range(1, N_DEV):
                cp = pltpu.make_async_remote_copy(
                    src_ref=ins[t], dst_ref=mine, send_sem=ssem.at[t, k - 1], recv_sem=rsem.at[t, k - 1],
                    device_id=_peer(k), device_id_type=pl.DeviceIdType.MESH)
                cp.start()
                copies.append(cp)
        for cp in copies:
            cp.wait()

    out_shape = []
    for t in range(nt):
        s = list(shards[t].shape)
        s[axes[t]] *= N_DEV
        out_shape.append(jax.ShapeDtypeStruct(tuple(s), shards[t].dtype))
    return pl.pallas_call(
        body,
        in_specs=[pl.BlockSpec(memory_space=pltpu.HBM)] * nt,
        out_specs=[pl.BlockSpec(memory_space=pltpu.HBM)] * nt,
        out_shape=out_shape,
        scratch_shapes=[pltpu.SemaphoreType.DMA((nt, N_DEV - 1)), pltpu.SemaphoreType.DMA((nt, N_DEV - 1)),
                        pltpu.SemaphoreType.DMA((nt,))],
        name="all_gather_weights")(*shards)


def _reduce_scatter(grads, axes, small):
    nt = len(grads)

    def body(*refs):
        ins = [refs[t * DEPTH:(t + 1) * DEPTH] for t in range(nt)]
        small_ref = refs[nt * DEPTH]
        outs = refs[nt * DEPTH + 1: nt * DEPTH + 1 + nt]
        small_out = refs[nt * DEPTH + 1 + nt]
        ssem, rsem, lsem, sssem, srsem = refs[nt * DEPTH + 2 + nt:]
        me = _my_index()
        for t in range(nt):
            width = grads[t][0].shape[axes[t]] // N_DEV
            for l in range(DEPTH):
                pltpu.make_async_copy(_slab(ins[t][l], me, width, axes[t]), outs[t].at[me, l], lsem.at[t]).start()
            for k in range(1, N_DEV):
                tgt = jnp.bitwise_xor(me, k)
                for l in range(DEPTH):
                    pltpu.make_async_remote_copy(
                        src_ref=_slab(ins[t][l], tgt, width, axes[t]), dst_ref=outs[t].at[me, l],
                        send_sem=ssem.at[t, k - 1], recv_sem=rsem.at[t, k - 1],
                        device_id=_peer(k), device_id_type=pl.DeviceIdType.MESH).start()
        pltpu.make_async_copy(small_ref, small_out.at[me], lsem.at[nt]).start()
        for k in range(1, N_DEV):
            pltpu.make_async_remote_copy(
                src_ref=small_ref, dst_ref=small_out.at[me], send_sem=sssem.at[k - 1], recv_sem=srsem.at[k - 1],
                device_id=_peer(k), device_id_type=pl.DeviceIdType.MESH).start()
        for t in range(nt):
            pltpu.make_async_copy(outs[t].at[me], outs[t].at[me], lsem.at[t]).wait()
            for k in range(1, N_DEV):
                src = jnp.bitwise_xor(me, k)
                pltpu.make_async_remote_copy(
                    src_ref=outs[t].at[src], dst_ref=outs[t].at[src], send_sem=ssem.at[t, k - 1],
                    recv_sem=rsem.at[t, k - 1], device_id=_peer(k), device_id_type=pl.DeviceIdType.MESH).wait()
        pltpu.make_async_copy(small_ref, small_out.at[me], lsem.at[nt]).wait()
        for k in range(1, N_DEV):
            src = jnp.bitwise_xor(me, k)
            pltpu.make_async_remote_copy(
                src_ref=small_ref, dst_ref=small_out.at[src], send_sem=sssem.at[k - 1], recv_sem=srsem.at[k - 1],
                device_id=_peer(k), device_id_type=pl.DeviceIdType.MESH).wait()

    out_shape = []
    for t in range(nt):
        s = list(grads[t][0].shape)
        s[axes[t]] //= N_DEV
        out_shape.append(jax.ShapeDtypeStruct((N_DEV, DEPTH, s[0], s[1]), F32))
    out_shape.append(jax.ShapeDtypeStruct((N_DEV,) + small.shape, F32))
    flat = [g for per_type in grads for g in per_type]
    return pl.pallas_call(
        body,
        in_specs=[pl.BlockSpec(memory_space=pltpu.HBM)] * (len(flat) + 1),
        out_specs=[pl.BlockSpec(memory_space=pltpu.HBM)] * (nt + 1),
        out_shape=out_shape,
        scratch_shapes=[pltpu.SemaphoreType.DMA((nt, N_DEV - 1)), pltpu.SemaphoreType.DMA((nt, N_DEV - 1)),
                        pltpu.SemaphoreType.DMA((nt + 1,)),
                        pltpu.SemaphoreType.DMA((N_DEV - 1,)), pltpu.SemaphoreType.DMA((N_DEV - 1,))],
        name="reduce_scatter_grads")(*flat, small)


def _adamw_math(w, g, m, v):
    m = ADAM_B1 * m + (1.0 - ADAM_B1) * g
    v = ADAM_B2 * v + (1.0 - ADAM_B2) * (g * g)
    m_hat = m / (1.0 - ADAM_B1 ** ADAM_STEP)
    v_hat = v / (1.0 - ADAM_B2 ** ADAM_STEP)
    delta = -ADAM_LR * (m_hat / (jnp.sqrt(v_hat) + ADAM_EPS) + ADAM_WD * w)
    return delta, m, v


def _adamw_big(recv, w, m, v, tk, name):
    _, k, n = w.shape
    npad = recv.shape[3]

    def body(r_ref, w_ref, m_ref, v_ref, g_out, d_out, m_out, v_out):
        g = r_ref[0, 0]
        for s in range(1, N_DEV):
            g = g + r_ref[s, 0]
        g = g[:, :n]
        delta, mn, vn = _adamw_math(w_ref[0], g, m_ref[0], v_ref[0])
        g_out[0] = g
        d_out[0] = delta
        m_out[0] = mn
        v_out[0] = vn

    blk = pl.BlockSpec((1, tk, n), lambda l, i: (l, i, 0))
    sds = jax.ShapeDtypeStruct(w.shape, F32)
    return pl.pallas_call(
        body, grid=(DEPTH, k // tk),
        in_specs=[pl.BlockSpec((N_DEV, 1, tk, npad), lambda l, i: (0, l, i, 0)), blk, blk, blk],
        out_specs=[blk, blk, blk, blk], out_shape=[sds, sds, sds, sds],
        compiler_params=_cp(("parallel", "parallel"), 48), name=name)(recv, w, m, v)


def _adamw_small(recv, w, m, v):
    rows = w.shape[0]
    tr = rows // SMALL_TILES

    def body(r_ref, w_ref, m_ref, v_ref, g_out, d_out, m_out, v_out):
        g = r_ref[0]
        for s in range(1, N_DEV):
            g = g + r_ref[s]
        delta, mn, vn = _adamw_math(w_ref[...], g, m_ref[...], v_ref[...])
        g_out[...] = g
        d_out[...] = delta
        m_out[...] = mn
        v_out[...] = vn

    blk = pl.BlockSpec((tr, LANES), lambda i: (i, 0))
    sds = jax.ShapeDtypeStruct(w.shape, F32)
    return pl.pallas_call(
        body, grid=(SMALL_TILES,),
        in_specs=[pl.BlockSpec((N_DEV, tr, LANES), lambda i: (0, i, 0)), blk, blk, blk],
        out_specs=[blk, blk, blk, blk], out_shape=[sds, sds, sds, sds],
        compiler_params=_cp(("parallel",), 40), name="adamw_small")(recv, w, m, v)


def _rms(t):
    return lax.rsqrt(jnp.mean(t * t, axis=-1, keepdims=True) + EPS)


def _rms_bwd(t, r, gain, dh, dres):
    u = dh * gain
    dt = dres + r * u - t * ((r * r * r) * (1.0 / D_MODEL) * jnp.sum(t * u, axis=-1, keepdims=True))
    return dt, dh * t * r


def _in_proj(x, gain, w):
    L = x.shape[0]
    n = w.shape[1]
    tm, tn = 512, 1024

    def body(x_ref, g_ref, w_ref, z_ref, h_ref):
        @pl.when(pl.program_id(1) == 0)
        def _():
            t = x_ref[...]
            h_ref[...] = (t * _rms(t) * g_ref[...]).astype(BF16)
        z_ref[...] = _nn(h_ref[...], w_ref[...])

    return pl.pallas_call(
        body, grid=(L // tm, n // tn),
        in_specs=[pl.BlockSpec((tm, D_MODEL), lambda i, j: (i, 0)), pl.BlockSpec((1, D_MODEL), lambda i, j: (0, 0)),
                  pl.BlockSpec((D_MODEL, tn), lambda i, j: (0, j))],
        out_specs=[pl.BlockSpec((tm, tn), lambda i, j: (i, j)), pl.BlockSpec((tm, D_MODEL), lambda i, j: (i, 0))],
        out_shape=[jax.ShapeDtypeStruct((L, n), F32), jax.ShapeDtypeStruct((L, D_MODEL), BF16)],
        compiler_params=_cp(("parallel", "arbitrary"), 40), name="in_proj")(x, gain, w)


def _in_proj_bwd(dz, w, x, gain, dres):
    L, n = dz.shape
    tm, tk = 512, 1024
    nk = n // tk

    def body(dz_ref, w_ref, x_ref, g_ref, dr_ref, dx_ref, dg_ref, acc):
        i, k = pl.program_id(0), pl.program_id(1)

        @pl.when(k == 0)
        def _():
            acc[...] = jnp.zeros_like(acc)

        acc[...] += _nt(dz_ref[...], w_ref[...])

        @pl.when(k == nk - 1)
        def _():
            t = x_ref[...]
            dt, dgt = _rms_bwd(t, _rms(t), g_ref[...], acc[...], dr_ref[...])
            dx_ref[...] = dt

            @pl.when(i == 0)
            def _():
                dg_ref[...] = jnp.zeros_like(dg_ref)
            dg_ref[...] += jnp.sum(dgt, axis=0, keepdims=True)

    row = pl.BlockSpec((tm, D_MODEL), lambda i, k: (i, 0))
    vec = pl.BlockSpec((1, D_MODEL), lambda i, k: (0, 0))
    return pl.pallas_call(
        body, grid=(L // tm, nk),
        in_specs=[pl.BlockSpec((tm, tk), lambda i, k: (i, k)), pl.BlockSpec((D_MODEL, tk), lambda i, k: (0, k)),
                  row, vec, row],
        out_specs=[row, vec],
        out_shape=[jax.ShapeDtypeStruct((L, D_MODEL), F32), jax.ShapeDtypeStruct((1, D_MODEL), F32)],
        scratch_shapes=[pltpu.VMEM((tm, D_MODEL), F32)],
        compiler_params=_cp(("arbitrary", "arbitrary"), 40), name="in_proj_bwd")(dz, w, x, gain, dres)


def _tn(a, b, name):
    m, na = a.shape
    nb = b.shape[1]
    ta, tb, tm = min(na, 1024), min(nb, 1024), 1024
    nm = m // tm

    def body(a_ref, b_ref, o_ref):
        @pl.when(pl.program_id(2) == 0)
        def _():
            o_ref[...] = jnp.zeros_like(o_ref)
        o_ref[...] += _tn_dot(a_ref[...].astype(BF16), b_ref[...].astype(BF16))

    return pl.pallas_call(
        body, grid=(na // ta, nb // tb, nm),
        in_specs=[pl.BlockSpec((tm, ta), lambda i, j, k: (k, i)), pl.BlockSpec((tm, tb), lambda i, j, k: (k, j))],
        out_specs=pl.BlockSpec((ta, tb), lambda i, j, k: (i, j)),
        out_shape=jax.ShapeDtypeStruct((na, nb), F32),
        compiler_params=_cp(("parallel", "parallel", "arbitrary"), 48), name=name)(a, b)


def _loss_grad(xf, target):
    L = xf.shape[0]
    tm = 1024

    def body(x_ref, t_ref, dy_ref, l_ref):
        e = x_ref[...] - t_ref[...]
        dy_ref[...] = e * (1.0 / D_MODEL)

        @pl.when(pl.program_id(0) == 0)
        def _():
            l_ref[...] = jnp.zeros_like(l_ref)
        l_ref[...] += jnp.sum(jnp.sum(e * e, axis=1, keepdims=True), axis=0, keepdims=True) * (0.5 / D_MODEL)

    row = pl.BlockSpec((tm, D_MODEL), lambda i: (i, 0))
    return pl.pallas_call(
        body, grid=(L // tm,), in_specs=[row, row],
        out_specs=[row, pl.BlockSpec((1, 1), lambda i: (0, 0))],
        out_shape=[jax.ShapeDtypeStruct((L, D_MODEL), F32), jax.ShapeDtypeStruct((1, 1), F32)],
        compiler_params=_cp(("arbitrary",), 40), name="loss_grad")(xf, target)


def _ffn_fwd(x, gain, wg, wu, wd):
    L = x.shape[0]
    ff = wg.shape[1]
    tm, tf = 512, 512
    nf = ff // tf

    def body(x_ref, g_ref, wg_ref, wu_ref, wd_ref, o_ref, h_scr, acc):
        c = pl.program_id(1)

        @pl.when(c == 0)
        def _():
            t = x_ref[...]
            h_scr[...] = (t * _rms(t) * g_ref[...]).astype(BF16)
            acc[...] = jnp.zeros_like(acc)

        h = h_scr[...]
        gate = _nn(h, wg_ref[...])
        up = _nn(h, wu_ref[...])
        hid = gate * _sigmoid(gate) * up
        acc[...] += _nn(hid.astype(BF16), wd_ref[...])

        @pl.when(c == nf - 1)
        def _():
            o_ref[...] = x_ref[...] + acc[...]

    row = pl.BlockSpec((tm, D_MODEL), lambda i, c: (i, 0))
    return pl.pallas_call(
        body, grid=(L // tm, nf),
        in_specs=[row, pl.BlockSpec((1, D_MODEL), lambda i, c: (0, 0)),
                  pl.BlockSpec((D_MODEL, tf), lambda i, c: (0, c)), pl.BlockSpec((D_MODEL, tf), lambda i, c: (0, c)),
                  pl.BlockSpec((tf, D_MODEL), lambda i, c: (c, 0))],
        out_specs=row, out_shape=jax.ShapeDtypeStruct((L, D_MODEL), F32),
        scratch_shapes=[pltpu.VMEM((tm, D_MODEL), BF16), pltpu.VMEM((tm, D_MODEL), F32)],
        compiler_params=_cp(("parallel", "arbitrary"), 40), name="ffn_fwd")(x, gain, wg, wu, wd)


def _ffn_bwd(x, gain, wg, wu, wd, dxo):
    L = x.shape[0]
    ff = wg.shape[1]
    tm, tf = 512, 512
    nf = ff // tf

    def body(x_ref, g_ref, wg_ref, wu_ref, wd_ref, dxo_ref, dx_ref, h_ref, hid_ref, dgate_ref, dup_ref, dg_ref,
             acc, dxo_b):
        i, c = pl.program_id(0), pl.program_id(1)

        @pl.when(c == 0)
        def _():
            t = x_ref[...]
            h_ref[...] = (t * _rms(t) * g_ref[...]).astype(BF16)
            acc[...] = jnp.zeros_like(acc)
            dxo_b[...] = dxo_ref[...].astype(BF16)

        h = h_ref[...]
        gate = _nn(h, wg_ref[...])
        up = _nn(h, wu_ref[...])
        sg = _sigmoid(gate)
        silu = gate * sg
        hid_ref[...] = (silu * up).astype(BF16)
        dhid = _nt(dxo_b[...], wd_ref[...])
        dup = (dhid * silu).astype(BF16)
        dgate = (dhid * up * (sg * (1.0 + gate * (1.0 - sg)))).astype(BF16)
        dup_ref[...] = dup
        dgate_ref[...] = dgate
        acc[...] += _nt(dgate, wg_ref[...]) + _nt(dup, wu_ref[...])

        @pl.when(c == nf - 1)
        def _():
            t = x_ref[...]
            dt, dgt = _rms_bwd(t, _rms(t), g_ref[...], acc[...], dxo_ref[...])
            dx_ref[...] = dt

            @pl.when(i == 0)
            def _():
                dg_ref[...] = jnp.zeros_like(dg_ref)
            dg_ref[...] += jnp.sum(dgt, axis=0, keepdims=True)

    row = pl.BlockSpec((tm, D_MODEL), lambda i, c: (i, 0))
    vec = pl.BlockSpec((1, D_MODEL), lambda i, c: (0, 0))
    wcol = pl.BlockSpec((D_MODEL, tf), lambda i, c: (0, c))
    hcol = pl.BlockSpec((tm, tf), lambda i, c: (i, c))
    return pl.pallas_call(
        body, grid=(L // tm, nf),
        in_specs=[row, vec, wcol, wcol, pl.BlockSpec((tf, D_MODEL), lambda i, c: (c, 0)), row],
        out_specs=[row, row, hcol, hcol, hcol, vec],
        out_shape=[jax.ShapeDtypeStruct((L, D_MODEL), F32), jax.ShapeDtypeStruct((L, D_MODEL), BF16),
                   jax.ShapeDtypeStruct((L, ff), BF16), jax.ShapeDtypeStruct((L, ff), BF16),
                   jax.ShapeDtypeStruct((L, ff), BF16), jax.ShapeDtypeStruct((1, D_MODEL), F32)],
        scratch_shapes=[pltpu.VMEM((tm, D_MODEL), F32), pltpu.VMEM((tm, D_MODEL), BF16)],
        compiler_params=_cp(("arbitrary", "arbitrary"), 48), name="ffn_bwd")(x, gain, wg, wu, wd, dxo)


GELU_K = math.sqrt(2.0 / math.pi)
GELU_C = 0.044715


def _gelu(y):
    return 0.5 * y * (1.0 + jnp.tanh(GELU_K * (y + GELU_C * (y * y * y))))


def _gelu_grad(y):
    th = jnp.tanh(GELU_K * (y + GELU_C * (y * y * y)))
    return 0.5 * (1.0 + th) + 0.5 * y * (1.0 - th * th) * (GELU_K * (1.0 + 3.0 * GELU_C * (y * y)))


def _merge_groups(o_refs, l_refs):
    ls = [r[...] for r in l_refs]
    os_ = [r[...] for r in o_refs]
    lmax = jnp.maximum(jnp.maximum(ls[0], ls[1]), ls[2])
    es = [jnp.exp(l - lmax) for l in ls]
    inv = 1.0 / (es[0] + es[1] + es[2])
    ws = [e * inv for e in es]
    a = ws[0] * os_[0] + ws[1] * os_[1] + ws[2] * os_[2]
    return ws, os_, a


def _mix_fwd(ols, y, z, x, wp, wa, wb, wo):
    L = x.shape[0]
    tm = 256

    def body(o0, l0, o1, l1, o2, l2, y_ref, ga_ref, gs_ref, x_ref, wp_ref, wa_ref, wb_ref, wo_ref, out_ref):
        _, _, a = _merge_groups((o0, o1, o2), (l0, l1, l2))
        a_out = _nn(a.astype(BF16), wp_ref[...])
        yg = _gelu(y_ref[...]).astype(BF16)
        s_out = _nn(yg, wa_ref[...]) * _sigmoid(_nn(yg, wb_ref[...]))
        mix = _sigmoid(ga_ref[...]) * a_out + _sigmoid(gs_ref[...]) * s_out
        out_ref[...] = x_ref[...] + _nn(mix.astype(BF16), wo_ref[...])

    half = pl.BlockSpec((tm, ATTN_W), lambda i: (i, 0))
    row = pl.BlockSpec((tm, D_MODEL), lambda i: (i, 0))
    w512 = pl.BlockSpec((ATTN_W, D_MODEL), lambda i: (0, 0))
    return pl.pallas_call(
        body, grid=(L // tm,),
        in_specs=[half] * 7 + [pl.BlockSpec((tm, D_MODEL), lambda i: (i, 5)),
                               pl.BlockSpec((tm, D_MODEL), lambda i: (i, 6)), row, w512, w512, w512,
                               pl.BlockSpec((D_MODEL, D_MODEL), lambda i: (0, 0))],
        out_specs=row, out_shape=jax.ShapeDtypeStruct((L, D_MODEL), F32),
        compiler_params=_cp(("parallel",), 48), name="mix_fwd")(*ols, y, z, z, x, wp, wa, wb, wo)


def _mix_bwd(dxm, ols, y, z, wp, wa, wb, wo):
    L = dxm.shape[0]
    tm = 256

    def body(dx_ref, o0, l0, o1, l1, o2, l2, y_ref, ga_ref, gs_ref, wp_ref, wa_ref, wb_ref, wo_ref,
             do0, dl0, do1, dl1, do2, dl2, dy_ref, dga_ref, dgs_ref, a_ref, yg_ref, mix_ref, dao_ref, dpa_ref,
             dpb_ref):
        ws, os_, a = _merge_groups((o0, o1, o2), (l0, l1, l2))
        ab = a.astype(BF16)
        a_out = _nn(ab, wp_ref[...])
        yv = y_ref[...]
        yg = _gelu(yv).astype(BF16)
        pa = _nn(yg, wa_ref[...])
        spb = _sigmoid(_nn(yg, wb_ref[...]))
        s_out = pa * spb
        sga = _sigmoid(ga_ref[...])
        sgs = _sigmoid(gs_ref[...])
        mix = sga * a_out + sgs * s_out
        dmix = _nt(dx_ref[...].astype(BF16), wo_ref[...])
        da_out = (sga * dmix).astype(BF16)
        ds_out = sgs * dmix
        dpa = (ds_out * spb).astype(BF16)
        dpb = (ds_out * pa * spb * (1.0 - spb)).astype(BF16)
        dga_ref[...] = (dmix * a_out * sga * (1.0 - sga)).astype(BF16)
        dgs_ref[...] = (dmix * s_out * sgs * (1.0 - sgs)).astype(BF16)
        dy_ref[...] = (_nt(dpa, wa_ref[...]) + _nt(dpb, wb_ref[...])) * _gelu_grad(yv)
        da = _nt(da_out, wp_ref[...])
        for w, o, do_ref, dl_ref in zip(ws, os_, (do0, do1, do2), (dl0, dl1, dl2)):
            do_ref[...] = w * da
            dl_ref[...] = da * w * (o - a)
        a_ref[...] = ab
        yg_ref[...] = yg
        mix_ref[...] = mix.astype(BF16)
        dao_ref[...] = da_out
        dpa_ref[...] = dpa
        dpb_ref[...] = dpb

    half = pl.BlockSpec((tm, ATTN_W), lambda i: (i, 0))
    row = pl.BlockSpec((tm, D_MODEL), lambda i: (i, 0))
    w512 = pl.BlockSpec((ATTN_W, D_MODEL), lambda i: (0, 0))
    hf = jax.ShapeDtypeStruct((L, ATTN_W), F32)
    hb = jax.ShapeDtypeStruct((L, ATTN_W), BF16)
    rb = jax.ShapeDtypeStruct((L, D_MODEL), BF16)
    return pl.pallas_call(
        body, grid=(L // tm,),
        in_specs=[row] + [half] * 7 + [pl.BlockSpec((tm, D_MODEL), lambda i: (i, 5)),
                                       pl.BlockSpec((tm, D_MODEL), lambda i: (i, 6)), w512, w512, w512,
                                       pl.BlockSpec((D_MODEL, D_MODEL), lambda i: (0, 0))],
        out_specs=[half] * 7 + [row, row, half, half, row, row, row, row],
        out_shape=[hf] * 7 + [rb, rb, hb, hb, rb, rb, rb, rb],
        compiler_params=_cp(("parallel",), 56), name="mix_bwd")(dxm, *ols, y, z, z, wp, wa, wb, wo)


def _rows(ref, r, d):
    return ref[...] if d == 1 else ref[pl.ds(r, BLK, stride=d), :]


def _set_rows(ref, r, d, val):
    if d == 1:
        ref[...] = val
    else:
        ref[pl.ds(r, BLK, stride=d), :] = val


def _head_masks():
    lane = lax.broadcasted_iota(jnp.int32, (1, LANES), 1)
    m0 = (lane < HEAD_DIM).astype(F32)
    return m0, 1.0 - m0


def _head_norm(t, gain2, m0, m1):
    tt = t * t
    r0 = lax.rsqrt(jnp.sum(tt * m0, axis=-1, keepdims=True) * (1.0 / HEAD_DIM) + EPS)
    r1 = lax.rsqrt(jnp.sum(tt * m1, axis=-1, keepdims=True) * (1.0 / HEAD_DIM) + EPS)
    r = m0 * r0 + m1 * r1
    return t * r * gain2, r


def _head_norm_bwd(t, r, gain2, dy, m0, m1):
    u = dy * gain2
    tu = t * u
    s = m0 * jnp.sum(tu * m0, axis=-1, keepdims=True) + m1 * jnp.sum(tu * m1, axis=-1, keepdims=True)
    return r * u - t * (r * r * r) * s * (1.0 / HEAD_DIM), jnp.sum(dy * t * r, axis=0, keepdims=True)


def _band_mask(has_prev):
    qi = lax.broadcasted_iota(jnp.int32, (BLK, 2 * BLK), 0)
    ki = lax.broadcasted_iota(jnp.int32, (BLK, 2 * BLK), 1)
    dist = BLK + qi - ki
    return (dist >= 0) & (dist <= BLK) & ((ki >= BLK) | has_prev)


def _attn_probs(qm, kw, ok):
    s = _nt(qm, kw) * (HEAD_DIM ** -0.5)
    s = jnp.where(ok, s, -1e30)
    mx = jnp.max(s, axis=-1, keepdims=True)
    p = jnp.exp(s - mx)
    den = jnp.sum(p, axis=-1, keepdims=True)
    return p, den, mx


def _attn_fwd(z, gq2, gk2, group):
    L = z.shape[0]
    d = DILATIONS[group]
    rows = d * BLK
    nsb = L // rows
    cq, ck, cv = group * 4, 12 + group * 4, 24 + group * 4

    def body(q_ref, kc_ref, kp_ref, vc_ref, vp_ref, gq_ref, gk_ref, o_ref, l_ref):
        has_prev = pl.program_id(1) > 0
        ok = _band_mask(has_prev)
        m0, m1 = _head_masks()
        gq, gk = gq_ref[...], gk_ref[...]

        def per_class(r, carry):
            qn, _ = _head_norm(_rows(q_ref, r, d), gq, m0, m1)
            kpn, _ = _head_norm(_rows(kp_ref, r, d), gk, m0, m1)
            kcn, _ = _head_norm(_rows(kc_ref, r, d), gk, m0, m1)
            kw = jnp.concatenate([kpn, kcn], axis=0).astype(BF16)
            vw = jnp.concatenate([_rows(vp_ref, r, d), _rows(vc_ref, r, d)], axis=0).astype(BF16)
            o2 = jnp.zeros((BLK, LANES), F32)
            l2 = jnp.zeros((BLK, LANES), F32)
            for mh in (m0, m1):
                p, den, mx = _attn_probs((qn * mh).astype(BF16), kw, ok)
                o2 = o2 + mh * (_nn(p.astype(BF16), vw) / den)
                l2 = l2 + mh * (mx + jnp.log(den))
            _set_rows(o_ref, r, d, o2)
            _set_rows(l_ref, r, d, l2)
            return carry

        lax.fori_loop(0, d, per_class, 0)

    def cur(c):
        return pl.BlockSpec((rows, LANES), lambda hp, n: (n, c + hp))

    def prev(c):
        return pl.BlockSpec((rows, LANES), lambda hp, n: (jnp.maximum(n - 1, 0), c + hp))

    vec = pl.BlockSpec((1, LANES), lambda hp, n: (0, 0))
    out = pl.BlockSpec((rows, LANES), lambda hp, n: (n, hp))
    sds = jax.ShapeDtypeStruct((L, ATTN_W), F32)
    return pl.pallas_call(
        body, grid=(4, nsb),
        in_specs=[cur(cq), cur(ck), prev(ck), cur(cv), prev(cv), vec, vec],
        out_specs=[out, out], out_shape=[sds, sds],
        compiler_params=_cp(("parallel", "arbitrary"), 48), name=f"attn_fwd_g{group}")(z, z, z, z, z, gq2, gk2)


def _attn_bwd(z, gq2, gk2, o, do, dl, group):
    L = z.shape[0]
    d = DILATIONS[group]
    rows = d * BLK
    nsb = L // rows
    cq, ck, cv = group * 4, 12 + group * 4, 24 + group * 4

    def body(q_ref, kc_ref, kp_ref, vc_ref, vp_ref, gq_ref, gk_ref, o_ref, do_ref, dl_ref,
             dq_ref, dk_ref, dv_ref, dgq_ref, dgk_ref, ck_scr, cv_scr):
        hp, n = pl.program_id(0), pl.program_id(1)
        m0, m1 = _head_masks()

        @pl.when((hp == 0) & (n == 0))
        def _():
            dgq_ref[...] = jnp.zeros_like(dgq_ref)
            dgk_ref[...] = jnp.zeros_like(dgk_ref)

        @pl.when(n == 0)
        def _():
            ck_scr[...] = jnp.zeros_like(ck_scr)
            cv_scr[...] = jnp.zeros_like(cv_scr)

        @pl.when(n < nsb)
        def _():
            ok = _band_mask(n > 0)
            gq, gk = gq_ref[...], gk_ref[...]

            def per_class(r, carry):
                dgq_acc, dgk_acc = carry
                q2, kp2, kc2 = _rows(q_ref, r, d), _rows(kp_ref, r, d), _rows(kc_ref, r, d)
                qn, rq = _head_norm(q2, gq, m0, m1)
                kpn, rkp = _head_norm(kp2, gk, m0, m1)
                kcn, rkc = _head_norm(kc2, gk, m0, m1)
                kw = jnp.concatenate([kpn, kcn], axis=0).astype(BF16)
                vw = jnp.concatenate([_rows(vp_ref, r, d), _rows(vc_ref, r, d)], axis=0).astype(BF16)
                o2, do2, dl2 = _rows(o_ref, r, d), _rows(do_ref, r, d), _rows(dl_ref, r, d)
                dqn = jnp.zeros((BLK, LANES), F32)
                dkw = jnp.zeros((2 * BLK, LANES), F32)
                dvw = jnp.zeros((2 * BLK, LANES), F32)
                for mh in (m0, m1):
                    qm = (qn * mh).astype(BF16)
                    p, den, _ = _attn_probs(qm, kw, ok)
                    pn = p / den
                    doh = do2 * mh
                    dohb = doh.astype(BF16)
                    dvw = dvw + _tn_dot(pn.astype(BF16), dohb)
                    dp = _nt(dohb, vw)
                    delta = jnp.sum(doh * o2, axis=-1, keepdims=True)
                    dlse = jnp.sum(dl2 * mh, axis=-1, keepdims=True)
                    ds = (pn * (dp - delta + dlse) * (HEAD_DIM ** -0.5)).astype(BF16)
                    dqn = dqn + mh * _nn(ds, kw)
                    dkw = dkw + _tn_dot(ds, qm)
                dq2, gq_t = _head_norm_bwd(q2, rq, gq, dqn, m0, m1)
                dkp2, gk_p = _head_norm_bwd(kp2, rkp, gk, dkw[:BLK], m0, m1)
                dkc2, gk_c = _head_norm_bwd(kc2, rkc, gk, dkw[BLK:], m0, m1)
                _set_rows(dq_ref, r, d, dq2)
                _set_rows(dk_ref, r, d, ck_scr[r] + dkp2)
                _set_rows(dv_ref, r, d, cv_scr[r] + dvw[:BLK])
                ck_scr[r] = dkc2
                cv_scr[r] = dvw[BLK:]
                return dgq_acc + gq_t, dgk_acc + gk_p + gk_c

            zero = jnp.zeros((1, LANES), F32)
            dgq_t, dgk_t = lax.fori_loop(0, d, per_class, (zero, zero))
            dgq_ref[...] += dgq_t
            dgk_ref[...] += dgk_t

        @pl.when(n == nsb)
        def _():
            def flush(r, carry):
                _set_rows(dk_ref, r, d, ck_scr[r])
                _set_rows(dv_ref, r, d, cv_scr[r])
                return carry
            lax.fori_loop(0, d, flush, 0)

    last = nsb - 1

    def cur(c):
        return pl.BlockSpec((rows, LANES), lambda hp, n: (jnp.minimum(n, last), c + hp))

    def prev(c):
        return pl.BlockSpec((rows, LANES), lambda hp, n: (jnp.maximum(jnp.minimum(n, last) - 1, 0), c + hp))

    vec = pl.BlockSpec((1, LANES), lambda hp, n: (0, 0))
    lag = pl.BlockSpec((rows, LANES), lambda hp, n: (jnp.maximum(n - 1, 0), hp))
    sds = jax.ShapeDtypeStruct((L, ATTN_W), F32)
    vsd = jax.ShapeDtypeStruct((1, LANES), F32)
    return pl.pallas_call(
        body, grid=(4, nsb + 1),
        in_specs=[cur(cq), cur(ck), prev(ck), cur(cv), prev(cv), vec, vec, cur(0), cur(0), cur(0)],
        out_specs=[cur(0), lag, lag, vec, vec], out_shape=[sds, sds, sds, vsd, vsd],
        scratch_shapes=[pltpu.VMEM((d, BLK, LANES), F32), pltpu.VMEM((d, BLK, LANES), F32)],
        compiler_params=_cp(("arbitrary", "arbitrary"), 56),
        name=f"attn_bwd_g{group}")(z, z, z, z, z, gq2, gk2, o, do, dl)


N_SLAB = SSM_STATES // LANES


def _ssm_project_in(ub, bdr_ref, bdi_ref, sr, si, tm):
    for k in range(N_SLAB):
        cols = slice(k * LANES, (k + 1) * LANES)
        sr[k % 2][pl.ds(k // 2, tm, stride=8), :] = _nn(ub, bdr_ref[:, cols])
        si[k % 2][pl.ds(k // 2, tm, stride=8), :] = _nn(ub, bdi_ref[:, cols])


def _ssm_scan(a, x0, sr, si, tm):
    ar0, ar1, ai0, ai1 = a

    def step(t, c):
        xr0, xr1, xi0, xi1 = c
        i8 = pl.multiple_of(t * 8, 8)
        nr0 = ar0 * xr0 - ai0 * xi0 + sr[0][pl.ds(i8, 8), :]
        ni0 = ar0 * xi0 + ai0 * xr0 + si[0][pl.ds(i8, 8), :]
        nr1 = ar1 * xr1 - ai1 * xi1 + sr[1][pl.ds(i8, 8), :]
        ni1 = ar1 * xi1 + ai1 * xr1 + si[1][pl.ds(i8, 8), :]
        sr[0][pl.ds(i8, 8), :] = nr0
        si[0][pl.ds(i8, 8), :] = ni0
        sr[1][pl.ds(i8, 8), :] = nr1
        si[1][pl.ds(i8, 8), :] = ni1
        return nr0, nr1, ni0, ni1

    return lax.fori_loop(0, tm, step, x0, unroll=4)


def _load_a(ar_ref, ai_ref):
    return ar_ref[:, :LANES], ar_ref[:, LANES:], ai_ref[:, :LANES], ai_ref[:, LANES:]


def _ssm_fwd(z, ar8, ai8, bdr, bdi, cdr, cdi, dsk):
    L = z.shape[0]
    tm = SSM_TM
    nc = L // tm

    def body(u_ref, ar_ref, ai_ref, bdr_ref, bdi_ref, cdr_ref, cdi_ref, dsk_ref, y_ref, cin_ref,
             sr0, sr1, si0, si1, car):
        sr, si = (sr0, sr1), (si0, si1)

        @pl.when(pl.program_id(0) == 0)
        def _():
            car[...] = jnp.zeros_like(car)

        u = u_ref[...]
        _ssm_project_in(u.astype(BF16), bdr_ref, bdi_ref, sr, si, tm)
        cin_ref[0] = car[...]
        xr0, xr1, xi0, xi1 = _ssm_scan(_load_a(ar_ref, ai_ref), (car[0], car[1], car[2], car[3]), sr, si, tm)
        car[0], car[1], car[2], car[3] = xr0, xr1, xi0, xi1
        y = dsk_ref[...] * u
        for k in range(N_SLAB):
            rws = slice(k * LANES, (k + 1) * LANES)
            xr = sr[k % 2][pl.ds(k // 2, tm, stride=8), :].astype(BF16)
            xi = si[k % 2][pl.ds(k // 2, tm, stride=8), :].astype(BF16)
            y = y + _nn(xr, cdr_ref[rws, :]) - _nn(xi, cdi_ref[rws, :])
        y_ref[...] = y

    def const(shape):
        return pl.BlockSpec(shape, lambda i: (0,) * len(shape))

    state = pltpu.VMEM((tm * 8, LANES), F32)
    return pl.pallas_call(
        body, grid=(nc,),
        in_specs=[pl.BlockSpec((tm, SSM_W), lambda i: (i, COL_U)), const((8, 256)), const((8, 256)),
                  const((SSM_W, SSM_STATES)), const((SSM_W, SSM_STATES)), const((SSM_STATES, SSM_W)),
                  const((SSM_STATES, SSM_W)), const((1, SSM_W))],
        out_specs=[pl.BlockSpec((tm, SSM_W), lambda i: (i, 0)), pl.BlockSpec((1, 4, 8, LANES), lambda i: (i, 0, 0, 0))],
        out_shape=[jax.ShapeDtypeStruct((L, SSM_W), F32), jax.ShapeDtypeStruct((nc, 4, 8, LANES), F32)],
        scratch_shapes=[state, state, state, state, pltpu.VMEM((4, 8, LANES), F32)],
        compiler_params=_cp(("arbitrary",), 48), name="ssm_fwd")(z, ar8, ai8, bdr, bdi, cdr, cdi, dsk)


def _ssm_bwd(z, dy, cin, ar8, ai8, bdr, bdi, cdr, cdi, dsk):
    L = z.shape[0]
    tm = SSM_TM
    nc = L // tm

    def body(u_ref, dy_ref, cin_ref, ar_ref, ai_ref, dsk_ref, bdr_hbm, bdi_hbm, cdr_hbm, cdi_hbm,
             du_ref, da_ref, dds_ref, dbdr_hbm, dbdi_hbm, dcdr_hbm, dcdi_hbm,
             bdr_ref, bdi_ref, cdr_ref, cdi_ref, sr0, sr1, si0, si1, gr0, gr1, gi0, gi1, carg,
             abdr, abdi, acdr, acdi):
        i = pl.program_id(0)
        sr, si, gr, gi = (sr0, sr1), (si0, si1), (gr0, gr1), (gi0, gi1)

        @pl.when(i == 0)
        def _():
            pltpu.sync_copy(bdr_hbm, bdr_ref)
            pltpu.sync_copy(bdi_hbm, bdi_ref)
            pltpu.sync_copy(cdr_hbm, cdr_ref)
            pltpu.sync_copy(cdi_hbm, cdi_ref)
            carg[...] = jnp.zeros_like(carg)
            da_ref[...] = jnp.zeros_like(da_ref)
            dds_ref[...] = jnp.zeros_like(dds_ref)
            abdr[...] = jnp.zeros_like(abdr)
            abdi[...] = jnp.zeros_like(abdi)
            acdr[...] = jnp.zeros_like(acdr)
            acdi[...] = jnp.zeros_like(acdi)

        u = u_ref[...]
        ub = u.astype(BF16)
        dyv = dy_ref[...]
        dyb = dyv.astype(BF16)
        a = _load_a(ar_ref, ai_ref)
        ar0, ar1, ai0, ai1 = a
        x_in = (cin_ref[0, 0], cin_ref[0, 1], cin_ref[0, 2], cin_ref[0, 3])
        _ssm_project_in(ub, bdr_ref, bdi_ref, sr, si, tm)
        _ssm_scan(a, x_in, sr, si, tm)
        for k in range(N_SLAB):
            rws = slice(k * LANES, (k + 1) * LANES)
            gr[k % 2][pl.ds(k // 2, tm, stride=8), :] = _nt(dyb, cdr_ref[rws, :])
            gi[k % 2][pl.ds(k // 2, tm, stride=8), :] = -_nt(dyb, cdi_ref[rws, :])

        def grad_step(t, g_next, x_prev, acc):
            i8 = pl.multiple_of(t * 8, 8)
            nr0, nr1, ni0, ni1 = g_next
            pr0, pr1, pi0, pi1 = x_prev
            d_r0, d_r1, d_i0, d_i1 = acc
            g_r0 = gr[0][pl.ds(i8, 8), :] + ar0 * nr0 + ai0 * ni0
            g_i0 = gi[0][pl.ds(i8, 8), :] + ar0 * ni0 - ai0 * nr0
            g_r1 = gr[1][pl.ds(i8, 8), :] + ar1 * nr1 + ai1 * ni1
            g_i1 = gi[1][pl.ds(i8, 8), :] + ar1 * ni1 - ai1 * nr1
            gr[0][pl.ds(i8, 8), :] = g_r0
            gi[0][pl.ds(i8, 8), :] = g_i0
            gr[1][pl.ds(i8, 8), :] = g_r1
            gi[1][pl.ds(i8, 8), :] = g_i1
            acc = (d_r0 + pr0 * g_r0 + pi0 * g_i0, d_r1 + pr1 * g_r1 + pi1 * g_i1,
                   d_i0 + pr0 * g_i0 - pi0 * g_r0, d_i1 + pr1 * g_i1 - pi1 * g_r1)
            return (g_r0, g_r1, g_i0, g_i1), acc

        def rstep(j, c):
            t = tm - 1 - j
            p8 = pl.multiple_of((t - 1) * 8, 8)
            x_prev = (sr[0][pl.ds(p8, 8), :], sr[1][pl.ds(p8, 8), :], si[0][pl.ds(p8, 8), :], si[1][pl.ds(p8, 8), :])
            return grad_step(t, c[0], x_prev, c[1])

        acc0 = (da_ref[0], da_ref[1], da_ref[2], da_ref[3])
        g_next, acc = lax.fori_loop(0, tm - 1, rstep, ((carg[0], carg[1], carg[2], carg[3]), acc0), unroll=4)
        g_first, acc = grad_step(0, g_next, x_in, acc)
        carg[0], carg[1], carg[2], carg[3] = g_first
        da_ref[0], da_ref[1], da_ref[2], da_ref[3] = acc

        du = dsk_ref[...] * dyv
        for k in range(N_SLAB):
            sl = slice(k * LANES, (k + 1) * LANES)
            grk = gr[k % 2][pl.ds(k // 2, tm, stride=8), :].astype(BF16)
            gik = gi[k % 2][pl.ds(k // 2, tm, stride=8), :].astype(BF16)
            xrk = sr[k % 2][pl.ds(k // 2, tm, stride=8), :].astype(BF16)
            xik = si[k % 2][pl.ds(k // 2, tm, stride=8), :].astype(BF16)
            du = du + _nt(grk, bdr_ref[:, sl]) + _nt(gik, bdi_ref[:, sl])
            abdr[:, sl] += _tn_dot(ub, grk)
            abdi[:, sl] += _tn_dot(ub, gik)
            acdr[sl, :] += _tn_dot(xrk, dyb)
            acdi[sl, :] -= _tn_dot(xik, dyb)
        du_ref[...] = du
        dds_ref[...] += jnp.sum(dyv * u, axis=0, keepdims=True)

        @pl.when(i == nc - 1)
        def _():
            pltpu.sync_copy(abdr, dbdr_hbm)
            pltpu.sync_copy(abdi, dbdi_hbm)
            pltpu.sync_copy(acdr, dcdr_hbm)
            pltpu.sync_copy(acdi, dcdi_hbm)

    def const(shape):
        return pl.BlockSpec(shape, lambda i: (0,) * len(shape))

    hbm = pl.BlockSpec(memory_space=pltpu.HBM)
    state = pltpu.VMEM((tm * 8, LANES), F32)
    wb = jax.ShapeDtypeStruct((SSM_W, SSM_STATES), F32)
    wc = jax.ShapeDtypeStruct((SSM_STATES, SSM_W), F32)
    return pl.pallas_call(
        body, grid=(nc,),
        in_specs=[pl.BlockSpec((tm, SSM_W), lambda i: (nc - 1 - i, COL_U)),
                  pl.BlockSpec((tm, SSM_W), lambda i: (nc - 1 - i, 0)),
                  pl.BlockSpec((1, 4, 8, LANES), lambda i: (nc - 1 - i, 0, 0, 0)),
                  const((8, 256)), const((8, 256)), const((1, SSM_W)), hbm, hbm, hbm, hbm],
        out_specs=[pl.BlockSpec((tm, SSM_W), lambda i: (nc - 1 - i, 0)), const((4, 8, LANES)), const((1, SSM_W)),
                   hbm, hbm, hbm, hbm],
        out_shape=[jax.ShapeDtypeStruct((L, SSM_W), F32), jax.ShapeDtypeStruct((4, 8, LANES), F32),
                   jax.ShapeDtypeStruct((1, SSM_W), F32), wb, wb, wc, wc],
        scratch_shapes=[pltpu.VMEM((SSM_W, SSM_STATES), BF16), pltpu.VMEM((SSM_W, SSM_STATES), BF16),
                        pltpu.VMEM((SSM_STATES, SSM_W), BF16), pltpu.VMEM((SSM_STATES, SSM_W), BF16),
                        state, state, state, state, state, state, state, state,
                        pltpu.VMEM((4, 8, LANES), F32),
                        pltpu.VMEM((SSM_W, SSM_STATES), F32), pltpu.VMEM((SSM_W, SSM_STATES), F32),
                        pltpu.VMEM((SSM_STATES, SSM_W), F32), pltpu.VMEM((SSM_STATES, SSM_W), F32)],
        compiler_params=_cp(("arbitrary",), 56), name="ssm_bwd")(z, dy, cin, ar8, ai8, dsk, bdr, bdi, cdr, cdi)


def _discretise(lam_re, lam_im, log_dt, b_re, b_im):
    dt = jnp.exp(log_dt)[:, None]
    mag = jnp.exp(lam_re * dt)
    ang = lam_im * dt
    abar_re = mag * jnp.cos(ang)
    abar_im = mag * jnp.sin(ang)
    nr = abar_re - 1.0
    ni = abar_im
    den = lam_re * lam_re + lam_im * lam_im
    cr = ((nr * lam_re + ni * lam_im) / den)[..., None]
    ci = ((ni * lam_re - nr * lam_im) / den)[..., None]
    return abar_re, abar_im, cr * b_re - ci * b_im, cr * b_im + ci * b_re


def _block_diag_in(bbar):
    eye = jnp.eye(32, dtype=F32)
    return jnp.einsum("gpc,gh->gchp", bbar, eye).reshape(SSM_W, SSM_STATES)


def _block_diag_in_t(dense):
    eye = jnp.eye(32, dtype=F32)
    return jnp.einsum("gchp,gh->gpc", dense.reshape(32, 16, 32, 64), eye)


def _block_diag_out(c):
    eye = jnp.eye(32, dtype=F32)
    return jnp.einsum("gcp,gh->gphc", c, eye).reshape(SSM_STATES, SSM_W)


def _block_diag_out_t(dense):
    eye = jnp.eye(32, dtype=F32)
    return jnp.einsum("gphc,gh->gcp", dense.reshape(32, 64, 32, 16), eye)


SMALL_NAMES = ("g_mix", "g_q", "g_k", "lambda_re", "lambda_im", "log_dt", "b_re", "b_im", "c_re", "c_im",
               "d_skip", "g_ffn")


def _pack_small(parts):
    flat = jnp.concatenate([parts[n].reshape(-1) for n in SMALL_NAMES])
    pad = (-flat.shape[0]) % (8 * LANES * SMALL_TILES)
    return jnp.pad(flat, (0, pad)).reshape(-1, LANES)


def _unpack_small(packed, like):
    flat = packed.reshape(-1)
    out, off = {}, 0
    for n in SMALL_NAMES:
        size = like[n].size
        out[n] = flat[off:off + size].reshape(like[n].shape)
        off += size
    return out


BIG_NAMES = ("w_in", "w_attn_proj", "w_glu_a", "w_glu_b", "w_out", "w_ffn_gate", "w_ffn_up", "w_ffn_down")
BIG_SHARD_AXIS = {"w_in": 2, "w_attn_proj": 2, "w_glu_a": 2, "w_glu_b": 2, "w_out": 1,
                  "w_ffn_gate": 2, "w_ffn_up": 2, "w_ffn_down": 1}
ADAMW_ROWS = {"w_in": 256, "w_attn_proj": 512, "w_glu_a": 512, "w_glu_b": 512, "w_out": 128,
              "w_ffn_gate": 256, "w_ffn_up": 256, "w_ffn_down": 176}


def kernel(x, g_mix, w_in, g_q, g_k, w_attn_proj, lambda_re, lambda_im, log_dt, b_re, b_im, c_re, c_im, d_skip, w_glu_a, w_glu_b, w_out, g_ffn, w_ffn_gate, w_ffn_up, w_ffn_down, loss_target, m_g_mix, m_w_in, m_g_q, m_g_k, m_w_attn_proj, m_lambda_re, m_lambda_im, m_log_dt, m_b_re, m_b_im, m_c_re, m_c_im, m_d_skip, m_w_glu_a, m_w_glu_b, m_w_out, m_g_ffn, m_w_ffn_gate, m_w_ffn_up, m_w_ffn_down, v_g_mix, v_w_in, v_g_q, v_g_k, v_w_attn_proj, v_lambda_re, v_lambda_im, v_log_dt, v_b_re, v_b_im, v_c_re, v_c_im, v_d_skip, v_w_glu_a, v_w_glu_b, v_w_out, v_g_ffn, v_w_ffn_gate, v_w_ffn_up, v_w_ffn_down):
    args = dict(locals())
    weights = {n: args[n] for n in BIG_NAMES + SMALL_NAMES}
    moments_m = {n: args["m_" + n] for n in BIG_NAMES + SMALL_NAMES}
    moments_v = {n: args["v_" + n] for n in BIG_NAMES + SMALL_NAMES}
    x0 = x[0]
    target = loss_target[0]

    shards = []
    for n in BIG_NAMES:
        w = weights[n]
        rows_to, cols_to = w.shape[1], w.shape[2]
        if n in ("w_ffn_gate", "w_ffn_up"):
            cols_to = FF_SHARD_PAD
        if n == "w_ffn_down":
            rows_to = FF_SHARD_PAD
        shards.append(_prep_weight(w, rows_to, cols_to, "prep_" + n))
    full = dict(zip(BIG_NAMES, _all_gather(shards, [BIG_SHARD_AXIS[n] for n in BIG_NAMES])))

    saved = []
    xl = x0
    for l in range(DEPTH):
        abar_re, abar_im, bb_re, bb_im = _discretise(lambda_re[l], lambda_im[l], log_dt[l], b_re[l], b_im[l])
        ssm = dict(ar8=abar_re.reshape(8, 256), ai8=abar_im.reshape(8, 256),
                   bdr=_block_diag_in(bb_re).astype(BF16), bdi=_block_diag_in(bb_im).astype(BF16),
                   cdr=_block_diag_out(c_re[l]).astype(BF16), cdi=_block_diag_out(c_im[l]).astype(BF16),
                   dsk=d_skip[l][None])
        gq2 = jnp.tile(g_q[l], 2)[None]
        gk2 = jnp.tile(g_k[l], 2)[None]
        z, h = _in_proj(xl, g_mix[l][None], full["w_in"][l])
        ols = []
        for g in range(N_GROUPS):
            ols.extend(_attn_fwd(z, gq2, gk2, g))
        y, cin = _ssm_fwd(z, **ssm)
        xm = _mix_fwd(ols, y, z, xl, full["w_attn_proj"][l], full["w_glu_a"][l], full["w_glu_b"][l], full["w_out"][l])
        xo = _ffn_fwd(xm, g_ffn[l][None], full["w_ffn_gate"][l], full["w_ffn_up"][l], full["w_ffn_down"][l])
        saved.append(dict(x=xl, z=z, h=h, ols=ols, y=y, cin=cin, xm=xm, ssm=ssm, gq2=gq2, gk2=gk2))
        xl = xo

    dxo, loss_local = _loss_grad(xl, target)
    loss = lax.psum(loss_local[0, 0], MESH_AXES)
    big_grads = {n: [None] * DEPTH for n in BIG_NAMES}
    small_grads = {n: [None] * DEPTH for n in SMALL_NAMES}
    for l in reversed(range(DEPTH)):
        s = saved[l]
        dxm, h2, hid, dgate, dup, dgffn = _ffn_bwd(s["xm"], g_ffn[l][None], full["w_ffn_gate"][l],
                                                   full["w_ffn_up"][l], full["w_ffn_down"][l], dxo)
        big_grads["w_ffn_down"][l] = _tn(hid, dxo, "grad_w_ffn_down")
        big_grads["w_ffn_gate"][l] = _tn(h2, dgate, "grad_w_ffn_gate")
        big_grads["w_ffn_up"][l] = _tn(h2, dup, "grad_w_ffn_up")
        (do0, dl0, do1, dl1, do2, dl2, dy, dga, dgs, a_b, yg_b, mix_b, dao_b, dpa_b, dpb_b) = _mix_bwd(
            dxm, s["ols"], s["y"], s["z"], full["w_attn_proj"][l], full["w_glu_a"][l], full["w_glu_b"][l],
            full["w_out"][l])
        big_grads["w_out"][l] = _tn(mix_b, dxm, "grad_w_out")
        big_grads["w_attn_proj"][l] = _tn(a_b, dao_b, "grad_w_attn_proj")
        big_grads["w_glu_a"][l] = _tn(yg_b, dpa_b, "grad_w_glu_a")
        big_grads["w_glu_b"][l] = _tn(yg_b, dpb_b, "grad_w_glu_b")
        du, da4, ddsk, dbdr, dbdi, dcdr, dcdi = _ssm_bwd(s["z"], dy, s["cin"], **s["ssm"])
        dqkv = []
        dgq = jnp.zeros((1, LANES), F32)
        dgk = jnp.zeros((1, LANES), F32)
        for g, (do_g, dl_g) in enumerate(((do0, dl0), (do1, dl1), (do2, dl2))):
            dq, dk, dv, dgq_g, dgk_g = _attn_bwd(s["z"], s["gq2"], s["gk2"], s["ols"][2 * g], do_g, dl_g, g)
            dqkv.append((dq, dk, dv))
            dgq, dgk = dgq + dgq_g, dgk + dgk_g
        dz = jnp.concatenate([dqkv[0][0], dqkv[1][0], dqkv[2][0], dqkv[0][1], dqkv[1][1], dqkv[2][1],
                              dqkv[0][2], dqkv[1][2], dqkv[2][2], du, dga.astype(F32), dgs.astype(F32)],
                             axis=1).astype(BF16)
        dxo, dgmix = _in_proj_bwd(dz, full["w_in"][l], s["x"], g_mix[l][None], dxm)
        big_grads["w_in"][l] = _tn(s["h"], dz, "grad_w_in")
        _, disc_vjp = jax.vjp(_discretise, lambda_re[l], lambda_im[l], log_dt[l], b_re[l], b_im[l])
        dar = jnp.concatenate([da4[0], da4[1]], axis=1).reshape(32, 64)
        dai = jnp.concatenate([da4[2], da4[3]], axis=1).reshape(32, 64)
        dlr, dli, dldt, dbre, dbim = disc_vjp((dar, dai, _block_diag_in_t(dbdr), _block_diag_in_t(dbdi)))
        small_grads["g_mix"][l] = dgmix[0]
        small_grads["g_q"][l] = dgq[0, :HEAD_DIM] + dgq[0, HEAD_DIM:]
        small_grads["g_k"][l] = dgk[0, :HEAD_DIM] + dgk[0, HEAD_DIM:]
        small_grads["lambda_re"][l] = dlr
        small_grads["lambda_im"][l] = dli
        small_grads["log_dt"][l] = dldt
        small_grads["b_re"][l] = dbre
        small_grads["b_im"][l] = dbim
        small_grads["c_re"][l] = _block_diag_out_t(dcdr)
        small_grads["c_im"][l] = _block_diag_out_t(dcdi)
        small_grads["d_skip"][l] = ddsk[0]
        small_grads["g_ffn"][l] = dgffn[0]
    grad_x = dxo[None]

    small_local = {n: jnp.stack(small_grads[n]) for n in SMALL_NAMES}
    rs_axes = [BIG_SHARD_AXIS[n] - 1 for n in BIG_NAMES]
    recv = _reduce_scatter([big_grads[n] for n in BIG_NAMES], rs_axes, _pack_small(small_local))
    out_g, out_d, out_m, out_v = {}, {}, {}, {}
    for n, r in zip(BIG_NAMES, recv[:-1]):
        out_g[n], out_d[n], out_m[n], out_v[n] = _adamw_big(r, weights[n], moments_m[n], moments_v[n],
                                                            ADAMW_ROWS[n], "adamw_" + n)
    like = {n: weights[n] for n in SMALL_NAMES}
    packed = _adamw_small(recv[-1], _pack_small(like), _pack_small({n: moments_m[n] for n in SMALL_NAMES}),
                          _pack_small({n: moments_v[n] for n in SMALL_NAMES}))
    for dst, p in zip((out_g, out_d, out_m, out_v), packed):
        dst.update(_unpack_small(p, like))

    order = ("g_mix", "w_in", "g_q", "g_k", "w_attn_proj", "lambda_re", "lambda_im", "log_dt", "b_re", "b_im",
             "c_re", "c_im", "d_skip", "w_glu_a", "w_glu_b", "w_out", "g_ffn", "w_ffn_gate", "w_ffn_up",
             "w_ffn_down")
    return (loss, grad_x, *[out_g[n] for n in order], *[out_d[n] for n in order],
            *[out_m[n] for n in order], *[out_v[n] for n in order])
```

```python
import functools
import math

import jax
import jax.numpy as jnp
from jax import lax
from jax.experimental import pallas as pl
from jax.experimental.pallas import tpu as pltpu

F32 = jnp.float32
BF16 = jnp.bfloat16

D_MODEL = 1024
DEPTH = 4
N_DEV = 8
N_CHIPS = 4
HEAD_DIM = 64
BLK = 128
LANES = 128
ATTN_W = 512
N_GROUPS = 3
DILATIONS = (1, 4, 16)
ATTN_ROWS = 2048
SSM_W = 512
SSM_STATES = 2048
SSM_BLOCKS = 4
IN_COLS = 7168
COL_U = 9
D_FF = 2816
FF_SHARD = D_FF // N_DEV
FF_SHARD_PAD = 384
FF_PAD = FF_SHARD_PAD * N_DEV
EPS = 1e-6
SSM_TM = 512
SMALL_TILES = 4

ADAM_LR = 0.001
ADAM_B1 = 0.9
ADAM_B2 = 0.999
ADAM_EPS = 1e-08
ADAM_WD = 0.01
ADAM_STEP = 10

MESH_AXES = ("x", "y", "c")
MIB = 1024 * 1024


def _cp(sem=None, vmem_mib=None):
    kw = {}
    if sem is not None:
        kw["dimension_semantics"] = sem
    if vmem_mib is not None:
        kw["vmem_limit_bytes"] = vmem_mib * MIB
    return pltpu.CompilerParams(**kw)


def _nt(a, b):
    return lax.dot_general(a, b, (((1,), (1,)), ((), ())), preferred_element_type=F32)


def _tn_dot(a, b):
    return lax.dot_general(a, b, (((0,), (0,)), ((), ())), preferred_element_type=F32)


def _nn(a, b):
    return jnp.dot(a, b, preferred_element_type=F32)


def _sigmoid(t):
    return jax.nn.sigmoid(t)


def _prep_weight(w, rows_to, cols_to, name):
    _, k, n = w.shape

    def body(w_ref, o_ref):
        if rows_to != k or cols_to != n:
            o_ref[...] = jnp.zeros(o_ref.shape, BF16)
        o_ref[0, :k, :n] = w_ref[0].astype(BF16)

    return pl.pallas_call(
        body, grid=(DEPTH,),
        in_specs=[pl.BlockSpec((1, k, n), lambda l: (l, 0, 0))],
        out_specs=pl.BlockSpec((1, rows_to, cols_to), lambda l: (l, 0, 0)),
        out_shape=jax.ShapeDtypeStruct((DEPTH, rows_to, cols_to), BF16),
        compiler_params=_cp(("parallel",), 40), name=name)(w)


def _my_index():
    return 4 * lax.axis_index("x") + 2 * lax.axis_index("y") + lax.axis_index("c")


def _my_chip():
    return 2 * lax.axis_index("x") + lax.axis_index("y")


def _sibling():
    return (lax.axis_index("x"), lax.axis_index("y"), 1 - lax.axis_index("c"))


def _other_chip(j):
    return (jnp.bitwise_xor(lax.axis_index("x"), (j >> 1) & 1), jnp.bitwise_xor(lax.axis_index("y"), j & 1))


def _slab(ref, idx, width, axis):
    start = pl.multiple_of(idx * width, width)
    sl = [slice(None)] * len(ref.shape)
    sl[axis] = pl.ds(start, width)
    return ref.at[tuple(sl)]


def _remote(src, dst, ssem, rsem, device):
    return pltpu.make_async_remote_copy(src_ref=src, dst_ref=dst, send_sem=ssem, recv_sem=rsem,
                                        device_id=device, device_id_type=pl.DeviceIdType.MESH)


def _two_level_gather(srcs, blocks, ssem, rsem, lsem):
    nt = len(srcs)
    x, y, c = lax.axis_index("x"), lax.axis_index("y"), lax.axis_index("c")
    me = _my_index()
    local, sends = [], []
    for t in range(nt):
        mine = blocks[t](me)
        loc = pltpu.make_async_copy(srcs[t], mine, lsem.at[t])
        loc.start()
        local.append(loc)
        first = [_remote(srcs[t], mine, ssem.at[t, 0], rsem.at[t, 0], _sibling())]
        for j in range(1, N_CHIPS):
            first.append(_remote(srcs[t], mine, ssem.at[t, j], rsem.at[t, j], (*_other_chip(j), c)))
        for cp in first:
            cp.start()
        sends.extend(first)
    for t in range(nt):
        for j in range(1, N_CHIPS):
            ox, oy = _other_chip(j)
            landed = blocks[t](4 * ox + 2 * oy + c)
            _remote(landed, landed, ssem.at[t, j], rsem.at[t, j], _sibling()).wait_recv()
            fwd = _remote(landed, landed, ssem.at[t, 3 + j], rsem.at[t, 3 + j], _sibling())
            fwd.start()
            sends.append(fwd)
    for t in range(nt):
        got = blocks[t](4 * x + 2 * y + (1 - c))
        _remote(got, got, ssem.at[t, 0], rsem.at[t, 0], _sibling()).wait_recv()
        for j in range(1, N_CHIPS):
            ox, oy = _other_chip(j)
            got = blocks[t](4 * ox + 2 * oy + (1 - c))
            _remote(got, got, ssem.at[t, 3 + j], rsem.at[t, 3 + j], _sibling()).wait_recv()
    for cp in sends:
        cp.wait_send()
    for cp in local:
        cp.wait()


def _gather_sems(nt):
    return [pltpu.SemaphoreType.DMA((nt, N_DEV - 1)), pltpu.SemaphoreType.DMA((nt, N_DEV - 1)),
            pltpu.SemaphoreType.DMA((nt,))]


def _all_gather(shards, axes):
    nt = len(shards)

    def body(*refs):
        ins, outs = refs[:nt], refs[nt:2 * nt]
        ssem, rsem, lsem = refs[2 * nt:]
        blocks = [functools.partial(_slab, outs[t], width=shards[t].shape[axes[t]], axis=axes[t]) for t in range(nt)]
        _two_level_gather(ins, blocks, ssem, rsem, lsem)

    out_shape = []
    for t in range(nt):
        s = list(shards[t].shape)
        s[axes[t]] *= N_DEV
        out_shape.append(jax.ShapeDtypeStruct(tuple(s), shards[t].dtype))
    return pl.pallas_call(
        body,
        in_specs=[pl.BlockSpec(memory_space=pltpu.HBM)] * nt,
        out_specs=[pl.BlockSpec(memory_space=pltpu.HBM)] * nt,
        out_shape=out_shape, scratch_shapes=_gather_sems(nt),
        name="all_gather_weights")(*shards)


def _exchange_with_sibling(grads, axes):
    nt = len(grads)

    def body(*refs):
        ins = [refs[t * DEPTH:(t + 1) * DEPTH] for t in range(nt)]
        outs = refs[nt * DEPTH: nt * DEPTH + nt]
        ssem, rsem = refs[nt * DEPTH + nt:]
        c = lax.axis_index("c")
        for t in range(nt):
            width = grads[t][0].shape[axes[t]] // N_DEV
            for q in range(N_CHIPS):
                for l in range(DEPTH):
                    _remote(_slab(ins[t][l], 2 * q + (1 - c), width, axes[t]), outs[t].at[q, l],
                            ssem.at[t], rsem.at[t], _sibling()).start()
        for t in range(nt):
            _remote(outs[t], outs[t], ssem.at[t], rsem.at[t], _sibling()).wait()

    out_shape = []
    for t in range(nt):
        s = list(grads[t][0].shape)
        s[axes[t]] //= N_DEV
        out_shape.append(jax.ShapeDtypeStruct((N_CHIPS, DEPTH, s[0], s[1]), F32))
    flat = [g for per_type in grads for g in per_type]
    return pl.pallas_call(
        body,
        in_specs=[pl.BlockSpec(memory_space=pltpu.HBM)] * len(flat),
        out_specs=[pl.BlockSpec(memory_space=pltpu.HBM)] * nt,
        out_shape=out_shape,
        scratch_shapes=[pltpu.SemaphoreType.DMA((nt,)), pltpu.SemaphoreType.DMA((nt,))],
        name="grads_to_sibling")(*flat)


def _chip_sum(grad, got, layer, axis, core, name):
    _, _, r, c = got.shape
    tr = min(r, 512)

    def body(core_ref, g_ref, s_ref, o_ref):
        o_ref[0] = (g_ref[...] + s_ref[0, 0]).astype(BF16)

    if axis == 1:
        g_spec = pl.BlockSpec((tr, c), lambda q, i, core_ref: (i, 2 * q + core_ref[0]))
    else:
        g_spec = pl.BlockSpec((tr, c), lambda q, i, core_ref: ((2 * q + core_ref[0]) * (r // tr) + i, 0))
    return pl.pallas_call(
        body,
        grid_spec=pltpu.PrefetchScalarGridSpec(
            num_scalar_prefetch=1, grid=(N_CHIPS, r // tr),
            in_specs=[g_spec, pl.BlockSpec((1, 1, tr, c), lambda q, i, core_ref: (q, layer, i, 0))],
            out_specs=pl.BlockSpec((1, tr, c), lambda q, i, core_ref: (q, i, 0))),
        out_shape=jax.ShapeDtypeStruct((N_CHIPS, r, c), BF16),
        compiler_params=_cp(("parallel", "parallel"), 40), name=name)(core, grad, got)


def _exchange_chip_sums(sums, small):
    nt = len(sums)

    def body(*refs):
        ins = [refs[t * DEPTH:(t + 1) * DEPTH] for t in range(nt)]
        small_ref = refs[nt * DEPTH]
        outs = refs[nt * DEPTH + 1: nt * DEPTH + 1 + nt]
        small_out = refs[nt * DEPTH + 1 + nt]
        ssem, rsem, lsem, g_ssem, g_rsem, g_lsem = refs[nt * DEPTH + 2 + nt:]
        c = lax.axis_index("c")
        chip = _my_chip()
        for t in range(nt):
            for l in range(DEPTH):
                pltpu.make_async_copy(ins[t][l].at[chip], outs[t].at[chip, l], lsem.at[t]).start()
            for j in range(1, N_CHIPS):
                other = jnp.bitwise_xor(chip, j)
                for l in range(DEPTH):
                    _remote(ins[t][l].at[other], outs[t].at[chip, l], ssem.at[t, j - 1], rsem.at[t, j - 1],
                            (*_other_chip(j), c)).start()
        _two_level_gather([small_ref], [lambda idx: small_out.at[idx]], g_ssem, g_rsem, g_lsem)
        for t in range(nt):
            pltpu.make_async_copy(outs[t].at[chip], outs[t].at[chip], lsem.at[t]).wait()
            for j in range(1, N_CHIPS):
                other = jnp.bitwise_xor(chip, j)
                _remote(outs[t].at[other], outs[t].at[other], ssem.at[t, j - 1], rsem.at[t, j - 1],
                        (*_other_chip(j), c)).wait()

    out_shape = []
    for t in range(nt):
        _, r, c = sums[t][0].shape
        out_shape.append(jax.ShapeDtypeStruct((N_CHIPS, DEPTH, r, c), BF16))
    out_shape.append(jax.ShapeDtypeStruct((N_DEV,) + small.shape, F32))
    flat = [s for per_type in sums for s in per_type]
    return pl.pallas_call(
        body,
        in_specs=[pl.BlockSpec(memory_space=pltpu.HBM)] * (len(flat) + 1),
        out_specs=[pl.BlockSpec(memory_space=pltpu.HBM)] * (nt + 1),
        out_shape=out_shape,
        scratch_shapes=[pltpu.SemaphoreType.DMA((nt, N_CHIPS - 1)), pltpu.SemaphoreType.DMA((nt, N_CHIPS - 1)),
                        pltpu.SemaphoreType.DMA((nt,))] + _gather_sems(1),
        name="chip_sums_over_ici")(*flat, small)


def _adamw_math(w, g, m, v):
    m = ADAM_B1 * m + (1.0 - ADAM_B1) * g
    v = ADAM_B2 * v + (1.0 - ADAM_B2) * (g * g)
    m_hat = m / (1.0 - ADAM_B1 ** ADAM_STEP)
    v_hat = v / (1.0 - ADAM_B2 ** ADAM_STEP)
    delta = -ADAM_LR * (m_hat / (jnp.sqrt(v_hat) + ADAM_EPS) + ADAM_WD * w)
    return delta, m, v


def _adamw_big(recv, w, m, v, tk, name):
    _, k, n = w.shape
    npad = recv.shape[3]

    def body(r_ref, w_ref, m_ref, v_ref, g_out, d_out, m_out, v_out):
        g = r_ref[0, 0].astype(F32)
        for s in range(1, N_CHIPS):
            g = g + r_ref[s, 0].astype(F32)
        g = g[:, :n]
        delta, mn, vn = _adamw_math(w_ref[0], g, m_ref[0], v_ref[0])
        g_out[0] = g
        d_out[0] = delta
        m_out[0] = mn
        v_out[0] = vn

    blk = pl.BlockSpec((1, tk, n), lambda l, i: (l, i, 0))
    sds = jax.ShapeDtypeStruct(w.shape, F32)
    return pl.pallas_call(
        body, grid=(DEPTH, k // tk),
        in_specs=[pl.BlockSpec((N_CHIPS, 1, tk, npad), lambda l, i: (0, l, i, 0)), blk, blk, blk],
        out_specs=[blk, blk, blk, blk], out_shape=[sds, sds, sds, sds],
        compiler_params=_cp(("parallel", "parallel"), 48), name=name)(recv, w, m, v)


def _adamw_small(recv, w, m, v):
    rows = w.shape[0]
    tr = rows // SMALL_TILES

    def body(r_ref, w_ref, m_ref, v_ref, g_out, d_out, m_out, v_out):
        g = r_ref[0]
        for s in range(1, N_DEV):
            g = g + r_ref[s]
        delta, mn, vn = _adamw_math(w_ref[...], g, m_ref[...], v_ref[...])
        g_out[...] = g
        d_out[...] = delta
        m_out[...] = mn
        v_out[...] = vn

    blk = pl.BlockSpec((tr, LANES), lambda i: (i, 0))
    sds = jax.ShapeDtypeStruct(w.shape, F32)
    return pl.pallas_call(
        body, grid=(SMALL_TILES,),
        in_specs=[pl.BlockSpec((N_DEV, tr, LANES), lambda i: (0, i, 0)), blk, blk, blk],
        out_specs=[blk, blk, blk, blk], out_shape=[sds, sds, sds, sds],
        compiler_params=_cp(("parallel",), 40), name="adamw_small")(recv, w, m, v)


def _rms(t):
    return lax.rsqrt(jnp.mean(t * t, axis=-1, keepdims=True) + EPS)


def _rms_bwd(t, r, gain, dh, dres):
    u = dh * gain
    dt = dres + r * u - t * ((r * r * r) * (1.0 / D_MODEL) * jnp.sum(t * u, axis=-1, keepdims=True))
    return dt, dh * t * r


def _in_proj(x, gain, w):
    L = x.shape[0]
    n = w.shape[1]
    tm, tn = 512, 1024

    def body(x_ref, g_ref, w_ref, z_ref, h_ref):
        @pl.when(pl.program_id(1) == 0)
        def _():
            t = x_ref[...]
            h_ref[...] = (t * _rms(t) * g_ref[...]).astype(BF16)
        z_ref[...] = _nn(h_ref[...], w_ref[...])

    return pl.pallas_call(
        body, grid=(L // tm, n // tn),
        in_specs=[pl.BlockSpec((tm, D_MODEL), lambda i, j: (i, 0)), pl.BlockSpec((1, D_MODEL), lambda i, j: (0, 0)),
                  pl.BlockSpec((D_MODEL, tn), lambda i, j: (0, j))],
        out_specs=[pl.BlockSpec((tm, tn), lambda i, j: (i, j)), pl.BlockSpec((tm, D_MODEL), lambda i, j: (i, 0))],
        out_shape=[jax.ShapeDtypeStruct((L, n), F32), jax.ShapeDtypeStruct((L, D_MODEL), BF16)],
        compiler_params=_cp(("parallel", "arbitrary"), 40), name="in_proj")(x, gain, w)


def _in_proj_bwd(pieces, w, x, gain, dres):
    L = x.shape[0]
    tm = 512
    nk = len(pieces)

    def body(*refs):
        dz_refs = refs[:nk]
        w_ref, x_ref, g_ref, dr_ref, dx_ref, dg_ref, acc = refs[nk:]
        i, k = pl.program_id(0), pl.program_id(1)

        @pl.when(k == 0)
        def _():
            acc[...] = jnp.zeros_like(acc)

        for kk in range(nk):
            @pl.when(k == kk)
            def _(kk=kk):
                acc[...] += _nt(dz_refs[kk][...], w_ref[...])

        @pl.when(k == nk - 1)
        def _():
            t = x_ref[...]
            dt, dgt = _rms_bwd(t, _rms(t), g_ref[...], acc[...], dr_ref[...])
            dx_ref[...] = dt

            @pl.when(i == 0)
            def _():
                dg_ref[...] = jnp.zeros_like(dg_ref)
            dg_ref[...] += jnp.sum(dgt, axis=0, keepdims=True)

    row = pl.BlockSpec((tm, D_MODEL), lambda i, k: (i, 0))
    vec = pl.BlockSpec((1, D_MODEL), lambda i, k: (0, 0))
    return pl.pallas_call(
        body, grid=(L // tm, nk),
        in_specs=[row] * nk + [pl.BlockSpec((D_MODEL, D_MODEL), lambda i, k: (0, k)), row, vec, row],
        out_specs=[row, vec],
        out_shape=[jax.ShapeDtypeStruct((L, D_MODEL), F32), jax.ShapeDtypeStruct((1, D_MODEL), F32)],
        scratch_shapes=[pltpu.VMEM((tm, D_MODEL), F32)],
        compiler_params=_cp(("arbitrary", "arbitrary"), 56), name="in_proj_bwd")(*pieces, w, x, gain, dres)


def _tn(a, b, name):
    m, na = a.shape
    nb = b.shape[1]
    ta, tb, tm = min(na, 1024), min(nb, 1024), 1024
    nm = m // tm

    def body(a_ref, b_ref, o_ref):
        @pl.when(pl.program_id(2) == 0)
        def _():
            o_ref[...] = jnp.zeros_like(o_ref)
        o_ref[...] += _tn_dot(a_ref[...].astype(BF16), b_ref[...].astype(BF16))

    return pl.pallas_call(
        body, grid=(na // ta, nb // tb, nm),
        in_specs=[pl.BlockSpec((tm, ta), lambda i, j, k: (k, i)), pl.BlockSpec((tm, tb), lambda i, j, k: (k, j))],
        out_specs=pl.BlockSpec((ta, tb), lambda i, j, k: (i, j)),
        out_shape=jax.ShapeDtypeStruct((na, nb), F32),
        compiler_params=_cp(("parallel", "parallel", "arbitrary"), 48), name=name)(a, b)


def _tn_pieces(a, pieces, name):
    m, na = a.shape
    npc = len(pieces)
    tb, tm = D_MODEL, 1024
    nm = m // tm

    def body(*refs):
        a_ref = refs[0]
        b_refs = refs[1:1 + npc]
        o_ref = refs[1 + npc]
        j = pl.program_id(0)

        @pl.when(pl.program_id(1) == 0)
        def _():
            o_ref[...] = jnp.zeros_like(o_ref)

        for jj in range(npc):
            @pl.when(j == jj)
            def _(jj=jj):
                o_ref[...] += _tn_dot(a_ref[...], b_refs[jj][...])

    piece_specs = [pl.BlockSpec((tm, tb), functools.partial(lambda j, k, jj: (jnp.where(j == jj, k, 0), 0), jj=jj))
                   for jj in range(npc)]
    return pl.pallas_call(
        body, grid=(npc, nm),
        in_specs=[pl.BlockSpec((tm, na), lambda j, k: (k, 0))] + piece_specs,
        out_specs=pl.BlockSpec((na, tb), lambda j, k: (0, j)),
        out_shape=jax.ShapeDtypeStruct((na, tb * npc), F32),
        compiler_params=_cp(("parallel", "arbitrary"), 56), name=name)(a, *pieces)


def _loss_grad(xf, target):
    L = xf.shape[0]
    tm = 1024

    def body(x_ref, t_ref, dy_ref, l_ref):
        e = x_ref[...] - t_ref[...]
        dy_ref[...] = e * (1.0 / D_MODEL)

        @pl.when(pl.program_id(0) == 0)
        def _():
            l_ref[...] = jnp.zeros_like(l_ref)
        l_ref[...] += jnp.sum(jnp.sum(e * e, axis=1, keepdims=True), axis=0, keepdims=True) * (0.5 / D_MODEL)

    row = pl.BlockSpec((tm, D_MODEL), lambda i: (i, 0))
    return pl.pallas_call(
        body, grid=(L // tm,), in_specs=[row, row],
        out_specs=[row, pl.BlockSpec((1, 1), lambda i: (0, 0))],
        out_shape=[jax.ShapeDtypeStruct((L, D_MODEL), F32), jax.ShapeDtypeStruct((1, 1), F32)],
        compiler_params=_cp(("arbitrary",), 40), name="loss_grad")(xf, target)


def _ffn_fwd(x, gain, wg, wu, wd):
    L = x.shape[0]
    ff = wg.shape[1]
    tm, tf = 512, 512
    nf = ff // tf

    def body(x_ref, g_ref, wg_ref, wu_ref, wd_ref, o_ref, h_scr, acc):
        c = pl.program_id(1)

        @pl.when(c == 0)
        def _():
            t = x_ref[...]
            h_scr[...] = (t * _rms(t) * g_ref[...]).astype(BF16)
            acc[...] = jnp.zeros_like(acc)

        h = h_scr[...]
        gate = _nn(h, wg_ref[...])
        up = _nn(h, wu_ref[...])
        hid = gate * _sigmoid(gate) * up
        acc[...] += _nn(hid.astype(BF16), wd_ref[...])

        @pl.when(c == nf - 1)
        def _():
            o_ref[...] = x_ref[...] + acc[...]

    row = pl.BlockSpec((tm, D_MODEL), lambda i, c: (i, 0))
    return pl.pallas_call(
        body, grid=(L // tm, nf),
        in_specs=[row, pl.BlockSpec((1, D_MODEL), lambda i, c: (0, 0)),
                  pl.BlockSpec((D_MODEL, tf), lambda i, c: (0, c)), pl.BlockSpec((D_MODEL, tf), lambda i, c: (0, c)),
                  pl.BlockSpec((tf, D_MODEL), lambda i, c: (c, 0))],
        out_specs=row, out_shape=jax.ShapeDtypeStruct((L, D_MODEL), F32),
        scratch_shapes=[pltpu.VMEM((tm, D_MODEL), BF16), pltpu.VMEM((tm, D_MODEL), F32)],
        compiler_params=_cp(("parallel", "arbitrary"), 40), name="ffn_fwd")(x, gain, wg, wu, wd)


def _ffn_bwd(x, gain, wg, wu, wd, dxo):
    L = x.shape[0]
    ff = wg.shape[1]
    tm, tf = 512, 512
    nf = ff // tf

    def body(x_ref, g_ref, wg_ref, wu_ref, wd_ref, dxo_ref, dx_ref, h_ref, hid_ref, dgate_ref, dup_ref, dg_ref,
             acc, dxo_b):
        i, c = pl.program_id(0), pl.program_id(1)

        @pl.when(c == 0)
        def _():
            t = x_ref[...]
            h_ref[...] = (t * _rms(t) * g_ref[...]).astype(BF16)
            acc[...] = jnp.zeros_like(acc)
            dxo_b[...] = dxo_ref[...].astype(BF16)

        h = h_ref[...]
        gate = _nn(h, wg_ref[...])
        up = _nn(h, wu_ref[...])
        sg = _sigmoid(gate)
        silu = gate * sg
        hid_ref[...] = (silu * up).astype(BF16)
        dhid = _nt(dxo_b[...], wd_ref[...])
        dup = (dhid * silu).astype(BF16)
        dgate = (dhid * up * (sg * (1.0 + gate * (1.0 - sg)))).astype(BF16)
        dup_ref[...] = dup
        dgate_ref[...] = dgate
        acc[...] += _nt(dgate, wg_ref[...]) + _nt(dup, wu_ref[...])

        @pl.when(c == nf - 1)
        def _():
            t = x_ref[...]
            dt, dgt = _rms_bwd(t, _rms(t), g_ref[...], acc[...], dxo_ref[...])
            dx_ref[...] = dt

            @pl.when(i == 0)
            def _():
                dg_ref[...] = jnp.zeros_like(dg_ref)
            dg_ref[...] += jnp.sum(dgt, axis=0, keepdims=True)

    row = pl.BlockSpec((tm, D_MODEL), lambda i, c: (i, 0))
    vec = pl.BlockSpec((1, D_MODEL), lambda i, c: (0, 0))
    wcol = pl.BlockSpec((D_MODEL, tf), lambda i, c: (0, c))
    hcol = pl.BlockSpec((tm, tf), lambda i, c: (i, c))
    return pl.pallas_call(
        body, grid=(L // tm, nf),
        in_specs=[row, vec, wcol, wcol, pl.BlockSpec((tf, D_MODEL), lambda i, c: (c, 0)), row],
        out_specs=[row, row, hcol, hcol, hcol, vec],
        out_shape=[jax.ShapeDtypeStruct((L, D_MODEL), F32), jax.ShapeDtypeStruct((L, D_MODEL), BF16),
                   jax.ShapeDtypeStruct((L, ff), BF16), jax.ShapeDtypeStruct((L, ff), BF16),
                   jax.ShapeDtypeStruct((L, ff), BF16), jax.ShapeDtypeStruct((1, D_MODEL), F32)],
        scratch_shapes=[pltpu.VMEM((tm, D_MODEL), F32), pltpu.VMEM((tm, D_MODEL), BF16)],
        compiler_params=_cp(("arbitrary", "arbitrary"), 48), name="ffn_bwd")(x, gain, wg, wu, wd, dxo)


GELU_K = math.sqrt(2.0 / math.pi)
GELU_C = 0.044715


def _gelu(y):
    return 0.5 * y * (1.0 + jnp.tanh(GELU_K * (y + GELU_C * (y * y * y))))


def _gelu_grad(y):
    th = jnp.tanh(GELU_K * (y + GELU_C * (y * y * y)))
    return 0.5 * (1.0 + th) + 0.5 * y * (1.0 - th * th) * (GELU_K * (1.0 + 3.0 * GELU_C * (y * y)))


def _merge_groups(o_refs, l_refs):
    ls = [r[...] for r in l_refs]
    os_ = [r[...] for r in o_refs]
    lmax = jnp.maximum(jnp.maximum(ls[0], ls[1]), ls[2])
    es = [jnp.exp(l - lmax) for l in ls]
    inv = 1.0 / (es[0] + es[1] + es[2])
    ws = [e * inv for e in es]
    a = ws[0] * os_[0] + ws[1] * os_[1] + ws[2] * os_[2]
    return ws, os_, a


def _mix_fwd(ols, y, z, x, wp, wa, wb, wo):
    L = x.shape[0]
    tm = 256

    def body(o0, l0, o1, l1, o2, l2, y_ref, ga_ref, gs_ref, x_ref, wp_ref, wa_ref, wb_ref, wo_ref, out_ref):
        _, _, a = _merge_groups((o0, o1, o2), (l0, l1, l2))
        a_out = _nn(a.astype(BF16), wp_ref[...])
        yg = _gelu(y_ref[...]).astype(BF16)
        s_out = _nn(yg, wa_ref[...]) * _sigmoid(_nn(yg, wb_ref[...]))
        mix = _sigmoid(ga_ref[...]) * a_out + _sigmoid(gs_ref[...]) * s_out
        out_ref[...] = x_ref[...] + _nn(mix.astype(BF16), wo_ref[...])

    half = pl.BlockSpec((tm, ATTN_W), lambda i: (i, 0))
    row = pl.BlockSpec((tm, D_MODEL), lambda i: (i, 0))
    w512 = pl.BlockSpec((ATTN_W, D_MODEL), lambda i: (0, 0))
    return pl.pallas_call(
        body, grid=(L // tm,),
        in_specs=[half] * 7 + [pl.BlockSpec((tm, D_MODEL), lambda i: (i, 5)),
                               pl.BlockSpec((tm, D_MODEL), lambda i: (i, 6)), row, w512, w512, w512,
                               pl.BlockSpec((D_MODEL, D_MODEL), lambda i: (0, 0))],
        out_specs=row, out_shape=jax.ShapeDtypeStruct((L, D_MODEL), F32),
        compiler_params=_cp(("parallel",), 48), name="mix_fwd")(*ols, y, z, z, x, wp, wa, wb, wo)


def _mix_bwd(dxm, ols, y, z, wp, wa, wb, wo):
    L = dxm.shape[0]
    tm = 256

    def body(dx_ref, o0, l0, o1, l1, o2, l2, y_ref, ga_ref, gs_ref, wp_ref, wa_ref, wb_ref, wo_ref,
             do0, dl0, do1, dl1, do2, dl2, dy_ref, dga_ref, dgs_ref, a_ref, yg_ref, mix_ref, dao_ref, dpa_ref,
             dpb_ref):
        ws, os_, a = _merge_groups((o0, o1, o2), (l0, l1, l2))
        ab = a.astype(BF16)
        a_out = _nn(ab, wp_ref[...])
        yv = y_ref[...]
        yg = _gelu(yv).astype(BF16)
        pa = _nn(yg, wa_ref[...])
        spb = _sigmoid(_nn(yg, wb_ref[...]))
        s_out = pa * spb
        sga = _sigmoid(ga_ref[...])
        sgs = _sigmoid(gs_ref[...])
        mix = sga * a_out + sgs * s_out
        dmix = _nt(dx_ref[...].astype(BF16), wo_ref[...])
        da_out = (sga * dmix).astype(BF16)
        ds_out = sgs * dmix
        dpa = (ds_out * spb).astype(BF16)
        dpb = (ds_out * pa * spb * (1.0 - spb)).astype(BF16)
        dga_ref[...] = (dmix * a_out * sga * (1.0 - sga)).astype(BF16)
        dgs_ref[...] = (dmix * s_out * sgs * (1.0 - sgs)).astype(BF16)
        dy_ref[...] = (_nt(dpa, wa_ref[...]) + _nt(dpb, wb_ref[...])) * _gelu_grad(yv)
        da = _nt(da_out, wp_ref[...])
        for w, o, do_ref, dl_ref in zip(ws, os_, (do0, do1, do2), (dl0, dl1, dl2)):
            do_ref[...] = w * da
            dl_ref[...] = da * w * (o - a)
        a_ref[...] = ab
        yg_ref[...] = yg
        mix_ref[...] = mix.astype(BF16)
        dao_ref[...] = da_out
        dpa_ref[...] = dpa
        dpb_ref[...] = dpb

    half = pl.BlockSpec((tm, ATTN_W), lambda i: (i, 0))
    row = pl.BlockSpec((tm, D_MODEL), lambda i: (i, 0))
    w512 = pl.BlockSpec((ATTN_W, D_MODEL), lambda i: (0, 0))
    hf = jax.ShapeDtypeStruct((L, ATTN_W), F32)
    hb = jax.ShapeDtypeStruct((L, ATTN_W), BF16)
    rb = jax.ShapeDtypeStruct((L, D_MODEL), BF16)
    return pl.pallas_call(
        body, grid=(L // tm,),
        in_specs=[row] + [half] * 7 + [pl.BlockSpec((tm, D_MODEL), lambda i: (i, 5)),
                                       pl.BlockSpec((tm, D_MODEL), lambda i: (i, 6)), w512, w512, w512,
                                       pl.BlockSpec((D_MODEL, D_MODEL), lambda i: (0, 0))],
        out_specs=[half] * 7 + [row, row, half, half, row, row, row, row],
        out_shape=[hf] * 7 + [rb, rb, hb, hb, rb, rb, rb, rb],
        compiler_params=_cp(("parallel",), 56), name="mix_bwd")(dxm, *ols, y, z, z, wp, wa, wb, wo)


N_ATTN_ITERS = ATTN_ROWS // BLK


def _class_rows(ref, start, d):
    if d == 1:
        return ref[pl.ds(pl.multiple_of(start, BLK), BLK), :]
    return ref[pl.ds(start, BLK, stride=d), :]


def _set_class_rows(ref, start, d, val):
    if d == 1:
        ref[pl.ds(pl.multiple_of(start, BLK), BLK), :] = val
    else:
        ref[pl.ds(start, BLK, stride=d), :] = val


def _head_masks():
    lane = lax.broadcasted_iota(jnp.int32, (1, LANES), 1)
    m0 = (lane < HEAD_DIM).astype(F32)
    return m0, 1.0 - m0


def _head_norm(t, gain2, m0, m1):
    tt = t * t
    r0 = lax.rsqrt(jnp.sum(tt * m0, axis=-1, keepdims=True) * (1.0 / HEAD_DIM) + EPS)
    r1 = lax.rsqrt(jnp.sum(tt * m1, axis=-1, keepdims=True) * (1.0 / HEAD_DIM) + EPS)
    r = m0 * r0 + m1 * r1
    return t * r * gain2, r


def _head_norm_bwd(t, r, gain2, dy, m0, m1):
    u = dy * gain2
    tu = t * u
    s = m0 * jnp.sum(tu * m0, axis=-1, keepdims=True) + m1 * jnp.sum(tu * m1, axis=-1, keepdims=True)
    return r * u - t * (r * r * r) * s * (1.0 / HEAD_DIM), jnp.sum(dy * t * r, axis=0, keepdims=True)


def _band_masks():
    qi = lax.broadcasted_iota(jnp.int32, (BLK, 2 * BLK), 0)
    ki = lax.broadcasted_iota(jnp.int32, (BLK, 2 * BLK), 1)
    dist = BLK + qi - ki
    return (dist >= 0) & (dist <= BLK), ki >= BLK


def _attn_probs(qm, kw, ok):
    s = _nt(qm, kw) * (HEAD_DIM ** -0.5)
    s = jnp.where(ok, s, -1e30)
    mx = jnp.max(s, axis=-1, keepdims=True)
    p = jnp.exp(s - mx)
    den = jnp.sum(p, axis=-1, keepdims=True)
    return p, den, mx


def _attn_operands(it, d, first_step, q_ref, kc_ref, kp_ref, vc_ref, vp_ref, band, is_cur):
    j = it // d
    start = (it - j * d) + (d * BLK) * j
    before = jnp.maximum(start - d * BLK, 0)
    inside = j > 0
    q2 = _class_rows(q_ref, start, d)
    kc2 = _class_rows(kc_ref, start, d)
    vc2 = _class_rows(vc_ref, start, d)
    kp2 = jnp.where(inside, _class_rows(kc_ref, before, d), _class_rows(kp_ref, it - j * d, d))
    vp2 = jnp.where(inside, _class_rows(vc_ref, before, d), _class_rows(vp_ref, it - j * d, d))
    has_prev = inside | jnp.logical_not(first_step)
    return start, q2, kp2, kc2, vp2, vc2, band & (is_cur | has_prev)


def _attn_specs(d, step_of):
    nq = N_ATTN_ITERS // d

    def cur(c):
        return pl.BlockSpec((ATTN_ROWS, LANES), lambda hp, n: (step_of(n), c + hp))

    def prev(c):
        return pl.BlockSpec((d * BLK, LANES), lambda hp, n: (jnp.maximum(step_of(n) * nq - 1, 0), c + hp))

    return cur, prev, pl.BlockSpec((1, LANES), lambda hp, n: (0, 0))


def _attn_fwd(z, gq2, gk2, group):
    L = z.shape[0]
    d = DILATIONS[group]
    nsb = L // ATTN_ROWS
    cq, ck, cv = group * 4, 12 + group * 4, 24 + group * 4

    def body(q_ref, kc_ref, kp_ref, vc_ref, vp_ref, gq_ref, gk_ref, o_ref, l_ref):
        first_step = pl.program_id(1) == 0
        band, is_cur = _band_masks()
        m0, m1 = _head_masks()
        gq, gk = gq_ref[...], gk_ref[...]

        def per_block(it, carry):
            start, q2, kp2, kc2, vp2, vc2, ok = _attn_operands(
                it, d, first_step, q_ref, kc_ref, kp_ref, vc_ref, vp_ref, band, is_cur)
            qn, _ = _head_norm(q2, gq, m0, m1)
            kpn, _ = _head_norm(kp2, gk, m0, m1)
            kcn, _ = _head_norm(kc2, gk, m0, m1)
            kw = jnp.concatenate([kpn, kcn], axis=0).astype(BF16)
            vw = jnp.concatenate([vp2, vc2], axis=0).astype(BF16)
            o2 = jnp.zeros((BLK, LANES), F32)
            l2 = jnp.zeros((BLK, LANES), F32)
            for mh in (m0, m1):
                p, den, mx = _attn_probs((qn * mh).astype(BF16), kw, ok)
                o2 = o2 + mh * (_nn(p.astype(BF16), vw) / den)
                l2 = l2 + mh * (mx + jnp.log(den))
            _set_class_rows(o_ref, start, d, o2)
            _set_class_rows(l_ref, start, d, l2)
            return carry

        lax.fori_loop(0, N_ATTN_ITERS, per_block, 0, unroll=2)

    cur, prev, vec = _attn_specs(d, lambda n: n)
    out = pl.BlockSpec((ATTN_ROWS, LANES), lambda hp, n: (n, hp))
    sds = jax.ShapeDtypeStruct((L, ATTN_W), F32)
    return pl.pallas_call(
        body, grid=(4, nsb),
        in_specs=[cur(cq), cur(ck), prev(ck), cur(cv), prev(cv), vec, vec],
        out_specs=[out, out], out_shape=[sds, sds],
        compiler_params=_cp(("parallel", "arbitrary"), 48), name=f"attn_fwd_g{group}")(z, z, z, z, z, gq2, gk2)


def _attn_bwd(z, gq2, gk2, o, do, dl, group):
    L = z.shape[0]
    d = DILATIONS[group]
    nsb = L // ATTN_ROWS
    cq, ck, cv = group * 4, 12 + group * 4, 24 + group * 4

    def body(q_ref, kc_ref, kp_ref, vc_ref, vp_ref, gq_ref, gk_ref, o_ref, do_ref, dl_ref,
             dq_ref, dk_ref, dv_ref, dgq_ref, dgk_ref, ck_scr, cv_scr):
        hp, n = pl.program_id(0), pl.program_id(1)
        first_step = n == nsb - 1
        band, is_cur = _band_masks()
        m0, m1 = _head_masks()
        gq, gk = gq_ref[...], gk_ref[...]

        @pl.when((hp == 0) & (n == 0))
        def _():
            dgq_ref[...] = jnp.zeros_like(dgq_ref)
            dgk_ref[...] = jnp.zeros_like(dgk_ref)

        @pl.when(n == 0)
        def _():
            ck_scr[...] = jnp.zeros_like(ck_scr)
            cv_scr[...] = jnp.zeros_like(cv_scr)

        def per_block(i, carry):
            dgq_acc, dgk_acc = carry
            it = N_ATTN_ITERS - 1 - i
            start, q2, kp2, kc2, vp2, vc2, ok = _attn_operands(
                it, d, first_step, q_ref, kc_ref, kp_ref, vc_ref, vp_ref, band, is_cur)
            r = it - (it // d) * d
            qn, rq = _head_norm(q2, gq, m0, m1)
            kpn, rkp = _head_norm(kp2, gk, m0, m1)
            kcn, rkc = _head_norm(kc2, gk, m0, m1)
            kw = jnp.concatenate([kpn, kcn], axis=0).astype(BF16)
            vw = jnp.concatenate([vp2, vc2], axis=0).astype(BF16)
            o2 = _class_rows(o_ref, start, d)
            do2 = _class_rows(do_ref, start, d)
            dl2 = _class_rows(dl_ref, start, d)
            dqn = jnp.zeros((BLK, LANES), F32)
            dkw = jnp.zeros((2 * BLK, LANES), F32)
            dvw = jnp.zeros((2 * BLK, LANES), F32)
            for mh in (m0, m1):
                qm = (qn * mh).astype(BF16)
                p, den, _ = _attn_probs(qm, kw, ok)
                pn = p / den
                doh = do2 * mh
                dohb = doh.astype(BF16)
                dvw = dvw + _tn_dot(pn.astype(BF16), dohb)
                dp = _nt(dohb, vw)
                delta = jnp.sum(doh * o2, axis=-1, keepdims=True)
                dlse = jnp.sum(dl2 * mh, axis=-1, keepdims=True)
                ds = (pn * (dp - delta + dlse) * (HEAD_DIM ** -0.5)).astype(BF16)
                dqn = dqn + mh * _nn(ds, kw)
                dkw = dkw + _tn_dot(ds, qm)
            dq2, gq_t = _head_norm_bwd(q2, rq, gq, dqn, m0, m1)
            dkp2, gk_p = _head_norm_bwd(kp2, rkp, gk, dkw[:BLK], m0, m1)
            dkc2, gk_c = _head_norm_bwd(kc2, rkc, gk, dkw[BLK:], m0, m1)
            _set_class_rows(dq_ref, start, d, dq2)
            _set_class_rows(dk_ref, start, d, ck_scr[r] + dkc2)
            _set_class_rows(dv_ref, start, d, cv_scr[r] + dvw[BLK:])
            ck_scr[r] = dkp2
            cv_scr[r] = dvw[:BLK]
            return dgq_acc + gq_t, dgk_acc + gk_p + gk_c

        zero = jnp.zeros((1, LANES), F32)
        dgq_t, dgk_t = lax.fori_loop(0, N_ATTN_ITERS, per_block, (zero, zero))
        dgq_ref[...] += dgq_t
        dgk_ref[...] += dgk_t

    cur, prev, vec = _attn_specs(d, lambda n: nsb - 1 - n)
    sds = jax.ShapeDtypeStruct((L, ATTN_W), F32)
    vsd = jax.ShapeDtypeStruct((1, LANES), F32)
    return pl.pallas_call(
        body, grid=(4, nsb),
        in_specs=[cur(cq), cur(ck), prev(ck), cur(cv), prev(cv), vec, vec, cur(0), cur(0), cur(0)],
        out_specs=[cur(0), cur(0), cur(0), vec, vec], out_shape=[sds, sds, sds, vsd, vsd],
        scratch_shapes=[pltpu.VMEM((d, BLK, LANES), F32), pltpu.VMEM((d, BLK, LANES), F32)],
        compiler_params=_cp(("arbitrary", "arbitrary"), 56),
        name=f"attn_bwd_g{group}")(z, z, z, z, z, gq2, gk2, o, do, dl)


BLOCK_STATES = SSM_STATES // SSM_BLOCKS
BLOCK_CH = SSM_W // SSM_BLOCKS
SLABS_PER_BLOCK = BLOCK_STATES // LANES


def _store_block(bufs, b, val, tm):
    for s in range(SLABS_PER_BLOCK):
        k = SLABS_PER_BLOCK * b + s
        bufs[k % 2][pl.ds(k // 2, tm, stride=8), :] = val[:, s * LANES:(s + 1) * LANES]


def _load_block(bufs, b, tm):
    tiles = []
    for s in range(SLABS_PER_BLOCK):
        k = SLABS_PER_BLOCK * b + s
        tiles.append(bufs[k % 2][pl.ds(k // 2, tm, stride=8), :])
    return jnp.concatenate(tiles, axis=1).astype(BF16)


def _ssm_project_in(ub, bdr_ref, bdi_ref, sr, si, tm):
    for b in range(SSM_BLOCKS):
        ubb = ub[:, b * BLOCK_CH:(b + 1) * BLOCK_CH]
        _store_block(sr, b, _nn(ubb, bdr_ref[b]), tm)
        _store_block(si, b, _nn(ubb, bdi_ref[b]), tm)


def _ssm_scan(a, x0, sr, si, tm):
    ar0, ar1, ai0, ai1 = a

    def step(t, c):
        xr0, xr1, xi0, xi1 = c
        i8 = pl.multiple_of(t * 8, 8)
        nr0 = ar0 * xr0 - ai0 * xi0 + sr[0][pl.ds(i8, 8), :]
        ni0 = ar0 * xi0 + ai0 * xr0 + si[0][pl.ds(i8, 8), :]
        nr1 = ar1 * xr1 - ai1 * xi1 + sr[1][pl.ds(i8, 8), :]
        ni1 = ar1 * xi1 + ai1 * xr1 + si[1][pl.ds(i8, 8), :]
        sr[0][pl.ds(i8, 8), :] = nr0
        si[0][pl.ds(i8, 8), :] = ni0
        sr[1][pl.ds(i8, 8), :] = nr1
        si[1][pl.ds(i8, 8), :] = ni1
        return nr0, nr1, ni0, ni1

    return lax.fori_loop(0, tm, step, x0, unroll=8)


def _load_a(ar_ref, ai_ref):
    return ar_ref[:, :LANES], ar_ref[:, LANES:], ai_ref[:, :LANES], ai_ref[:, LANES:]


def _ssm_fwd(z, ar8, ai8, bdr, bdi, cdr, cdi, dsk):
    L = z.shape[0]
    tm = SSM_TM
    nc = L // tm

    def body(u_ref, ar_ref, ai_ref, bdr_ref, bdi_ref, cdr_ref, cdi_ref, dsk_ref, y_ref, cin_ref,
             sr0, sr1, si0, si1, car):
        sr, si = (sr0, sr1), (si0, si1)

        @pl.when(pl.program_id(0) == 0)
        def _():
            car[...] = jnp.zeros_like(car)

        u = u_ref[...]
        _ssm_project_in(u.astype(BF16), bdr_ref, bdi_ref, sr, si, tm)
        cin_ref[0] = car[...]
        xr0, xr1, xi0, xi1 = _ssm_scan(_load_a(ar_ref, ai_ref), (car[0], car[1], car[2], car[3]), sr, si, tm)
        car[0], car[1], car[2], car[3] = xr0, xr1, xi0, xi1
        for b in range(SSM_BLOCKS):
            cols = slice(b * BLOCK_CH, (b + 1) * BLOCK_CH)
            y_ref[:, cols] = (dsk_ref[:, cols] * u[:, cols] + _nn(_load_block(sr, b, tm), cdr_ref[b])
                              - _nn(_load_block(si, b, tm), cdi_ref[b]))

    def const(shape):
        return pl.BlockSpec(shape, lambda i: (0,) * len(shape))

    state = pltpu.VMEM((tm * 8, LANES), F32)
    wb = const((SSM_BLOCKS, BLOCK_CH, BLOCK_STATES))
    wc = const((SSM_BLOCKS, BLOCK_STATES, BLOCK_CH))
    return pl.pallas_call(
        body, grid=(nc,),
        in_specs=[pl.BlockSpec((tm, SSM_W), lambda i: (i, COL_U)), const((8, 256)), const((8, 256)),
                  wb, wb, wc, wc, const((1, SSM_W))],
        out_specs=[pl.BlockSpec((tm, SSM_W), lambda i: (i, 0)), pl.BlockSpec((1, 4, 8, LANES), lambda i: (i, 0, 0, 0))],
        out_shape=[jax.ShapeDtypeStruct((L, SSM_W), F32), jax.ShapeDtypeStruct((nc, 4, 8, LANES), F32)],
        scratch_shapes=[state, state, state, state, pltpu.VMEM((4, 8, LANES), F32)],
        compiler_params=_cp(("arbitrary",), 48), name="ssm_fwd")(z, ar8, ai8, bdr, bdi, cdr, cdi, dsk)


def _ssm_bwd(z, dy, cin, ar8, ai8, bdr, bdi, cdr, cdi, dsk):
    L = z.shape[0]
    tm = SSM_TM
    nc = L // tm

    def body(u_ref, dy_ref, cin_ref, ar_ref, ai_ref, dsk_ref, bdr_ref, bdi_ref, cdr_ref, cdi_ref,
             du_ref, da_ref, dds_ref, dbdr_ref, dbdi_ref, dcdr_ref, dcdi_ref,
             sr0, sr1, si0, si1, gr0, gr1, gi0, gi1, carg):
        sr, si, gr, gi = (sr0, sr1), (si0, si1), (gr0, gr1), (gi0, gi1)

        @pl.when(pl.program_id(0) == 0)
        def _():
            carg[...] = jnp.zeros_like(carg)
            for ref in (da_ref, dds_ref, dbdr_ref, dbdi_ref, dcdr_ref, dcdi_ref):
                ref[...] = jnp.zeros_like(ref)

        u = u_ref[...]
        ub = u.astype(BF16)
        dyv = dy_ref[...]
        dyb = dyv.astype(BF16)
        a = _load_a(ar_ref, ai_ref)
        ar0, ar1, ai0, ai1 = a
        x_in = (cin_ref[0, 0], cin_ref[0, 1], cin_ref[0, 2], cin_ref[0, 3])
        _ssm_project_in(ub, bdr_ref, bdi_ref, sr, si, tm)
        _ssm_scan(a, x_in, sr, si, tm)
        for b in range(SSM_BLOCKS):
            dyb_b = dyb[:, b * BLOCK_CH:(b + 1) * BLOCK_CH]
            _store_block(gr, b, _nt(dyb_b, cdr_ref[b]), tm)
            _store_block(gi, b, -_nt(dyb_b, cdi_ref[b]), tm)

        def grad_step(t, g_next, x_prev, acc):
            i8 = pl.multiple_of(t * 8, 8)
            nr0, nr1, ni0, ni1 = g_next
            pr0, pr1, pi0, pi1 = x_prev
            d_r0, d_r1, d_i0, d_i1 = acc
            g_r0 = gr[0][pl.ds(i8, 8), :] + ar0 * nr0 + ai0 * ni0
            g_i0 = gi[0][pl.ds(i8, 8), :] + ar0 * ni0 - ai0 * nr0
            g_r1 = gr[1][pl.ds(i8, 8), :] + ar1 * nr1 + ai1 * ni1
            g_i1 = gi[1][pl.ds(i8, 8), :] + ar1 * ni1 - ai1 * nr1
            gr[0][pl.ds(i8, 8), :] = g_r0
            gi[0][pl.ds(i8, 8), :] = g_i0
            gr[1][pl.ds(i8, 8), :] = g_r1
            gi[1][pl.ds(i8, 8), :] = g_i1
            acc = (d_r0 + pr0 * g_r0 + pi0 * g_i0, d_r1 + pr1 * g_r1 + pi1 * g_i1,
                   d_i0 + pr0 * g_i0 - pi0 * g_r0, d_i1 + pr1 * g_i1 - pi1 * g_r1)
            return (g_r0, g_r1, g_i0, g_i1), acc

        def rstep(j, c):
            t = tm - 1 - j
            p8 = pl.multiple_of((t - 1) * 8, 8)
            x_prev = (sr[0][pl.ds(p8, 8), :], sr[1][pl.ds(p8, 8), :], si[0][pl.ds(p8, 8), :], si[1][pl.ds(p8, 8), :])
            return grad_step(t, c[0], x_prev, c[1])

        acc0 = (da_ref[0], da_ref[1], da_ref[2], da_ref[3])
        g_next, acc = lax.fori_loop(0, tm - 1, rstep, ((carg[0], carg[1], carg[2], carg[3]), acc0), unroll=7)
        g_first, acc = grad_step(0, g_next, x_in, acc)
        carg[0], carg[1], carg[2], carg[3] = g_first
        da_ref[0], da_ref[1], da_ref[2], da_ref[3] = acc

        for b in range(SSM_BLOCKS):
            cols = slice(b * BLOCK_CH, (b + 1) * BLOCK_CH)
            grb, gib = _load_block(gr, b, tm), _load_block(gi, b, tm)
            du_ref[:, cols] = dsk_ref[:, cols] * dyv[:, cols] + _nt(grb, bdr_ref[b]) + _nt(gib, bdi_ref[b])
            dbdr_ref[b] += _tn_dot(ub[:, cols], grb)
            dbdi_ref[b] += _tn_dot(ub[:, cols], gib)
            dcdr_ref[b] += _tn_dot(_load_block(sr, b, tm), dyb[:, cols])
            dcdi_ref[b] -= _tn_dot(_load_block(si, b, tm), dyb[:, cols])
        dds_ref[...] += jnp.sum(dyv * u, axis=0, keepdims=True)

    def const(shape):
        return pl.BlockSpec(shape, lambda i: (0,) * len(shape))

    state = pltpu.VMEM((tm * 8, LANES), F32)
    wb = const((SSM_BLOCKS, BLOCK_CH, BLOCK_STATES))
    wc = const((SSM_BLOCKS, BLOCK_STATES, BLOCK_CH))
    return pl.pallas_call(
        body, grid=(nc,),
        in_specs=[pl.BlockSpec((tm, SSM_W), lambda i: (nc - 1 - i, COL_U)),
                  pl.BlockSpec((tm, SSM_W), lambda i: (nc - 1 - i, 0)),
                  pl.BlockSpec((1, 4, 8, LANES), lambda i: (nc - 1 - i, 0, 0, 0)),
                  const((8, 256)), const((8, 256)), const((1, SSM_W)), wb, wb, wc, wc],
        out_specs=[pl.BlockSpec((tm, SSM_W), lambda i: (nc - 1 - i, 0)), const((4, 8, LANES)), const((1, SSM_W)),
                   wb, wb, wc, wc],
        out_shape=[jax.ShapeDtypeStruct((L, SSM_W), F32), jax.ShapeDtypeStruct((4, 8, LANES), F32),
                   jax.ShapeDtypeStruct((1, SSM_W), F32),
                   jax.ShapeDtypeStruct((SSM_BLOCKS, BLOCK_CH, BLOCK_STATES), F32),
                   jax.ShapeDtypeStruct((SSM_BLOCKS, BLOCK_CH, BLOCK_STATES), F32),
                   jax.ShapeDtypeStruct((SSM_BLOCKS, BLOCK_STATES, BLOCK_CH), F32),
                   jax.ShapeDtypeStruct((SSM_BLOCKS, BLOCK_STATES, BLOCK_CH), F32)],
        scratch_shapes=[state] * 8 + [pltpu.VMEM((4, 8, LANES), F32)],
        compiler_params=_cp(("arbitrary",), 56), name="ssm_bwd")(z, dy, cin, ar8, ai8, dsk, bdr, bdi, cdr, cdi)


def _discretise(lam_re, lam_im, log_dt, b_re, b_im):
    dt = jnp.exp(log_dt)[:, None]
    mag = jnp.exp(lam_re * dt)
    ang = lam_im * dt
    abar_re = mag * jnp.cos(ang)
    abar_im = mag * jnp.sin(ang)
    nr = abar_re - 1.0
    ni = abar_im
    den = lam_re * lam_re + lam_im * lam_im
    cr = ((nr * lam_re + ni * lam_im) / den)[..., None]
    ci = ((ni * lam_re - nr * lam_im) / den)[..., None]
    return abar_re, abar_im, cr * b_re - ci * b_im, cr * b_im + ci * b_re


GROUPS_PER_BLOCK = 8


def _block_diag_in(bbar):
    eye = jnp.eye(GROUPS_PER_BLOCK, dtype=F32)
    return jnp.einsum("igpc,gh->igchp", bbar.reshape(SSM_BLOCKS, GROUPS_PER_BLOCK, 64, 16), eye).reshape(
        SSM_BLOCKS, BLOCK_CH, BLOCK_STATES)


def _block_diag_in_t(blocks):
    eye = jnp.eye(GROUPS_PER_BLOCK, dtype=F32)
    return jnp.einsum("igchp,gh->igpc", blocks.reshape(SSM_BLOCKS, GROUPS_PER_BLOCK, 16, GROUPS_PER_BLOCK, 64),
                      eye).reshape(32, 64, 16)


def _block_diag_out(c):
    eye = jnp.eye(GROUPS_PER_BLOCK, dtype=F32)
    return jnp.einsum("igcp,gh->igphc", c.reshape(SSM_BLOCKS, GROUPS_PER_BLOCK, 16, 64), eye).reshape(
        SSM_BLOCKS, BLOCK_STATES, BLOCK_CH)


def _block_diag_out_t(blocks):
    eye = jnp.eye(GROUPS_PER_BLOCK, dtype=F32)
    return jnp.einsum("igphc,gh->igcp", blocks.reshape(SSM_BLOCKS, GROUPS_PER_BLOCK, 64, GROUPS_PER_BLOCK, 16),
                      eye).reshape(32, 16, 64)


SMALL_NAMES = ("g_mix", "g_q", "g_k", "lambda_re", "lambda_im", "log_dt", "b_re", "b_im", "c_re", "c_im",
               "d_skip", "g_ffn")


def _pack_small(parts):
    flat = jnp.concatenate([parts[n].reshape(-1) for n in SMALL_NAMES])
    pad = (-flat.shape[0]) % (8 * LANES * SMALL_TILES)
    return jnp.pad(flat, (0, pad)).reshape(-1, LANES)


def _unpack_small(packed, like):
    flat = packed.reshape(-1)
    out, off = {}, 0
    for n in SMALL_NAMES:
        size = like[n].size
        out[n] = flat[off:off + size].reshape(like[n].shape)
        off += size
    return out


BIG_NAMES = ("w_in", "w_attn_proj", "w_glu_a", "w_glu_b", "w_out", "w_ffn_gate", "w_ffn_up", "w_ffn_down")
BIG_SHARD_AXIS = {"w_in": 2, "w_attn_proj": 2, "w_glu_a": 2, "w_glu_b": 2, "w_out": 1,
                  "w_ffn_gate": 2, "w_ffn_up": 2, "w_ffn_down": 1}
ADAMW_ROWS = {"w_in": 256, "w_attn_proj": 512, "w_glu_a": 512, "w_glu_b": 512, "w_out": 128,
              "w_ffn_gate": 256, "w_ffn_up": 256, "w_ffn_down": 176}


def kernel(x, g_mix, w_in, g_q, g_k, w_attn_proj, lambda_re, lambda_im, log_dt, b_re, b_im, c_re, c_im, d_skip, w_glu_a, w_glu_b, w_out, g_ffn, w_ffn_gate, w_ffn_up, w_ffn_down, loss_target, m_g_mix, m_w_in, m_g_q, m_g_k, m_w_attn_proj, m_lambda_re, m_lambda_im, m_log_dt, m_b_re, m_b_im, m_c_re, m_c_im, m_d_skip, m_w_glu_a, m_w_glu_b, m_w_out, m_g_ffn, m_w_ffn_gate, m_w_ffn_up, m_w_ffn_down, v_g_mix, v_w_in, v_g_q, v_g_k, v_w_attn_proj, v_lambda_re, v_lambda_im, v_log_dt, v_b_re, v_b_im, v_c_re, v_c_im, v_d_skip, v_w_glu_a, v_w_glu_b, v_w_out, v_g_ffn, v_w_ffn_gate, v_w_ffn_up, v_w_ffn_down):
    args = dict(locals())
    weights = {n: args[n] for n in BIG_NAMES + SMALL_NAMES}
    moments_m = {n: args["m_" + n] for n in BIG_NAMES + SMALL_NAMES}
    moments_v = {n: args["v_" + n] for n in BIG_NAMES + SMALL_NAMES}
    x0 = x[0]
    target = loss_target[0]

    shards = []
    for n in BIG_NAMES:
        w = weights[n]
        rows_to, cols_to = w.shape[1], w.shape[2]
        if n in ("w_ffn_gate", "w_ffn_up"):
            cols_to = FF_SHARD_PAD
        if n == "w_ffn_down":
            rows_to = FF_SHARD_PAD
        shards.append(_prep_weight(w, rows_to, cols_to, "prep_" + n))
    full = dict(zip(BIG_NAMES, _all_gather(shards, [BIG_SHARD_AXIS[n] for n in BIG_NAMES])))

    saved = []
    xl = x0
    for l in range(DEPTH):
        abar_re, abar_im, bb_re, bb_im = _discretise(lambda_re[l], lambda_im[l], log_dt[l], b_re[l], b_im[l])
        ssm = dict(ar8=abar_re.reshape(8, 256), ai8=abar_im.reshape(8, 256),
                   bdr=_block_diag_in(bb_re).astype(BF16), bdi=_block_diag_in(bb_im).astype(BF16),
                   cdr=_block_diag_out(c_re[l]).astype(BF16), cdi=_block_diag_out(c_im[l]).astype(BF16),
                   dsk=d_skip[l][None])
        gq2 = jnp.tile(g_q[l], 2)[None]
        gk2 = jnp.tile(g_k[l], 2)[None]
        z, h = _in_proj(xl, g_mix[l][None], full["w_in"][l])
        ols = []
        for g in range(N_GROUPS):
            ols.extend(_attn_fwd(z, gq2, gk2, g))
        y, cin = _ssm_fwd(z, **ssm)
        xm = _mix_fwd(ols, y, z, xl, full["w_attn_proj"][l], full["w_glu_a"][l], full["w_glu_b"][l], full["w_out"][l])
        xo = _ffn_fwd(xm, g_ffn[l][None], full["w_ffn_gate"][l], full["w_ffn_up"][l], full["w_ffn_down"][l])
        saved.append(dict(x=xl, z=z, h=h, ols=ols, y=y, cin=cin, xm=xm, ssm=ssm, gq2=gq2, gk2=gk2))
        xl = xo

    dxo, loss_local = _loss_grad(xl, target)
    loss = lax.psum(loss_local[0, 0], MESH_AXES)
    big_grads = {n: [None] * DEPTH for n in BIG_NAMES}
    small_grads = {n: [None] * DEPTH for n in SMALL_NAMES}
    for l in reversed(range(DEPTH)):
        s = saved[l]
        dxm, h2, hid, dgate, dup, dgffn = _ffn_bwd(s["xm"], g_ffn[l][None], full["w_ffn_gate"][l],
                                                   full["w_ffn_up"][l], full["w_ffn_down"][l], dxo)
        big_grads["w_ffn_down"][l] = _tn(hid, dxo, "grad_w_ffn_down")
        big_grads["w_ffn_gate"][l] = _tn(h2, dgate, "grad_w_ffn_gate")
        big_grads["w_ffn_up"][l] = _tn(h2, dup, "grad_w_ffn_up")
        (do0, dl0, do1, dl1, do2, dl2, dy, dga, dgs, a_b, yg_b, mix_b, dao_b, dpa_b, dpb_b) = _mix_bwd(
            dxm, s["ols"], s["y"], s["z"], full["w_attn_proj"][l], full["w_glu_a"][l], full["w_glu_b"][l],
            full["w_out"][l])
        big_grads["w_out"][l] = _tn(mix_b, dxm, "grad_w_out")
        big_grads["w_attn_proj"][l] = _tn(a_b, dao_b, "grad_w_attn_proj")
        big_grads["w_glu_a"][l] = _tn(yg_b, dpa_b, "grad_w_glu_a")
        big_grads["w_glu_b"][l] = _tn(yg_b, dpb_b, "grad_w_glu_b")
        du, da4, ddsk, dbdr, dbdi, dcdr, dcdi = _ssm_bwd(s["z"], dy, s["cin"], **s["ssm"])
        dqkv = []
        dgq = jnp.zeros((1, LANES), F32)
        dgk = jnp.zeros((1, LANES), F32)
        for g, (do_g, dl_g) in enumerate(((do0, dl0), (do1, dl1), (do2, dl2))):
            dq, dk, dv, dgq_g, dgk_g = _attn_bwd(s["z"], s["gq2"], s["gk2"], s["ols"][2 * g], do_g, dl_g, g)
            dqkv.append((dq, dk, dv))
            dgq, dgk = dgq + dgq_g, dgk + dgk_g
        def cat(a, b):
            return jnp.concatenate([a, b], axis=1).astype(BF16)

        pieces = [cat(dqkv[0][0], dqkv[1][0]), cat(dqkv[2][0], dqkv[0][1]), cat(dqkv[1][1], dqkv[2][1]),
                  cat(dqkv[0][2], dqkv[1][2]), cat(dqkv[2][2], du), dga, dgs]
        dxo, dgmix = _in_proj_bwd(pieces, full["w_in"][l], s["x"], g_mix[l][None], dxm)
        big_grads["w_in"][l] = _tn_pieces(s["h"], pieces, "grad_w_in")
        _, disc_vjp = jax.vjp(_discretise, lambda_re[l], lambda_im[l], log_dt[l], b_re[l], b_im[l])
        dar = jnp.concatenate([da4[0], da4[1]], axis=1).reshape(32, 64)
        dai = jnp.concatenate([da4[2], da4[3]], axis=1).reshape(32, 64)
        dlr, dli, dldt, dbre, dbim = disc_vjp((dar, dai, _block_diag_in_t(dbdr), _block_diag_in_t(dbdi)))
        small_grads["g_mix"][l] = dgmix[0]
        small_grads["g_q"][l] = dgq[0, :HEAD_DIM] + dgq[0, HEAD_DIM:]
        small_grads["g_k"][l] = dgk[0, :HEAD_DIM] + dgk[0, HEAD_DIM:]
        small_grads["lambda_re"][l] = dlr
        small_grads["lambda_im"][l] = dli
        small_grads["log_dt"][l] = dldt
        small_grads["b_re"][l] = dbre
        small_grads["b_im"][l] = dbim
        small_grads["c_re"][l] = _block_diag_out_t(dcdr)
        small_grads["c_im"][l] = _block_diag_out_t(dcdi)
        small_grads["d_skip"][l] = ddsk[0]
        small_grads["g_ffn"][l] = dgffn[0]
    grad_x = dxo[None]

    small_local = {n: jnp.stack(small_grads[n]) for n in SMALL_NAMES}
    rs_axes = [BIG_SHARD_AXIS[n] - 1 for n in BIG_NAMES]
    got = _exchange_with_sibling([big_grads[n] for n in BIG_NAMES], rs_axes)
    core = lax.axis_index("c").astype(jnp.int32).reshape(1)
    sums = [[_chip_sum(big_grads[n][l], got[t], l, rs_axes[t], core, "chip_sum_" + n) for l in range(DEPTH)]
            for t, n in enumerate(BIG_NAMES)]
    recv = _exchange_chip_sums(sums, _pack_small(small_local))
    out_g, out_d, out_m, out_v = {}, {}, {}, {}
    for n, r in zip(BIG_NAMES, recv[:-1]):
        out_g[n], out_d[n], out_m[n], out_v[n] = _adamw_big(r, weights[n], moments_m[n], moments_v[n],
                                                            ADAMW_ROWS[n], "adamw_" + n)
    like = {n: weights[n] for n in SMALL_NAMES}
    packed = _adamw_small(recv[-1], _pack_small(like), _pack_small({n: moments_m[n] for n in SMALL_NAMES}),
                          _pack_small({n: moments_v[n] for n in SMALL_NAMES}))
    for dst, p in zip((out_g, out_d, out_m, out_v), packed):
        dst.update(_unpack_small(p, like))

    order = ("g_mix", "w_in", "g_q", "g_k", "w_attn_proj", "lambda_re", "lambda_im", "log_dt", "b_re", "b_im",
             "c_re", "c_im", "d_skip", "w_glu_a", "w_glu_b", "w_out", "g_ffn", "w_ffn_gate", "w_ffn_up",
             "w_ffn_down")
    return (loss, grad_x, *[out_g[n] for n in order], *[out_d[n] for n in order],
            *[out_m[n] for n in order], *[out_v[n] for n in order])
```

```python
import functools
import math

import jax
import jax.numpy as jnp
from jax import lax
from jax.experimental import pallas as pl
from jax.experimental.pallas import tpu as pltpu

F32 = jnp.float32
BF16 = jnp.bfloat16

D_MODEL = 1024
DEPTH = 4
N_DEV = 8
N_CHIPS = 4
HEAD_DIM = 64
BLK = 128
LANES = 128
ATTN_W = 512
N_GROUPS = 3
DILATIONS = (1, 4, 16)
ATTN_ROWS = 2048
SSM_W = 512
SSM_STATES = 2048
SSM_BLOCKS = 4
IN_COLS = 7168
COL_U = 9
D_FF = 2816
FF_SHARD = D_FF // N_DEV
FF_SHARD_PAD = 384
FF_PAD = FF_SHARD_PAD * N_DEV
EPS = 1e-6
SSM_TM = 512
SMALL_TILES = 4

ADAM_LR = 0.001
ADAM_B1 = 0.9
ADAM_B2 = 0.999
ADAM_EPS = 1e-08
ADAM_WD = 0.01
ADAM_STEP = 10

MESH_AXES = ("x", "y", "c")
MIB = 1024 * 1024


def _cp(sem=None, vmem_mib=None):
    kw = {}
    if sem is not None:
        kw["dimension_semantics"] = sem
    if vmem_mib is not None:
        kw["vmem_limit_bytes"] = vmem_mib * MIB
    return pltpu.CompilerParams(**kw)


def _nt(a, b):
    return lax.dot_general(a, b, (((1,), (1,)), ((), ())), preferred_element_type=F32)


def _tn_dot(a, b):
    return lax.dot_general(a, b, (((0,), (0,)), ((), ())), preferred_element_type=F32)


def _nn(a, b):
    return jnp.dot(a, b, preferred_element_type=F32)


def _sigmoid(t):
    return jax.nn.sigmoid(t)


def _prep_weight(w, rows_to, cols_to, name):
    _, k, n = w.shape

    def body(w_ref, o_ref):
        if rows_to != k or cols_to != n:
            o_ref[...] = jnp.zeros(o_ref.shape, BF16)
        o_ref[0, :k, :n] = w_ref[0].astype(BF16)

    return pl.pallas_call(
        body, grid=(DEPTH,),
        in_specs=[pl.BlockSpec((1, k, n), lambda l: (l, 0, 0))],
        out_specs=pl.BlockSpec((1, rows_to, cols_to), lambda l: (l, 0, 0)),
        out_shape=jax.ShapeDtypeStruct((DEPTH, rows_to, cols_to), BF16),
        compiler_params=_cp(("parallel",), 40), name=name)(w)


def _my_index():
    return 4 * lax.axis_index("x") + 2 * lax.axis_index("y") + lax.axis_index("c")


def _my_chip():
    return 2 * lax.axis_index("x") + lax.axis_index("y")


def _sibling():
    return (lax.axis_index("x"), lax.axis_index("y"), 1 - lax.axis_index("c"))


def _other_chip(j):
    return (jnp.bitwise_xor(lax.axis_index("x"), (j >> 1) & 1), jnp.bitwise_xor(lax.axis_index("y"), j & 1))


def _slab(ref, idx, width, axis):
    start = pl.multiple_of(idx * width, width)
    sl = [slice(None)] * len(ref.shape)
    sl[axis] = pl.ds(start, width)
    return ref.at[tuple(sl)]


def _remote(src, dst, ssem, rsem, device):
    return pltpu.make_async_remote_copy(src_ref=src, dst_ref=dst, send_sem=ssem, recv_sem=rsem,
                                        device_id=device, device_id_type=pl.DeviceIdType.MESH)


def _two_level_gather(srcs, blocks, ssem, rsem, lsem):
    nt = len(srcs)
    x, y, c = lax.axis_index("x"), lax.axis_index("y"), lax.axis_index("c")
    me = _my_index()
    local, sends = [], []
    for t in range(nt):
        mine = blocks[t](me)
        loc = pltpu.make_async_copy(srcs[t], mine, lsem.at[t])
        loc.start()
        local.append(loc)
        first = [_remote(srcs[t], mine, ssem.at[t, 0], rsem.at[t, 0], _sibling())]
        for j in range(1, N_CHIPS):
            first.append(_remote(srcs[t], mine, ssem.at[t, j], rsem.at[t, j], (*_other_chip(j), c)))
        for cp in first:
            cp.start()
        sends.extend(first)
    for t in range(nt):
        for j in range(1, N_CHIPS):
            ox, oy = _other_chip(j)
            landed = blocks[t](4 * ox + 2 * oy + c)
            _remote(landed, landed, ssem.at[t, j], rsem.at[t, j], _sibling()).wait_recv()
            fwd = _remote(landed, landed, ssem.at[t, 3 + j], rsem.at[t, 3 + j], _sibling())
            fwd.start()
            sends.append(fwd)
    for t in range(nt):
        got = blocks[t](4 * x + 2 * y + (1 - c))
        _remote(got, got, ssem.at[t, 0], rsem.at[t, 0], _sibling()).wait_recv()
        for j in range(1, N_CHIPS):
            ox, oy = _other_chip(j)
            got = blocks[t](4 * ox + 2 * oy + (1 - c))
            _remote(got, got, ssem.at[t, 3 + j], rsem.at[t, 3 + j], _sibling()).wait_recv()
    for cp in sends:
        cp.wait_send()
    for cp in local:
        cp.wait()


def _gather_sems(nt):
    return [pltpu.SemaphoreType.DMA((nt, N_DEV - 1)), pltpu.SemaphoreType.DMA((nt, N_DEV - 1)),
            pltpu.SemaphoreType.DMA((nt,))]


def _all_gather(shards, axes):
    nt = len(shards)

    def body(*refs):
        ins, outs = refs[:nt], refs[nt:2 * nt]
        ssem, rsem, lsem = refs[2 * nt:]
        blocks = [functools.partial(_slab, outs[t], width=shards[t].shape[axes[t]], axis=axes[t]) for t in range(nt)]
        _two_level_gather(ins, blocks, ssem, rsem, lsem)

    out_shape = []
    for t in range(nt):
        s = list(shards[t].shape)
        s[axes[t]] *= N_DEV
        out_shape.append(jax.ShapeDtypeStruct(tuple(s), shards[t].dtype))
    return pl.pallas_call(
        body,
        in_specs=[pl.BlockSpec(memory_space=pltpu.HBM)] * nt,
        out_specs=[pl.BlockSpec(memory_space=pltpu.HBM)] * nt,
        out_shape=out_shape, scratch_shapes=_gather_sems(nt),
        name="all_gather_weights")(*shards)


def _exchange_with_sibling(grads, axes):
    nt = len(grads)

    def body(*refs):
        ins = [refs[t * DEPTH:(t + 1) * DEPTH] for t in range(nt)]
        outs = refs[nt * DEPTH: nt * DEPTH + nt]
        ssem, rsem = refs[nt * DEPTH + nt:]
        c = lax.axis_index("c")
        for t in range(nt):
            width = grads[t][0].shape[axes[t]] // N_DEV
            for q in range(N_CHIPS):
                for l in range(DEPTH):
                    _remote(_slab(ins[t][l], 2 * q + (1 - c), width, axes[t]), outs[t].at[q, l],
                            ssem.at[t], rsem.at[t], _sibling()).start()
        for t in range(nt):
            _remote(outs[t], outs[t], ssem.at[t], rsem.at[t], _sibling()).wait()

    out_shape = []
    for t in range(nt):
        s = list(grads[t][0].shape)
        s[axes[t]] //= N_DEV
        out_shape.append(jax.ShapeDtypeStruct((N_CHIPS, DEPTH, s[0], s[1]), F32))
    flat = [g for per_type in grads for g in per_type]
    return pl.pallas_call(
        body,
        in_specs=[pl.BlockSpec(memory_space=pltpu.HBM)] * len(flat),
        out_specs=[pl.BlockSpec(memory_space=pltpu.HBM)] * nt,
        out_shape=out_shape,
        scratch_shapes=[pltpu.SemaphoreType.DMA((nt,)), pltpu.SemaphoreType.DMA((nt,))],
        name="grads_to_sibling")(*flat)


def _chip_sum(grad, got, layer, axis, core, name):
    _, _, r, c = got.shape
    tr = min(r, 512)

    def body(core_ref, g_ref, s_ref, o_ref):
        o_ref[0] = (g_ref[...] + s_ref[0, 0]).astype(BF16)

    if axis == 1:
        g_spec = pl.BlockSpec((tr, c), lambda q, i, core_ref: (i, 2 * q + core_ref[0]))
    else:
        g_spec = pl.BlockSpec((tr, c), lambda q, i, core_ref: ((2 * q + core_ref[0]) * (r // tr) + i, 0))
    return pl.pallas_call(
        body,
        grid_spec=pltpu.PrefetchScalarGridSpec(
            num_scalar_prefetch=1, grid=(N_CHIPS, r // tr),
            in_specs=[g_spec, pl.BlockSpec((1, 1, tr, c), lambda q, i, core_ref: (q, layer, i, 0))],
            out_specs=pl.BlockSpec((1, tr, c), lambda q, i, core_ref: (q, i, 0))),
        out_shape=jax.ShapeDtypeStruct((N_CHIPS, r, c), BF16),
        compiler_params=_cp(("parallel", "parallel"), 40), name=name)(core, grad, got)


def _exchange_chip_sums(sums, small):
    nt = len(sums)

    def body(*refs):
        ins = [refs[t * DEPTH:(t + 1) * DEPTH] for t in range(nt)]
        small_ref = refs[nt * DEPTH]
        outs = refs[nt * DEPTH + 1: nt * DEPTH + 1 + nt]
        small_out = refs[nt * DEPTH + 1 + nt]
        ssem, rsem, lsem, g_ssem, g_rsem, g_lsem = refs[nt * DEPTH + 2 + nt:]
        c = lax.axis_index("c")
        chip = _my_chip()
        for t in range(nt):
            for l in range(DEPTH):
                pltpu.make_async_copy(ins[t][l].at[chip], outs[t].at[chip, l], lsem.at[t]).start()
            for j in range(1, N_CHIPS):
                other = jnp.bitwise_xor(chip, j)
                for l in range(DEPTH):
                    _remote(ins[t][l].at[other], outs[t].at[chip, l], ssem.at[t, j - 1], rsem.at[t, j - 1],
                            (*_other_chip(j), c)).start()
        _two_level_gather([small_ref], [lambda idx: small_out.at[idx]], g_ssem, g_rsem, g_lsem)
        for t in range(nt):
            pltpu.make_async_copy(outs[t].at[chip], outs[t].at[chip], lsem.at[t]).wait()
            for j in range(1, N_CHIPS):
                other = jnp.bitwise_xor(chip, j)
                _remote(outs[t].at[other], outs[t].at[other], ssem.at[t, j - 1], rsem.at[t, j - 1],
                        (*_other_chip(j), c)).wait()

    out_shape = []
    for t in range(nt):
        _, r, c = sums[t][0].shape
        out_shape.append(jax.ShapeDtypeStruct((N_CHIPS, DEPTH, r, c), BF16))
    out_shape.append(jax.ShapeDtypeStruct((N_DEV,) + small.shape, F32))
    flat = [s for per_type in sums for s in per_type]
    return pl.pallas_call(
        body,
        in_specs=[pl.BlockSpec(memory_space=pltpu.HBM)] * (len(flat) + 1),
        out_specs=[pl.BlockSpec(memory_space=pltpu.HBM)] * (nt + 1),
        out_shape=out_shape,
        scratch_shapes=[pltpu.SemaphoreType.DMA((nt, N_CHIPS - 1)), pltpu.SemaphoreType.DMA((nt, N_CHIPS - 1)),
                        pltpu.SemaphoreType.DMA((nt,))] + _gather_sems(1),
        name="chip_sums_over_ici")(*flat, small)


def _adamw_math(w, g, m, v):
    m = ADAM_B1 * m + (1.0 - ADAM_B1) * g
    v = ADAM_B2 * v + (1.0 - ADAM_B2) * (g * g)
    m_hat = m / (1.0 - ADAM_B1 ** ADAM_STEP)
    v_hat = v / (1.0 - ADAM_B2 ** ADAM_STEP)
    delta = -ADAM_LR * (m_hat / (jnp.sqrt(v_hat) + ADAM_EPS) + ADAM_WD * w)
    return delta, m, v


def _adamw_big(recv, w, m, v, tk, name):
    _, k, n = w.shape
    npad = recv.shape[3]

    def body(r_ref, w_ref, m_ref, v_ref, g_out, d_out, m_out, v_out):
        g = r_ref[0, 0].astype(F32)
        for s in range(1, N_CHIPS):
            g = g + r_ref[s, 0].astype(F32)
        g = g[:, :n]
        delta, mn, vn = _adamw_math(w_ref[0], g, m_ref[0], v_ref[0])
        g_out[0] = g
        d_out[0] = delta
        m_out[0] = mn
        v_out[0] = vn

    blk = pl.BlockSpec((1, tk, n), lambda l, i: (l, i, 0))
    sds = jax.ShapeDtypeStruct(w.shape, F32)
    return pl.pallas_call(
        body, grid=(DEPTH, k // tk),
        in_specs=[pl.BlockSpec((N_CHIPS, 1, tk, npad), lambda l, i: (0, l, i, 0)), blk, blk, blk],
        out_specs=[blk, blk, blk, blk], out_shape=[sds, sds, sds, sds],
        compiler_params=_cp(("parallel", "parallel"), 48), name=name)(recv, w, m, v)


def _adamw_small(recv, w, m, v):
    rows = w.shape[0]
    tr = rows // SMALL_TILES

    def body(r_ref, w_ref, m_ref, v_ref, g_out, d_out, m_out, v_out):
        g = r_ref[0]
        for s in range(1, N_DEV):
            g = g + r_ref[s]
        delta, mn, vn = _adamw_math(w_ref[...], g, m_ref[...], v_ref[...])
        g_out[...] = g
        d_out[...] = delta
        m_out[...] = mn
        v_out[...] = vn

    blk = pl.BlockSpec((tr, LANES), lambda i: (i, 0))
    sds = jax.ShapeDtypeStruct(w.shape, F32)
    return pl.pallas_call(
        body, grid=(SMALL_TILES,),
        in_specs=[pl.BlockSpec((N_DEV, tr, LANES), lambda i: (0, i, 0)), blk, blk, blk],
        out_specs=[blk, blk, blk, blk], out_shape=[sds, sds, sds, sds],
        compiler_params=_cp(("parallel",), 40), name="adamw_small")(recv, w, m, v)


def _rms(t):
    return lax.rsqrt(jnp.mean(t * t, axis=-1, keepdims=True) + EPS)


def _rms_bwd(t, r, gain, dh, dres):
    u = dh * gain
    dt = dres + r * u - t * ((r * r * r) * (1.0 / D_MODEL) * jnp.sum(t * u, axis=-1, keepdims=True))
    return dt, dh * t * r


def _in_proj(x, gain, w):
    L = x.shape[0]
    n = w.shape[1]
    tm, tn = 512, 1024

    def body(x_ref, g_ref, w_ref, z_ref, h_ref):
        @pl.when(pl.program_id(1) == 0)
        def _():
            t = x_ref[...]
            h_ref[...] = (t * _rms(t) * g_ref[...]).astype(BF16)
        z_ref[...] = _nn(h_ref[...], w_ref[...])

    return pl.pallas_call(
        body, grid=(L // tm, n // tn),
        in_specs=[pl.BlockSpec((tm, D_MODEL), lambda i, j: (i, 0)), pl.BlockSpec((1, D_MODEL), lambda i, j: (0, 0)),
                  pl.BlockSpec((D_MODEL, tn), lambda i, j: (0, j))],
        out_specs=[pl.BlockSpec((tm, tn), lambda i, j: (i, j)), pl.BlockSpec((tm, D_MODEL), lambda i, j: (i, 0))],
        out_shape=[jax.ShapeDtypeStruct((L, n), F32), jax.ShapeDtypeStruct((L, D_MODEL), BF16)],
        compiler_params=_cp(("parallel", "arbitrary"), 40), name="in_proj")(x, gain, w)


def _in_proj_bwd(pieces, w, x, gain, dres):
    L = x.shape[0]
    tm = 512
    nk = len(pieces)

    def body(*refs):
        dz_refs = refs[:nk]
        w_ref, x_ref, g_ref, dr_ref, dx_ref, dg_ref, acc = refs[nk:]
        i, k = pl.program_id(0), pl.program_id(1)

        @pl.when(k == 0)
        def _():
            acc[...] = jnp.zeros_like(acc)

        for kk in range(nk):
            @pl.when(k == kk)
            def _(kk=kk):
                acc[...] += _nt(dz_refs[kk][...], w_ref[...])

        @pl.when(k == nk - 1)
        def _():
            t = x_ref[...]
            dt, dgt = _rms_bwd(t, _rms(t), g_ref[...], acc[...], dr_ref[...])
            dx_ref[...] = dt

            @pl.when(i == 0)
            def _():
                dg_ref[...] = jnp.zeros_like(dg_ref)
            dg_ref[...] += jnp.sum(dgt, axis=0, keepdims=True)

    row = pl.BlockSpec((tm, D_MODEL), lambda i, k: (i, 0))
    vec = pl.BlockSpec((1, D_MODEL), lambda i, k: (0, 0))
    return pl.pallas_call(
        body, grid=(L // tm, nk),
        in_specs=[row] * nk + [pl.BlockSpec((D_MODEL, D_MODEL), lambda i, k: (0, k)), row, vec, row],
        out_specs=[row, vec],
        out_shape=[jax.ShapeDtypeStruct((L, D_MODEL), F32), jax.ShapeDtypeStruct((1, D_MODEL), F32)],
        scratch_shapes=[pltpu.VMEM((tm, D_MODEL), F32)],
        compiler_params=_cp(("arbitrary", "arbitrary"), 56), name="in_proj_bwd")(*pieces, w, x, gain, dres)


def _tn(a, b, name):
    m, na = a.shape
    nb = b.shape[1]
    ta, tb, tm = min(na, 1024), min(nb, 1024), 1024
    nm = m // tm

    def body(a_ref, b_ref, o_ref):
        @pl.when(pl.program_id(2) == 0)
        def _():
            o_ref[...] = jnp.zeros_like(o_ref)
        o_ref[...] += _tn_dot(a_ref[...].astype(BF16), b_ref[...].astype(BF16))

    return pl.pallas_call(
        body, grid=(na // ta, nb // tb, nm),
        in_specs=[pl.BlockSpec((tm, ta), lambda i, j, k: (k, i)), pl.BlockSpec((tm, tb), lambda i, j, k: (k, j))],
        out_specs=pl.BlockSpec((ta, tb), lambda i, j, k: (i, j)),
        out_shape=jax.ShapeDtypeStruct((na, nb), F32),
        compiler_params=_cp(("parallel", "parallel", "arbitrary"), 48), name=name)(a, b)


def _tn_pieces(a, pieces, name):
    m, na = a.shape
    npc = len(pieces)
    tb, tm = D_MODEL, 1024
    nm = m // tm

    def body(*refs):
        a_ref = refs[0]
        b_refs = refs[1:1 + npc]
        o_ref = refs[1 + npc]
        j = pl.program_id(0)

        @pl.when(pl.program_id(1) == 0)
        def _():
            o_ref[...] = jnp.zeros_like(o_ref)

        for jj in range(npc):
            @pl.when(j == jj)
            def _(jj=jj):
                o_ref[...] += _tn_dot(a_ref[...], b_refs[jj][...])

    piece_specs = [pl.BlockSpec((tm, tb), functools.partial(lambda j, k, jj: (jnp.where(j == jj, k, 0), 0), jj=jj))
                   for jj in range(npc)]
    return pl.pallas_call(
        body, grid=(npc, nm),
        in_specs=[pl.BlockSpec((tm, na), lambda j, k: (k, 0))] + piece_specs,
        out_specs=pl.BlockSpec((na, tb), lambda j, k: (0, j)),
        out_shape=jax.ShapeDtypeStruct((na, tb * npc), F32),
        compiler_params=_cp(("parallel", "arbitrary"), 56), name=name)(a, *pieces)


def _loss_grad(xf, target):
    L = xf.shape[0]
    tm = 1024

    def body(x_ref, t_ref, dy_ref, l_ref):
        e = x_ref[...] - t_ref[...]
        dy_ref[...] = e * (1.0 / D_MODEL)

        @pl.when(pl.program_id(0) == 0)
        def _():
            l_ref[...] = jnp.zeros_like(l_ref)
        l_ref[...] += jnp.sum(jnp.sum(e * e, axis=1, keepdims=True), axis=0, keepdims=True) * (0.5 / D_MODEL)

    row = pl.BlockSpec((tm, D_MODEL), lambda i: (i, 0))
    return pl.pallas_call(
        body, grid=(L // tm,), in_specs=[row, row],
        out_specs=[row, pl.BlockSpec((1, 1), lambda i: (0, 0))],
        out_shape=[jax.ShapeDtypeStruct((L, D_MODEL), F32), jax.ShapeDtypeStruct((1, 1), F32)],
        compiler_params=_cp(("arbitrary",), 40), name="loss_grad")(xf, target)


def _ffn_fwd(x, gain, wg, wu, wd):
    L = x.shape[0]
    ff = wg.shape[1]
    tm, tf = 512, 512
    nf = ff // tf

    def body(x_ref, g_ref, wg_ref, wu_ref, wd_ref, o_ref, h_scr, acc):
        c = pl.program_id(1)

        @pl.when(c == 0)
        def _():
            t = x_ref[...]
            h_scr[...] = (t * _rms(t) * g_ref[...]).astype(BF16)
            acc[...] = jnp.zeros_like(acc)

        h = h_scr[...]
        gate = _nn(h, wg_ref[...])
        up = _nn(h, wu_ref[...])
        hid = gate * _sigmoid(gate) * up
        acc[...] += _nn(hid.astype(BF16), wd_ref[...])

        @pl.when(c == nf - 1)
        def _():
            o_ref[...] = x_ref[...] + acc[...]

    row = pl.BlockSpec((tm, D_MODEL), lambda i, c: (i, 0))
    return pl.pallas_call(
        body, grid=(L // tm, nf),
        in_specs=[row, pl.BlockSpec((1, D_MODEL), lambda i, c: (0, 0)),
                  pl.BlockSpec((D_MODEL, tf), lambda i, c: (0, c)), pl.BlockSpec((D_MODEL, tf), lambda i, c: (0, c)),
                  pl.BlockSpec((tf, D_MODEL), lambda i, c: (c, 0))],
        out_specs=row, out_shape=jax.ShapeDtypeStruct((L, D_MODEL), F32),
        scratch_shapes=[pltpu.VMEM((tm, D_MODEL), BF16), pltpu.VMEM((tm, D_MODEL), F32)],
        compiler_params=_cp(("parallel", "arbitrary"), 40), name="ffn_fwd")(x, gain, wg, wu, wd)


def _ffn_bwd(x, gain, wg, wu, wd, dxo):
    L = x.shape[0]
    ff = wg.shape[1]
    tm, tf = 512, 512
    nf = ff // tf

    def body(x_ref, g_ref, wg_ref, wu_ref, wd_ref, dxo_ref, dx_ref, h_ref, hid_ref, dgate_ref, dup_ref, dg_ref,
             acc, dxo_b):
        i, c = pl.program_id(0), pl.program_id(1)

        @pl.when(c == 0)
        def _():
            t = x_ref[...]
            h_ref[...] = (t * _rms(t) * g_ref[...]).astype(BF16)
            acc[...] = jnp.zeros_like(acc)
            dxo_b[...] = dxo_ref[...].astype(BF16)

        h = h_ref[...]
        gate = _nn(h, wg_ref[...])
        up = _nn(h, wu_ref[...])
        sg = _sigmoid(gate)
        silu = gate * sg
        hid_ref[...] = (silu * up).astype(BF16)
        dhid = _nt(dxo_b[...], wd_ref[...])
        dup = (dhid * silu).astype(BF16)
        dgate = (dhid * up * (sg * (1.0 + gate * (1.0 - sg)))).astype(BF16)
        dup_ref[...] = dup
        dgate_ref[...] = dgate
        acc[...] += _nt(dgate, wg_ref[...]) + _nt(dup, wu_ref[...])

        @pl.when(c == nf - 1)
        def _():
            t = x_ref[...]
            dt, dgt = _rms_bwd(t, _rms(t), g_ref[...], acc[...], dxo_ref[...])
            dx_ref[...] = dt

            @pl.when(i == 0)
            def _():
                dg_ref[...] = jnp.zeros_like(dg_ref)
            dg_ref[...] += jnp.sum(dgt, axis=0, keepdims=True)

    row = pl.BlockSpec((tm, D_MODEL), lambda i, c: (i, 0))
    vec = pl.BlockSpec((1, D_MODEL), lambda i, c: (0, 0))
    wcol = pl.BlockSpec((D_MODEL, tf), lambda i, c: (0, c))
    hcol = pl.BlockSpec((tm, tf), lambda i, c: (i, c))
    return pl.pallas_call(
        body, grid=(L // tm, nf),
        in_specs=[row, vec, wcol, wcol, pl.BlockSpec((tf, D_MODEL), lambda i, c: (c, 0)), row],
        out_specs=[row, row, hcol, hcol, hcol, vec],
        out_shape=[jax.ShapeDtypeStruct((L, D_MODEL), F32), jax.ShapeDtypeStruct((L, D_MODEL), BF16),
                   jax.ShapeDtypeStruct((L, ff), BF16), jax.ShapeDtypeStruct((L, ff), BF16),
                   jax.ShapeDtypeStruct((L, ff), BF16), jax.ShapeDtypeStruct((1, D_MODEL), F32)],
        scratch_shapes=[pltpu.VMEM((tm, D_MODEL), F32), pltpu.VMEM((tm, D_MODEL), BF16)],
        compiler_params=_cp(("arbitrary", "arbitrary"), 48), name="ffn_bwd")(x, gain, wg, wu, wd, dxo)


GELU_K = math.sqrt(2.0 / math.pi)
GELU_C = 0.044715


def _gelu(y):
    return 0.5 * y * (1.0 + jnp.tanh(GELU_K * (y + GELU_C * (y * y * y))))


def _gelu_grad(y):
    th = jnp.tanh(GELU_K * (y + GELU_C * (y * y * y)))
    return 0.5 * (1.0 + th) + 0.5 * y * (1.0 - th * th) * (GELU_K * (1.0 + 3.0 * GELU_C * (y * y)))


def _merge_groups(o_refs, l_refs):
    ls = [r[...] for r in l_refs]
    os_ = [r[...] for r in o_refs]
    lmax = jnp.maximum(jnp.maximum(ls[0], ls[1]), ls[2])
    es = [jnp.exp(l - lmax) for l in ls]
    inv = 1.0 / (es[0] + es[1] + es[2])
    ws = [e * inv for e in es]
    a = ws[0] * os_[0] + ws[1] * os_[1] + ws[2] * os_[2]
    return ws, os_, a


def _mix_fwd(ols, y, z, x, wp, wa, wb, wo):
    L = x.shape[0]
    tm = 256

    def body(o0, l0, o1, l1, o2, l2, y_ref, ga_ref, gs_ref, x_ref, wp_ref, wa_ref, wb_ref, wo_ref, out_ref):
        _, _, a = _merge_groups((o0, o1, o2), (l0, l1, l2))
        a_out = _nn(a.astype(BF16), wp_ref[...])
        yg = _gelu(y_ref[...]).astype(BF16)
        s_out = _nn(yg, wa_ref[...]) * _sigmoid(_nn(yg, wb_ref[...]))
        mix = _sigmoid(ga_ref[...]) * a_out + _sigmoid(gs_ref[...]) * s_out
        out_ref[...] = x_ref[...] + _nn(mix.astype(BF16), wo_ref[...])

    half = pl.BlockSpec((tm, ATTN_W), lambda i: (i, 0))
    row = pl.BlockSpec((tm, D_MODEL), lambda i: (i, 0))
    w512 = pl.BlockSpec((ATTN_W, D_MODEL), lambda i: (0, 0))
    return pl.pallas_call(
        body, grid=(L // tm,),
        in_specs=[half] * 7 + [pl.BlockSpec((tm, D_MODEL), lambda i: (i, 5)),
                               pl.BlockSpec((tm, D_MODEL), lambda i: (i, 6)), row, w512, w512, w512,
                               pl.BlockSpec((D_MODEL, D_MODEL), lambda i: (0, 0))],
        out_specs=row, out_shape=jax.ShapeDtypeStruct((L, D_MODEL), F32),
        compiler_params=_cp(("parallel",), 48), name="mix_fwd")(*ols, y, z, z, x, wp, wa, wb, wo)


def _mix_bwd(dxm, ols, y, z, wp, wa, wb, wo):
    L = dxm.shape[0]
    tm = 256

    def body(dx_ref, o0, l0, o1, l1, o2, l2, y_ref, ga_ref, gs_ref, wp_ref, wa_ref, wb_ref, wo_ref,
             do0, dl0, do1, dl1, do2, dl2, dy_ref, dga_ref, dgs_ref, a_ref, yg_ref, mix_ref, dao_ref, dpa_ref,
             dpb_ref):
        ws, os_, a = _merge_groups((o0, o1, o2), (l0, l1, l2))
        ab = a.astype(BF16)
        a_out = _nn(ab, wp_ref[...])
        yv = y_ref[...]
        yg = _gelu(yv).astype(BF16)
        pa = _nn(yg, wa_ref[...])
        spb = _sigmoid(_nn(yg, wb_ref[...]))
        s_out = pa * spb
        sga = _sigmoid(ga_ref[...])
        sgs = _sigmoid(gs_ref[...])
        mix = sga * a_out + sgs * s_out
        dmix = _nt(dx_ref[...].astype(BF16), wo_ref[...])
        da_out = (sga * dmix).astype(BF16)
        ds_out = sgs * dmix
        dpa = (ds_out * spb).astype(BF16)
        dpb = (ds_out * pa * spb * (1.0 - spb)).astype(BF16)
        dga_ref[...] = (dmix * a_out * sga * (1.0 - sga)).astype(BF16)
        dgs_ref[...] = (dmix * s_out * sgs * (1.0 - sgs)).astype(BF16)
        dy_ref[...] = (_nt(dpa, wa_ref[...]) + _nt(dpb, wb_ref[...])) * _gelu_grad(yv)
        da = _nt(da_out, wp_ref[...])
        for w, o, do_ref, dl_ref in zip(ws, os_, (do0, do1, do2), (dl0, dl1, dl2)):
            do_ref[...] = w * da
            dl_ref[...] = da * w * (o - a)
        a_ref[...] = ab
        yg_ref[...] = yg
        mix_ref[...] = mix.astype(BF16)
        dao_ref[...] = da_out
        dpa_ref[...] = dpa
        dpb_ref[...] = dpb

    half = pl.BlockSpec((tm, ATTN_W), lambda i: (i, 0))
    row = pl.BlockSpec((tm, D_MODEL), lambda i: (i, 0))
    w512 = pl.BlockSpec((ATTN_W, D_MODEL), lambda i: (0, 0))
    hf = jax.ShapeDtypeStruct((L, ATTN_W), F32)
    hb = jax.ShapeDtypeStruct((L, ATTN_W), BF16)
    rb = jax.ShapeDtypeStruct((L, D_MODEL), BF16)
    return pl.pallas_call(
        body, grid=(L // tm,),
        in_specs=[row] + [half] * 7 + [pl.BlockSpec((tm, D_MODEL), lambda i: (i, 5)),
                                       pl.BlockSpec((tm, D_MODEL), lambda i: (i, 6)), w512, w512, w512,
                                       pl.BlockSpec((D_MODEL, D_MODEL), lambda i: (0, 0))],
        out_specs=[half] * 7 + [row, row, half, half, row, row, row, row],
        out_shape=[hf] * 7 + [rb, rb, hb, hb, rb, rb, rb, rb],
        compiler_params=_cp(("parallel",), 56), name="mix_bwd")(dxm, *ols, y, z, z, wp, wa, wb, wo)


N_ATTN_ITERS = ATTN_ROWS // BLK


def _class_rows(ref, start, d):
    if d == 1:
        return ref[pl.ds(pl.multiple_of(start, BLK), BLK), :]
    return ref[pl.ds(start, BLK, stride=d), :]


def _set_class_rows(ref, start, d, val):
    if d == 1:
        ref[pl.ds(pl.multiple_of(start, BLK), BLK), :] = val
    else:
        ref[pl.ds(start, BLK, stride=d), :] = val


def _head_masks():
    lane = lax.broadcasted_iota(jnp.int32, (1, LANES), 1)
    m0 = (lane < HEAD_DIM).astype(F32)
    return m0, 1.0 - m0


def _head_norm(t, gain2, m0, m1):
    tt = t * t
    r0 = lax.rsqrt(jnp.sum(tt * m0, axis=-1, keepdims=True) * (1.0 / HEAD_DIM) + EPS)
    r1 = lax.rsqrt(jnp.sum(tt * m1, axis=-1, keepdims=True) * (1.0 / HEAD_DIM) + EPS)
    r = m0 * r0 + m1 * r1
    return t * r * gain2, r


def _head_norm_bwd(t, r, gain2, dy, m0, m1):
    u = dy * gain2
    tu = t * u
    s = m0 * jnp.sum(tu * m0, axis=-1, keepdims=True) + m1 * jnp.sum(tu * m1, axis=-1, keepdims=True)
    return r * u - t * (r * r * r) * s * (1.0 / HEAD_DIM), jnp.sum(dy * t * r, axis=0, keepdims=True)


def _band_masks():
    qi = lax.broadcasted_iota(jnp.int32, (BLK, 2 * BLK), 0)
    ki = lax.broadcasted_iota(jnp.int32, (BLK, 2 * BLK), 1)
    dist = BLK + qi - ki
    return (dist >= 0) & (dist <= BLK), ki >= BLK


def _attn_probs(qm, kw, ok):
    s = _nt(qm, kw) * (HEAD_DIM ** -0.5)
    s = jnp.where(ok, s, -1e30)
    mx = jnp.max(s, axis=-1, keepdims=True)
    p = jnp.exp(s - mx)
    den = jnp.sum(p, axis=-1, keepdims=True)
    return p, den, mx


NORM_ROWS = 256


def _norm_rows(src_ref, gain2, dst_ref, m0, m1):
    n = src_ref.shape[0]
    step = min(NORM_ROWS, n)
    for r0 in range(0, n, step):
        dst_ref[r0:r0 + step, :] = _head_norm(src_ref[r0:r0 + step, :], gain2, m0, m1)[0]


def _norm_rows_bwd(src_ref, gain2, dy_ref, dst_ref, m0, m1):
    n = src_ref.shape[0]
    step = min(NORM_ROWS, n)
    dgain = jnp.zeros((1, LANES), F32)
    for r0 in range(0, n, step):
        t = src_ref[r0:r0 + step, :]
        _, r = _head_norm(t, gain2, m0, m1)
        dt, dg = _head_norm_bwd(t, r, gain2, dy_ref[r0:r0 + step, :], m0, m1)
        dst_ref[r0:r0 + step, :] = dt
        dgain = dgain + dg
    return dgain


def _attn_operands(it, d, first_step, q_ref, kc_ref, kp_ref, vc_ref, vp_ref, band, is_cur):
    j = it // d
    start = (it - j * d) + (d * BLK) * j
    before = jnp.maximum(start - d * BLK, 0)
    inside = j > 0
    q2 = _class_rows(q_ref, start, d)
    kc2 = _class_rows(kc_ref, start, d)
    vc2 = _class_rows(vc_ref, start, d)
    kp2 = jnp.where(inside, _class_rows(kc_ref, before, d), _class_rows(kp_ref, it - j * d, d))
    vp2 = jnp.where(inside, _class_rows(vc_ref, before, d), _class_rows(vp_ref, it - j * d, d))
    has_prev = inside | jnp.logical_not(first_step)
    return start, q2, kp2, kc2, vp2, vc2, band & (is_cur | has_prev)


def _attn_specs(d, step_of):
    nq = N_ATTN_ITERS // d

    def cur(c):
        return pl.BlockSpec((ATTN_ROWS, LANES), lambda hp, n: (step_of(n), c + hp))

    def prev(c):
        return pl.BlockSpec((d * BLK, LANES), lambda hp, n: (jnp.maximum(step_of(n) * nq - 1, 0), c + hp))

    return cur, prev, pl.BlockSpec((1, LANES), lambda hp, n: (0, 0))


def _attn_fwd(z, gq2, gk2, group):
    L = z.shape[0]
    d = DILATIONS[group]
    nsb = L // ATTN_ROWS
    cq, ck, cv = group * 4, 12 + group * 4, 24 + group * 4

    def body(q_ref, kc_ref, kp_ref, vc_ref, vp_ref, gq_ref, gk_ref, o_ref, l_ref, qn_scr, kn_scr, kpn_scr):
        first_step = pl.program_id(1) == 0
        band, is_cur = _band_masks()
        m0, m1 = _head_masks()
        _norm_rows(q_ref, gq_ref[...], qn_scr, m0, m1)
        _norm_rows(kc_ref, gk_ref[...], kn_scr, m0, m1)
        _norm_rows(kp_ref, gk_ref[...], kpn_scr, m0, m1)

        def per_block(it, carry):
            start, qn, kpn, kcn, vp2, vc2, ok = _attn_operands(
                it, d, first_step, qn_scr, kn_scr, kpn_scr, vc_ref, vp_ref, band, is_cur)
            kw = jnp.concatenate([kpn, kcn], axis=0).astype(BF16)
            vw = jnp.concatenate([vp2, vc2], axis=0).astype(BF16)
            o2 = jnp.zeros((BLK, LANES), F32)
            l2 = jnp.zeros((BLK, LANES), F32)
            for mh in (m0, m1):
                p, den, mx = _attn_probs((qn * mh).astype(BF16), kw, ok)
                o2 = o2 + mh * (_nn(p.astype(BF16), vw) / den)
                l2 = l2 + mh * (mx + jnp.log(den))
            _set_class_rows(o_ref, start, d, o2)
            _set_class_rows(l_ref, start, d, l2)
            return carry

        lax.fori_loop(0, N_ATTN_ITERS, per_block, 0, unroll=2)

    cur, prev, vec = _attn_specs(d, lambda n: n)
    out = pl.BlockSpec((ATTN_ROWS, LANES), lambda hp, n: (n, hp))
    sds = jax.ShapeDtypeStruct((L, ATTN_W), F32)
    return pl.pallas_call(
        body, grid=(4, nsb),
        in_specs=[cur(cq), cur(ck), prev(ck), cur(cv), prev(cv), vec, vec],
        out_specs=[out, out], out_shape=[sds, sds],
        scratch_shapes=[pltpu.VMEM((ATTN_ROWS, LANES), F32), pltpu.VMEM((ATTN_ROWS, LANES), F32),
                        pltpu.VMEM((d * BLK, LANES), F32)],
        compiler_params=_cp(("parallel", "arbitrary"), 48), name=f"attn_fwd_g{group}")(z, z, z, z, z, gq2, gk2)


def _attn_bwd(z, gq2, gk2, o, do, dl, group):
    L = z.shape[0]
    d = DILATIONS[group]
    nsb = L // ATTN_ROWS
    cq, ck, cv = group * 4, 12 + group * 4, 24 + group * 4

    def body(q_ref, kc_ref, kp_ref, vc_ref, vp_ref, gq_ref, gk_ref, o_ref, do_ref, dl_ref,
             dq_ref, dk_ref, dv_ref, dgq_ref, dgk_ref, ck_scr, cv_scr, qn_scr, kn_scr, kpn_scr, dqn_scr, dkn_scr):
        hp, n = pl.program_id(0), pl.program_id(1)
        first_step = n == nsb - 1
        band, is_cur = _band_masks()
        m0, m1 = _head_masks()
        gq, gk = gq_ref[...], gk_ref[...]
        _norm_rows(q_ref, gq, qn_scr, m0, m1)
        _norm_rows(kc_ref, gk, kn_scr, m0, m1)
        _norm_rows(kp_ref, gk, kpn_scr, m0, m1)

        @pl.when((hp == 0) & (n == 0))
        def _():
            dgq_ref[...] = jnp.zeros_like(dgq_ref)
            dgk_ref[...] = jnp.zeros_like(dgk_ref)

        @pl.when(n == 0)
        def _():
            ck_scr[...] = jnp.zeros_like(ck_scr)
            cv_scr[...] = jnp.zeros_like(cv_scr)

        def per_block(i, carry):
            it = N_ATTN_ITERS - 1 - i
            start, qn, kpn, kcn, vp2, vc2, ok = _attn_operands(
                it, d, first_step, qn_scr, kn_scr, kpn_scr, vc_ref, vp_ref, band, is_cur)
            r = it - (it // d) * d
            kw = jnp.concatenate([kpn, kcn], axis=0).astype(BF16)
            vw = jnp.concatenate([vp2, vc2], axis=0).astype(BF16)
            o2 = _class_rows(o_ref, start, d)
            do2 = _class_rows(do_ref, start, d)
            dl2 = _class_rows(dl_ref, start, d)
            dqn = jnp.zeros((BLK, LANES), F32)
            dkw = jnp.zeros((2 * BLK, LANES), F32)
            dvw = jnp.zeros((2 * BLK, LANES), F32)
            for mh in (m0, m1):
                qm = (qn * mh).astype(BF16)
                p, den, _ = _attn_probs(qm, kw, ok)
                pn = p / den
                doh = do2 * mh
                dohb = doh.astype(BF16)
                dvw = dvw + _tn_dot(pn.astype(BF16), dohb)
                dp = _nt(dohb, vw)
                delta = jnp.sum(doh * o2, axis=-1, keepdims=True)
                dlse = jnp.sum(dl2 * mh, axis=-1, keepdims=True)
                ds = (pn * (dp - delta + dlse) * (HEAD_DIM ** -0.5)).astype(BF16)
                dqn = dqn + mh * _nn(ds, kw)
                dkw = dkw + _tn_dot(ds, qm)
            _set_class_rows(dqn_scr, start, d, dqn)
            _set_class_rows(dkn_scr, start, d, ck_scr[r] + dkw[BLK:])
            _set_class_rows(dv_ref, start, d, cv_scr[r] + dvw[BLK:])
            ck_scr[r] = dkw[:BLK]
            cv_scr[r] = dvw[:BLK]
            return carry

        lax.fori_loop(0, N_ATTN_ITERS, per_block, 0, unroll=2)
        dgq_ref[...] += _norm_rows_bwd(q_ref, gq, dqn_scr, dq_ref, m0, m1)
        dgk_ref[...] += _norm_rows_bwd(kc_ref, gk, dkn_scr, dk_ref, m0, m1)

    cur, prev, vec = _attn_specs(d, lambda n: nsb - 1 - n)
    sds = jax.ShapeDtypeStruct((L, ATTN_W), F32)
    vsd = jax.ShapeDtypeStruct((1, LANES), F32)
    return pl.pallas_call(
        body, grid=(4, nsb),
        in_specs=[cur(cq), cur(ck), prev(ck), cur(cv), prev(cv), vec, vec, cur(0), cur(0), cur(0)],
        out_specs=[cur(0), cur(0), cur(0), vec, vec], out_shape=[sds, sds, sds, vsd, vsd],
        scratch_shapes=[pltpu.VMEM((d, BLK, LANES), F32), pltpu.VMEM((d, BLK, LANES), F32),
                        pltpu.VMEM((ATTN_ROWS, LANES), F32), pltpu.VMEM((ATTN_ROWS, LANES), F32),
                        pltpu.VMEM((d * BLK, LANES), F32),
                        pltpu.VMEM((ATTN_ROWS, LANES), F32), pltpu.VMEM((ATTN_ROWS, LANES), F32)],
        compiler_params=_cp(("arbitrary", "arbitrary"), 56),
        name=f"attn_bwd_g{group}")(z, z, z, z, z, gq2, gk2, o, do, dl)


BLOCK_STATES = SSM_STATES // SSM_BLOCKS
BLOCK_CH = SSM_W // SSM_BLOCKS
SLABS_PER_BLOCK = BLOCK_STATES // LANES


def _store_block(bufs, b, val, tm):
    for s in range(SLABS_PER_BLOCK):
        k = SLABS_PER_BLOCK * b + s
        bufs[k % 2][pl.ds(k // 2, tm, stride=8), :] = val[:, s * LANES:(s + 1) * LANES]


def _load_block(bufs, b, tm):
    tiles = []
    for s in range(SLABS_PER_BLOCK):
        k = SLABS_PER_BLOCK * b + s
        tiles.append(bufs[k % 2][pl.ds(k // 2, tm, stride=8), :])
    return jnp.concatenate(tiles, axis=1).astype(BF16)


def _ssm_project_in(ub, bdr_ref, bdi_ref, sr, si, tm):
    for b in range(SSM_BLOCKS):
        ubb = ub[:, b * BLOCK_CH:(b + 1) * BLOCK_CH]
        _store_block(sr, b, _nn(ubb, bdr_ref[b]), tm)
        _store_block(si, b, _nn(ubb, bdi_ref[b]), tm)


def _ssm_scan(a, x0, sr, si, tm):
    ar0, ar1, ai0, ai1 = a

    def step(t, c):
        xr0, xr1, xi0, xi1 = c
        i8 = pl.multiple_of(t * 8, 8)
        nr0 = ar0 * xr0 - ai0 * xi0 + sr[0][pl.ds(i8, 8), :]
        ni0 = ar0 * xi0 + ai0 * xr0 + si[0][pl.ds(i8, 8), :]
        nr1 = ar1 * xr1 - ai1 * xi1 + sr[1][pl.ds(i8, 8), :]
        ni1 = ar1 * xi1 + ai1 * xr1 + si[1][pl.ds(i8, 8), :]
        sr[0][pl.ds(i8, 8), :] = nr0
        si[0][pl.ds(i8, 8), :] = ni0
        sr[1][pl.ds(i8, 8), :] = nr1
        si[1][pl.ds(i8, 8), :] = ni1
        return nr0, nr1, ni0, ni1

    return lax.fori_loop(0, tm, step, x0, unroll=8)


def _load_a(ar_ref, ai_ref):
    return ar_ref[:, :LANES], ar_ref[:, LANES:], ai_ref[:, :LANES], ai_ref[:, LANES:]


def _ssm_fwd(z, ar8, ai8, bdr, bdi, cdr, cdi, dsk):
    L = z.shape[0]
    tm = SSM_TM
    nc = L // tm

    def body(u_ref, ar_ref, ai_ref, bdr_ref, bdi_ref, cdr_ref, cdi_ref, dsk_ref, y_ref, cin_ref,
             sr0, sr1, si0, si1, car):
        sr, si = (sr0, sr1), (si0, si1)

        @pl.when(pl.program_id(0) == 0)
        def _():
            car[...] = jnp.zeros_like(car)

        u = u_ref[...]
        _ssm_project_in(u.astype(BF16), bdr_ref, bdi_ref, sr, si, tm)
        cin_ref[0] = car[...]
        xr0, xr1, xi0, xi1 = _ssm_scan(_load_a(ar_ref, ai_ref), (car[0], car[1], car[2], car[3]), sr, si, tm)
        car[0], car[1], car[2], car[3] = xr0, xr1, xi0, xi1
        for b in range(SSM_BLOCKS):
            cols = slice(b * BLOCK_CH, (b + 1) * BLOCK_CH)
            y_ref[:, cols] = (dsk_ref[:, cols] * u[:, cols] + _nn(_load_block(sr, b, tm), cdr_ref[b])
                              - _nn(_load_block(si, b, tm), cdi_ref[b]))

    def const(shape):
        return pl.BlockSpec(shape, lambda i: (0,) * len(shape))

    state = pltpu.VMEM((tm * 8, LANES), F32)
    wb = const((SSM_BLOCKS, BLOCK_CH, BLOCK_STATES))
    wc = const((SSM_BLOCKS, BLOCK_STATES, BLOCK_CH))
    return pl.pallas_call(
        body, grid=(nc,),
        in_specs=[pl.BlockSpec((tm, SSM_W), lambda i: (i, COL_U)), const((8, 256)), const((8, 256)),
                  wb, wb, wc, wc, const((1, SSM_W))],
        out_specs=[pl.BlockSpec((tm, SSM_W), lambda i: (i, 0)), pl.BlockSpec((1, 4, 8, LANES), lambda i: (i, 0, 0, 0))],
        out_shape=[jax.ShapeDtypeStruct((L, SSM_W), F32), jax.ShapeDtypeStruct((nc, 4, 8, LANES), F32)],
        scratch_shapes=[state, state, state, state, pltpu.VMEM((4, 8, LANES), F32)],
        compiler_params=_cp(("arbitrary",), 48), name="ssm_fwd")(z, ar8, ai8, bdr, bdi, cdr, cdi, dsk)


def _ssm_bwd(z, dy, cin, ar8, ai8, bdr, bdi, cdr, cdi, dsk):
    L = z.shape[0]
    tm = SSM_TM
    nc = L // tm

    def body(u_ref, dy_ref, cin_ref, ar_ref, ai_ref, dsk_ref, bdr_ref, bdi_ref, cdr_ref, cdi_ref,
             du_ref, da_ref, dds_ref, dbdr_ref, dbdi_ref, dcdr_ref, dcdi_ref,
             sr0, sr1, si0, si1, gr0, gr1, gi0, gi1, carg):
        sr, si, gr, gi = (sr0, sr1), (si0, si1), (gr0, gr1), (gi0, gi1)

        @pl.when(pl.program_id(0) == 0)
        def _():
            carg[...] = jnp.zeros_like(carg)
            for ref in (da_ref, dds_ref, dbdr_ref, dbdi_ref, dcdr_ref, dcdi_ref):
                ref[...] = jnp.zeros_like(ref)

        u = u_ref[...]
        ub = u.astype(BF16)
        dyv = dy_ref[...]
        dyb = dyv.astype(BF16)
        a = _load_a(ar_ref, ai_ref)
        ar0, ar1, ai0, ai1 = a
        x_in = (cin_ref[0, 0], cin_ref[0, 1], cin_ref[0, 2], cin_ref[0, 3])
        _ssm_project_in(ub, bdr_ref, bdi_ref, sr, si, tm)
        _ssm_scan(a, x_in, sr, si, tm)
        for b in range(SSM_BLOCKS):
            dyb_b = dyb[:, b * BLOCK_CH:(b + 1) * BLOCK_CH]
            _store_block(gr, b, _nt(dyb_b, cdr_ref[b]), tm)
            _store_block(gi, b, -_nt(dyb_b, cdi_ref[b]), tm)

        def grad_step(t, g_next, x_prev, acc):
            i8 = pl.multiple_of(t * 8, 8)
            nr0, nr1, ni0, ni1 = g_next
            pr0, pr1, pi0, pi1 = x_prev
            d_r0, d_r1, d_i0, d_i1 = acc
            g_r0 = gr[0][pl.ds(i8, 8), :] + ar0 * nr0 + ai0 * ni0
            g_i0 = gi[0][pl.ds(i8, 8), :] + ar0 * ni0 - ai0 * nr0
            g_r1 = gr[1][pl.ds(i8, 8), :] + ar1 * nr1 + ai1 * ni1
            g_i1 = gi[1][pl.ds(i8, 8), :] + ar1 * ni1 - ai1 * nr1
            gr[0][pl.ds(i8, 8), :] = g_r0
            gi[0][pl.ds(i8, 8), :] = g_i0
            gr[1][pl.ds(i8, 8), :] = g_r1
            gi[1][pl.ds(i8, 8), :] = g_i1
            acc = (d_r0 + pr0 * g_r0 + pi0 * g_i0, d_r1 + pr1 * g_r1 + pi1 * g_i1,
                   d_i0 + pr0 * g_i0 - pi0 * g_r0, d_i1 + pr1 * g_i1 - pi1 * g_r1)
            return (g_r0, g_r1, g_i0, g_i1), acc

        def rstep(j, c):
            t = tm - 1 - j
            p8 = pl.multiple_of((t - 1) * 8, 8)
            x_prev = (sr[0][pl.ds(p8, 8), :], sr[1][pl.ds(p8, 8), :], si[0][pl.ds(p8, 8), :], si[1][pl.ds(p8, 8), :])
            return grad_step(t, c[0], x_prev, c[1])

        acc0 = (da_ref[0], da_ref[1], da_ref[2], da_ref[3])
        g_next, acc = lax.fori_loop(0, tm - 1, rstep, ((carg[0], carg[1], carg[2], carg[3]), acc0), unroll=7)
        g_first, acc = grad_step(0, g_next, x_in, acc)
        carg[0], carg[1], carg[2], carg[3] = g_first
        da_ref[0], da_ref[1], da_ref[2], da_ref[3] = acc

        for b in range(SSM_BLOCKS):
            cols = slice(b * BLOCK_CH, (b + 1) * BLOCK_CH)
            grb, gib = _load_block(gr, b, tm), _load_block(gi, b, tm)
            du_ref[:, cols] = dsk_ref[:, cols] * dyv[:, cols] + _nt(grb, bdr_ref[b]) + _nt(gib, bdi_ref[b])
            dbdr_ref[b] += _tn_dot(ub[:, cols], grb)
            dbdi_ref[b] += _tn_dot(ub[:, cols], gib)
            dcdr_ref[b] += _tn_dot(_load_block(sr, b, tm), dyb[:, cols])
            dcdi_ref[b] -= _tn_dot(_load_block(si, b, tm), dyb[:, cols])
        dds_ref[...] += jnp.sum(dyv * u, axis=0, keepdims=True)

    def const(shape):
        return pl.BlockSpec(shape, lambda i: (0,) * len(shape))

    state = pltpu.VMEM((tm * 8, LANES), F32)
    wb = const((SSM_BLOCKS, BLOCK_CH, BLOCK_STATES))
    wc = const((SSM_BLOCKS, BLOCK_STATES, BLOCK_CH))
    return pl.pallas_call(
        body, grid=(nc,),
        in_specs=[pl.BlockSpec((tm, SSM_W), lambda i: (nc - 1 - i, COL_U)),
                  pl.BlockSpec((tm, SSM_W), lambda i: (nc - 1 - i, 0)),
                  pl.BlockSpec((1, 4, 8, LANES), lambda i: (nc - 1 - i, 0, 0, 0)),
                  const((8, 256)), const((8, 256)), const((1, SSM_W)), wb, wb, wc, wc],
        out_specs=[pl.BlockSpec((tm, SSM_W), lambda i: (nc - 1 - i, 0)), const((4, 8, LANES)), const((1, SSM_W)),
                   wb, wb, wc, wc],
        out_shape=[jax.ShapeDtypeStruct((L, SSM_W), F32), jax.ShapeDtypeStruct((4, 8, LANES), F32),
                   jax.ShapeDtypeStruct((1, SSM_W), F32),
                   jax.ShapeDtypeStruct((SSM_BLOCKS, BLOCK_CH, BLOCK_STATES), F32),
                   jax.ShapeDtypeStruct((SSM_BLOCKS, BLOCK_CH, BLOCK_STATES), F32),
                   jax.ShapeDtypeStruct((SSM_BLOCKS, BLOCK_STATES, BLOCK_CH), F32),
                   jax.ShapeDtypeStruct((SSM_BLOCKS, BLOCK_STATES, BLOCK_CH), F32)],
        scratch_shapes=[state] * 8 + [pltpu.VMEM((4, 8, LANES), F32)],
        compiler_params=_cp(("arbitrary",), 56), name="ssm_bwd")(z, dy, cin, ar8, ai8, dsk, bdr, bdi, cdr, cdi)


def _discretise(lam_re, lam_im, log_dt, b_re, b_im):
    dt = jnp.exp(log_dt)[:, None]
    mag = jnp.exp(lam_re * dt)
    ang = lam_im * dt
    abar_re = mag * jnp.cos(ang)
    abar_im = mag * jnp.sin(ang)
    nr = abar_re - 1.0
    ni = abar_im
    den = lam_re * lam_re + lam_im * lam_im
    cr = ((nr * lam_re + ni * lam_im) / den)[..., None]
    ci = ((ni * lam_re - nr * lam_im) / den)[..., None]
    return abar_re, abar_im, cr * b_re - ci * b_im, cr * b_im + ci * b_re


GROUPS_PER_BLOCK = 8


def _block_diag_in(bbar):
    eye = jnp.eye(GROUPS_PER_BLOCK, dtype=F32)
    return jnp.einsum("igpc,gh->igchp", bbar.reshape(SSM_BLOCKS, GROUPS_PER_BLOCK, 64, 16), eye).reshape(
        SSM_BLOCKS, BLOCK_CH, BLOCK_STATES)


def _block_diag_in_t(blocks):
    eye = jnp.eye(GROUPS_PER_BLOCK, dtype=F32)
    return jnp.einsum("igchp,gh->igpc", blocks.reshape(SSM_BLOCKS, GROUPS_PER_BLOCK, 16, GROUPS_PER_BLOCK, 64),
                      eye).reshape(32, 64, 16)


def _block_diag_out(c):
    eye = jnp.eye(GROUPS_PER_BLOCK, dtype=F32)
    return jnp.einsum("igcp,gh->igphc", c.reshape(SSM_BLOCKS, GROUPS_PER_BLOCK, 16, 64), eye).reshape(
        SSM_BLOCKS, BLOCK_STATES, BLOCK_CH)


def _block_diag_out_t(blocks):
    eye = jnp.eye(GROUPS_PER_BLOCK, dtype=F32)
    return jnp.einsum("igphc,gh->igcp", blocks.reshape(SSM_BLOCKS, GROUPS_PER_BLOCK, 64, GROUPS_PER_BLOCK, 16),
                      eye).reshape(32, 16, 64)


SMALL_NAMES = ("g_mix", "g_q", "g_k", "lambda_re", "lambda_im", "log_dt", "b_re", "b_im", "c_re", "c_im",
               "d_skip", "g_ffn")


def _pack_small(parts):
    flat = jnp.concatenate([parts[n].reshape(-1) for n in SMALL_NAMES])
    pad = (-flat.shape[0]) % (8 * LANES * SMALL_TILES)
    return jnp.pad(flat, (0, pad)).reshape(-1, LANES)


def _unpack_small(packed, like):
    flat = packed.reshape(-1)
    out, off = {}, 0
    for n in SMALL_NAMES:
        size = like[n].size
        out[n] = flat[off:off + size].reshape(like[n].shape)
        off += size
    return out


BIG_NAMES = ("w_in", "w_attn_proj", "w_glu_a", "w_glu_b", "w_out", "w_ffn_gate", "w_ffn_up", "w_ffn_down")
BIG_SHARD_AXIS = {"w_in": 2, "w_attn_proj": 2, "w_glu_a": 2, "w_glu_b": 2, "w_out": 1,
                  "w_ffn_gate": 2, "w_ffn_up": 2, "w_ffn_down": 1}
ADAMW_ROWS = {"w_in": 256, "w_attn_proj": 512, "w_glu_a": 512, "w_glu_b": 512, "w_out": 128,
              "w_ffn_gate": 256, "w_ffn_up": 256, "w_ffn_down": 176}


def kernel(x, g_mix, w_in, g_q, g_k, w_attn_proj, lambda_re, lambda_im, log_dt, b_re, b_im, c_re, c_im, d_skip, w_glu_a, w_glu_b, w_out, g_ffn, w_ffn_gate, w_ffn_up, w_ffn_down, loss_target, m_g_mix, m_w_in, m_g_q, m_g_k, m_w_attn_proj, m_lambda_re, m_lambda_im, m_log_dt, m_b_re, m_b_im, m_c_re, m_c_im, m_d_skip, m_w_glu_a, m_w_glu_b, m_w_out, m_g_ffn, m_w_ffn_gate, m_w_ffn_up, m_w_ffn_down, v_g_mix, v_w_in, v_g_q, v_g_k, v_w_attn_proj, v_lambda_re, v_lambda_im, v_log_dt, v_b_re, v_b_im, v_c_re, v_c_im, v_d_skip, v_w_glu_a, v_w_glu_b, v_w_out, v_g_ffn, v_w_ffn_gate, v_w_ffn_up, v_w_ffn_down):
    args = dict(locals())
    weights = {n: args[n] for n in BIG_NAMES + SMALL_NAMES}
    moments_m = {n: args["m_" + n] for n in BIG_NAMES + SMALL_NAMES}
    moments_v = {n: args["v_" + n] for n in BIG_NAMES + SMALL_NAMES}
    x0 = x[0]
    target = loss_target[0]

    shards = []
    for n in BIG_NAMES:
        w = weights[n]
        rows_to, cols_to = w.shape[1], w.shape[2]
        if n in ("w_ffn_gate", "w_ffn_up"):
            cols_to = FF_SHARD_PAD
        if n == "w_ffn_down":
            rows_to = FF_SHARD_PAD
        shards.append(_prep_weight(w, rows_to, cols_to, "prep_" + n))
    full = dict(zip(BIG_NAMES, _all_gather(shards, [BIG_SHARD_AXIS[n] for n in BIG_NAMES])))

    saved = []
    xl = x0
    for l in range(DEPTH):
        abar_re, abar_im, bb_re, bb_im = _discretise(lambda_re[l], lambda_im[l], log_dt[l], b_re[l], b_im[l])
        ssm = dict(ar8=abar_re.reshape(8, 256), ai8=abar_im.reshape(8, 256),
                   bdr=_block_diag_in(bb_re).astype(BF16), bdi=_block_diag_in(bb_im).astype(BF16),
                   cdr=_block_diag_out(c_re[l]).astype(BF16), cdi=_block_diag_out(c_im[l]).astype(BF16),
                   dsk=d_skip[l][None])
        gq2 = jnp.tile(g_q[l], 2)[None]
        gk2 = jnp.tile(g_k[l], 2)[None]
        z, h = _in_proj(xl, g_mix[l][None], full["w_in"][l])
        ols = []
        for g in range(N_GROUPS):
            ols.extend(_attn_fwd(z, gq2, gk2, g))
        y, cin = _ssm_fwd(z, **ssm)
        xm = _mix_fwd(ols, y, z, xl, full["w_attn_proj"][l], full["w_glu_a"][l], full["w_glu_b"][l], full["w_out"][l])
        xo = _ffn_fwd(xm, g_ffn[l][None], full["w_ffn_gate"][l], full["w_ffn_up"][l], full["w_ffn_down"][l])
        saved.append(dict(x=xl, z=z, h=h, ols=ols, y=y, cin=cin, xm=xm, ssm=ssm, gq2=gq2, gk2=gk2))
        xl = xo

    dxo, loss_local = _loss_grad(xl, target)
    loss = lax.psum(loss_local[0, 0], MESH_AXES)
    big_grads = {n: [None] * DEPTH for n in BIG_NAMES}
    small_grads = {n: [None] * DEPTH for n in SMALL_NAMES}
    for l in reversed(range(DEPTH)):
        s = saved[l]
        dxm, h2, hid, dgate, dup, dgffn = _ffn_bwd(s["xm"], g_ffn[l][None], full["w_ffn_gate"][l],
                                                   full["w_ffn_up"][l], full["w_ffn_down"][l], dxo)
        big_grads["w_ffn_down"][l] = _tn(hid, dxo, "grad_w_ffn_down")
        big_grads["w_ffn_gate"][l] = _tn(h2, dgate, "grad_w_ffn_gate")
        big_grads["w_ffn_up"][l] = _tn(h2, dup, "grad_w_ffn_up")
        (do0, dl0, do1, dl1, do2, dl2, dy, dga, dgs, a_b, yg_b, mix_b, dao_b, dpa_b, dpb_b) = _mix_bwd(
            dxm, s["ols"], s["y"], s["z"], full["w_attn_proj"][l], full["w_glu_a"][l], full["w_glu_b"][l],
            full["w_out"][l])
        big_grads["w_out"][l] = _tn(mix_b, dxm, "grad_w_out")
        big_grads["w_attn_proj"][l] = _tn(a_b, dao_b, "grad_w_attn_proj")
        big_grads["w_glu_a"][l] = _tn(yg_b, dpa_b, "grad_w_glu_a")
        big_grads["w_glu_b"][l] = _tn(yg_b, dpb_b, "grad_w_glu_b")
        du, da4, ddsk, dbdr, dbdi, dcdr, dcdi = _ssm_bwd(s["z"], dy, s["cin"], **s["ssm"])
        dqkv = []
        dgq = jnp.zeros((1, LANES), F32)
        dgk = jnp.zeros((1, LANES), F32)
        for g, (do_g, dl_g) in enumerate(((do0, dl0), (do1, dl1), (do2, dl2))):
            dq, dk, dv, dgq_g, dgk_g = _attn_bwd(s["z"], s["gq2"], s["gk2"], s["ols"][2 * g], do_g, dl_g, g)
            dqkv.append((dq, dk, dv))
            dgq, dgk = dgq + dgq_g, dgk + dgk_g
        def cat(a, b):
            return jnp.concatenate([a, b], axis=1).astype(BF16)

        pieces = [cat(dqkv[0][0], dqkv[1][0]), cat(dqkv[2][0], dqkv[0][1]), cat(dqkv[1][1], dqkv[2][1]),
                  cat(dqkv[0][2], dqkv[1][2]), cat(dqkv[2][2], du), dga, dgs]
        dxo, dgmix = _in_proj_bwd(pieces, full["w_in"][l], s["x"], g_mix[l][None], dxm)
        big_grads["w_in"][l] = _tn_pieces(s["h"], pieces, "grad_w_in")
        _, disc_vjp = jax.vjp(_discretise, lambda_re[l], lambda_im[l], log_dt[l], b_re[l], b_im[l])
        dar = jnp.concatenate([da4[0], da4[1]], axis=1).reshape(32, 64)
        dai = jnp.concatenate([da4[2], da4[3]], axis=1).reshape(32, 64)
        dlr, dli, dldt, dbre, dbim = disc_vjp((dar, dai, _block_diag_in_t(dbdr), _block_diag_in_t(dbdi)))
        small_grads["g_mix"][l] = dgmix[0]
        small_grads["g_q"][l] = dgq[0, :HEAD_DIM] + dgq[0, HEAD_DIM:]
        small_grads["g_k"][l] = dgk[0, :HEAD_DIM] + dgk[0, HEAD_DIM:]
        small_grads["lambda_re"][l] = dlr
        small_grads["lambda_im"][l] = dli
        small_grads["log_dt"][l] = dldt
        small_grads["b_re"][l] = dbre
        small_grads["b_im"][l] = dbim
        small_grads["c_re"][l] = _block_diag_out_t(dcdr)
        small_grads["c_im"][l] = _block_diag_out_t(dcdi)
        small_grads["d_skip"][l] = ddsk[0]
        small_grads["g_ffn"][l] = dgffn[0]
    grad_x = dxo[None]

    small_local = {n: jnp.stack(small_grads[n]) for n in SMALL_NAMES}
    rs_axes = [BIG_SHARD_AXIS[n] - 1 for n in BIG_NAMES]
    got = _exchange_with_sibling([big_grads[n] for n in BIG_NAMES], rs_axes)
    core = lax.axis_index("c").astype(jnp.int32).reshape(1)
    sums = [[_chip_sum(big_grads[n][l], got[t], l, rs_axes[t], core, "chip_sum_" + n) for l in range(DEPTH)]
            for t, n in enumerate(BIG_NAMES)]
    recv = _exchange_chip_sums(sums, _pack_small(small_local))
    out_g, out_d, out_m, out_v = {}, {}, {}, {}
    for n, r in zip(BIG_NAMES, recv[:-1]):
        out_g[n], out_d[n], out_m[n], out_v[n] = _adamw_big(r, weights[n], moments_m[n], moments_v[n],
                                                            ADAMW_ROWS[n], "adamw_" + n)
    like = {n: weights[n] for n in SMALL_NAMES}
    packed = _adamw_small(recv[-1], _pack_small(like), _pack_small({n: moments_m[n] for n in SMALL_NAMES}),
                          _pack_small({n: moments_v[n] for n in SMALL_NAMES}))
    for dst, p in zip((out_g, out_d, out_m, out_v), packed):
        dst.update(_unpack_small(p, like))

    order = ("g_mix", "w_in", "g_q", "g_k", "w_attn_proj", "lambda_re", "lambda_im", "log_dt", "b_re", "b_im",
             "c_re", "c_im", "d_skip", "w_glu_a", "w_glu_b", "w_out", "g_ffn", "w_ffn_gate", "w_ffn_up",
             "w_ffn_down")
    return (loss, grad_x, *[out_g[n] for n in order], *[out_d[n] for n in order],
            *[out_m[n] for n in order], *[out_v[n] for n in order])
```

```python
import functools
import math

import jax
import jax.numpy as jnp
from jax import lax
from jax.experimental import pallas as pl
from jax.experimental.pallas import tpu as pltpu

F32 = jnp.float32
BF16 = jnp.bfloat16

D_MODEL = 1024
DEPTH = 4
N_DEV = 8
N_CHIPS = 4
HEAD_DIM = 64
BLK = 128
LANES = 128
ATTN_W = 512
N_GROUPS = 3
DILATIONS = (1, 4, 16)
ATTN_ROWS = 2048
SSM_W = 512
SSM_STATES = 2048
SSM_BLOCKS = 4
IN_COLS = 7168
COL_U = 9
D_FF = 2816
FF_SHARD = D_FF // N_DEV
FF_SHARD_PAD = 384
FF_PAD = FF_SHARD_PAD * N_DEV
EPS = 1e-6
SSM_TM = 512
SMALL_TILES = 4

ADAM_LR = 0.001
ADAM_B1 = 0.9
ADAM_B2 = 0.999
ADAM_EPS = 1e-08
ADAM_WD = 0.01
ADAM_STEP = 10

MESH_AXES = ("x", "y", "c")
MIB = 1024 * 1024


def _cp(sem=None, vmem_mib=None):
    kw = {}
    if sem is not None:
        kw["dimension_semantics"] = sem
    if vmem_mib is not None:
        kw["vmem_limit_bytes"] = vmem_mib * MIB
    return pltpu.CompilerParams(**kw)


def _nt(a, b):
    return lax.dot_general(a, b, (((1,), (1,)), ((), ())), preferred_element_type=F32)


def _tn_dot(a, b):
    return lax.dot_general(a, b, (((0,), (0,)), ((), ())), preferred_element_type=F32)


def _nn(a, b):
    return jnp.dot(a, b, preferred_element_type=F32)


def _sigmoid(t):
    return jax.nn.sigmoid(t)


def _prep_weight(w, rows_to, cols_to, name):
    _, k, n = w.shape

    def body(w_ref, o_ref):
        if rows_to != k or cols_to != n:
            o_ref[...] = jnp.zeros(o_ref.shape, BF16)
        o_ref[0, :k, :n] = w_ref[0].astype(BF16)

    return pl.pallas_call(
        body, grid=(DEPTH,),
        in_specs=[pl.BlockSpec((1, k, n), lambda l: (l, 0, 0))],
        out_specs=pl.BlockSpec((1, rows_to, cols_to), lambda l: (l, 0, 0)),
        out_shape=jax.ShapeDtypeStruct((DEPTH, rows_to, cols_to), BF16),
        compiler_params=_cp(("parallel",), 40), name=name)(w)


def _my_index():
    return 4 * lax.axis_index("x") + 2 * lax.axis_index("y") + lax.axis_index("c")


def _my_chip():
    return 2 * lax.axis_index("x") + lax.axis_index("y")


def _sibling():
    return (lax.axis_index("x"), lax.axis_index("y"), 1 - lax.axis_index("c"))


def _other_chip(j):
    return (jnp.bitwise_xor(lax.axis_index("x"), (j >> 1) & 1), jnp.bitwise_xor(lax.axis_index("y"), j & 1))


def _slab(ref, idx, width, axis):
    start = pl.multiple_of(idx * width, width)
    sl = [slice(None)] * len(ref.shape)
    sl[axis] = pl.ds(start, width)
    return ref.at[tuple(sl)]


def _remote(src, dst, ssem, rsem, device):
    return pltpu.make_async_remote_copy(src_ref=src, dst_ref=dst, send_sem=ssem, recv_sem=rsem,
                                        device_id=device, device_id_type=pl.DeviceIdType.MESH)


def _two_level_gather(srcs, blocks, ssem, rsem, lsem):
    nt = len(srcs)
    x, y, c = lax.axis_index("x"), lax.axis_index("y"), lax.axis_index("c")
    me = _my_index()
    local, sends = [], []
    for t in range(nt):
        mine = blocks[t](me)
        loc = pltpu.make_async_copy(srcs[t], mine, lsem.at[t])
        loc.start()
        local.append(loc)
        first = [_remote(srcs[t], mine, ssem.at[t, 0], rsem.at[t, 0], _sibling())]
        for j in range(1, N_CHIPS):
            first.append(_remote(srcs[t], mine, ssem.at[t, j], rsem.at[t, j], (*_other_chip(j), c)))
        for cp in first:
            cp.start()
        sends.extend(first)
    for t in range(nt):
        for j in range(1, N_CHIPS):
            ox, oy = _other_chip(j)
            landed = blocks[t](4 * ox + 2 * oy + c)
            _remote(landed, landed, ssem.at[t, j], rsem.at[t, j], _sibling()).wait_recv()
            fwd = _remote(landed, landed, ssem.at[t, 3 + j], rsem.at[t, 3 + j], _sibling())
            fwd.start()
            sends.append(fwd)
    for t in range(nt):
        got = blocks[t](4 * x + 2 * y + (1 - c))
        _remote(got, got, ssem.at[t, 0], rsem.at[t, 0], _sibling()).wait_recv()
        for j in range(1, N_CHIPS):
            ox, oy = _other_chip(j)
            got = blocks[t](4 * ox + 2 * oy + (1 - c))
            _remote(got, got, ssem.at[t, 3 + j], rsem.at[t, 3 + j], _sibling()).wait_recv()
    for cp in sends:
        cp.wait_send()
    for cp in local:
        cp.wait()


def _gather_sems(nt):
    return [pltpu.SemaphoreType.DMA((nt, N_DEV - 1)), pltpu.SemaphoreType.DMA((nt, N_DEV - 1)),
            pltpu.SemaphoreType.DMA((nt,))]


def _all_gather(shards, axes):
    nt = len(shards)

    def body(*refs):
        ins, outs = refs[:nt], refs[nt:2 * nt]
        ssem, rsem, lsem = refs[2 * nt:]
        blocks = [functools.partial(_slab, outs[t], width=shards[t].shape[axes[t]], axis=axes[t]) for t in range(nt)]
        _two_level_gather(ins, blocks, ssem, rsem, lsem)

    out_shape = []
    for t in range(nt):
        s = list(shards[t].shape)
        s[axes[t]] *= N_DEV
        out_shape.append(jax.ShapeDtypeStruct(tuple(s), shards[t].dtype))
    return pl.pallas_call(
        body,
        in_specs=[pl.BlockSpec(memory_space=pltpu.HBM)] * nt,
        out_specs=[pl.BlockSpec(memory_space=pltpu.HBM)] * nt,
        out_shape=out_shape, scratch_shapes=_gather_sems(nt),
        name="all_gather_weights")(*shards)


def _exchange_with_sibling(grads, axes):
    nt = len(grads)

    def body(*refs):
        ins = [refs[t * DEPTH:(t + 1) * DEPTH] for t in range(nt)]
        outs = refs[nt * DEPTH: nt * DEPTH + nt]
        ssem, rsem = refs[nt * DEPTH + nt:]
        c = lax.axis_index("c")
        for t in range(nt):
            width = grads[t][0].shape[axes[t]] // N_DEV
            for q in range(N_CHIPS):
                for l in range(DEPTH):
                    _remote(_slab(ins[t][l], 2 * q + (1 - c), width, axes[t]), outs[t].at[q, l],
                            ssem.at[t], rsem.at[t], _sibling()).start()
        for t in range(nt):
            _remote(outs[t], outs[t], ssem.at[t], rsem.at[t], _sibling()).wait()

    out_shape = []
    for t in range(nt):
        s = list(grads[t][0].shape)
        s[axes[t]] //= N_DEV
        out_shape.append(jax.ShapeDtypeStruct((N_CHIPS, DEPTH, s[0], s[1]), F32))
    flat = [g for per_type in grads for g in per_type]
    return pl.pallas_call(
        body,
        in_specs=[pl.BlockSpec(memory_space=pltpu.HBM)] * len(flat),
        out_specs=[pl.BlockSpec(memory_space=pltpu.HBM)] * nt,
        out_shape=out_shape,
        scratch_shapes=[pltpu.SemaphoreType.DMA((nt,)), pltpu.SemaphoreType.DMA((nt,))],
        name="grads_to_sibling")(*flat)


def _chip_sum(grad, got, layer, axis, core, name):
    _, _, r, c = got.shape
    tr = min(r, 512)

    def body(core_ref, g_ref, s_ref, o_ref):
        o_ref[0] = (g_ref[...] + s_ref[0, 0]).astype(BF16)

    if axis == 1:
        g_spec = pl.BlockSpec((tr, c), lambda q, i, core_ref: (i, 2 * q + core_ref[0]))
    else:
        g_spec = pl.BlockSpec((tr, c), lambda q, i, core_ref: ((2 * q + core_ref[0]) * (r // tr) + i, 0))
    return pl.pallas_call(
        body,
        grid_spec=pltpu.PrefetchScalarGridSpec(
            num_scalar_prefetch=1, grid=(N_CHIPS, r // tr),
            in_specs=[g_spec, pl.BlockSpec((1, 1, tr, c), lambda q, i, core_ref: (q, layer, i, 0))],
            out_specs=pl.BlockSpec((1, tr, c), lambda q, i, core_ref: (q, i, 0))),
        out_shape=jax.ShapeDtypeStruct((N_CHIPS, r, c), BF16),
        compiler_params=_cp(("parallel", "parallel"), 40), name=name)(core, grad, got)


def _exchange_chip_sums(sums, small):
    nt = len(sums)

    def body(*refs):
        ins = [refs[t * DEPTH:(t + 1) * DEPTH] for t in range(nt)]
        small_ref = refs[nt * DEPTH]
        outs = refs[nt * DEPTH + 1: nt * DEPTH + 1 + nt]
        small_out = refs[nt * DEPTH + 1 + nt]
        ssem, rsem, lsem, g_ssem, g_rsem, g_lsem = refs[nt * DEPTH + 2 + nt:]
        c = lax.axis_index("c")
        chip = _my_chip()
        for t in range(nt):
            for l in range(DEPTH):
                pltpu.make_async_copy(ins[t][l].at[chip], outs[t].at[chip, l], lsem.at[t]).start()
            for j in range(1, N_CHIPS):
                other = jnp.bitwise_xor(chip, j)
                for l in range(DEPTH):
                    _remote(ins[t][l].at[other], outs[t].at[chip, l], ssem.at[t, j - 1], rsem.at[t, j - 1],
                            (*_other_chip(j), c)).start()
        _two_level_gather([small_ref], [lambda idx: small_out.at[idx]], g_ssem, g_rsem, g_lsem)
        for t in range(nt):
            pltpu.make_async_copy(outs[t].at[chip], outs[t].at[chip], lsem.at[t]).wait()
            for j in range(1, N_CHIPS):
                other = jnp.bitwise_xor(chip, j)
                _remote(outs[t].at[other], outs[t].at[other], ssem.at[t, j - 1], rsem.at[t, j - 1],
                        (*_other_chip(j), c)).wait()

    out_shape = []
    for t in range(nt):
        _, r, c = sums[t][0].shape
        out_shape.append(jax.ShapeDtypeStruct((N_CHIPS, DEPTH, r, c), BF16))
    out_shape.append(jax.ShapeDtypeStruct((N_DEV,) + small.shape, F32))
    flat = [s for per_type in sums for s in per_type]
    return pl.pallas_call(
        body,
        in_specs=[pl.BlockSpec(memory_space=pltpu.HBM)] * (len(flat) + 1),
        out_specs=[pl.BlockSpec(memory_space=pltpu.HBM)] * (nt + 1),
        out_shape=out_shape,
        scratch_shapes=[pltpu.SemaphoreType.DMA((nt, N_CHIPS - 1)), pltpu.SemaphoreType.DMA((nt, N_CHIPS - 1)),
                        pltpu.SemaphoreType.DMA((nt,))] + _gather_sems(1),
        name="chip_sums_over_ici")(*flat, small)


def _adamw_math(w, g, m, v):
    m = ADAM_B1 * m + (1.0 - ADAM_B1) * g
    v = ADAM_B2 * v + (1.0 - ADAM_B2) * (g * g)
    m_hat = m / (1.0 - ADAM_B1 ** ADAM_STEP)
    v_hat = v / (1.0 - ADAM_B2 ** ADAM_STEP)
    delta = -ADAM_LR * (m_hat / (jnp.sqrt(v_hat) + ADAM_EPS) + ADAM_WD * w)
    return delta, m, v


def _adamw_big(recv, w, m, v, tk, name):
    _, k, n = w.shape
    npad = recv.shape[3]

    def body(r_ref, w_ref, m_ref, v_ref, g_out, d_out, m_out, v_out):
        g = r_ref[0, 0].astype(F32)
        for s in range(1, N_CHIPS):
            g = g + r_ref[s, 0].astype(F32)
        g = g[:, :n]
        delta, mn, vn = _adamw_math(w_ref[0], g, m_ref[0], v_ref[0])
        g_out[0] = g
        d_out[0] = delta
        m_out[0] = mn
        v_out[0] = vn

    blk = pl.BlockSpec((1, tk, n), lambda l, i: (l, i, 0))
    sds = jax.ShapeDtypeStruct(w.shape, F32)
    return pl.pallas_call(
        body, grid=(DEPTH, k // tk),
        in_specs=[pl.BlockSpec((N_CHIPS, 1, tk, npad), lambda l, i: (0, l, i, 0)), blk, blk, blk],
        out_specs=[blk, blk, blk, blk], out_shape=[sds, sds, sds, sds],
        compiler_params=_cp(("parallel", "parallel"), 48), name=name)(recv, w, m, v)


def _adamw_small(recv, w, m, v):
    rows = w.shape[0]
    tr = rows // SMALL_TILES

    def body(r_ref, w_ref, m_ref, v_ref, g_out, d_out, m_out, v_out):
        g = r_ref[0]
        for s in range(1, N_DEV):
            g = g + r_ref[s]
        delta, mn, vn = _adamw_math(w_ref[...], g, m_ref[...], v_ref[...])
        g_out[...] = g
        d_out[...] = delta
        m_out[...] = mn
        v_out[...] = vn

    blk = pl.BlockSpec((tr, LANES), lambda i: (i, 0))
    sds = jax.ShapeDtypeStruct(w.shape, F32)
    return pl.pallas_call(
        body, grid=(SMALL_TILES,),
        in_specs=[pl.BlockSpec((N_DEV, tr, LANES), lambda i: (0, i, 0)), blk, blk, blk],
        out_specs=[blk, blk, blk, blk], out_shape=[sds, sds, sds, sds],
        compiler_params=_cp(("parallel",), 40), name="adamw_small")(recv, w, m, v)


def _rms(t):
    return lax.rsqrt(jnp.mean(t * t, axis=-1, keepdims=True) + EPS)


def _rms_bwd(t, r, gain, dh, dres):
    u = dh * gain
    dt = dres + r * u - t * ((r * r * r) * (1.0 / D_MODEL) * jnp.sum(t * u, axis=-1, keepdims=True))
    return dt, dh * t * r


def _in_proj(x, gain, w):
    L = x.shape[0]
    n = w.shape[1]
    tm, tn = 512, 1024

    def body(x_ref, g_ref, w_ref, z_ref, h_ref):
        @pl.when(pl.program_id(1) == 0)
        def _():
            t = x_ref[...]
            h_ref[...] = (t * _rms(t) * g_ref[...]).astype(BF16)
        z_ref[...] = _nn(h_ref[...], w_ref[...])

    return pl.pallas_call(
        body, grid=(L // tm, n // tn),
        in_specs=[pl.BlockSpec((tm, D_MODEL), lambda i, j: (i, 0)), pl.BlockSpec((1, D_MODEL), lambda i, j: (0, 0)),
                  pl.BlockSpec((D_MODEL, tn), lambda i, j: (0, j))],
        out_specs=[pl.BlockSpec((tm, tn), lambda i, j: (i, j)), pl.BlockSpec((tm, D_MODEL), lambda i, j: (i, 0))],
        out_shape=[jax.ShapeDtypeStruct((L, n), F32), jax.ShapeDtypeStruct((L, D_MODEL), BF16)],
        compiler_params=_cp(("parallel", "arbitrary"), 40), name="in_proj")(x, gain, w)


def _in_proj_bwd(pieces, w, x, gain, dres):
    L = x.shape[0]
    tm = 512
    nk = len(pieces)

    def body(*refs):
        dz_refs = refs[:nk]
        w_ref, x_ref, g_ref, dr_ref, dx_ref, dg_ref, acc = refs[nk:]
        i, k = pl.program_id(0), pl.program_id(1)

        @pl.when(k == 0)
        def _():
            acc[...] = jnp.zeros_like(acc)

        for kk in range(nk):
            @pl.when(k == kk)
            def _(kk=kk):
                acc[...] += _nt(dz_refs[kk][...], w_ref[...])

        @pl.when(k == nk - 1)
        def _():
            t = x_ref[...]
            dt, dgt = _rms_bwd(t, _rms(t), g_ref[...], acc[...], dr_ref[...])
            dx_ref[...] = dt

            @pl.when(i == 0)
            def _():
                dg_ref[...] = jnp.zeros_like(dg_ref)
            dg_ref[...] += jnp.sum(dgt, axis=0, keepdims=True)

    row = pl.BlockSpec((tm, D_MODEL), lambda i, k: (i, 0))
    vec = pl.BlockSpec((1, D_MODEL), lambda i, k: (0, 0))
    return pl.pallas_call(
        body, grid=(L // tm, nk),
        in_specs=[row] * nk + [pl.BlockSpec((D_MODEL, D_MODEL), lambda i, k: (0, k)), row, vec, row],
        out_specs=[row, vec],
        out_shape=[jax.ShapeDtypeStruct((L, D_MODEL), F32), jax.ShapeDtypeStruct((1, D_MODEL), F32)],
        scratch_shapes=[pltpu.VMEM((tm, D_MODEL), F32)],
        compiler_params=_cp(("arbitrary", "arbitrary"), 56), name="in_proj_bwd")(*pieces, w, x, gain, dres)


def _tn(a, b, name):
    m, na = a.shape
    nb = b.shape[1]
    ta, tb, tm = min(na, 1024), min(nb, 1024), 1024
    nm = m // tm

    def body(a_ref, b_ref, o_ref):
        @pl.when(pl.program_id(2) == 0)
        def _():
            o_ref[...] = jnp.zeros_like(o_ref)
        o_ref[...] += _tn_dot(a_ref[...].astype(BF16), b_ref[...].astype(BF16))

    return pl.pallas_call(
        body, grid=(na // ta, nb // tb, nm),
        in_specs=[pl.BlockSpec((tm, ta), lambda i, j, k: (k, i)), pl.BlockSpec((tm, tb), lambda i, j, k: (k, j))],
        out_specs=pl.BlockSpec((ta, tb), lambda i, j, k: (i, j)),
        out_shape=jax.ShapeDtypeStruct((na, nb), F32),
        compiler_params=_cp(("parallel", "parallel", "arbitrary"), 48), name=name)(a, b)


def _tn_pieces(a, pieces, name):
    m, na = a.shape
    npc = len(pieces)
    tb, tm = D_MODEL, 1024
    nm = m // tm

    def body(*refs):
        a_ref = refs[0]
        b_refs = refs[1:1 + npc]
        o_ref = refs[1 + npc]
        j = pl.program_id(0)

        @pl.when(pl.program_id(1) == 0)
        def _():
            o_ref[...] = jnp.zeros_like(o_ref)

        for jj in range(npc):
            @pl.when(j == jj)
            def _(jj=jj):
                o_ref[...] += _tn_dot(a_ref[...], b_refs[jj][...])

    piece_specs = [pl.BlockSpec((tm, tb), functools.partial(lambda j, k, jj: (jnp.where(j == jj, k, 0), 0), jj=jj))
                   for jj in range(npc)]
    return pl.pallas_call(
        body, grid=(npc, nm),
        in_specs=[pl.BlockSpec((tm, na), lambda j, k: (k, 0))] + piece_specs,
        out_specs=pl.BlockSpec((na, tb), lambda j, k: (0, j)),
        out_shape=jax.ShapeDtypeStruct((na, tb * npc), F32),
        compiler_params=_cp(("parallel", "arbitrary"), 56), name=name)(a, *pieces)


def _loss_grad(xf, target):
    L = xf.shape[0]
    tm = 1024

    def body(x_ref, t_ref, dy_ref, l_ref):
        e = x_ref[...] - t_ref[...]
        dy_ref[...] = e * (1.0 / D_MODEL)

        @pl.when(pl.program_id(0) == 0)
        def _():
            l_ref[...] = jnp.zeros_like(l_ref)
        l_ref[...] += jnp.sum(jnp.sum(e * e, axis=1, keepdims=True), axis=0, keepdims=True) * (0.5 / D_MODEL)

    row = pl.BlockSpec((tm, D_MODEL), lambda i: (i, 0))
    return pl.pallas_call(
        body, grid=(L // tm,), in_specs=[row, row],
        out_specs=[row, pl.BlockSpec((1, 1), lambda i: (0, 0))],
        out_shape=[jax.ShapeDtypeStruct((L, D_MODEL), F32), jax.ShapeDtypeStruct((1, 1), F32)],
        compiler_params=_cp(("arbitrary",), 40), name="loss_grad")(xf, target)


def _ffn_fwd(x, gain, wg, wu, wd):
    L = x.shape[0]
    ff = wg.shape[1]
    tm, tf = 512, 512
    nf = ff // tf

    def body(x_ref, g_ref, wg_ref, wu_ref, wd_ref, o_ref, h_scr, acc):
        c = pl.program_id(1)

        @pl.when(c == 0)
        def _():
            t = x_ref[...]
            h_scr[...] = (t * _rms(t) * g_ref[...]).astype(BF16)
            acc[...] = jnp.zeros_like(acc)

        h = h_scr[...]
        gate = _nn(h, wg_ref[...])
        up = _nn(h, wu_ref[...])
        hid = gate * _sigmoid(gate) * up
        acc[...] += _nn(hid.astype(BF16), wd_ref[...])

        @pl.when(c == nf - 1)
        def _():
            o_ref[...] = x_ref[...] + acc[...]

    row = pl.BlockSpec((tm, D_MODEL), lambda i, c: (i, 0))
    return pl.pallas_call(
        body, grid=(L // tm, nf),
        in_specs=[row, pl.BlockSpec((1, D_MODEL), lambda i, c: (0, 0)),
                  pl.BlockSpec((D_MODEL, tf), lambda i, c: (0, c)), pl.BlockSpec((D_MODEL, tf), lambda i, c: (0, c)),
                  pl.BlockSpec((tf, D_MODEL), lambda i, c: (c, 0))],
        out_specs=row, out_shape=jax.ShapeDtypeStruct((L, D_MODEL), F32),
        scratch_shapes=[pltpu.VMEM((tm, D_MODEL), BF16), pltpu.VMEM((tm, D_MODEL), F32)],
        compiler_params=_cp(("parallel", "arbitrary"), 40), name="ffn_fwd")(x, gain, wg, wu, wd)


def _ffn_bwd(x, gain, wg, wu, wd, dxo):
    L = x.shape[0]
    ff = wg.shape[1]
    tm, tf = 512, 512
    nf = ff // tf

    def body(x_ref, g_ref, wg_ref, wu_ref, wd_ref, dxo_ref, dx_ref, h_ref, hid_ref, dgate_ref, dup_ref, dg_ref,
             acc, dxo_b):
        i, c = pl.program_id(0), pl.program_id(1)

        @pl.when(c == 0)
        def _():
            t = x_ref[...]
            h_ref[...] = (t * _rms(t) * g_ref[...]).astype(BF16)
            acc[...] = jnp.zeros_like(acc)
            dxo_b[...] = dxo_ref[...].astype(BF16)

        h = h_ref[...]
        gate = _nn(h, wg_ref[...])
        up = _nn(h, wu_ref[...])
        sg = _sigmoid(gate)
        silu = gate * sg
        hid_ref[...] = (silu * up).astype(BF16)
        dhid = _nt(dxo_b[...], wd_ref[...])
        dup = (dhid * silu).astype(BF16)
        dgate = (dhid * up * (sg * (1.0 + gate * (1.0 - sg)))).astype(BF16)
        dup_ref[...] = dup
        dgate_ref[...] = dgate
        acc[...] += _nt(dgate, wg_ref[...]) + _nt(dup, wu_ref[...])

        @pl.when(c == nf - 1)
        def _():
            t = x_ref[...]
            dt, dgt = _rms_bwd(t, _rms(t), g_ref[...], acc[...], dxo_ref[...])
            dx_ref[...] = dt

            @pl.when(i == 0)
            def _():
                dg_ref[...] = jnp.zeros_like(dg_ref)
            dg_ref[...] += jnp.sum(dgt, axis=0, keepdims=True)

    row = pl.BlockSpec((tm, D_MODEL), lambda i, c: (i, 0))
    vec = pl.BlockSpec((1, D_MODEL), lambda i, c: (0, 0))
    wcol = pl.BlockSpec((D_MODEL, tf), lambda i, c: (0, c))
    hcol = pl.BlockSpec((tm, tf), lambda i, c: (i, c))
    return pl.pallas_call(
        body, grid=(L // tm, nf),
        in_specs=[row, vec, wcol, wcol, pl.BlockSpec((tf, D_MODEL), lambda i, c: (c, 0)), row],
        out_specs=[row, row, hcol, hcol, hcol, vec],
        out_shape=[jax.ShapeDtypeStruct((L, D_MODEL), F32), jax.ShapeDtypeStruct((L, D_MODEL), BF16),
                   jax.ShapeDtypeStruct((L, ff), BF16), jax.ShapeDtypeStruct((L, ff), BF16),
                   jax.ShapeDtypeStruct((L, ff), BF16), jax.ShapeDtypeStruct((1, D_MODEL), F32)],
        scratch_shapes=[pltpu.VMEM((tm, D_MODEL), F32), pltpu.VMEM((tm, D_MODEL), BF16)],
        compiler_params=_cp(("arbitrary", "arbitrary"), 48), name="ffn_bwd")(x, gain, wg, wu, wd, dxo)


GELU_K = math.sqrt(2.0 / math.pi)
GELU_C = 0.044715


def _gelu(y):
    return 0.5 * y * (1.0 + jnp.tanh(GELU_K * (y + GELU_C * (y * y * y))))


def _gelu_grad(y):
    th = jnp.tanh(GELU_K * (y + GELU_C * (y * y * y)))
    return 0.5 * (1.0 + th) + 0.5 * y * (1.0 - th * th) * (GELU_K * (1.0 + 3.0 * GELU_C * (y * y)))


def _merge_groups(o_refs, l_refs):
    ls = [r[...] for r in l_refs]
    os_ = [r[...] for r in o_refs]
    lmax = jnp.maximum(jnp.maximum(ls[0], ls[1]), ls[2])
    es = [jnp.exp(l - lmax) for l in ls]
    inv = 1.0 / (es[0] + es[1] + es[2])
    ws = [e * inv for e in es]
    a = ws[0] * os_[0] + ws[1] * os_[1] + ws[2] * os_[2]
    return ws, os_, a


def _mix_fwd(ols, y, z, x, wp, wa, wb, wo):
    L = x.shape[0]
    tm = 256

    def body(o0, l0, o1, l1, o2, l2, y_ref, ga_ref, gs_ref, x_ref, wp_ref, wa_ref, wb_ref, wo_ref, out_ref):
        _, _, a = _merge_groups((o0, o1, o2), (l0, l1, l2))
        a_out = _nn(a.astype(BF16), wp_ref[...])
        yg = _gelu(y_ref[...]).astype(BF16)
        s_out = _nn(yg, wa_ref[...]) * _sigmoid(_nn(yg, wb_ref[...]))
        mix = _sigmoid(ga_ref[...]) * a_out + _sigmoid(gs_ref[...]) * s_out
        out_ref[...] = x_ref[...] + _nn(mix.astype(BF16), wo_ref[...])

    half = pl.BlockSpec((tm, ATTN_W), lambda i: (i, 0))
    row = pl.BlockSpec((tm, D_MODEL), lambda i: (i, 0))
    w512 = pl.BlockSpec((ATTN_W, D_MODEL), lambda i: (0, 0))
    return pl.pallas_call(
        body, grid=(L // tm,),
        in_specs=[half] * 7 + [pl.BlockSpec((tm, D_MODEL), lambda i: (i, 5)),
                               pl.BlockSpec((tm, D_MODEL), lambda i: (i, 6)), row, w512, w512, w512,
                               pl.BlockSpec((D_MODEL, D_MODEL), lambda i: (0, 0))],
        out_specs=row, out_shape=jax.ShapeDtypeStruct((L, D_MODEL), F32),
        compiler_params=_cp(("parallel",), 48), name="mix_fwd")(*ols, y, z, z, x, wp, wa, wb, wo)


def _mix_bwd(dxm, ols, y, z, wp, wa, wb, wo):
    L = dxm.shape[0]
    tm = 256

    def body(dx_ref, o0, l0, o1, l1, o2, l2, y_ref, ga_ref, gs_ref, wp_ref, wa_ref, wb_ref, wo_ref,
             do0, dl0, do1, dl1, do2, dl2, dy_ref, dga_ref, dgs_ref, a_ref, yg_ref, mix_ref, dao_ref, dpa_ref,
             dpb_ref):
        ws, os_, a = _merge_groups((o0, o1, o2), (l0, l1, l2))
        ab = a.astype(BF16)
        a_out = _nn(ab, wp_ref[...])
        yv = y_ref[...]
        yg = _gelu(yv).astype(BF16)
        pa = _nn(yg, wa_ref[...])
        spb = _sigmoid(_nn(yg, wb_ref[...]))
        s_out = pa * spb
        sga = _sigmoid(ga_ref[...])
        sgs = _sigmoid(gs_ref[...])
        mix = sga * a_out + sgs * s_out
        dmix = _nt(dx_ref[...].astype(BF16), wo_ref[...])
        da_out = (sga * dmix).astype(BF16)
        ds_out = sgs * dmix
        dpa = (ds_out * spb).astype(BF16)
        dpb = (ds_out * pa * spb * (1.0 - spb)).astype(BF16)
        dga_ref[...] = (dmix * a_out * sga * (1.0 - sga)).astype(BF16)
        dgs_ref[...] = (dmix * s_out * sgs * (1.0 - sgs)).astype(BF16)
        dy_ref[...] = (_nt(dpa, wa_ref[...]) + _nt(dpb, wb_ref[...])) * _gelu_grad(yv)
        da = _nt(da_out, wp_ref[...])
        for w, o, do_ref, dl_ref in zip(ws, os_, (do0, do1, do2), (dl0, dl1, dl2)):
            do_ref[...] = w * da
            dl_ref[...] = da * w * (o - a)
        a_ref[...] = ab
        yg_ref[...] = yg
        mix_ref[...] = mix.astype(BF16)
        dao_ref[...] = da_out
        dpa_ref[...] = dpa
        dpb_ref[...] = dpb

    half = pl.BlockSpec((tm, ATTN_W), lambda i: (i, 0))
    row = pl.BlockSpec((tm, D_MODEL), lambda i: (i, 0))
    w512 = pl.BlockSpec((ATTN_W, D_MODEL), lambda i: (0, 0))
    hf = jax.ShapeDtypeStruct((L, ATTN_W), F32)
    hb = jax.ShapeDtypeStruct((L, ATTN_W), BF16)
    rb = jax.ShapeDtypeStruct((L, D_MODEL), BF16)
    return pl.pallas_call(
        body, grid=(L // tm,),
        in_specs=[row] + [half] * 7 + [pl.BlockSpec((tm, D_MODEL), lambda i: (i, 5)),
                                       pl.BlockSpec((tm, D_MODEL), lambda i: (i, 6)), w512, w512, w512,
                                       pl.BlockSpec((D_MODEL, D_MODEL), lambda i: (0, 0))],
        out_specs=[half] * 7 + [row, row, half, half, row, row, row, row],
        out_shape=[hf] * 7 + [rb, rb, hb, hb, rb, rb, rb, rb],
        compiler_params=_cp(("parallel",), 56), name="mix_bwd")(dxm, *ols, y, z, z, wp, wa, wb, wo)


N_ATTN_ITERS = ATTN_ROWS // BLK


def _class_rows(ref, start, d):
    if d == 1:
        return ref[pl.ds(pl.multiple_of(start, BLK), BLK), :]
    return ref[pl.ds(start, BLK, stride=d), :]


def _set_class_rows(ref, start, d, val):
    if d == 1:
        ref[pl.ds(pl.multiple_of(start, BLK), BLK), :] = val
    else:
        ref[pl.ds(start, BLK, stride=d), :] = val


def _head_masks():
    lane = lax.broadcasted_iota(jnp.int32, (1, LANES), 1)
    m0 = (lane < HEAD_DIM).astype(F32)
    return m0, 1.0 - m0


def _head_norm(t, gain2, m0, m1):
    tt = t * t
    r0 = lax.rsqrt(jnp.sum(tt * m0, axis=-1, keepdims=True) * (1.0 / HEAD_DIM) + EPS)
    r1 = lax.rsqrt(jnp.sum(tt * m1, axis=-1, keepdims=True) * (1.0 / HEAD_DIM) + EPS)
    r = m0 * r0 + m1 * r1
    return t * r * gain2, r


def _head_norm_bwd(t, r, gain2, dy, m0, m1):
    u = dy * gain2
    tu = t * u
    s = m0 * jnp.sum(tu * m0, axis=-1, keepdims=True) + m1 * jnp.sum(tu * m1, axis=-1, keepdims=True)
    return r * u - t * (r * r * r) * s * (1.0 / HEAD_DIM), jnp.sum(dy * t * r, axis=0, keepdims=True)


def _band_masks():
    qi = lax.broadcasted_iota(jnp.int32, (BLK, 2 * BLK), 0)
    ki = lax.broadcasted_iota(jnp.int32, (BLK, 2 * BLK), 1)
    dist = BLK + qi - ki
    return (dist >= 0) & (dist <= BLK), ki >= BLK


ATTN_SCALE = HEAD_DIM ** -0.5


def _attn_scores(qm, kw, ok):
    return jnp.where(ok, _nt(qm, kw), -1e30)


def _attn_probs(qm, kw, ok):
    s = _attn_scores(qm, kw, ok)
    mx = jnp.max(s, axis=-1, keepdims=True)
    p = jnp.exp(s - mx)
    den = jnp.sum(p, axis=-1, keepdims=True)
    return p, den, mx


NORM_ROWS = 256


def _norm_rows(src_ref, gain2, dst_ref, m0, m1):
    n = src_ref.shape[0]
    step = min(NORM_ROWS, n)
    for r0 in range(0, n, step):
        dst_ref[r0:r0 + step, :] = _head_norm(src_ref[r0:r0 + step, :], gain2, m0, m1)[0]


def _norm_rows_bwd(src_ref, gain2, dy_ref, dst_ref, m0, m1):
    n = src_ref.shape[0]
    step = min(NORM_ROWS, n)
    dgain = jnp.zeros((1, LANES), F32)
    for r0 in range(0, n, step):
        t = src_ref[r0:r0 + step, :]
        _, r = _head_norm(t, gain2, m0, m1)
        dt, dg = _head_norm_bwd(t, r, gain2, dy_ref[r0:r0 + step, :], m0, m1)
        dst_ref[r0:r0 + step, :] = dt
        dgain = dgain + dg
    return dgain


def _attn_operands(it, d, first_step, q_ref, kc_ref, kp_ref, vc_ref, vp_ref, band, is_cur):
    j = it // d
    start = (it - j * d) + (d * BLK) * j
    before = jnp.maximum(start - d * BLK, 0)
    inside = j > 0
    q2 = _class_rows(q_ref, start, d)
    kc2 = _class_rows(kc_ref, start, d)
    vc2 = _class_rows(vc_ref, start, d)
    kp2 = jnp.where(inside, _class_rows(kc_ref, before, d), _class_rows(kp_ref, it - j * d, d))
    vp2 = jnp.where(inside, _class_rows(vc_ref, before, d), _class_rows(vp_ref, it - j * d, d))
    has_prev = inside | jnp.logical_not(first_step)
    return start, q2, kp2, kc2, vp2, vc2, band & (is_cur | has_prev)


def _attn_specs(d, step_of):
    nq = N_ATTN_ITERS // d

    def cur(c):
        return pl.BlockSpec((ATTN_ROWS, LANES), lambda hp, n: (step_of(n), c + hp))

    def prev(c):
        return pl.BlockSpec((d * BLK, LANES), lambda hp, n: (jnp.maximum(step_of(n) * nq - 1, 0), c + hp))

    return cur, prev, pl.BlockSpec((1, LANES), lambda hp, n: (0, 0))


def _attn_fwd(z, gq2, gk2, group):
    L = z.shape[0]
    d = DILATIONS[group]
    nsb = L // ATTN_ROWS
    cq, ck, cv = group * 4, 12 + group * 4, 24 + group * 4

    def body(q_ref, kc_ref, kp_ref, vc_ref, vp_ref, gq_ref, gk_ref, o_ref, l_ref, qn_scr, kn_scr, kpn_scr):
        first_step = pl.program_id(1) == 0
        band, is_cur = _band_masks()
        m0, m1 = _head_masks()
        _norm_rows(q_ref, gq_ref[...], qn_scr, m0, m1)
        _norm_rows(kc_ref, gk_ref[...], kn_scr, m0, m1)
        _norm_rows(kp_ref, gk_ref[...], kpn_scr, m0, m1)

        def per_block(it, carry):
            start, qn, kpn, kcn, vp2, vc2, ok = _attn_operands(
                it, d, first_step, qn_scr, kn_scr, kpn_scr, vc_ref, vp_ref, band, is_cur)
            kw = jnp.concatenate([kpn, kcn], axis=0).astype(BF16)
            vw = jnp.concatenate([vp2, vc2], axis=0).astype(BF16)
            o2 = jnp.zeros((BLK, LANES), F32)
            l2 = jnp.zeros((BLK, LANES), F32)
            for mh in (m0, m1):
                p, den, mx = _attn_probs((qn * (mh * ATTN_SCALE)).astype(BF16), kw, ok)
                o2 = o2 + mh * (_nn(p.astype(BF16), vw) / den)
                l2 = l2 + mh * (mx + jnp.log(den))
            _set_class_rows(o_ref, start, d, o2)
            _set_class_rows(l_ref, start, d, l2)
            return carry

        lax.fori_loop(0, N_ATTN_ITERS, per_block, 0, unroll=2)

    cur, prev, vec = _attn_specs(d, lambda n: n)
    out = pl.BlockSpec((ATTN_ROWS, LANES), lambda hp, n: (n, hp))
    sds = jax.ShapeDtypeStruct((L, ATTN_W), F32)
    return pl.pallas_call(
        body, grid=(4, nsb),
        in_specs=[cur(cq), cur(ck), prev(ck), cur(cv), prev(cv), vec, vec],
        out_specs=[out, out], out_shape=[sds, sds],
        scratch_shapes=[pltpu.VMEM((ATTN_ROWS, LANES), F32), pltpu.VMEM((ATTN_ROWS, LANES), F32),
                        pltpu.VMEM((d * BLK, LANES), F32)],
        compiler_params=_cp(("parallel", "arbitrary"), 48), name=f"attn_fwd_g{group}")(z, z, z, z, z, gq2, gk2)


def _attn_bwd(z, gq2, gk2, o, lse, do, dl, group):
    L = z.shape[0]
    d = DILATIONS[group]
    nsb = L // ATTN_ROWS
    cq, ck, cv = group * 4, 12 + group * 4, 24 + group * 4

    def body(q_ref, kc_ref, kp_ref, vc_ref, vp_ref, gq_ref, gk_ref, o_ref, l_ref, do_ref, dl_ref,
             dq_ref, dk_ref, dv_ref, dgq_ref, dgk_ref, ck_scr, cv_scr, qn_scr, kn_scr, kpn_scr, dqn_scr, dkn_scr):
        hp, n = pl.program_id(0), pl.program_id(1)
        first_step = n == nsb - 1
        band, is_cur = _band_masks()
        m0, m1 = _head_masks()
        gq, gk = gq_ref[...], gk_ref[...]
        _norm_rows(q_ref, gq, qn_scr, m0, m1)
        _norm_rows(kc_ref, gk, kn_scr, m0, m1)
        _norm_rows(kp_ref, gk, kpn_scr, m0, m1)

        @pl.when((hp == 0) & (n == 0))
        def _():
            dgq_ref[...] = jnp.zeros_like(dgq_ref)
            dgk_ref[...] = jnp.zeros_like(dgk_ref)

        @pl.when(n == 0)
        def _():
            ck_scr[...] = jnp.zeros_like(ck_scr)
            cv_scr[...] = jnp.zeros_like(cv_scr)

        def per_block(i, carry):
            it = N_ATTN_ITERS - 1 - i
            start, qn, kpn, kcn, vp2, vc2, ok = _attn_operands(
                it, d, first_step, qn_scr, kn_scr, kpn_scr, vc_ref, vp_ref, band, is_cur)
            r = it - (it // d) * d
            kw = jnp.concatenate([kpn, kcn], axis=0).astype(BF16)
            vw = jnp.concatenate([vp2, vc2], axis=0).astype(BF16)
            l2 = _class_rows(l_ref, start, d)
            c2 = _class_rows(dl_ref, start, d) - _class_rows(do_ref, start, d) * _class_rows(o_ref, start, d)
            do2 = _class_rows(do_ref, start, d)
            dqn = jnp.zeros((BLK, LANES), F32)
            dkw = jnp.zeros((2 * BLK, LANES), F32)
            dvw = jnp.zeros((2 * BLK, LANES), F32)
            for mh in (m0, m1):
                qm = (qn * (mh * ATTN_SCALE)).astype(BF16)
                lse = jnp.max(jnp.where(mh > 0.5, l2, -3e38), axis=-1, keepdims=True)
                pn = jnp.exp(_attn_scores(qm, kw, ok) - lse)
                dohb = (do2 * mh).astype(BF16)
                dvw = dvw + _tn_dot(pn.astype(BF16), dohb)
                ds = (pn * (_nt(dohb, vw) + jnp.sum(c2 * mh, axis=-1, keepdims=True))).astype(BF16)
                dqn = dqn + (mh * ATTN_SCALE) * _nn(ds, kw)
                dkw = dkw + _tn_dot(ds, qm)
            _set_class_rows(dqn_scr, start, d, dqn)
            _set_class_rows(dkn_scr, start, d, ck_scr[r] + dkw[BLK:])
            _set_class_rows(dv_ref, start, d, cv_scr[r] + dvw[BLK:])
            ck_scr[r] = dkw[:BLK]
            cv_scr[r] = dvw[:BLK]
            return carry

        lax.fori_loop(0, N_ATTN_ITERS, per_block, 0, unroll=2)
        dgq_ref[...] += _norm_rows_bwd(q_ref, gq, dqn_scr, dq_ref, m0, m1)
        dgk_ref[...] += _norm_rows_bwd(kc_ref, gk, dkn_scr, dk_ref, m0, m1)

    cur, prev, vec = _attn_specs(d, lambda n: nsb - 1 - n)
    sds = jax.ShapeDtypeStruct((L, ATTN_W), F32)
    vsd = jax.ShapeDtypeStruct((1, LANES), F32)
    return pl.pallas_call(
        body, grid=(4, nsb),
        in_specs=[cur(cq), cur(ck), prev(ck), cur(cv), prev(cv), vec, vec, cur(0), cur(0), cur(0), cur(0)],
        out_specs=[cur(0), cur(0), cur(0), vec, vec], out_shape=[sds, sds, sds, vsd, vsd],
        scratch_shapes=[pltpu.VMEM((d, BLK, LANES), F32), pltpu.VMEM((d, BLK, LANES), F32),
                        pltpu.VMEM((ATTN_ROWS, LANES), F32), pltpu.VMEM((ATTN_ROWS, LANES), F32),
                        pltpu.VMEM((d * BLK, LANES), F32),
                        pltpu.VMEM((ATTN_ROWS, LANES), F32), pltpu.VMEM((ATTN_ROWS, LANES), F32)],
        compiler_params=_cp(("arbitrary", "arbitrary"), 56),
        name=f"attn_bwd_g{group}")(z, z, z, z, z, gq2, gk2, o, lse, do, dl)


BLOCK_STATES = SSM_STATES // SSM_BLOCKS
BLOCK_CH = SSM_W // SSM_BLOCKS
SLABS_PER_BLOCK = BLOCK_STATES // LANES


SCAN_STEPS = 4


def _store_block(bufs, b, val, tm):
    for s in range(SLABS_PER_BLOCK):
        k = SLABS_PER_BLOCK * b + s
        bufs[k % 2][pl.ds(8 + k // 2, tm, stride=8), :] = val[:, s * LANES:(s + 1) * LANES]


def _load_block(bufs, b, tm):
    tiles = []
    for s in range(SLABS_PER_BLOCK):
        k = SLABS_PER_BLOCK * b + s
        tiles.append(bufs[k % 2][pl.ds(8 + k // 2, tm, stride=8), :])
    return jnp.concatenate(tiles, axis=1).astype(BF16)


def _ssm_project_in(ub, bdr_ref, bdi_ref, sr, si, tm):
    for b in range(SSM_BLOCKS):
        ubb = ub[:, b * BLOCK_CH:(b + 1) * BLOCK_CH]
        _store_block(sr, b, _nn(ubb, bdr_ref[b]), tm)
        _store_block(si, b, _nn(ubb, bdi_ref[b]), tm)


def _ssm_scan(a, x0, sr, si, tm):
    ar0, ar1, ai0, ai1 = a
    sr[0][0:8, :], sr[1][0:8, :], si[0][0:8, :], si[1][0:8, :] = x0

    def steps(it, c):
        xr0, xr1, xi0, xi1 = c
        base = it * (8 * SCAN_STEPS) + 8
        for q in range(SCAN_STEPS):
            rows = pl.ds(pl.multiple_of(base + 8 * q, 8), 8)
            nr0 = ar0 * xr0 - ai0 * xi0 + sr[0][rows, :]
            ni0 = ar0 * xi0 + ai0 * xr0 + si[0][rows, :]
            nr1 = ar1 * xr1 - ai1 * xi1 + sr[1][rows, :]
            ni1 = ar1 * xi1 + ai1 * xr1 + si[1][rows, :]
            sr[0][rows, :] = nr0
            si[0][rows, :] = ni0
            sr[1][rows, :] = nr1
            si[1][rows, :] = ni1
            xr0, xr1, xi0, xi1 = nr0, nr1, ni0, ni1
        return xr0, xr1, xi0, xi1

    return lax.fori_loop(0, tm // SCAN_STEPS, steps, x0)


def _load_a(ar_ref, ai_ref):
    return ar_ref[:, :LANES], ar_ref[:, LANES:], ai_ref[:, :LANES], ai_ref[:, LANES:]


def _ssm_fwd(z, ar8, ai8, bdr, bdi, cdr, cdi, dsk):
    L = z.shape[0]
    tm = SSM_TM
    nc = L // tm

    def body(u_ref, ar_ref, ai_ref, bdr_ref, bdi_ref, cdr_ref, cdi_ref, dsk_ref, y_ref, cin_ref,
             sr0, sr1, si0, si1, car):
        sr, si = (sr0, sr1), (si0, si1)

        @pl.when(pl.program_id(0) == 0)
        def _():
            car[...] = jnp.zeros_like(car)

        u = u_ref[...]
        _ssm_project_in(u.astype(BF16), bdr_ref, bdi_ref, sr, si, tm)
        cin_ref[0] = car[...]
        xr0, xr1, xi0, xi1 = _ssm_scan(_load_a(ar_ref, ai_ref), (car[0], car[1], car[2], car[3]), sr, si, tm)
        car[0], car[1], car[2], car[3] = xr0, xr1, xi0, xi1
        for b in range(SSM_BLOCKS):
            cols = slice(b * BLOCK_CH, (b + 1) * BLOCK_CH)
            y_ref[:, cols] = (dsk_ref[:, cols] * u[:, cols] + _nn(_load_block(sr, b, tm), cdr_ref[b])
                              - _nn(_load_block(si, b, tm), cdi_ref[b]))

    def const(shape):
        return pl.BlockSpec(shape, lambda i: (0,) * len(shape))

    state = pltpu.VMEM(((tm + 1) * 8, LANES), F32)
    wb = const((SSM_BLOCKS, BLOCK_CH, BLOCK_STATES))
    wc = const((SSM_BLOCKS, BLOCK_STATES, BLOCK_CH))
    return pl.pallas_call(
        body, grid=(nc,),
        in_specs=[pl.BlockSpec((tm, SSM_W), lambda i: (i, COL_U)), const((8, 256)), const((8, 256)),
                  wb, wb, wc, wc, const((1, SSM_W))],
        out_specs=[pl.BlockSpec((tm, SSM_W), lambda i: (i, 0)), pl.BlockSpec((1, 4, 8, LANES), lambda i: (i, 0, 0, 0))],
        out_shape=[jax.ShapeDtypeStruct((L, SSM_W), F32), jax.ShapeDtypeStruct((nc, 4, 8, LANES), F32)],
        scratch_shapes=[state, state, state, state, pltpu.VMEM((4, 8, LANES), F32)],
        compiler_params=_cp(("arbitrary",), 48), name="ssm_fwd")(z, ar8, ai8, bdr, bdi, cdr, cdi, dsk)


def _ssm_bwd(z, dy, cin, ar8, ai8, bdr, bdi, cdr, cdi, dsk):
    L = z.shape[0]
    tm = SSM_TM
    nc = L // tm

    def body(u_ref, dy_ref, cin_ref, ar_ref, ai_ref, dsk_ref, bdr_ref, bdi_ref, cdr_ref, cdi_ref,
             du_ref, da_ref, dds_ref, dbdr_ref, dbdi_ref, dcdr_ref, dcdi_ref,
             sr0, sr1, si0, si1, gr0, gr1, gi0, gi1, carg):
        sr, si, gr, gi = (sr0, sr1), (si0, si1), (gr0, gr1), (gi0, gi1)

        @pl.when(pl.program_id(0) == 0)
        def _():
            carg[...] = jnp.zeros_like(carg)
            for ref in (da_ref, dds_ref, dbdr_ref, dbdi_ref, dcdr_ref, dcdi_ref):
                ref[...] = jnp.zeros_like(ref)

        u = u_ref[...]
        ub = u.astype(BF16)
        dyv = dy_ref[...]
        dyb = dyv.astype(BF16)
        a = _load_a(ar_ref, ai_ref)
        ar0, ar1, ai0, ai1 = a
        x_in = (cin_ref[0, 0], cin_ref[0, 1], cin_ref[0, 2], cin_ref[0, 3])
        _ssm_project_in(ub, bdr_ref, bdi_ref, sr, si, tm)
        _ssm_scan(a, x_in, sr, si, tm)
        for b in range(SSM_BLOCKS):
            dyb_b = dyb[:, b * BLOCK_CH:(b + 1) * BLOCK_CH]
            _store_block(gr, b, _nt(dyb_b, cdr_ref[b]), tm)
            _store_block(gi, b, -_nt(dyb_b, cdi_ref[b]), tm)

        def grad_steps(it, c):
            (nr0, nr1, ni0, ni1), (d_r0, d_r1, d_i0, d_i1) = c
            base = (tm - SCAN_STEPS * (it + 1)) * 8
            for q in reversed(range(SCAN_STEPS)):
                prev = pl.ds(pl.multiple_of(base + 8 * q, 8), 8)
                rows = pl.ds(pl.multiple_of(base + 8 * q + 8, 8), 8)
                g_r0 = gr[0][rows, :] + ar0 * nr0 + ai0 * ni0
                g_i0 = gi[0][rows, :] + ar0 * ni0 - ai0 * nr0
                g_r1 = gr[1][rows, :] + ar1 * nr1 + ai1 * ni1
                g_i1 = gi[1][rows, :] + ar1 * ni1 - ai1 * nr1
                gr[0][rows, :] = g_r0
                gi[0][rows, :] = g_i0
                gr[1][rows, :] = g_r1
                gi[1][rows, :] = g_i1
                pr0, pr1, pi0, pi1 = sr[0][prev, :], sr[1][prev, :], si[0][prev, :], si[1][prev, :]
                d_r0 = d_r0 + pr0 * g_r0 + pi0 * g_i0
                d_r1 = d_r1 + pr1 * g_r1 + pi1 * g_i1
                d_i0 = d_i0 + pr0 * g_i0 - pi0 * g_r0
                d_i1 = d_i1 + pr1 * g_i1 - pi1 * g_r1
                nr0, nr1, ni0, ni1 = g_r0, g_r1, g_i0, g_i1
            return (nr0, nr1, ni0, ni1), (d_r0, d_r1, d_i0, d_i1)

        acc0 = (da_ref[0], da_ref[1], da_ref[2], da_ref[3])
        g_first, acc = lax.fori_loop(0, tm // SCAN_STEPS, grad_steps,
                                     ((carg[0], carg[1], carg[2], carg[3]), acc0))
        carg[0], carg[1], carg[2], carg[3] = g_first
        da_ref[0], da_ref[1], da_ref[2], da_ref[3] = acc

        for b in range(SSM_BLOCKS):
            cols = slice(b * BLOCK_CH, (b + 1) * BLOCK_CH)
            grb, gib = _load_block(gr, b, tm), _load_block(gi, b, tm)
            du_ref[:, cols] = dsk_ref[:, cols] * dyv[:, cols] + _nt(grb, bdr_ref[b]) + _nt(gib, bdi_ref[b])
            dbdr_ref[b] += _tn_dot(ub[:, cols], grb)
            dbdi_ref[b] += _tn_dot(ub[:, cols], gib)
            dcdr_ref[b] += _tn_dot(_load_block(sr, b, tm), dyb[:, cols])
            dcdi_ref[b] -= _tn_dot(_load_block(si, b, tm), dyb[:, cols])
        dds_ref[...] += jnp.sum(dyv * u, axis=0, keepdims=True)

    def const(shape):
        return pl.BlockSpec(shape, lambda i: (0,) * len(shape))

    state = pltpu.VMEM(((tm + 1) * 8, LANES), F32)
    wb = const((SSM_BLOCKS, BLOCK_CH, BLOCK_STATES))
    wc = const((SSM_BLOCKS, BLOCK_STATES, BLOCK_CH))
    return pl.pallas_call(
        body, grid=(nc,),
        in_specs=[pl.BlockSpec((tm, SSM_W), lambda i: (nc - 1 - i, COL_U)),
                  pl.BlockSpec((tm, SSM_W), lambda i: (nc - 1 - i, 0)),
                  pl.BlockSpec((1, 4, 8, LANES), lambda i: (nc - 1 - i, 0, 0, 0)),
                  const((8, 256)), const((8, 256)), const((1, SSM_W)), wb, wb, wc, wc],
        out_specs=[pl.BlockSpec((tm, SSM_W), lambda i: (nc - 1 - i, 0)), const((4, 8, LANES)), const((1, SSM_W)),
                   wb, wb, wc, wc],
        out_shape=[jax.ShapeDtypeStruct((L, SSM_W), F32), jax.ShapeDtypeStruct((4, 8, LANES), F32),
                   jax.ShapeDtypeStruct((1, SSM_W), F32),
                   jax.ShapeDtypeStruct((SSM_BLOCKS, BLOCK_CH, BLOCK_STATES), F32),
                   jax.ShapeDtypeStruct((SSM_BLOCKS, BLOCK_CH, BLOCK_STATES), F32),
                   jax.ShapeDtypeStruct((SSM_BLOCKS, BLOCK_STATES, BLOCK_CH), F32),
                   jax.ShapeDtypeStruct((SSM_BLOCKS, BLOCK_STATES, BLOCK_CH), F32)],
        scratch_shapes=[state] * 8 + [pltpu.VMEM((4, 8, LANES), F32)],
        compiler_params=_cp(("arbitrary",), 56), name="ssm_bwd")(z, dy, cin, ar8, ai8, dsk, bdr, bdi, cdr, cdi)


def _discretise(lam_re, lam_im, log_dt, b_re, b_im):
    dt = jnp.exp(log_dt)[:, None]
    mag = jnp.exp(lam_re * dt)
    ang = lam_im * dt
    abar_re = mag * jnp.cos(ang)
    abar_im = mag * jnp.sin(ang)
    nr = abar_re - 1.0
    ni = abar_im
    den = lam_re * lam_re + lam_im * lam_im
    cr = ((nr * lam_re + ni * lam_im) / den)[..., None]
    ci = ((ni * lam_re - nr * lam_im) / den)[..., None]
    return abar_re, abar_im, cr * b_re - ci * b_im, cr * b_im + ci * b_re


GROUPS_PER_BLOCK = 8


def _block_diag_in(bbar):
    eye = jnp.eye(GROUPS_PER_BLOCK, dtype=F32)
    return jnp.einsum("igpc,gh->igchp", bbar.reshape(SSM_BLOCKS, GROUPS_PER_BLOCK, 64, 16), eye).reshape(
        SSM_BLOCKS, BLOCK_CH, BLOCK_STATES)


def _block_diag_in_t(blocks):
    eye = jnp.eye(GROUPS_PER_BLOCK, dtype=F32)
    return jnp.einsum("igchp,gh->igpc", blocks.reshape(SSM_BLOCKS, GROUPS_PER_BLOCK, 16, GROUPS_PER_BLOCK, 64),
                      eye).reshape(32, 64, 16)


def _block_diag_out(c):
    eye = jnp.eye(GROUPS_PER_BLOCK, dtype=F32)
    return jnp.einsum("igcp,gh->igphc", c.reshape(SSM_BLOCKS, GROUPS_PER_BLOCK, 16, 64), eye).reshape(
        SSM_BLOCKS, BLOCK_STATES, BLOCK_CH)


def _block_diag_out_t(blocks):
    eye = jnp.eye(GROUPS_PER_BLOCK, dtype=F32)
    return jnp.einsum("igphc,gh->igcp", blocks.reshape(SSM_BLOCKS, GROUPS_PER_BLOCK, 64, GROUPS_PER_BLOCK, 16),
                      eye).reshape(32, 16, 64)


SMALL_NAMES = ("g_mix", "g_q", "g_k", "lambda_re", "lambda_im", "log_dt", "b_re", "b_im", "c_re", "c_im",
               "d_skip", "g_ffn")


def _pack_small(parts):
    flat = jnp.concatenate([parts[n].reshape(-1) for n in SMALL_NAMES])
    pad = (-flat.shape[0]) % (8 * LANES * SMALL_TILES)
    return jnp.pad(flat, (0, pad)).reshape(-1, LANES)


def _unpack_small(packed, like):
    flat = packed.reshape(-1)
    out, off = {}, 0
    for n in SMALL_NAMES:
        size = like[n].size
        out[n] = flat[off:off + size].reshape(like[n].shape)
        off += size
    return out


BIG_NAMES = ("w_in", "w_attn_proj", "w_glu_a", "w_glu_b", "w_out", "w_ffn_gate", "w_ffn_up", "w_ffn_down")
BIG_SHARD_AXIS = {"w_in": 2, "w_attn_proj": 2, "w_glu_a": 2, "w_glu_b": 2, "w_out": 1,
                  "w_ffn_gate": 2, "w_ffn_up": 2, "w_ffn_down": 1}
ADAMW_ROWS = {"w_in": 256, "w_attn_proj": 512, "w_glu_a": 512, "w_glu_b": 512, "w_out": 128,
              "w_ffn_gate": 256, "w_ffn_up": 256, "w_ffn_down": 176}


def kernel(x, g_mix, w_in, g_q, g_k, w_attn_proj, lambda_re, lambda_im, log_dt, b_re, b_im, c_re, c_im, d_skip, w_glu_a, w_glu_b, w_out, g_ffn, w_ffn_gate, w_ffn_up, w_ffn_down, loss_target, m_g_mix, m_w_in, m_g_q, m_g_k, m_w_attn_proj, m_lambda_re, m_lambda_im, m_log_dt, m_b_re, m_b_im, m_c_re, m_c_im, m_d_skip, m_w_glu_a, m_w_glu_b, m_w_out, m_g_ffn, m_w_ffn_gate, m_w_ffn_up, m_w_ffn_down, v_g_mix, v_w_in, v_g_q, v_g_k, v_w_attn_proj, v_lambda_re, v_lambda_im, v_log_dt, v_b_re, v_b_im, v_c_re, v_c_im, v_d_skip, v_w_glu_a, v_w_glu_b, v_w_out, v_g_ffn, v_w_ffn_gate, v_w_ffn_up, v_w_ffn_down):
    args = dict(locals())
    weights = {n: args[n] for n in BIG_NAMES + SMALL_NAMES}
    moments_m = {n: args["m_" + n] for n in BIG_NAMES + SMALL_NAMES}
    moments_v = {n: args["v_" + n] for n in BIG_NAMES + SMALL_NAMES}
    x0 = x[0]
    target = loss_target[0]

    shards = []
    for n in BIG_NAMES:
        w = weights[n]
        rows_to, cols_to = w.shape[1], w.shape[2]
        if n in ("w_ffn_gate", "w_ffn_up"):
            cols_to = FF_SHARD_PAD
        if n == "w_ffn_down":
            rows_to = FF_SHARD_PAD
        shards.append(_prep_weight(w, rows_to, cols_to, "prep_" + n))
    full = dict(zip(BIG_NAMES, _all_gather(shards, [BIG_SHARD_AXIS[n] for n in BIG_NAMES])))

    saved = []
    xl = x0
    for l in range(DEPTH):
        abar_re, abar_im, bb_re, bb_im = _discretise(lambda_re[l], lambda_im[l], log_dt[l], b_re[l], b_im[l])
        ssm = dict(ar8=abar_re.reshape(8, 256), ai8=abar_im.reshape(8, 256),
                   bdr=_block_diag_in(bb_re).astype(BF16), bdi=_block_diag_in(bb_im).astype(BF16),
                   cdr=_block_diag_out(c_re[l]).astype(BF16), cdi=_block_diag_out(c_im[l]).astype(BF16),
                   dsk=d_skip[l][None])
        gq2 = jnp.tile(g_q[l], 2)[None]
        gk2 = jnp.tile(g_k[l], 2)[None]
        z, h = _in_proj(xl, g_mix[l][None], full["w_in"][l])
        ols = []
        for g in range(N_GROUPS):
            ols.extend(_attn_fwd(z, gq2, gk2, g))
        y, cin = _ssm_fwd(z, **ssm)
        xm = _mix_fwd(ols, y, z, xl, full["w_attn_proj"][l], full["w_glu_a"][l], full["w_glu_b"][l], full["w_out"][l])
        xo = _ffn_fwd(xm, g_ffn[l][None], full["w_ffn_gate"][l], full["w_ffn_up"][l], full["w_ffn_down"][l])
        saved.append(dict(x=xl, z=z, h=h, ols=ols, y=y, cin=cin, xm=xm, ssm=ssm, gq2=gq2, gk2=gk2))
        xl = xo

    dxo, loss_local = _loss_grad(xl, target)
    loss = lax.psum(loss_local[0, 0], MESH_AXES)
    big_grads = {n: [None] * DEPTH for n in BIG_NAMES}
    small_grads = {n: [None] * DEPTH for n in SMALL_NAMES}
    for l in reversed(range(DEPTH)):
        s = saved[l]
        dxm, h2, hid, dgate, dup, dgffn = _ffn_bwd(s["xm"], g_ffn[l][None], full["w_ffn_gate"][l],
                                                   full["w_ffn_up"][l], full["w_ffn_down"][l], dxo)
        big_grads["w_ffn_down"][l] = _tn(hid, dxo, "grad_w_ffn_down")
        big_grads["w_ffn_gate"][l] = _tn(h2, dgate, "grad_w_ffn_gate")
        big_grads["w_ffn_up"][l] = _tn(h2, dup, "grad_w_ffn_up")
        (do0, dl0, do1, dl1, do2, dl2, dy, dga, dgs, a_b, yg_b, mix_b, dao_b, dpa_b, dpb_b) = _mix_bwd(
            dxm, s["ols"], s["y"], s["z"], full["w_attn_proj"][l], full["w_glu_a"][l], full["w_glu_b"][l],
            full["w_out"][l])
        big_grads["w_out"][l] = _tn(mix_b, dxm, "grad_w_out")
        big_grads["w_attn_proj"][l] = _tn(a_b, dao_b, "grad_w_attn_proj")
        big_grads["w_glu_a"][l] = _tn(yg_b, dpa_b, "grad_w_glu_a")
        big_grads["w_glu_b"][l] = _tn(yg_b, dpb_b, "grad_w_glu_b")
        du, da4, ddsk, dbdr, dbdi, dcdr, dcdi = _ssm_bwd(s["z"], dy, s["cin"], **s["ssm"])
        dqkv = []
        dgq = jnp.zeros((1, LANES), F32)
        dgk = jnp.zeros((1, LANES), F32)
        for g, (do_g, dl_g) in enumerate(((do0, dl0), (do1, dl1), (do2, dl2))):
            dq, dk, dv, dgq_g, dgk_g = _attn_bwd(s["z"], s["gq2"], s["gk2"], s["ols"][2 * g], s["ols"][2 * g + 1],
                                                 do_g, dl_g, g)
            dqkv.append((dq, dk, dv))
            dgq, dgk = dgq + dgq_g, dgk + dgk_g
        def cat(a, b):
            return jnp.concatenate([a, b], axis=1).astype(BF16)

        pieces = [cat(dqkv[0][0], dqkv[1][0]), cat(dqkv[2][0], dqkv[0][1]), cat(dqkv[1][1], dqkv[2][1]),
                  cat(dqkv[0][2], dqkv[1][2]), cat(dqkv[2][2], du), dga, dgs]
        dxo, dgmix = _in_proj_bwd(pieces, full["w_in"][l], s["x"], g_mix[l][None], dxm)
        big_grads["w_in"][l] = _tn_pieces(s["h"], pieces, "grad_w_in")
        _, disc_vjp = jax.vjp(_discretise, lambda_re[l], lambda_im[l], log_dt[l], b_re[l], b_im[l])
        dar = jnp.concatenate([da4[0], da4[1]], axis=1).reshape(32, 64)
        dai = jnp.concatenate([da4[2], da4[3]], axis=1).reshape(32, 64)
        dlr, dli, dldt, dbre, dbim = disc_vjp((dar, dai, _block_diag_in_t(dbdr), _block_diag_in_t(dbdi)))
        small_grads["g_mix"][l] = dgmix[0]
        small_grads["g_q"][l] = dgq[0, :HEAD_DIM] + dgq[0, HEAD_DIM:]
        small_grads["g_k"][l] = dgk[0, :HEAD_DIM] + dgk[0, HEAD_DIM:]
        small_grads["lambda_re"][l] = dlr
        small_grads["lambda_im"][l] = dli
        small_grads["log_dt"][l] = dldt
        small_grads["b_re"][l] = dbre
        small_grads["b_im"][l] = dbim
        small_grads["c_re"][l] = _block_diag_out_t(dcdr)
        small_grads["c_im"][l] = _block_diag_out_t(dcdi)
        small_grads["d_skip"][l] = ddsk[0]
        small_grads["g_ffn"][l] = dgffn[0]
    grad_x = dxo[None]

    small_local = {n: jnp.stack(small_grads[n]) for n in SMALL_NAMES}
    rs_axes = [BIG_SHARD_AXIS[n] - 1 for n in BIG_NAMES]
    got = _exchange_with_sibling([big_grads[n] for n in BIG_NAMES], rs_axes)
    core = lax.axis_index("c").astype(jnp.int32).reshape(1)
    sums = [[_chip_sum(big_grads[n][l], got[t], l, rs_axes[t], core, "chip_sum_" + n) for l in range(DEPTH)]
            for t, n in enumerate(BIG_NAMES)]
    recv = _exchange_chip_sums(sums, _pack_small(small_local))
    out_g, out_d, out_m, out_v = {}, {}, {}, {}
    for n, r in zip(BIG_NAMES, recv[:-1]):
        out_g[n], out_d[n], out_m[n], out_v[n] = _adamw_big(r, weights[n], moments_m[n], moments_v[n],
                                                            ADAMW_ROWS[n], "adamw_" + n)
    like = {n: weights[n] for n in SMALL_NAMES}
    packed = _adamw_small(recv[-1], _pack_small(like), _pack_small({n: moments_m[n] for n in SMALL_NAMES}),
                          _pack_small({n: moments_v[n] for n in SMALL_NAMES}))
    for dst, p in zip((out_g, out_d, out_m, out_v), packed):
        dst.update(_unpack_small(p, like))

    order = ("g_mix", "w_in", "g_q", "g_k", "w_attn_proj", "lambda_re", "lambda_im", "log_dt", "b_re", "b_im",
             "c_re", "c_im", "d_skip", "w_glu_a", "w_glu_b", "w_out", "g_ffn", "w_ffn_gate", "w_ffn_up",
             "w_ffn_down")
    return (loss, grad_x, *[out_g[n] for n in order], *[out_d[n] for n in order],
            *[out_m[n] for n in order], *[out_v[n] for n in order])
```

```python
import functools
import math

import jax
import jax.numpy as jnp
from jax import lax
from jax.experimental import pallas as pl
from jax.experimental.pallas import tpu as pltpu

F32 = jnp.float32
BF16 = jnp.bfloat16

D_MODEL = 1024
DEPTH = 4
N_DEV = 8
N_CHIPS = 4
HEAD_DIM = 64
BLK = 128
LANES = 128
ATTN_W = 512
N_GROUPS = 3
DILATIONS = (1, 4, 16)
ATTN_ROWS = 2048
SSM_W = 512
SSM_STATES = 2048
SSM_BLOCKS = 4
IN_COLS = 7168
COL_U = 9
D_FF = 2816
FF_SHARD = D_FF // N_DEV
FF_SHARD_PAD = 384
FF_PAD = FF_SHARD_PAD * N_DEV
EPS = 1e-6
SSM_TM = 512
SMALL_TILES = 4

ADAM_LR = 0.001
ADAM_B1 = 0.9
ADAM_B2 = 0.999
ADAM_EPS = 1e-08
ADAM_WD = 0.01
ADAM_STEP = 10

MESH_AXES = ("x", "y", "c")
MIB = 1024 * 1024


def _cp(sem=None, vmem_mib=None):
    kw = {}
    if sem is not None:
        kw["dimension_semantics"] = sem
    if vmem_mib is not None:
        kw["vmem_limit_bytes"] = vmem_mib * MIB
    return pltpu.CompilerParams(**kw)


def _nt(a, b):
    return lax.dot_general(a, b, (((1,), (1,)), ((), ())), preferred_element_type=F32)


def _tn_dot(a, b):
    return lax.dot_general(a, b, (((0,), (0,)), ((), ())), preferred_element_type=F32)


def _nn(a, b):
    return jnp.dot(a, b, preferred_element_type=F32)


def _sigmoid(t):
    return jax.nn.sigmoid(t)


def _prep_weight(w, rows_to, cols_to, name):
    _, k, n = w.shape

    def body(w_ref, o_ref):
        if rows_to != k or cols_to != n:
            o_ref[...] = jnp.zeros(o_ref.shape, BF16)
        o_ref[0, :k, :n] = w_ref[0].astype(BF16)

    return pl.pallas_call(
        body, grid=(DEPTH,),
        in_specs=[pl.BlockSpec((1, k, n), lambda l: (l, 0, 0))],
        out_specs=pl.BlockSpec((1, rows_to, cols_to), lambda l: (l, 0, 0)),
        out_shape=jax.ShapeDtypeStruct((DEPTH, rows_to, cols_to), BF16),
        compiler_params=_cp(("parallel",), 40), name=name)(w)


def _my_index():
    return 4 * lax.axis_index("x") + 2 * lax.axis_index("y") + lax.axis_index("c")


def _my_chip():
    return 2 * lax.axis_index("x") + lax.axis_index("y")


def _sibling():
    return (lax.axis_index("x"), lax.axis_index("y"), 1 - lax.axis_index("c"))


def _other_chip(j):
    return (jnp.bitwise_xor(lax.axis_index("x"), (j >> 1) & 1), jnp.bitwise_xor(lax.axis_index("y"), j & 1))


def _slab(ref, idx, width, axis):
    start = pl.multiple_of(idx * width, width)
    sl = [slice(None)] * len(ref.shape)
    sl[axis] = pl.ds(start, width)
    return ref.at[tuple(sl)]


def _remote(src, dst, ssem, rsem, device):
    return pltpu.make_async_remote_copy(src_ref=src, dst_ref=dst, send_sem=ssem, recv_sem=rsem,
                                        device_id=device, device_id_type=pl.DeviceIdType.MESH)


def _two_level_gather(srcs, blocks, ssem, rsem, lsem):
    nt = len(srcs)
    x, y, c = lax.axis_index("x"), lax.axis_index("y"), lax.axis_index("c")
    me = _my_index()
    local, sends = [], []
    for t in range(nt):
        mine = blocks[t](me)
        loc = pltpu.make_async_copy(srcs[t], mine, lsem.at[t])
        loc.start()
        local.append(loc)
        first = [_remote(srcs[t], mine, ssem.at[t, 0], rsem.at[t, 0], _sibling())]
        for j in range(1, N_CHIPS):
            first.append(_remote(srcs[t], mine, ssem.at[t, j], rsem.at[t, j], (*_other_chip(j), c)))
        for cp in first:
            cp.start()
        sends.extend(first)
    for t in range(nt):
        for j in range(1, N_CHIPS):
            ox, oy = _other_chip(j)
            landed = blocks[t](4 * ox + 2 * oy + c)
            _remote(landed, landed, ssem.at[t, j], rsem.at[t, j], _sibling()).wait_recv()
            fwd = _remote(landed, landed, ssem.at[t, 3 + j], rsem.at[t, 3 + j], _sibling())
            fwd.start()
            sends.append(fwd)
    for t in range(nt):
        got = blocks[t](4 * x + 2 * y + (1 - c))
        _remote(got, got, ssem.at[t, 0], rsem.at[t, 0], _sibling()).wait_recv()
        for j in range(1, N_CHIPS):
            ox, oy = _other_chip(j)
            got = blocks[t](4 * ox + 2 * oy + (1 - c))
            _remote(got, got, ssem.at[t, 3 + j], rsem.at[t, 3 + j], _sibling()).wait_recv()
    for cp in sends:
        cp.wait_send()
    for cp in local:
        cp.wait()


def _gather_sems(nt):
    return [pltpu.SemaphoreType.DMA((nt, N_DEV - 1)), pltpu.SemaphoreType.DMA((nt, N_DEV - 1)),
            pltpu.SemaphoreType.DMA((nt,))]


def _all_gather(shards, axes):
    nt = len(shards)

    def body(*refs):
        ins, outs = refs[:nt], refs[nt:2 * nt]
        ssem, rsem, lsem = refs[2 * nt:]
        blocks = [functools.partial(_slab, outs[t], width=shards[t].shape[axes[t]], axis=axes[t]) for t in range(nt)]
        _two_level_gather(ins, blocks, ssem, rsem, lsem)

    out_shape = []
    for t in range(nt):
        s = list(shards[t].shape)
        s[axes[t]] *= N_DEV
        out_shape.append(jax.ShapeDtypeStruct(tuple(s), shards[t].dtype))
    return pl.pallas_call(
        body,
        in_specs=[pl.BlockSpec(memory_space=pltpu.HBM)] * nt,
        out_specs=[pl.BlockSpec(memory_space=pltpu.HBM)] * nt,
        out_shape=out_shape, scratch_shapes=_gather_sems(nt),
        name="all_gather_weights")(*shards)


def _exchange_with_sibling(grads, axes):
    nt = len(grads)

    def body(*refs):
        ins = [refs[t * DEPTH:(t + 1) * DEPTH] for t in range(nt)]
        outs = refs[nt * DEPTH: nt * DEPTH + nt]
        ssem, rsem = refs[nt * DEPTH + nt:]
        c = lax.axis_index("c")
        for t in range(nt):
            width = grads[t][0].shape[axes[t]] // N_DEV
            for q in range(N_CHIPS):
                for l in range(DEPTH):
                    _remote(_slab(ins[t][l], 2 * q + (1 - c), width, axes[t]), outs[t].at[q, l],
                            ssem.at[t], rsem.at[t], _sibling()).start()
        for t in range(nt):
            _remote(outs[t], outs[t], ssem.at[t], rsem.at[t], _sibling()).wait()

    out_shape = []
    for t in range(nt):
        s = list(grads[t][0].shape)
        s[axes[t]] //= N_DEV
        out_shape.append(jax.ShapeDtypeStruct((N_CHIPS, DEPTH, s[0], s[1]), F32))
    flat = [g for per_type in grads for g in per_type]
    return pl.pallas_call(
        body,
        in_specs=[pl.BlockSpec(memory_space=pltpu.HBM)] * len(flat),
        out_specs=[pl.BlockSpec(memory_space=pltpu.HBM)] * nt,
        out_shape=out_shape,
        scratch_shapes=[pltpu.SemaphoreType.DMA((nt,)), pltpu.SemaphoreType.DMA((nt,))],
        name="grads_to_sibling")(*flat)


def _chip_sum(grad, got, layer, axis, core, name):
    _, _, r, c = got.shape
    tr = min(r, 512)

    def body(core_ref, g_ref, s_ref, o_ref):
        o_ref[0] = (g_ref[...] + s_ref[0, 0]).astype(BF16)

    if axis == 1:
        g_spec = pl.BlockSpec((tr, c), lambda q, i, core_ref: (i, 2 * q + core_ref[0]))
    else:
        g_spec = pl.BlockSpec((tr, c), lambda q, i, core_ref: ((2 * q + core_ref[0]) * (r // tr) + i, 0))
    return pl.pallas_call(
        body,
        grid_spec=pltpu.PrefetchScalarGridSpec(
            num_scalar_prefetch=1, grid=(N_CHIPS, r // tr),
            in_specs=[g_spec, pl.BlockSpec((1, 1, tr, c), lambda q, i, core_ref: (q, layer, i, 0))],
            out_specs=pl.BlockSpec((1, tr, c), lambda q, i, core_ref: (q, i, 0))),
        out_shape=jax.ShapeDtypeStruct((N_CHIPS, r, c), BF16),
        compiler_params=_cp(("parallel", "parallel"), 40), name=name)(core, grad, got)


def _exchange_chip_sums(sums, small):
    nt = len(sums)

    def body(*refs):
        ins = [refs[t * DEPTH:(t + 1) * DEPTH] for t in range(nt)]
        small_ref = refs[nt * DEPTH]
        outs = refs[nt * DEPTH + 1: nt * DEPTH + 1 + nt]
        small_out = refs[nt * DEPTH + 1 + nt]
        ssem, rsem, lsem, g_ssem, g_rsem, g_lsem = refs[nt * DEPTH + 2 + nt:]
        c = lax.axis_index("c")
        chip = _my_chip()
        for t in range(nt):
            for l in range(DEPTH):
                pltpu.make_async_copy(ins[t][l].at[chip], outs[t].at[chip, l], lsem.at[t]).start()
            for j in range(1, N_CHIPS):
                other = jnp.bitwise_xor(chip, j)
                for l in range(DEPTH):
                    _remote(ins[t][l].at[other], outs[t].at[chip, l], ssem.at[t, j - 1], rsem.at[t, j - 1],
                            (*_other_chip(j), c)).start()
        _two_level_gather([small_ref], [lambda idx: small_out.at[idx]], g_ssem, g_rsem, g_lsem)
        for t in range(nt):
            pltpu.make_async_copy(outs[t].at[chip], outs[t].at[chip], lsem.at[t]).wait()
            for j in range(1, N_CHIPS):
                other = jnp.bitwise_xor(chip, j)
                _remote(outs[t].at[other], outs[t].at[other], ssem.at[t, j - 1], rsem.at[t, j - 1],
                        (*_other_chip(j), c)).wait()

    out_shape = []
    for t in range(nt):
        _, r, c = sums[t][0].shape
        out_shape.append(jax.ShapeDtypeStruct((N_CHIPS, DEPTH, r, c), BF16))
    out_shape.append(jax.ShapeDtypeStruct((N_DEV,) + small.shape, F32))
    flat = [s for per_type in sums for s in per_type]
    return pl.pallas_call(
        body,
        in_specs=[pl.BlockSpec(memory_space=pltpu.HBM)] * (len(flat) + 1),
        out_specs=[pl.BlockSpec(memory_space=pltpu.HBM)] * (nt + 1),
        out_shape=out_shape,
        scratch_shapes=[pltpu.SemaphoreType.DMA((nt, N_CHIPS - 1)), pltpu.SemaphoreType.DMA((nt, N_CHIPS - 1)),
                        pltpu.SemaphoreType.DMA((nt,))] + _gather_sems(1),
        name="chip_sums_over_ici")(*flat, small)


def _adamw_math(w, g, m, v):
    m = ADAM_B1 * m + (1.0 - ADAM_B1) * g
    v = ADAM_B2 * v + (1.0 - ADAM_B2) * (g * g)
    m_hat = m / (1.0 - ADAM_B1 ** ADAM_STEP)
    v_hat = v / (1.0 - ADAM_B2 ** ADAM_STEP)
    delta = -ADAM_LR * (m_hat / (jnp.sqrt(v_hat) + ADAM_EPS) + ADAM_WD * w)
    return delta, m, v


def _adamw_big(recv, w, m, v, tk, name):
    _, k, n = w.shape
    npad = recv.shape[3]

    def body(r_ref, w_ref, m_ref, v_ref, g_out, d_out, m_out, v_out):
        g = r_ref[0, 0].astype(F32)
        for s in range(1, N_CHIPS):
            g = g + r_ref[s, 0].astype(F32)
        g = g[:, :n]
        delta, mn, vn = _adamw_math(w_ref[0], g, m_ref[0], v_ref[0])
        g_out[0] = g
        d_out[0] = delta
        m_out[0] = mn
        v_out[0] = vn

    blk = pl.BlockSpec((1, tk, n), lambda l, i: (l, i, 0))
    sds = jax.ShapeDtypeStruct(w.shape, F32)
    return pl.pallas_call(
        body, grid=(DEPTH, k // tk),
        in_specs=[pl.BlockSpec((N_CHIPS, 1, tk, npad), lambda l, i: (0, l, i, 0)), blk, blk, blk],
        out_specs=[blk, blk, blk, blk], out_shape=[sds, sds, sds, sds],
        compiler_params=_cp(("parallel", "parallel"), 48), name=name)(recv, w, m, v)


def _adamw_small(recv, w, m, v):
    rows = w.shape[0]
    tr = rows // SMALL_TILES

    def body(r_ref, w_ref, m_ref, v_ref, g_out, d_out, m_out, v_out):
        g = r_ref[0]
        for s in range(1, N_DEV):
            g = g + r_ref[s]
        delta, mn, vn = _adamw_math(w_ref[...], g, m_ref[...], v_ref[...])
        g_out[...] = g
        d_out[...] = delta
        m_out[...] = mn
        v_out[...] = vn

    blk = pl.BlockSpec((tr, LANES), lambda i: (i, 0))
    sds = jax.ShapeDtypeStruct(w.shape, F32)
    return pl.pallas_call(
        body, grid=(SMALL_TILES,),
        in_specs=[pl.BlockSpec((N_DEV, tr, LANES), lambda i: (0, i, 0)), blk, blk, blk],
        out_specs=[blk, blk, blk, blk], out_shape=[sds, sds, sds, sds],
        compiler_params=_cp(("parallel",), 40), name="adamw_small")(recv, w, m, v)


def _rms(t):
    return lax.rsqrt(jnp.mean(t * t, axis=-1, keepdims=True) + EPS)


def _rms_bwd(t, r, gain, dh, dres):
    u = dh * gain
    dt = dres + r * u - t * ((r * r * r) * (1.0 / D_MODEL) * jnp.sum(t * u, axis=-1, keepdims=True))
    return dt, dh * t * r


def _in_proj(x, gain, w):
    L = x.shape[0]
    n = w.shape[1]
    tm, tn = 1024, 1024

    def body(x_ref, g_ref, w_ref, z_ref, h_ref):
        @pl.when(pl.program_id(1) == 0)
        def _():
            t = x_ref[...]
            h_ref[...] = (t * _rms(t) * g_ref[...]).astype(BF16)
        z_ref[...] = _nn(h_ref[...], w_ref[...])

    return pl.pallas_call(
        body, grid=(L // tm, n // tn),
        in_specs=[pl.BlockSpec((tm, D_MODEL), lambda i, j: (i, 0)), pl.BlockSpec((1, D_MODEL), lambda i, j: (0, 0)),
                  pl.BlockSpec((D_MODEL, tn), lambda i, j: (0, j))],
        out_specs=[pl.BlockSpec((tm, tn), lambda i, j: (i, j)), pl.BlockSpec((tm, D_MODEL), lambda i, j: (i, 0))],
        out_shape=[jax.ShapeDtypeStruct((L, n), F32), jax.ShapeDtypeStruct((L, D_MODEL), BF16)],
        compiler_params=_cp(("parallel", "arbitrary"), 40), name="in_proj")(x, gain, w)


PIECE_W = 512


def _piece_columns(pieces):
    cols = []
    for p, arr in enumerate(pieces):
        cols.extend((p, off) for off in range(0, arr.shape[1], PIECE_W))
    return cols


def _in_proj_bwd(pieces, w, x, gain, dres):
    L = x.shape[0]
    tm = 512
    npc = len(pieces)
    cols = _piece_columns(pieces)
    nk = len(cols)

    def body(*refs):
        dz_refs = refs[:npc]
        w_ref, x_ref, g_ref, dr_ref, dx_ref, dg_ref, acc = refs[npc:]
        i, k = pl.program_id(0), pl.program_id(1)

        @pl.when(k == 0)
        def _():
            acc[...] = jnp.zeros_like(acc)

        for kk, (p, off) in enumerate(cols):
            @pl.when(k == kk)
            def _(p=p, off=off):
                acc[...] += _nt(dz_refs[p][:, off:off + PIECE_W].astype(BF16), w_ref[...])

        @pl.when(k == nk - 1)
        def _():
            t = x_ref[...]
            dt, dgt = _rms_bwd(t, _rms(t), g_ref[...], acc[...], dr_ref[...])
            dx_ref[...] = dt

            @pl.when(i == 0)
            def _():
                dg_ref[...] = jnp.zeros_like(dg_ref)
            dg_ref[...] += jnp.sum(dgt, axis=0, keepdims=True)

    row = pl.BlockSpec((tm, D_MODEL), lambda i, k: (i, 0))
    vec = pl.BlockSpec((1, D_MODEL), lambda i, k: (0, 0))
    piece_specs = [pl.BlockSpec((tm, arr.shape[1]), lambda i, k: (i, 0)) for arr in pieces]
    return pl.pallas_call(
        body, grid=(L // tm, nk),
        in_specs=piece_specs + [pl.BlockSpec((D_MODEL, PIECE_W), lambda i, k: (0, k)), row, vec, row],
        out_specs=[row, vec],
        out_shape=[jax.ShapeDtypeStruct((L, D_MODEL), F32), jax.ShapeDtypeStruct((1, D_MODEL), F32)],
        scratch_shapes=[pltpu.VMEM((tm, D_MODEL), F32)],
        compiler_params=_cp(("arbitrary", "arbitrary"), 56), name="in_proj_bwd")(*pieces, w, x, gain, dres)


def _tn(a, b, name):
    m, na = a.shape
    nb = b.shape[1]
    ta, tb, tm = min(na, 1024), min(nb, 1024), 1024
    nm = m // tm

    def body(a_ref, b_ref, o_ref):
        @pl.when(pl.program_id(2) == 0)
        def _():
            o_ref[...] = jnp.zeros_like(o_ref)
        o_ref[...] += _tn_dot(a_ref[...].astype(BF16), b_ref[...].astype(BF16))

    return pl.pallas_call(
        body, grid=(na // ta, nb // tb, nm),
        in_specs=[pl.BlockSpec((tm, ta), lambda i, j, k: (k, i)), pl.BlockSpec((tm, tb), lambda i, j, k: (k, j))],
        out_specs=pl.BlockSpec((ta, tb), lambda i, j, k: (i, j)),
        out_shape=jax.ShapeDtypeStruct((na, nb), F32),
        compiler_params=_cp(("parallel", "parallel", "arbitrary"), 48), name=name)(a, b)


def _tn_pieces(a, pieces, name):
    m, na = a.shape
    npc = len(pieces)
    cols = _piece_columns(pieces)
    per_block = D_MODEL // PIECE_W
    nj = len(cols) // per_block
    tm = 512
    nm = m // tm
    block_of_piece = {}
    for c, (p, _) in enumerate(cols):
        block_of_piece[p] = c // per_block

    def body(*refs):
        a_ref = refs[0]
        b_refs = refs[1:1 + npc]
        o_ref = refs[1 + npc]
        j = pl.program_id(0)

        @pl.when(pl.program_id(1) == 0)
        def _():
            o_ref[...] = jnp.zeros_like(o_ref)

        for jj in range(nj):
            @pl.when(j == jj)
            def _(jj=jj):
                parts = [b_refs[p][:, off:off + PIECE_W].astype(BF16)
                         for p, off in cols[jj * per_block:(jj + 1) * per_block]]
                o_ref[...] += _tn_dot(a_ref[...], jnp.concatenate(parts, axis=1))

    piece_specs = [pl.BlockSpec((tm, arr.shape[1]),
                                functools.partial(lambda j, k, jj: (jnp.where(j == jj, k, 0), 0), jj=block_of_piece[p]))
                   for p, arr in enumerate(pieces)]
    return pl.pallas_call(
        body, grid=(nj, nm),
        in_specs=[pl.BlockSpec((tm, na), lambda j, k: (k, 0))] + piece_specs,
        out_specs=pl.BlockSpec((na, D_MODEL), lambda j, k: (0, j)),
        out_shape=jax.ShapeDtypeStruct((na, D_MODEL * nj), F32),
        compiler_params=_cp(("parallel", "arbitrary"), 56), name=name)(a, *pieces)


def _loss_grad(xf, target):
    L = xf.shape[0]
    tm = 1024

    def body(x_ref, t_ref, dy_ref, l_ref):
        e = x_ref[...] - t_ref[...]
        dy_ref[...] = e * (1.0 / D_MODEL)

        @pl.when(pl.program_id(0) == 0)
        def _():
            l_ref[...] = jnp.zeros_like(l_ref)
        l_ref[...] += jnp.sum(jnp.sum(e * e, axis=1, keepdims=True), axis=0, keepdims=True) * (0.5 / D_MODEL)

    row = pl.BlockSpec((tm, D_MODEL), lambda i: (i, 0))
    return pl.pallas_call(
        body, grid=(L // tm,), in_specs=[row, row],
        out_specs=[row, pl.BlockSpec((1, 1), lambda i: (0, 0))],
        out_shape=[jax.ShapeDtypeStruct((L, D_MODEL), F32), jax.ShapeDtypeStruct((1, 1), F32)],
        compiler_params=_cp(("arbitrary",), 40), name="loss_grad")(xf, target)


def _ffn_fwd(x, gain, wg, wu, wd):
    L = x.shape[0]
    ff = wg.shape[1]
    tm, tf = 1024, 512
    nf = ff // tf

    def body(x_ref, g_ref, wg_ref, wu_ref, wd_ref, o_ref, h_scr, acc):
        c = pl.program_id(1)

        @pl.when(c == 0)
        def _():
            t = x_ref[...]
            h_scr[...] = (t * _rms(t) * g_ref[...]).astype(BF16)
            acc[...] = jnp.zeros_like(acc)

        h = h_scr[...]
        gate = _nn(h, wg_ref[...])
        up = _nn(h, wu_ref[...])
        hid = gate * _sigmoid(gate) * up
        acc[...] += _nn(hid.astype(BF16), wd_ref[...])

        @pl.when(c == nf - 1)
        def _():
            o_ref[...] = x_ref[...] + acc[...]

    row = pl.BlockSpec((tm, D_MODEL), lambda i, c: (i, 0))
    return pl.pallas_call(
        body, grid=(L // tm, nf),
        in_specs=[row, pl.BlockSpec((1, D_MODEL), lambda i, c: (0, 0)),
                  pl.BlockSpec((D_MODEL, tf), lambda i, c: (0, c)), pl.BlockSpec((D_MODEL, tf), lambda i, c: (0, c)),
                  pl.BlockSpec((tf, D_MODEL), lambda i, c: (c, 0))],
        out_specs=row, out_shape=jax.ShapeDtypeStruct((L, D_MODEL), F32),
        scratch_shapes=[pltpu.VMEM((tm, D_MODEL), BF16), pltpu.VMEM((tm, D_MODEL), F32)],
        compiler_params=_cp(("parallel", "arbitrary"), 48), name="ffn_fwd")(x, gain, wg, wu, wd)


def _ffn_bwd(x, gain, wg, wu, wd, dxo):
    L = x.shape[0]
    ff = wg.shape[1]
    tm, tf = 512, 512
    nf = ff // tf

    def body(x_ref, g_ref, wg_ref, wu_ref, wd_ref, dxo_ref, dx_ref, h_ref, hid_ref, dgate_ref, dup_ref, dg_ref,
             acc, dxo_b):
        i, c = pl.program_id(0), pl.program_id(1)

        @pl.when(c == 0)
        def _():
            t = x_ref[...]
            h_ref[...] = (t * _rms(t) * g_ref[...]).astype(BF16)
            acc[...] = jnp.zeros_like(acc)
            dxo_b[...] = dxo_ref[...].astype(BF16)

        h = h_ref[...]
        gate = _nn(h, wg_ref[...])
        up = _nn(h, wu_ref[...])
        sg = _sigmoid(gate)
        silu = gate * sg
        hid_ref[...] = (silu * up).astype(BF16)
        dhid = _nt(dxo_b[...], wd_ref[...])
        dup = (dhid * silu).astype(BF16)
        dgate = (dhid * up * (sg * (1.0 + gate * (1.0 - sg)))).astype(BF16)
        dup_ref[...] = dup
        dgate_ref[...] = dgate
        acc[...] += _nt(dgate, wg_ref[...]) + _nt(dup, wu_ref[...])

        @pl.when(c == nf - 1)
        def _():
            t = x_ref[...]
            dt, dgt = _rms_bwd(t, _rms(t), g_ref[...], acc[...], dxo_ref[...])
            dx_ref[...] = dt

            @pl.when(i == 0)
            def _():
                dg_ref[...] = jnp.zeros_like(dg_ref)
            dg_ref[...] += jnp.sum(dgt, axis=0, keepdims=True)

    row = pl.BlockSpec((tm, D_MODEL), lambda i, c: (i, 0))
    vec = pl.BlockSpec((1, D_MODEL), lambda i, c: (0, 0))
    wcol = pl.BlockSpec((D_MODEL, tf), lambda i, c: (0, c))
    hcol = pl.BlockSpec((tm, tf), lambda i, c: (i, c))
    return pl.pallas_call(
        body, grid=(L // tm, nf),
        in_specs=[row, vec, wcol, wcol, pl.BlockSpec((tf, D_MODEL), lambda i, c: (c, 0)), row],
        out_specs=[row, row, hcol, hcol, hcol, vec],
        out_shape=[jax.ShapeDtypeStruct((L, D_MODEL), F32), jax.ShapeDtypeStruct((L, D_MODEL), BF16),
                   jax.ShapeDtypeStruct((L, ff), BF16), jax.ShapeDtypeStruct((L, ff), BF16),
                   jax.ShapeDtypeStruct((L, ff), BF16), jax.ShapeDtypeStruct((1, D_MODEL), F32)],
        scratch_shapes=[pltpu.VMEM((tm, D_MODEL), F32), pltpu.VMEM((tm, D_MODEL), BF16)],
        compiler_params=_cp(("arbitrary", "arbitrary"), 56), name="ffn_bwd")(x, gain, wg, wu, wd, dxo)


GELU_K = math.sqrt(2.0 / math.pi)
GELU_C = 0.044715


def _gelu(y):
    return 0.5 * y * (1.0 + jnp.tanh(GELU_K * (y + GELU_C * (y * y * y))))


def _gelu_grad(y):
    th = jnp.tanh(GELU_K * (y + GELU_C * (y * y * y)))
    return 0.5 * (1.0 + th) + 0.5 * y * (1.0 - th * th) * (GELU_K * (1.0 + 3.0 * GELU_C * (y * y)))


def _merge_groups(o_refs, l_refs):
    ls = [r[...] for r in l_refs]
    os_ = [r[...] for r in o_refs]
    lmax = jnp.maximum(jnp.maximum(ls[0], ls[1]), ls[2])
    es = [jnp.exp(l - lmax) for l in ls]
    inv = 1.0 / (es[0] + es[1] + es[2])
    ws = [e * inv for e in es]
    a = ws[0] * os_[0] + ws[1] * os_[1] + ws[2] * os_[2]
    return ws, os_, a


def _mix_fwd(ols, y, z, x, wp, wa, wb, wo):
    L = x.shape[0]
    tm = 256

    def body(o0, l0, o1, l1, o2, l2, y_ref, ga_ref, gs_ref, x_ref, wp_ref, wa_ref, wb_ref, wo_ref, out_ref):
        _, _, a = _merge_groups((o0, o1, o2), (l0, l1, l2))
        a_out = _nn(a.astype(BF16), wp_ref[...])
        yg = _gelu(y_ref[...]).astype(BF16)
        s_out = _nn(yg, wa_ref[...]) * _sigmoid(_nn(yg, wb_ref[...]))
        mix = _sigmoid(ga_ref[...]) * a_out + _sigmoid(gs_ref[...]) * s_out
        out_ref[...] = x_ref[...] + _nn(mix.astype(BF16), wo_ref[...])

    half = pl.BlockSpec((tm, ATTN_W), lambda i: (i, 0))
    row = pl.BlockSpec((tm, D_MODEL), lambda i: (i, 0))
    w512 = pl.BlockSpec((ATTN_W, D_MODEL), lambda i: (0, 0))
    return pl.pallas_call(
        body, grid=(L // tm,),
        in_specs=[half] * 7 + [pl.BlockSpec((tm, D_MODEL), lambda i: (i, 5)),
                               pl.BlockSpec((tm, D_MODEL), lambda i: (i, 6)), row, w512, w512, w512,
                               pl.BlockSpec((D_MODEL, D_MODEL), lambda i: (0, 0))],
        out_specs=row, out_shape=jax.ShapeDtypeStruct((L, D_MODEL), F32),
        compiler_params=_cp(("parallel",), 48), name="mix_fwd")(*ols, y, z, z, x, wp, wa, wb, wo)


def _mix_bwd(dxm, ols, y, z, wp, wa, wb, wo):
    L = dxm.shape[0]
    tm = 256

    def body(dx_ref, o0, l0, o1, l1, o2, l2, y_ref, ga_ref, gs_ref, wp_ref, wa_ref, wb_ref, wo_ref,
             do0, dl0, do1, dl1, do2, dl2, dy_ref, dga_ref, dgs_ref, a_ref, yg_ref, mix_ref, dao_ref, dpa_ref,
             dpb_ref):
        ws, os_, a = _merge_groups((o0, o1, o2), (l0, l1, l2))
        ab = a.astype(BF16)
        a_out = _nn(ab, wp_ref[...])
        yv = y_ref[...]
        yg = _gelu(yv).astype(BF16)
        pa = _nn(yg, wa_ref[...])
        spb = _sigmoid(_nn(yg, wb_ref[...]))
        s_out = pa * spb
        sga = _sigmoid(ga_ref[...])
        sgs = _sigmoid(gs_ref[...])
        mix = sga * a_out + sgs * s_out
        dmix = _nt(dx_ref[...].astype(BF16), wo_ref[...])
        da_out = (sga * dmix).astype(BF16)
        ds_out = sgs * dmix
        dpa = (ds_out * spb).astype(BF16)
        dpb = (ds_out * pa * spb * (1.0 - spb)).astype(BF16)
        dga_ref[...] = (dmix * a_out * sga * (1.0 - sga)).astype(BF16)
        dgs_ref[...] = (dmix * s_out * sgs * (1.0 - sgs)).astype(BF16)
        dy_ref[...] = (_nt(dpa, wa_ref[...]) + _nt(dpb, wb_ref[...])) * _gelu_grad(yv)
        da = _nt(da_out, wp_ref[...])
        for w, o, do_ref, dl_ref in zip(ws, os_, (do0, do1, do2), (dl0, dl1, dl2)):
            do_ref[...] = w * da
            dl_ref[...] = da * w * (o - a)
        a_ref[...] = ab
        yg_ref[...] = yg
        mix_ref[...] = mix.astype(BF16)
        dao_ref[...] = da_out
        dpa_ref[...] = dpa
        dpb_ref[...] = dpb

    half = pl.BlockSpec((tm, ATTN_W), lambda i: (i, 0))
    row = pl.BlockSpec((tm, D_MODEL), lambda i: (i, 0))
    w512 = pl.BlockSpec((ATTN_W, D_MODEL), lambda i: (0, 0))
    hf = jax.ShapeDtypeStruct((L, ATTN_W), F32)
    hb = jax.ShapeDtypeStruct((L, ATTN_W), BF16)
    rb = jax.ShapeDtypeStruct((L, D_MODEL), BF16)
    return pl.pallas_call(
        body, grid=(L // tm,),
        in_specs=[row] + [half] * 7 + [pl.BlockSpec((tm, D_MODEL), lambda i: (i, 5)),
                                       pl.BlockSpec((tm, D_MODEL), lambda i: (i, 6)), w512, w512, w512,
                                       pl.BlockSpec((D_MODEL, D_MODEL), lambda i: (0, 0))],
        out_specs=[half] * 7 + [row, row, half, half, row, row, row, row],
        out_shape=[hf] * 7 + [rb, rb, hb, hb, rb, rb, rb, rb],
        compiler_params=_cp(("parallel",), 56), name="mix_bwd")(dxm, *ols, y, z, z, wp, wa, wb, wo)


N_ATTN_ITERS = ATTN_ROWS // BLK


def _class_rows(ref, start, d):
    if d == 1:
        return ref[pl.ds(pl.multiple_of(start, BLK), BLK), :]
    return ref[pl.ds(start, BLK, stride=d), :]


def _set_class_rows(ref, start, d, val):
    if d == 1:
        ref[pl.ds(pl.multiple_of(start, BLK), BLK), :] = val
    else:
        ref[pl.ds(start, BLK, stride=d), :] = val


def _head_masks():
    lane = lax.broadcasted_iota(jnp.int32, (1, LANES), 1)
    m0 = (lane < HEAD_DIM).astype(F32)
    return m0, 1.0 - m0


def _head_norm(t, gain2, m0, m1):
    tt = t * t
    r0 = lax.rsqrt(jnp.sum(tt * m0, axis=-1, keepdims=True) * (1.0 / HEAD_DIM) + EPS)
    r1 = lax.rsqrt(jnp.sum(tt * m1, axis=-1, keepdims=True) * (1.0 / HEAD_DIM) + EPS)
    r = m0 * r0 + m1 * r1
    return t * r * gain2, r


def _head_norm_bwd(t, r, gain2, dy, m0, m1):
    u = dy * gain2
    tu = t * u
    s = m0 * jnp.sum(tu * m0, axis=-1, keepdims=True) + m1 * jnp.sum(tu * m1, axis=-1, keepdims=True)
    return r * u - t * (r * r * r) * s * (1.0 / HEAD_DIM), jnp.sum(dy * t * r, axis=0, keepdims=True)


def _band_masks():
    qi = lax.broadcasted_iota(jnp.int32, (BLK, 2 * BLK), 0)
    ki = lax.broadcasted_iota(jnp.int32, (BLK, 2 * BLK), 1)
    dist = BLK + qi - ki
    return (dist >= 0) & (dist <= BLK), ki >= BLK


ATTN_SCALE = HEAD_DIM ** -0.5


def _attn_scores(qm, kw, ok):
    return jnp.where(ok, _nt(qm, kw), -1e30)


def _attn_probs(qm, kw, ok):
    s = _attn_scores(qm, kw, ok)
    mx = jnp.max(s, axis=-1, keepdims=True)
    p = jnp.exp(s - mx)
    den = jnp.sum(p, axis=-1, keepdims=True)
    return p, den, mx


NORM_ROWS = 256


def _norm_rows(src_ref, gain2, dst_ref, m0, m1):
    n = src_ref.shape[0]
    step = min(NORM_ROWS, n)
    for r0 in range(0, n, step):
        dst_ref[r0:r0 + step, :] = _head_norm(src_ref[r0:r0 + step, :], gain2, m0, m1)[0]


def _norm_rows_bwd(src_ref, gain2, dy_ref, dst_ref, m0, m1):
    n = src_ref.shape[0]
    step = min(NORM_ROWS, n)
    dgain = jnp.zeros((1, LANES), F32)
    for r0 in range(0, n, step):
        t = src_ref[r0:r0 + step, :]
        _, r = _head_norm(t, gain2, m0, m1)
        dt, dg = _head_norm_bwd(t, r, gain2, dy_ref[r0:r0 + step, :], m0, m1)
        dst_ref[r0:r0 + step, :] = dt
        dgain = dgain + dg
    return dgain


def _attn_operands(it, d, first_step, q_ref, kc_ref, kp_ref, vc_ref, vp_ref, band, is_cur):
    j = it // d
    start = (it - j * d) + (d * BLK) * j
    before = jnp.maximum(start - d * BLK, 0)
    inside = j > 0
    q2 = _class_rows(q_ref, start, d)
    kc2 = _class_rows(kc_ref, start, d)
    vc2 = _class_rows(vc_ref, start, d)
    kp2 = jnp.where(inside, _class_rows(kc_ref, before, d), _class_rows(kp_ref, it - j * d, d))
    vp2 = jnp.where(inside, _class_rows(vc_ref, before, d), _class_rows(vp_ref, it - j * d, d))
    has_prev = inside | jnp.logical_not(first_step)
    return start, q2, kp2, kc2, vp2, vc2, band & (is_cur | has_prev)


def _attn_specs(d, step_of):
    nq = N_ATTN_ITERS // d

    def cur(c):
        return pl.BlockSpec((ATTN_ROWS, LANES), lambda hp, n: (step_of(n), c + hp))

    def prev(c):
        return pl.BlockSpec((d * BLK, LANES), lambda hp, n: (jnp.maximum(step_of(n) * nq - 1, 0), c + hp))

    return cur, prev, pl.BlockSpec((1, LANES), lambda hp, n: (0, 0))


def _attn_fwd(z, gq2, gk2, group):
    L = z.shape[0]
    d = DILATIONS[group]
    nsb = L // ATTN_ROWS
    cq, ck, cv = group * 4, 12 + group * 4, 24 + group * 4

    def body(q_ref, kc_ref, kp_ref, vc_ref, vp_ref, gq_ref, gk_ref, o_ref, l_ref, qn_scr, kn_scr, kpn_scr):
        first_step = pl.program_id(1) == 0
        band, is_cur = _band_masks()
        m0, m1 = _head_masks()
        _norm_rows(q_ref, gq_ref[...], qn_scr, m0, m1)
        _norm_rows(kc_ref, gk_ref[...], kn_scr, m0, m1)
        _norm_rows(kp_ref, gk_ref[...], kpn_scr, m0, m1)

        def per_block(it, carry):
            start, qn, kpn, kcn, vp2, vc2, ok = _attn_operands(
                it, d, first_step, qn_scr, kn_scr, kpn_scr, vc_ref, vp_ref, band, is_cur)
            kw = jnp.concatenate([kpn, kcn], axis=0).astype(BF16)
            vw = jnp.concatenate([vp2, vc2], axis=0).astype(BF16)
            o2 = jnp.zeros((BLK, LANES), F32)
            l2 = jnp.zeros((BLK, LANES), F32)
            for mh in (m0, m1):
                p, den, mx = _attn_probs((qn * (mh * ATTN_SCALE)).astype(BF16), kw, ok)
                o2 = o2 + mh * (_nn(p.astype(BF16), vw) / den)
                l2 = l2 + mh * (mx + jnp.log(den))
            _set_class_rows(o_ref, start, d, o2)
            _set_class_rows(l_ref, start, d, l2)
            return carry

        lax.fori_loop(0, N_ATTN_ITERS, per_block, 0, unroll=2)

    cur, prev, vec = _attn_specs(d, lambda n: n)
    out = pl.BlockSpec((ATTN_ROWS, LANES), lambda hp, n: (n, hp))
    sds = jax.ShapeDtypeStruct((L, ATTN_W), F32)
    return pl.pallas_call(
        body, grid=(4, nsb),
        in_specs=[cur(cq), cur(ck), prev(ck), cur(cv), prev(cv), vec, vec],
        out_specs=[out, out], out_shape=[sds, sds],
        scratch_shapes=[pltpu.VMEM((ATTN_ROWS, LANES), F32), pltpu.VMEM((ATTN_ROWS, LANES), F32),
                        pltpu.VMEM((d * BLK, LANES), F32)],
        compiler_params=_cp(("parallel", "arbitrary"), 48), name=f"attn_fwd_g{group}")(z, z, z, z, z, gq2, gk2)


def _attn_bwd(z, gq2, gk2, o, lse, do, dl, group):
    L = z.shape[0]
    d = DILATIONS[group]
    nsb = L // ATTN_ROWS
    cq, ck, cv = group * 4, 12 + group * 4, 24 + group * 4

    def body(q_ref, kc_ref, kp_ref, vc_ref, vp_ref, gq_ref, gk_ref, o_ref, l_ref, do_ref, dl_ref,
             dq_ref, dk_ref, dv_ref, dgq_ref, dgk_ref, ck_scr, cv_scr, qn_scr, kn_scr, kpn_scr, dqn_scr, dkn_scr):
        hp, n = pl.program_id(0), pl.program_id(1)
        first_step = n == nsb - 1
        band, is_cur = _band_masks()
        m0, m1 = _head_masks()
        gq, gk = gq_ref[...], gk_ref[...]
        _norm_rows(q_ref, gq, qn_scr, m0, m1)
        _norm_rows(kc_ref, gk, kn_scr, m0, m1)
        _norm_rows(kp_ref, gk, kpn_scr, m0, m1)

        @pl.when((hp == 0) & (n == 0))
        def _():
            dgq_ref[...] = jnp.zeros_like(dgq_ref)
            dgk_ref[...] = jnp.zeros_like(dgk_ref)

        @pl.when(n == 0)
        def _():
            ck_scr[...] = jnp.zeros_like(ck_scr)
            cv_scr[...] = jnp.zeros_like(cv_scr)

        def per_block(i, carry):
            it = N_ATTN_ITERS - 1 - i
            start, qn, kpn, kcn, vp2, vc2, ok = _attn_operands(
                it, d, first_step, qn_scr, kn_scr, kpn_scr, vc_ref, vp_ref, band, is_cur)
            r = it - (it // d) * d
            kw = jnp.concatenate([kpn, kcn], axis=0).astype(BF16)
            vw = jnp.concatenate([vp2, vc2], axis=0).astype(BF16)
            l2 = _class_rows(l_ref, start, d)
            c2 = _class_rows(dl_ref, start, d) - _class_rows(do_ref, start, d) * _class_rows(o_ref, start, d)
            do2 = _class_rows(do_ref, start, d)
            dqn = jnp.zeros((BLK, LANES), F32)
            dkw = jnp.zeros((2 * BLK, LANES), F32)
            dvw = jnp.zeros((2 * BLK, LANES), F32)
            for mh in (m0, m1):
                qm = (qn * (mh * ATTN_SCALE)).astype(BF16)
                lse = jnp.max(jnp.where(mh > 0.5, l2, -3e38), axis=-1, keepdims=True)
                pn = jnp.exp(_attn_scores(qm, kw, ok) - lse)
                dohb = (do2 * mh).astype(BF16)
                dvw = dvw + _tn_dot(pn.astype(BF16), dohb)
                ds = (pn * (_nt(dohb, vw) + jnp.sum(c2 * mh, axis=-1, keepdims=True))).astype(BF16)
                dqn = dqn + (mh * ATTN_SCALE) * _nn(ds, kw)
                dkw = dkw + _tn_dot(ds, qm)
            _set_class_rows(dqn_scr, start, d, dqn)
            _set_class_rows(dkn_scr, start, d, ck_scr[r] + dkw[BLK:])
            _set_class_rows(dv_ref, start, d, cv_scr[r] + dvw[BLK:])
            ck_scr[r] = dkw[:BLK]
            cv_scr[r] = dvw[:BLK]
            return carry

        lax.fori_loop(0, N_ATTN_ITERS, per_block, 0, unroll=2)
        dgq_ref[...] += _norm_rows_bwd(q_ref, gq, dqn_scr, dq_ref, m0, m1)
        dgk_ref[...] += _norm_rows_bwd(kc_ref, gk, dkn_scr, dk_ref, m0, m1)

    cur, prev, vec = _attn_specs(d, lambda n: nsb - 1 - n)
    sds = jax.ShapeDtypeStruct((L, ATTN_W), F32)
    vsd = jax.ShapeDtypeStruct((1, LANES), F32)
    return pl.pallas_call(
        body, grid=(4, nsb),
        in_specs=[cur(cq), cur(ck), prev(ck), cur(cv), prev(cv), vec, vec, cur(0), cur(0), cur(0), cur(0)],
        out_specs=[cur(0), cur(0), cur(0), vec, vec], out_shape=[sds, sds, sds, vsd, vsd],
        scratch_shapes=[pltpu.VMEM((d, BLK, LANES), F32), pltpu.VMEM((d, BLK, LANES), F32),
                        pltpu.VMEM((ATTN_ROWS, LANES), F32), pltpu.VMEM((ATTN_ROWS, LANES), F32),
                        pltpu.VMEM((d * BLK, LANES), F32),
                        pltpu.VMEM((ATTN_ROWS, LANES), F32), pltpu.VMEM((ATTN_ROWS, LANES), F32)],
        compiler_params=_cp(("arbitrary", "arbitrary"), 56),
        name=f"attn_bwd_g{group}")(z, z, z, z, z, gq2, gk2, o, lse, do, dl)


BLOCK_STATES = SSM_STATES // SSM_BLOCKS
BLOCK_CH = SSM_W // SSM_BLOCKS
SLABS_PER_BLOCK = BLOCK_STATES // LANES


SCAN_STEPS = 4


def _store_block(bufs, b, val, tm):
    for s in range(SLABS_PER_BLOCK):
        k = SLABS_PER_BLOCK * b + s
        bufs[k % 2][pl.ds(8 + k // 2, tm, stride=8), :] = val[:, s * LANES:(s + 1) * LANES]


def _load_block(bufs, b, tm):
    tiles = []
    for s in range(SLABS_PER_BLOCK):
        k = SLABS_PER_BLOCK * b + s
        tiles.append(bufs[k % 2][pl.ds(8 + k // 2, tm, stride=8), :])
    return jnp.concatenate(tiles, axis=1).astype(BF16)


def _ssm_project_in(ub, bdr_ref, bdi_ref, sr, si, tm):
    for b in range(SSM_BLOCKS):
        ubb = ub[:, b * BLOCK_CH:(b + 1) * BLOCK_CH]
        _store_block(sr, b, _nn(ubb, bdr_ref[b]), tm)
        _store_block(si, b, _nn(ubb, bdi_ref[b]), tm)


def _ssm_scan(a, x0, sr, si, tm):
    ar0, ar1, ai0, ai1 = a
    sr[0][0:8, :], sr[1][0:8, :], si[0][0:8, :], si[1][0:8, :] = x0

    def steps(it, c):
        xr0, xr1, xi0, xi1 = c
        base = it * (8 * SCAN_STEPS) + 8
        for q in range(SCAN_STEPS):
            rows = pl.ds(pl.multiple_of(base + 8 * q, 8), 8)
            nr0 = ar0 * xr0 - ai0 * xi0 + sr[0][rows, :]
            ni0 = ar0 * xi0 + ai0 * xr0 + si[0][rows, :]
            nr1 = ar1 * xr1 - ai1 * xi1 + sr[1][rows, :]
            ni1 = ar1 * xi1 + ai1 * xr1 + si[1][rows, :]
            sr[0][rows, :] = nr0
            si[0][rows, :] = ni0
            sr[1][rows, :] = nr1
            si[1][rows, :] = ni1
            xr0, xr1, xi0, xi1 = nr0, nr1, ni0, ni1
        return xr0, xr1, xi0, xi1

    return lax.fori_loop(0, tm // SCAN_STEPS, steps, x0)


def _load_a(ar_ref, ai_ref):
    return ar_ref[:, :LANES], ar_ref[:, LANES:], ai_ref[:, :LANES], ai_ref[:, LANES:]


def _ssm_fwd(z, ar8, ai8, bdr, bdi, cdr, cdi, dsk):
    L = z.shape[0]
    tm = SSM_TM
    nc = L // tm

    def body(u_ref, ar_ref, ai_ref, bdr_ref, bdi_ref, cdr_ref, cdi_ref, dsk_ref, y_ref, cin_ref,
             sr0, sr1, si0, si1, car):
        sr, si = (sr0, sr1), (si0, si1)

        @pl.when(pl.program_id(0) == 0)
        def _():
            car[...] = jnp.zeros_like(car)

        u = u_ref[...]
        _ssm_project_in(u.astype(BF16), bdr_ref, bdi_ref, sr, si, tm)
        cin_ref[0] = car[...]
        xr0, xr1, xi0, xi1 = _ssm_scan(_load_a(ar_ref, ai_ref), (car[0], car[1], car[2], car[3]), sr, si, tm)
        car[0], car[1], car[2], car[3] = xr0, xr1, xi0, xi1
        for b in range(SSM_BLOCKS):
            cols = slice(b * BLOCK_CH, (b + 1) * BLOCK_CH)
            y_ref[:, cols] = (dsk_ref[:, cols] * u[:, cols] + _nn(_load_block(sr, b, tm), cdr_ref[b])
                              - _nn(_load_block(si, b, tm), cdi_ref[b]))

    def const(shape):
        return pl.BlockSpec(shape, lambda i: (0,) * len(shape))

    state = pltpu.VMEM(((tm + 1) * 8, LANES), F32)
    wb = const((SSM_BLOCKS, BLOCK_CH, BLOCK_STATES))
    wc = const((SSM_BLOCKS, BLOCK_STATES, BLOCK_CH))
    return pl.pallas_call(
        body, grid=(nc,),
        in_specs=[pl.BlockSpec((tm, SSM_W), lambda i: (i, COL_U)), const((8, 256)), const((8, 256)),
                  wb, wb, wc, wc, const((1, SSM_W))],
        out_specs=[pl.BlockSpec((tm, SSM_W), lambda i: (i, 0)), pl.BlockSpec((1, 4, 8, LANES), lambda i: (i, 0, 0, 0))],
        out_shape=[jax.ShapeDtypeStruct((L, SSM_W), F32), jax.ShapeDtypeStruct((nc, 4, 8, LANES), F32)],
        scratch_shapes=[state, state, state, state, pltpu.VMEM((4, 8, LANES), F32)],
        compiler_params=_cp(("arbitrary",), 48), name="ssm_fwd")(z, ar8, ai8, bdr, bdi, cdr, cdi, dsk)


def _ssm_bwd(z, dy, cin, ar8, ai8, bdr, bdi, cdr, cdi, dsk):
    L = z.shape[0]
    tm = SSM_TM
    nc = L // tm

    def body(u_ref, dy_ref, cin_ref, ar_ref, ai_ref, dsk_ref, bdr_ref, bdi_ref, cdr_ref, cdi_ref,
             du_ref, da_ref, dds_ref, dbdr_ref, dbdi_ref, dcdr_ref, dcdi_ref,
             sr0, sr1, si0, si1, gr0, gr1, gi0, gi1, carg):
        sr, si, gr, gi = (sr0, sr1), (si0, si1), (gr0, gr1), (gi0, gi1)

        @pl.when(pl.program_id(0) == 0)
        def _():
            carg[...] = jnp.zeros_like(carg)
            for ref in (da_ref, dds_ref, dbdr_ref, dbdi_ref, dcdr_ref, dcdi_ref):
                ref[...] = jnp.zeros_like(ref)

        u = u_ref[...]
        ub = u.astype(BF16)
        dyv = dy_ref[...]
        dyb = dyv.astype(BF16)
        a = _load_a(ar_ref, ai_ref)
        ar0, ar1, ai0, ai1 = a
        x_in = (cin_ref[0, 0], cin_ref[0, 1], cin_ref[0, 2], cin_ref[0, 3])
        _ssm_project_in(ub, bdr_ref, bdi_ref, sr, si, tm)
        _ssm_scan(a, x_in, sr, si, tm)
        for b in range(SSM_BLOCKS):
            dyb_b = dyb[:, b * BLOCK_CH:(b + 1) * BLOCK_CH]
            _store_block(gr, b, _nt(dyb_b, cdr_ref[b]), tm)
            _store_block(gi, b, -_nt(dyb_b, cdi_ref[b]), tm)

        def grad_steps(it, c):
            (nr0, nr1, ni0, ni1), (d_r0, d_r1, d_i0, d_i1) = c
            base = (tm - SCAN_STEPS * (it + 1)) * 8
            for q in reversed(range(SCAN_STEPS)):
                prev = pl.ds(pl.multiple_of(base + 8 * q, 8), 8)
                rows = pl.ds(pl.multiple_of(base + 8 * q + 8, 8), 8)
                g_r0 = gr[0][rows, :] + ar0 * nr0 + ai0 * ni0
                g_i0 = gi[0][rows, :] + ar0 * ni0 - ai0 * nr0
                g_r1 = gr[1][rows, :] + ar1 * nr1 + ai1 * ni1
                g_i1 = gi[1][rows, :] + ar1 * ni1 - ai1 * nr1
                gr[0][rows, :] = g_r0
                gi[0][rows, :] = g_i0
                gr[1][rows, :] = g_r1
                gi[1][rows, :] = g_i1
                pr0, pr1, pi0, pi1 = sr[0][prev, :], sr[1][prev, :], si[0][prev, :], si[1][prev, :]
                d_r0 = d_r0 + pr0 * g_r0 + pi0 * g_i0
                d_r1 = d_r1 + pr1 * g_r1 + pi1 * g_i1
                d_i0 = d_i0 + pr0 * g_i0 - pi0 * g_r0
                d_i1 = d_i1 + pr1 * g_i1 - pi1 * g_r1
                nr0, nr1, ni0, ni1 = g_r0, g_r1, g_i0, g_i1
            return (nr0, nr1, ni0, ni1), (d_r0, d_r1, d_i0, d_i1)

        acc0 = (da_ref[0], da_ref[1], da_ref[2], da_ref[3])
        g_first, acc = lax.fori_loop(0, tm // SCAN_STEPS, grad_steps,
                                     ((carg[0], carg[1], carg[2], carg[3]), acc0))
        carg[0], carg[1], carg[2], carg[3] = g_first
        da_ref[0], da_ref[1], da_ref[2], da_ref[3] = acc

        for b in range(SSM_BLOCKS):
            cols = slice(b * BLOCK_CH, (b + 1) * BLOCK_CH)
            grb, gib = _load_block(gr, b, tm), _load_block(gi, b, tm)
            du_ref[:, cols] = dsk_ref[:, cols] * dyv[:, cols] + _nt(grb, bdr_ref[b]) + _nt(gib, bdi_ref[b])
            dbdr_ref[b] += _tn_dot(ub[:, cols], grb)
            dbdi_ref[b] += _tn_dot(ub[:, cols], gib)
            dcdr_ref[b] += _tn_dot(_load_block(sr, b, tm), dyb[:, cols])
            dcdi_ref[b] -= _tn_dot(_load_block(si, b, tm), dyb[:, cols])
        dds_ref[...] += jnp.sum(dyv * u, axis=0, keepdims=True)

    def const(shape):
        return pl.BlockSpec(shape, lambda i: (0,) * len(shape))

    state = pltpu.VMEM(((tm + 1) * 8, LANES), F32)
    wb = const((SSM_BLOCKS, BLOCK_CH, BLOCK_STATES))
    wc = const((SSM_BLOCKS, BLOCK_STATES, BLOCK_CH))
    return pl.pallas_call(
        body, grid=(nc,),
        in_specs=[pl.BlockSpec((tm, SSM_W), lambda i: (nc - 1 - i, COL_U)),
                  pl.BlockSpec((tm, SSM_W), lambda i: (nc - 1 - i, 0)),
                  pl.BlockSpec((1, 4, 8, LANES), lambda i: (nc - 1 - i, 0, 0, 0)),
                  const((8, 256)), const((8, 256)), const((1, SSM_W)), wb, wb, wc, wc],
        out_specs=[pl.BlockSpec((tm, SSM_W), lambda i: (nc - 1 - i, 0)), const((4, 8, LANES)), const((1, SSM_W)),
                   wb, wb, wc, wc],
        out_shape=[jax.ShapeDtypeStruct((L, SSM_W), F32), jax.ShapeDtypeStruct((4, 8, LANES), F32),
                   jax.ShapeDtypeStruct((1, SSM_W), F32),
                   jax.ShapeDtypeStruct((SSM_BLOCKS, BLOCK_CH, BLOCK_STATES), F32),
                   jax.ShapeDtypeStruct((SSM_BLOCKS, BLOCK_CH, BLOCK_STATES), F32),
                   jax.ShapeDtypeStruct((SSM_BLOCKS, BLOCK_STATES, BLOCK_CH), F32),
                   jax.ShapeDtypeStruct((SSM_BLOCKS, BLOCK_STATES, BLOCK_CH), F32)],
        scratch_shapes=[state] * 8 + [pltpu.VMEM((4, 8, LANES), F32)],
        compiler_params=_cp(("arbitrary",), 56), name="ssm_bwd")(z, dy, cin, ar8, ai8, dsk, bdr, bdi, cdr, cdi)


def _discretise(lam_re, lam_im, log_dt, b_re, b_im):
    dt = jnp.exp(log_dt)[:, None]
    mag = jnp.exp(lam_re * dt)
    ang = lam_im * dt
    abar_re = mag * jnp.cos(ang)
    abar_im = mag * jnp.sin(ang)
    nr = abar_re - 1.0
    ni = abar_im
    den = lam_re * lam_re + lam_im * lam_im
    cr = ((nr * lam_re + ni * lam_im) / den)[..., None]
    ci = ((ni * lam_re - nr * lam_im) / den)[..., None]
    return abar_re, abar_im, cr * b_re - ci * b_im, cr * b_im + ci * b_re


GROUPS_PER_BLOCK = 8


def _block_diag_in(bbar):
    eye = jnp.eye(GROUPS_PER_BLOCK, dtype=F32)
    return jnp.einsum("igpc,gh->igchp", bbar.reshape(SSM_BLOCKS, GROUPS_PER_BLOCK, 64, 16), eye).reshape(
        SSM_BLOCKS, BLOCK_CH, BLOCK_STATES)


def _block_diag_in_t(blocks):
    eye = jnp.eye(GROUPS_PER_BLOCK, dtype=F32)
    return jnp.einsum("igchp,gh->igpc", blocks.reshape(SSM_BLOCKS, GROUPS_PER_BLOCK, 16, GROUPS_PER_BLOCK, 64),
                      eye).reshape(32, 64, 16)


def _block_diag_out(c):
    eye = jnp.eye(GROUPS_PER_BLOCK, dtype=F32)
    return jnp.einsum("igcp,gh->igphc", c.reshape(SSM_BLOCKS, GROUPS_PER_BLOCK, 16, 64), eye).reshape(
        SSM_BLOCKS, BLOCK_STATES, BLOCK_CH)


def _block_diag_out_t(blocks):
    eye = jnp.eye(GROUPS_PER_BLOCK, dtype=F32)
    return jnp.einsum("igphc,gh->igcp", blocks.reshape(SSM_BLOCKS, GROUPS_PER_BLOCK, 64, GROUPS_PER_BLOCK, 16),
                      eye).reshape(32, 16, 64)


SMALL_NAMES = ("g_mix", "g_q", "g_k", "lambda_re", "lambda_im", "log_dt", "b_re", "b_im", "c_re", "c_im",
               "d_skip", "g_ffn")


def _pack_small(parts):
    flat = jnp.concatenate([parts[n].reshape(-1) for n in SMALL_NAMES])
    pad = (-flat.shape[0]) % (8 * LANES * SMALL_TILES)
    return jnp.pad(flat, (0, pad)).reshape(-1, LANES)


def _unpack_small(packed, like):
    flat = packed.reshape(-1)
    out, off = {}, 0
    for n in SMALL_NAMES:
        size = like[n].size
        out[n] = flat[off:off + size].reshape(like[n].shape)
        off += size
    return out


BIG_NAMES = ("w_in", "w_attn_proj", "w_glu_a", "w_glu_b", "w_out", "w_ffn_gate", "w_ffn_up", "w_ffn_down")
BIG_SHARD_AXIS = {"w_in": 2, "w_attn_proj": 2, "w_glu_a": 2, "w_glu_b": 2, "w_out": 1,
                  "w_ffn_gate": 2, "w_ffn_up": 2, "w_ffn_down": 1}
ADAMW_ROWS = {"w_in": 256, "w_attn_proj": 512, "w_glu_a": 512, "w_glu_b": 512, "w_out": 128,
              "w_ffn_gate": 256, "w_ffn_up": 256, "w_ffn_down": 176}


def kernel(x, g_mix, w_in, g_q, g_k, w_attn_proj, lambda_re, lambda_im, log_dt, b_re, b_im, c_re, c_im, d_skip, w_glu_a, w_glu_b, w_out, g_ffn, w_ffn_gate, w_ffn_up, w_ffn_down, loss_target, m_g_mix, m_w_in, m_g_q, m_g_k, m_w_attn_proj, m_lambda_re, m_lambda_im, m_log_dt, m_b_re, m_b_im, m_c_re, m_c_im, m_d_skip, m_w_glu_a, m_w_glu_b, m_w_out, m_g_ffn, m_w_ffn_gate, m_w_ffn_up, m_w_ffn_down, v_g_mix, v_w_in, v_g_q, v_g_k, v_w_attn_proj, v_lambda_re, v_lambda_im, v_log_dt, v_b_re, v_b_im, v_c_re, v_c_im, v_d_skip, v_w_glu_a, v_w_glu_b, v_w_out, v_g_ffn, v_w_ffn_gate, v_w_ffn_up, v_w_ffn_down):
    args = dict(locals())
    weights = {n: args[n] for n in BIG_NAMES + SMALL_NAMES}
    moments_m = {n: args["m_" + n] for n in BIG_NAMES + SMALL_NAMES}
    moments_v = {n: args["v_" + n] for n in BIG_NAMES + SMALL_NAMES}
    x0 = x[0]
    target = loss_target[0]

    shards = []
    for n in BIG_NAMES:
        w = weights[n]
        rows_to, cols_to = w.shape[1], w.shape[2]
        if n in ("w_ffn_gate", "w_ffn_up"):
            cols_to = FF_SHARD_PAD
        if n == "w_ffn_down":
            rows_to = FF_SHARD_PAD
        shards.append(_prep_weight(w, rows_to, cols_to, "prep_" + n))
    full = dict(zip(BIG_NAMES, _all_gather(shards, [BIG_SHARD_AXIS[n] for n in BIG_NAMES])))

    saved = []
    xl = x0
    for l in range(DEPTH):
        abar_re, abar_im, bb_re, bb_im = _discretise(lambda_re[l], lambda_im[l], log_dt[l], b_re[l], b_im[l])
        ssm = dict(ar8=abar_re.reshape(8, 256), ai8=abar_im.reshape(8, 256),
                   bdr=_block_diag_in(bb_re).astype(BF16), bdi=_block_diag_in(bb_im).astype(BF16),
                   cdr=_block_diag_out(c_re[l]).astype(BF16), cdi=_block_diag_out(c_im[l]).astype(BF16),
                   dsk=d_skip[l][None])
        gq2 = jnp.tile(g_q[l], 2)[None]
        gk2 = jnp.tile(g_k[l], 2)[None]
        z, h = _in_proj(xl, g_mix[l][None], full["w_in"][l])
        ols = []
        for g in range(N_GROUPS):
            ols.extend(_attn_fwd(z, gq2, gk2, g))
        y, cin = _ssm_fwd(z, **ssm)
        xm = _mix_fwd(ols, y, z, xl, full["w_attn_proj"][l], full["w_glu_a"][l], full["w_glu_b"][l], full["w_out"][l])
        xo = _ffn_fwd(xm, g_ffn[l][None], full["w_ffn_gate"][l], full["w_ffn_up"][l], full["w_ffn_down"][l])
        saved.append(dict(x=xl, z=z, h=h, ols=ols, y=y, cin=cin, xm=xm, ssm=ssm, gq2=gq2, gk2=gk2))
        xl = xo

    dxo, loss_local = _loss_grad(xl, target)
    loss = lax.psum(loss_local[0, 0], MESH_AXES)
    big_grads = {n: [None] * DEPTH for n in BIG_NAMES}
    small_grads = {n: [None] * DEPTH for n in SMALL_NAMES}
    for l in reversed(range(DEPTH)):
        s = saved[l]
        dxm, h2, hid, dgate, dup, dgffn = _ffn_bwd(s["xm"], g_ffn[l][None], full["w_ffn_gate"][l],
                                                   full["w_ffn_up"][l], full["w_ffn_down"][l], dxo)
        big_grads["w_ffn_down"][l] = _tn(hid, dxo, "grad_w_ffn_down")
        big_grads["w_ffn_gate"][l] = _tn(h2, dgate, "grad_w_ffn_gate")
        big_grads["w_ffn_up"][l] = _tn(h2, dup, "grad_w_ffn_up")
        (do0, dl0, do1, dl1, do2, dl2, dy, dga, dgs, a_b, yg_b, mix_b, dao_b, dpa_b, dpb_b) = _mix_bwd(
            dxm, s["ols"], s["y"], s["z"], full["w_attn_proj"][l], full["w_glu_a"][l], full["w_glu_b"][l],
            full["w_out"][l])
        big_grads["w_out"][l] = _tn(mix_b, dxm, "grad_w_out")
        big_grads["w_attn_proj"][l] = _tn(a_b, dao_b, "grad_w_attn_proj")
        big_grads["w_glu_a"][l] = _tn(yg_b, dpa_b, "grad_w_glu_a")
        big_grads["w_glu_b"][l] = _tn(yg_b, dpb_b, "grad_w_glu_b")
        du, da4, ddsk, dbdr, dbdi, dcdr, dcdi = _ssm_bwd(s["z"], dy, s["cin"], **s["ssm"])
        dqkv = []
        dgq = jnp.zeros((1, LANES), F32)
        dgk = jnp.zeros((1, LANES), F32)
        for g, (do_g, dl_g) in enumerate(((do0, dl0), (do1, dl1), (do2, dl2))):
            dq, dk, dv, dgq_g, dgk_g = _attn_bwd(s["z"], s["gq2"], s["gk2"], s["ols"][2 * g], s["ols"][2 * g + 1],
                                                 do_g, dl_g, g)
            dqkv.append((dq, dk, dv))
            dgq, dgk = dgq + dgq_g, dgk + dgk_g
        pieces = [dqkv[g][j] for j in range(3) for g in range(N_GROUPS)] + [du, dga, dgs]
        dxo, dgmix = _in_proj_bwd(pieces, full["w_in"][l], s["x"], g_mix[l][None], dxm)
        big_grads["w_in"][l] = _tn_pieces(s["h"], pieces, "grad_w_in")
        _, disc_vjp = jax.vjp(_discretise, lambda_re[l], lambda_im[l], log_dt[l], b_re[l], b_im[l])
        dar = jnp.concatenate([da4[0], da4[1]], axis=1).reshape(32, 64)
        dai = jnp.concatenate([da4[2], da4[3]], axis=1).reshape(32, 64)
        dlr, dli, dldt, dbre, dbim = disc_vjp((dar, dai, _block_diag_in_t(dbdr), _block_diag_in_t(dbdi)))
        small_grads["g_mix"][l] = dgmix[0]
        small_grads["g_q"][l] = dgq[0, :HEAD_DIM] + dgq[0, HEAD_DIM:]
        small_grads["g_k"][l] = dgk[0, :HEAD_DIM] + dgk[0, HEAD_DIM:]
        small_grads["lambda_re"][l] = dlr
        small_grads["lambda_im"][l] = dli
        small_grads["log_dt"][l] = dldt
        small_grads["b_re"][l] = dbre
        small_grads["b_im"][l] = dbim
        small_grads["c_re"][l] = _block_diag_out_t(dcdr)
        small_grads["c_im"][l] = _block_diag_out_t(dcdi)
        small_grads["d_skip"][l] = ddsk[0]
        small_grads["g_ffn"][l] = dgffn[0]
    grad_x = dxo[None]

    small_local = {n: jnp.stack(small_grads[n]) for n in SMALL_NAMES}
    rs_axes = [BIG_SHARD_AXIS[n] - 1 for n in BIG_NAMES]
    got = _exchange_with_sibling([big_grads[n] for n in BIG_NAMES], rs_axes)
    core = lax.axis_index("c").astype(jnp.int32).reshape(1)
    sums = [[_chip_sum(big_grads[n][l], got[t], l, rs_axes[t], core, "chip_sum_" + n) for l in range(DEPTH)]
            for t, n in enumerate(BIG_NAMES)]
    recv = _exchange_chip_sums(sums, _pack_small(small_local))
    out_g, out_d, out_m, out_v = {}, {}, {}, {}
    for n, r in zip(BIG_NAMES, recv[:-1]):
        out_g[n], out_d[n], out_m[n], out_v[n] = _adamw_big(r, weights[n], moments_m[n], moments_v[n],
                                                            ADAMW_ROWS[n], "adamw_" + n)
    like = {n: weights[n] for n in SMALL_NAMES}
    packed = _adamw_small(recv[-1], _pack_small(like), _pack_small({n: moments_m[n] for n in SMALL_NAMES}),
                          _pack_small({n: moments_v[n] for n in SMALL_NAMES}))
    for dst, p in zip((out_g, out_d, out_m, out_v), packed):
        dst.update(_unpack_small(p, like))

    order = ("g_mix", "w_in", "g_q", "g_k", "w_attn_proj", "lambda_re", "lambda_im", "log_dt", "b_re", "b_im",
             "c_re", "c_im", "d_skip", "w_glu_a", "w_glu_b", "w_out", "g_ffn", "w_ffn_gate", "w_ffn_up",
             "w_ffn_down")
    return (loss, grad_x, *[out_g[n] for n in order], *[out_d[n] for n in order],
            *[out_m[n] for n in order], *[out_v[n] for n in order])
```

```python
import functools
import math

import jax
import jax.numpy as jnp
from jax import lax
from jax.experimental import pallas as pl
from jax.experimental.pallas import tpu as pltpu

F32 = jnp.float32
BF16 = jnp.bfloat16

D_MODEL = 1024
DEPTH = 4
N_DEV = 8
N_CHIPS = 4
HEAD_DIM = 64
BLK = 128
LANES = 128
ATTN_W = 512
N_GROUPS = 3
DILATIONS = (1, 4, 16)
ATTN_ROWS = 2048
SSM_W = 512
SSM_STATES = 2048
SSM_BLOCKS = 4
IN_COLS = 7168
COL_U = 9
D_FF = 2816
FF_SHARD = D_FF // N_DEV
FF_SHARD_PAD = 384
FF_PAD = FF_SHARD_PAD * N_DEV
FF_CHUNK = 512
EPS = 1e-6
SSM_TM = 512
SMALL_TILES = 4

ADAM_LR = 0.001
ADAM_B1 = 0.9
ADAM_B2 = 0.999
ADAM_EPS = 1e-08
ADAM_WD = 0.01
ADAM_STEP = 10

MESH_AXES = ("x", "y", "c")
MIB = 1024 * 1024


def _cp(sem=None, vmem_mib=None):
    kw = {}
    if sem is not None:
        kw["dimension_semantics"] = sem
    if vmem_mib is not None:
        kw["vmem_limit_bytes"] = vmem_mib * MIB
    return pltpu.CompilerParams(**kw)


def _nt(a, b):
    return lax.dot_general(a, b, (((1,), (1,)), ((), ())), preferred_element_type=F32)


def _tn_dot(a, b):
    return lax.dot_general(a, b, (((0,), (0,)), ((), ())), preferred_element_type=F32)


def _nn(a, b):
    return jnp.dot(a, b, preferred_element_type=F32)


def _sigmoid(t):
    return 0.5 * jnp.tanh(0.5 * t) + 0.5


def _prep_weight(w, rows_to, cols_to, name):
    _, k, n = w.shape

    def body(w_ref, o_ref):
        if rows_to != k or cols_to != n:
            o_ref[...] = jnp.zeros(o_ref.shape, BF16)
        o_ref[0, :k, :n] = w_ref[0].astype(BF16)

    return pl.pallas_call(
        body, grid=(DEPTH,),
        in_specs=[pl.BlockSpec((1, k, n), lambda l: (l, 0, 0))],
        out_specs=pl.BlockSpec((1, rows_to, cols_to), lambda l: (l, 0, 0)),
        out_shape=jax.ShapeDtypeStruct((DEPTH, rows_to, cols_to), BF16),
        compiler_params=_cp(("parallel",), 40), name=name)(w)


def _my_index():
    return 4 * lax.axis_index("x") + 2 * lax.axis_index("y") + lax.axis_index("c")


def _my_chip():
    return 2 * lax.axis_index("x") + lax.axis_index("y")


def _sibling():
    return (lax.axis_index("x"), lax.axis_index("y"), 1 - lax.axis_index("c"))


def _other_chip(j):
    return (jnp.bitwise_xor(lax.axis_index("x"), (j >> 1) & 1), jnp.bitwise_xor(lax.axis_index("y"), j & 1))


def _slab(ref, idx, width, axis):
    start = pl.multiple_of(idx * width, width)
    sl = [slice(None)] * len(ref.shape)
    sl[axis] = pl.ds(start, width)
    return ref.at[tuple(sl)]


def _remote(src, dst, ssem, rsem, device):
    return pltpu.make_async_remote_copy(src_ref=src, dst_ref=dst, send_sem=ssem, recv_sem=rsem,
                                        device_id=device, device_id_type=pl.DeviceIdType.MESH)


def _two_level_gather(srcs, blocks, ssem, rsem, lsem):
    nt = len(srcs)
    x, y, c = lax.axis_index("x"), lax.axis_index("y"), lax.axis_index("c")
    me = _my_index()
    local, sends = [], []
    for t in range(nt):
        mine = blocks[t](me)
        loc = pltpu.make_async_copy(srcs[t], mine, lsem.at[t])
        loc.start()
        local.append(loc)
        first = [_remote(srcs[t], mine, ssem.at[t, 0], rsem.at[t, 0], _sibling())]
        for j in range(1, N_CHIPS):
            first.append(_remote(srcs[t], mine, ssem.at[t, j], rsem.at[t, j], (*_other_chip(j), c)))
        for cp in first:
            cp.start()
        sends.extend(first)
    for t in range(nt):
        for j in range(1, N_CHIPS):
            ox, oy = _other_chip(j)
            landed = blocks[t](4 * ox + 2 * oy + c)
            _remote(landed, landed, ssem.at[t, j], rsem.at[t, j], _sibling()).wait_recv()
            fwd = _remote(landed, landed, ssem.at[t, 3 + j], rsem.at[t, 3 + j], _sibling())
            fwd.start()
            sends.append(fwd)
    for t in range(nt):
        got = blocks[t](4 * x + 2 * y + (1 - c))
        _remote(got, got, ssem.at[t, 0], rsem.at[t, 0], _sibling()).wait_recv()
        for j in range(1, N_CHIPS):
            ox, oy = _other_chip(j)
            got = blocks[t](4 * ox + 2 * oy + (1 - c))
            _remote(got, got, ssem.at[t, 3 + j], rsem.at[t, 3 + j], _sibling()).wait_recv()
    for cp in sends:
        cp.wait_send()
    for cp in local:
        cp.wait()


def _gather_sems(nt):
    return [pltpu.SemaphoreType.DMA((nt, N_DEV - 1)), pltpu.SemaphoreType.DMA((nt, N_DEV - 1)),
            pltpu.SemaphoreType.DMA((nt,))]


def _all_gather(shards, axes):
    nt = len(shards)

    def body(*refs):
        ins, outs = refs[:nt], refs[nt:2 * nt]
        ssem, rsem, lsem = refs[2 * nt:]
        blocks = [functools.partial(_slab, outs[t], width=shards[t].shape[axes[t]], axis=axes[t]) for t in range(nt)]
        _two_level_gather(ins, blocks, ssem, rsem, lsem)

    out_shape = []
    for t in range(nt):
        s = list(shards[t].shape)
        s[axes[t]] *= N_DEV
        out_shape.append(jax.ShapeDtypeStruct(tuple(s), shards[t].dtype))
    return pl.pallas_call(
        body,
        in_specs=[pl.BlockSpec(memory_space=pltpu.HBM)] * nt,
        out_specs=[pl.BlockSpec(memory_space=pltpu.HBM)] * nt,
        out_shape=out_shape, scratch_shapes=_gather_sems(nt),
        name="all_gather_weights")(*shards)


def _exchange_with_sibling(grads, axes):
    nt = len(grads)

    def body(*refs):
        ins = [refs[t * DEPTH:(t + 1) * DEPTH] for t in range(nt)]
        outs = refs[nt * DEPTH: nt * DEPTH + nt]
        ssem, rsem = refs[nt * DEPTH + nt:]
        c = lax.axis_index("c")
        for t in range(nt):
            width = grads[t][0].shape[axes[t]] // N_DEV
            for q in range(N_CHIPS):
                for l in range(DEPTH):
                    _remote(_slab(ins[t][l], 2 * q + (1 - c), width, axes[t]), outs[t].at[q, l],
                            ssem.at[t], rsem.at[t], _sibling()).start()
        for t in range(nt):
            _remote(outs[t], outs[t], ssem.at[t], rsem.at[t], _sibling()).wait()

    out_shape = []
    for t in range(nt):
        s = list(grads[t][0].shape)
        s[axes[t]] //= N_DEV
        out_shape.append(jax.ShapeDtypeStruct((N_CHIPS, DEPTH, s[0], s[1]), F32))
    flat = [g for per_type in grads for g in per_type]
    return pl.pallas_call(
        body,
        in_specs=[pl.BlockSpec(memory_space=pltpu.HBM)] * len(flat),
        out_specs=[pl.BlockSpec(memory_space=pltpu.HBM)] * nt,
        out_shape=out_shape,
        scratch_shapes=[pltpu.SemaphoreType.DMA((nt,)), pltpu.SemaphoreType.DMA((nt,))],
        name="grads_to_sibling")(*flat)


def _chip_sum(grad, got, layer, axis, core, name):
    _, _, r, c = got.shape
    tr = min(r, 512)

    def body(core_ref, g_ref, s_ref, o_ref):
        o_ref[0] = (g_ref[...] + s_ref[0, 0]).astype(BF16)

    if axis == 1:
        g_spec = pl.BlockSpec((tr, c), lambda q, i, core_ref: (i, 2 * q + core_ref[0]))
    else:
        g_spec = pl.BlockSpec((tr, c), lambda q, i, core_ref: ((2 * q + core_ref[0]) * (r // tr) + i, 0))
    return pl.pallas_call(
        body,
        grid_spec=pltpu.PrefetchScalarGridSpec(
            num_scalar_prefetch=1, grid=(N_CHIPS, r // tr),
            in_specs=[g_spec, pl.BlockSpec((1, 1, tr, c), lambda q, i, core_ref: (q, layer, i, 0))],
            out_specs=pl.BlockSpec((1, tr, c), lambda q, i, core_ref: (q, i, 0))),
        out_shape=jax.ShapeDtypeStruct((N_CHIPS, r, c), BF16),
        compiler_params=_cp(("parallel", "parallel"), 40), name=name)(core, grad, got)


def _exchange_chip_sums(sums, small):
    nt = len(sums)

    def body(*refs):
        ins = [refs[t * DEPTH:(t + 1) * DEPTH] for t in range(nt)]
        small_ref = refs[nt * DEPTH]
        outs = refs[nt * DEPTH + 1: nt * DEPTH + 1 + nt]
        small_out = refs[nt * DEPTH + 1 + nt]
        ssem, rsem, lsem, g_ssem, g_rsem, g_lsem = refs[nt * DEPTH + 2 + nt:]
        c = lax.axis_index("c")
        chip = _my_chip()
        for t in range(nt):
            for l in range(DEPTH):
                pltpu.make_async_copy(ins[t][l].at[chip], outs[t].at[chip, l], lsem.at[t]).start()
            for j in range(1, N_CHIPS):
                other = jnp.bitwise_xor(chip, j)
                for l in range(DEPTH):
                    _remote(ins[t][l].at[other], outs[t].at[chip, l], ssem.at[t, j - 1], rsem.at[t, j - 1],
                            (*_other_chip(j), c)).start()
        _two_level_gather([small_ref], [lambda idx: small_out.at[idx]], g_ssem, g_rsem, g_lsem)
        for t in range(nt):
            pltpu.make_async_copy(outs[t].at[chip], outs[t].at[chip], lsem.at[t]).wait()
            for j in range(1, N_CHIPS):
                other = jnp.bitwise_xor(chip, j)
                _remote(outs[t].at[other], outs[t].at[other], ssem.at[t, j - 1], rsem.at[t, j - 1],
                        (*_other_chip(j), c)).wait()

    out_shape = []
    for t in range(nt):
        _, r, c = sums[t][0].shape
        out_shape.append(jax.ShapeDtypeStruct((N_CHIPS, DEPTH, r, c), BF16))
    out_shape.append(jax.ShapeDtypeStruct((N_DEV,) + small.shape, F32))
    flat = [s for per_type in sums for s in per_type]
    return pl.pallas_call(
        body,
        in_specs=[pl.BlockSpec(memory_space=pltpu.HBM)] * (len(flat) + 1),
        out_specs=[pl.BlockSpec(memory_space=pltpu.HBM)] * (nt + 1),
        out_shape=out_shape,
        scratch_shapes=[pltpu.SemaphoreType.DMA((nt, N_CHIPS - 1)), pltpu.SemaphoreType.DMA((nt, N_CHIPS - 1)),
                        pltpu.SemaphoreType.DMA((nt,))] + _gather_sems(1),
        name="chip_sums_over_ici")(*flat, small)


def _adamw_math(w, g, m, v):
    m = ADAM_B1 * m + (1.0 - ADAM_B1) * g
    v = ADAM_B2 * v + (1.0 - ADAM_B2) * (g * g)
    m_hat = m / (1.0 - ADAM_B1 ** ADAM_STEP)
    v_hat = v / (1.0 - ADAM_B2 ** ADAM_STEP)
    delta = -ADAM_LR * (m_hat / (jnp.sqrt(v_hat) + ADAM_EPS) + ADAM_WD * w)
    return delta, m, v


def _adamw_big(recv, w, m, v, tk, name):
    _, k, n = w.shape
    npad = recv.shape[3]

    def body(r_ref, w_ref, m_ref, v_ref, g_out, d_out, m_out, v_out):
        g = r_ref[0, 0].astype(F32)
        for s in range(1, N_CHIPS):
            g = g + r_ref[s, 0].astype(F32)
        g = g[:, :n]
        delta, mn, vn = _adamw_math(w_ref[0], g, m_ref[0], v_ref[0])
        g_out[0] = g
        d_out[0] = delta
        m_out[0] = mn
        v_out[0] = vn

    blk = pl.BlockSpec((1, tk, n), lambda l, i: (l, i, 0))
    sds = jax.ShapeDtypeStruct(w.shape, F32)
    return pl.pallas_call(
        body, grid=(DEPTH, k // tk),
        in_specs=[pl.BlockSpec((N_CHIPS, 1, tk, npad), lambda l, i: (0, l, i, 0)), blk, blk, blk],
        out_specs=[blk, blk, blk, blk], out_shape=[sds, sds, sds, sds],
        compiler_params=_cp(("parallel", "parallel"), 48), name=name)(recv, w, m, v)


def _adamw_small(recv, w, m, v):
    rows = w.shape[0]
    tr = rows // SMALL_TILES

    def body(r_ref, w_ref, m_ref, v_ref, g_out, d_out, m_out, v_out):
        g = r_ref[0]
        for s in range(1, N_DEV):
            g = g + r_ref[s]
        delta, mn, vn = _adamw_math(w_ref[...], g, m_ref[...], v_ref[...])
        g_out[...] = g
        d_out[...] = delta
        m_out[...] = mn
        v_out[...] = vn

    blk = pl.BlockSpec((tr, LANES), lambda i: (i, 0))
    sds = jax.ShapeDtypeStruct(w.shape, F32)
    return pl.pallas_call(
        body, grid=(SMALL_TILES,),
        in_specs=[pl.BlockSpec((N_DEV, tr, LANES), lambda i: (0, i, 0)), blk, blk, blk],
        out_specs=[blk, blk, blk, blk], out_shape=[sds, sds, sds, sds],
        compiler_params=_cp(("parallel",), 40), name="adamw_small")(recv, w, m, v)


def _rms(t):
    return lax.rsqrt(jnp.mean(t * t, axis=-1, keepdims=True) + EPS)


def _rms_bwd(t, r, gain, dh, dres):
    u = dh * gain
    dt = dres + r * u - t * ((r * r * r) * (1.0 / D_MODEL) * jnp.sum(t * u, axis=-1, keepdims=True))
    return dt, dh * t * r


def _in_proj(x, gain, w):
    L = x.shape[0]
    n = w.shape[1]
    tm, tn = 1024, 1024

    def body(x_ref, g_ref, w_ref, z_ref, h_ref):
        @pl.when(pl.program_id(1) == 0)
        def _():
            t = x_ref[...]
            h_ref[...] = (t * _rms(t) * g_ref[...]).astype(BF16)
        z_ref[...] = _nn(h_ref[...], w_ref[...])

    return pl.pallas_call(
        body, grid=(L // tm, n // tn),
        in_specs=[pl.BlockSpec((tm, D_MODEL), lambda i, j: (i, 0)), pl.BlockSpec((1, D_MODEL), lambda i, j: (0, 0)),
                  pl.BlockSpec((D_MODEL, tn), lambda i, j: (0, j))],
        out_specs=[pl.BlockSpec((tm, tn), lambda i, j: (i, j)), pl.BlockSpec((tm, D_MODEL), lambda i, j: (i, 0))],
        out_shape=[jax.ShapeDtypeStruct((L, n), F32), jax.ShapeDtypeStruct((L, D_MODEL), BF16)],
        compiler_params=_cp(("parallel", "arbitrary"), 40), name="in_proj")(x, gain, w)


PIECE_W = 512


def _piece_columns(pieces):
    cols = []
    for p, arr in enumerate(pieces):
        cols.extend((p, off) for off in range(0, arr.shape[1], PIECE_W))
    return cols


def _in_proj_bwd(pieces, w, x, gain, dres):
    L = x.shape[0]
    tm = 512
    npc = len(pieces)
    cols = _piece_columns(pieces)
    per_step = D_MODEL // PIECE_W
    nk = len(cols) // per_step

    def body(*refs):
        dz_refs = refs[:npc]
        w_ref, x_ref, g_ref, dr_ref, dx_ref, dg_ref, acc = refs[npc:]
        i, k = pl.program_id(0), pl.program_id(1)

        @pl.when(k == 0)
        def _():
            acc[...] = jnp.zeros_like(acc)

        for kk in range(nk):
            @pl.when(k == kk)
            def _(kk=kk):
                parts = [dz_refs[p][:, off:off + PIECE_W].astype(BF16)
                         for p, off in cols[kk * per_step:(kk + 1) * per_step]]
                acc[...] += _nt(jnp.concatenate(parts, axis=1), w_ref[...])

        @pl.when(k == nk - 1)
        def _():
            t = x_ref[...]
            dt, dgt = _rms_bwd(t, _rms(t), g_ref[...], acc[...], dr_ref[...])
            dx_ref[...] = dt

            @pl.when(i == 0)
            def _():
                dg_ref[...] = jnp.zeros_like(dg_ref)
            dg_ref[...] += jnp.sum(dgt, axis=0, keepdims=True)

    row = pl.BlockSpec((tm, D_MODEL), lambda i, k: (i, 0))
    vec = pl.BlockSpec((1, D_MODEL), lambda i, k: (0, 0))
    piece_specs = [pl.BlockSpec((tm, arr.shape[1]), lambda i, k: (i, 0)) for arr in pieces]
    return pl.pallas_call(
        body, grid=(L // tm, nk),
        in_specs=piece_specs + [pl.BlockSpec((D_MODEL, D_MODEL), lambda i, k: (0, k)), row, vec, row],
        out_specs=[row, vec],
        out_shape=[jax.ShapeDtypeStruct((L, D_MODEL), F32), jax.ShapeDtypeStruct((1, D_MODEL), F32)],
        scratch_shapes=[pltpu.VMEM((tm, D_MODEL), F32)],
        compiler_params=_cp(("arbitrary", "arbitrary"), 56), name="in_proj_bwd")(*pieces, w, x, gain, dres)


def _tn(a, b, name):
    m, na = a.shape
    nb = b.shape[1]
    ta, tb, tm = min(na, 1024), min(nb, 1024), 1024
    nm = m // tm

    def body(a_ref, b_ref, o_ref):
        @pl.when(pl.program_id(2) == 0)
        def _():
            o_ref[...] = jnp.zeros_like(o_ref)
        o_ref[...] += _tn_dot(a_ref[...].astype(BF16), b_ref[...].astype(BF16))

    return pl.pallas_call(
        body, grid=(na // ta, nb // tb, nm),
        in_specs=[pl.BlockSpec((tm, ta), lambda i, j, k: (k, i)), pl.BlockSpec((tm, tb), lambda i, j, k: (k, j))],
        out_specs=pl.BlockSpec((ta, tb), lambda i, j, k: (i, j)),
        out_shape=jax.ShapeDtypeStruct((na, nb), F32),
        compiler_params=_cp(("parallel", "parallel", "arbitrary"), 48), name=name)(a, b)


def _tn_pieces(a, pieces, name):
    m, na = a.shape
    npc = len(pieces)
    cols = _piece_columns(pieces)
    per_block = D_MODEL // PIECE_W
    nj = len(cols) // per_block
    tm = 512
    nm = m // tm
    block_of_piece = {}
    for c, (p, _) in enumerate(cols):
        block_of_piece[p] = c // per_block

    def body(*refs):
        a_ref = refs[0]
        b_refs = refs[1:1 + npc]
        o_ref = refs[1 + npc]
        j = pl.program_id(0)

        @pl.when(pl.program_id(1) == 0)
        def _():
            o_ref[...] = jnp.zeros_like(o_ref)

        for jj in range(nj):
            @pl.when(j == jj)
            def _(jj=jj):
                parts = [b_refs[p][:, off:off + PIECE_W].astype(BF16)
                         for p, off in cols[jj * per_block:(jj + 1) * per_block]]
                o_ref[...] += _tn_dot(a_ref[...], jnp.concatenate(parts, axis=1))

    piece_specs = [pl.BlockSpec((tm, arr.shape[1]),
                                functools.partial(lambda j, k, jj: (jnp.where(j == jj, k, 0), 0), jj=block_of_piece[p]))
                   for p, arr in enumerate(pieces)]
    return pl.pallas_call(
        body, grid=(nj, nm),
        in_specs=[pl.BlockSpec((tm, na), lambda j, k: (k, 0))] + piece_specs,
        out_specs=pl.BlockSpec((na, D_MODEL), lambda j, k: (0, j)),
        out_shape=jax.ShapeDtypeStruct((na, D_MODEL * nj), F32),
        compiler_params=_cp(("parallel", "arbitrary"), 56), name=name)(a, *pieces)


def _loss_grad(xf, target):
    L = xf.shape[0]
    tm = 1024

    def body(x_ref, t_ref, dy_ref, l_ref):
        e = x_ref[...] - t_ref[...]
        dy_ref[...] = e * (1.0 / D_MODEL)

        @pl.when(pl.program_id(0) == 0)
        def _():
            l_ref[...] = jnp.zeros_like(l_ref)
        l_ref[...] += jnp.sum(jnp.sum(e * e, axis=1, keepdims=True), axis=0, keepdims=True) * (0.5 / D_MODEL)

    row = pl.BlockSpec((tm, D_MODEL), lambda i: (i, 0))
    return pl.pallas_call(
        body, grid=(L // tm,), in_specs=[row, row],
        out_specs=[row, pl.BlockSpec((1, 1), lambda i: (0, 0))],
        out_shape=[jax.ShapeDtypeStruct((L, D_MODEL), F32), jax.ShapeDtypeStruct((1, 1), F32)],
        compiler_params=_cp(("arbitrary",), 40), name="loss_grad")(xf, target)


def _ffn_fwd(x, gain, wg, wu, wd):
    L = x.shape[0]
    ff = wg.shape[1]
    tm, tf = 1024, 2 * FF_CHUNK
    nf = ff // tf

    def body(x_ref, g_ref, wg_ref, wu_ref, wd_ref, o_ref, h_scr, acc):
        c = pl.program_id(1)

        @pl.when(c == 0)
        def _():
            t = x_ref[...]
            h_scr[...] = (t * _rms(t) * g_ref[...]).astype(BF16)
            acc[...] = jnp.zeros_like(acc)

        h = h_scr[...]
        down = []
        for cols in (slice(0, FF_CHUNK), slice(FF_CHUNK, 2 * FF_CHUNK)):
            gate = _nn(h, wg_ref[:, cols])
            up = _nn(h, wu_ref[:, cols])
            hid = gate * _sigmoid(gate) * up
            down.append(_nn(hid.astype(BF16), wd_ref[cols, :]))
        acc[...] += down[0] + down[1]

        @pl.when(c == nf - 1)
        def _():
            o_ref[...] = x_ref[...] + acc[...]

    row = pl.BlockSpec((tm, D_MODEL), lambda i, c: (i, 0))
    return pl.pallas_call(
        body, grid=(L // tm, nf),
        in_specs=[row, pl.BlockSpec((1, D_MODEL), lambda i, c: (0, 0)),
                  pl.BlockSpec((D_MODEL, tf), lambda i, c: (0, c)), pl.BlockSpec((D_MODEL, tf), lambda i, c: (0, c)),
                  pl.BlockSpec((tf, D_MODEL), lambda i, c: (c, 0))],
        out_specs=row, out_shape=jax.ShapeDtypeStruct((L, D_MODEL), F32),
        scratch_shapes=[pltpu.VMEM((tm, D_MODEL), BF16), pltpu.VMEM((tm, D_MODEL), F32)],
        compiler_params=_cp(("parallel", "arbitrary"), 48), name="ffn_fwd")(x, gain, wg, wu, wd)


def _ffn_bwd(x, gain, wg, wu, wd, dxo):
    L = x.shape[0]
    ff = wg.shape[1]
    tm, tf = 512, 2 * FF_CHUNK
    nf = ff // tf

    def body(x_ref, g_ref, wg_ref, wu_ref, wd_ref, dxo_ref, dx_ref, h_ref, hid_ref, dgate_ref, dup_ref, dg_ref,
             acc, dxo_b):
        i, c = pl.program_id(0), pl.program_id(1)

        @pl.when(c == 0)
        def _():
            t = x_ref[...]
            h_ref[...] = (t * _rms(t) * g_ref[...]).astype(BF16)
            acc[...] = jnp.zeros_like(acc)
            dxo_b[...] = dxo_ref[...].astype(BF16)

        h = h_ref[...]
        back = []
        for cols in (slice(0, FF_CHUNK), slice(FF_CHUNK, 2 * FF_CHUNK)):
            gate = _nn(h, wg_ref[:, cols])
            up = _nn(h, wu_ref[:, cols])
            sg = _sigmoid(gate)
            silu = gate * sg
            hid_ref[:, cols] = (silu * up).astype(BF16)
            dhid = _nt(dxo_b[...], wd_ref[cols, :])
            dup = (dhid * silu).astype(BF16)
            dgate = (dhid * up * (sg * (1.0 + gate * (1.0 - sg)))).astype(BF16)
            dup_ref[:, cols] = dup
            dgate_ref[:, cols] = dgate
            back.append(_nt(dgate, wg_ref[:, cols]) + _nt(dup, wu_ref[:, cols]))
        acc[...] += back[0] + back[1]

        @pl.when(c == nf - 1)
        def _():
            t = x_ref[...]
            dt, dgt = _rms_bwd(t, _rms(t), g_ref[...], acc[...], dxo_ref[...])
            dx_ref[...] = dt

            @pl.when(i == 0)
            def _():
                dg_ref[...] = jnp.zeros_like(dg_ref)
            dg_ref[...] += jnp.sum(dgt, axis=0, keepdims=True)

    row = pl.BlockSpec((tm, D_MODEL), lambda i, c: (i, 0))
    vec = pl.BlockSpec((1, D_MODEL), lambda i, c: (0, 0))
    wcol = pl.BlockSpec((D_MODEL, tf), lambda i, c: (0, c))
    hcol = pl.BlockSpec((tm, tf), lambda i, c: (i, c))
    return pl.pallas_call(
        body, grid=(L // tm, nf),
        in_specs=[row, vec, wcol, wcol, pl.BlockSpec((tf, D_MODEL), lambda i, c: (c, 0)), row],
        out_specs=[row, row, hcol, hcol, hcol, vec],
        out_shape=[jax.ShapeDtypeStruct((L, D_MODEL), F32), jax.ShapeDtypeStruct((L, D_MODEL), BF16),
                   jax.ShapeDtypeStruct((L, ff), BF16), jax.ShapeDtypeStruct((L, ff), BF16),
                   jax.ShapeDtypeStruct((L, ff), BF16), jax.ShapeDtypeStruct((1, D_MODEL), F32)],
        scratch_shapes=[pltpu.VMEM((tm, D_MODEL), F32), pltpu.VMEM((tm, D_MODEL), BF16)],
        compiler_params=_cp(("arbitrary", "arbitrary"), 56), name="ffn_bwd")(x, gain, wg, wu, wd, dxo)


GELU_K = math.sqrt(2.0 / math.pi)
GELU_C = 0.044715


def _gelu(y):
    return 0.5 * y * (1.0 + jnp.tanh(GELU_K * (y + GELU_C * (y * y * y))))


def _gelu_grad(y):
    th = jnp.tanh(GELU_K * (y + GELU_C * (y * y * y)))
    return 0.5 * (1.0 + th) + 0.5 * y * (1.0 - th * th) * (GELU_K * (1.0 + 3.0 * GELU_C * (y * y)))


def _merge_groups(o_refs, l_refs):
    ls = [r[...] for r in l_refs]
    os_ = [r[...] for r in o_refs]
    lmax = jnp.maximum(jnp.maximum(ls[0], ls[1]), ls[2])
    es = [jnp.exp(l - lmax) for l in ls]
    inv = 1.0 / (es[0] + es[1] + es[2])
    ws = [e * inv for e in es]
    a = ws[0] * os_[0] + ws[1] * os_[1] + ws[2] * os_[2]
    return ws, os_, a


def _mix_fwd(ols, y, z, x, wp, wa, wb, wo):
    L = x.shape[0]
    tm = 256

    def body(o0, l0, o1, l1, o2, l2, y_ref, ga_ref, gs_ref, x_ref, wp_ref, wa_ref, wb_ref, wo_ref, out_ref):
        _, _, a = _merge_groups((o0, o1, o2), (l0, l1, l2))
        a_out = _nn(a.astype(BF16), wp_ref[...])
        yg = _gelu(y_ref[...]).astype(BF16)
        s_out = _nn(yg, wa_ref[...]) * _sigmoid(_nn(yg, wb_ref[...]))
        mix = _sigmoid(ga_ref[...]) * a_out + _sigmoid(gs_ref[...]) * s_out
        out_ref[...] = x_ref[...] + _nn(mix.astype(BF16), wo_ref[...])

    half = pl.BlockSpec((tm, ATTN_W), lambda i: (i, 0))
    row = pl.BlockSpec((tm, D_MODEL), lambda i: (i, 0))
    w512 = pl.BlockSpec((ATTN_W, D_MODEL), lambda i: (0, 0))
    return pl.pallas_call(
        body, grid=(L // tm,),
        in_specs=[half] * 7 + [pl.BlockSpec((tm, D_MODEL), lambda i: (i, 5)),
                               pl.BlockSpec((tm, D_MODEL), lambda i: (i, 6)), row, w512, w512, w512,
                               pl.BlockSpec((D_MODEL, D_MODEL), lambda i: (0, 0))],
        out_specs=row, out_shape=jax.ShapeDtypeStruct((L, D_MODEL), F32),
        compiler_params=_cp(("parallel",), 48), name="mix_fwd")(*ols, y, z, z, x, wp, wa, wb, wo)


def _mix_bwd(dxm, ols, y, z, wp, wa, wb, wo):
    L = dxm.shape[0]
    tm = 256

    def body(dx_ref, o0, l0, o1, l1, o2, l2, y_ref, ga_ref, gs_ref, wp_ref, wa_ref, wb_ref, wo_ref,
             do0, dl0, do1, dl1, do2, dl2, dy_ref, dga_ref, dgs_ref, a_ref, yg_ref, mix_ref, dao_ref, dpa_ref,
             dpb_ref):
        ws, os_, a = _merge_groups((o0, o1, o2), (l0, l1, l2))
        ab = a.astype(BF16)
        a_out = _nn(ab, wp_ref[...])
        yv = y_ref[...]
        yg = _gelu(yv).astype(BF16)
        pa = _nn(yg, wa_ref[...])
        spb = _sigmoid(_nn(yg, wb_ref[...]))
        s_out = pa * spb
        sga = _sigmoid(ga_ref[...])
        sgs = _sigmoid(gs_ref[...])
        mix = sga * a_out + sgs * s_out
        dmix = _nt(dx_ref[...].astype(BF16), wo_ref[...])
        da_out = (sga * dmix).astype(BF16)
        ds_out = sgs * dmix
        dpa = (ds_out * spb).astype(BF16)
        dpb = (ds_out * pa * spb * (1.0 - spb)).astype(BF16)
        dga_ref[...] = (dmix * a_out * sga * (1.0 - sga)).astype(BF16)
        dgs_ref[...] = (dmix * s_out * sgs * (1.0 - sgs)).astype(BF16)
        dy_ref[...] = (_nt(dpa, wa_ref[...]) + _nt(dpb, wb_ref[...])) * _gelu_grad(yv)
        da = _nt(da_out, wp_ref[...])
        for w, o, do_ref, dl_ref in zip(ws, os_, (do0, do1, do2), (dl0, dl1, dl2)):
            do_ref[...] = w * da
            dl_ref[...] = da * w * (o - a)
        a_ref[...] = ab
        yg_ref[...] = yg
        mix_ref[...] = mix.astype(BF16)
        dao_ref[...] = da_out
        dpa_ref[...] = dpa
        dpb_ref[...] = dpb

    half = pl.BlockSpec((tm, ATTN_W), lambda i: (i, 0))
    row = pl.BlockSpec((tm, D_MODEL), lambda i: (i, 0))
    w512 = pl.BlockSpec((ATTN_W, D_MODEL), lambda i: (0, 0))
    hf = jax.ShapeDtypeStruct((L, ATTN_W), F32)
    hb = jax.ShapeDtypeStruct((L, ATTN_W), BF16)
    rb = jax.ShapeDtypeStruct((L, D_MODEL), BF16)
    return pl.pallas_call(
        body, grid=(L // tm,),
        in_specs=[row] + [half] * 7 + [pl.BlockSpec((tm, D_MODEL), lambda i: (i, 5)),
                                       pl.BlockSpec((tm, D_MODEL), lambda i: (i, 6)), w512, w512, w512,
                                       pl.BlockSpec((D_MODEL, D_MODEL), lambda i: (0, 0))],
        out_specs=[half] * 7 + [row, row, half, half, row, row, row, row],
        out_shape=[hf] * 7 + [rb, rb, hb, hb, rb, rb, rb, rb],
        compiler_params=_cp(("parallel",), 56), name="mix_bwd")(dxm, *ols, y, z, z, wp, wa, wb, wo)


N_ATTN_ITERS = ATTN_ROWS // BLK


def _class_rows(ref, start, d):
    if d == 1:
        return ref[pl.ds(pl.multiple_of(start, BLK), BLK), :]
    return ref[pl.ds(start, BLK, stride=d), :]


def _set_class_rows(ref, start, d, val):
    if d == 1:
        ref[pl.ds(pl.multiple_of(start, BLK), BLK), :] = val
    else:
        ref[pl.ds(start, BLK, stride=d), :] = val


def _head_masks():
    lane = lax.broadcasted_iota(jnp.int32, (1, LANES), 1)
    m0 = (lane < HEAD_DIM).astype(F32)
    return m0, 1.0 - m0


def _head_norm(t, gain2, m0, m1):
    tt = t * t
    r0 = lax.rsqrt(jnp.sum(tt * m0, axis=-1, keepdims=True) * (1.0 / HEAD_DIM) + EPS)
    r1 = lax.rsqrt(jnp.sum(tt * m1, axis=-1, keepdims=True) * (1.0 / HEAD_DIM) + EPS)
    r = m0 * r0 + m1 * r1
    return t * r * gain2, r


def _head_norm_bwd(t, r, gain2, dy, m0, m1):
    u = dy * gain2
    tu = t * u
    s = m0 * jnp.sum(tu * m0, axis=-1, keepdims=True) + m1 * jnp.sum(tu * m1, axis=-1, keepdims=True)
    return r * u - t * (r * r * r) * s * (1.0 / HEAD_DIM), jnp.sum(dy * t * r, axis=0, keepdims=True)


def _band_masks():
    qi = lax.broadcasted_iota(jnp.int32, (BLK, 2 * BLK), 0)
    ki = lax.broadcasted_iota(jnp.int32, (BLK, 2 * BLK), 1)
    dist = BLK + qi - ki
    return (dist >= 0) & (dist <= BLK), ki >= BLK


ATTN_SCALE = HEAD_DIM ** -0.5


def _attn_scores(qm, kw, ok):
    return jnp.where(ok, _nt(qm, kw), -1e30)


def _attn_probs(qm, kw, ok):
    s = _attn_scores(qm, kw, ok)
    mx = jnp.max(s, axis=-1, keepdims=True)
    p = jnp.exp(s - mx)
    den = jnp.sum(p, axis=-1, keepdims=True)
    return p, den, mx


NORM_ROWS = 256


def _norm_rows(src_ref, gain2, dst_ref, m0, m1):
    n = src_ref.shape[0]
    step = min(NORM_ROWS, n)
    for r0 in range(0, n, step):
        dst_ref[r0:r0 + step, :] = _head_norm(src_ref[r0:r0 + step, :], gain2, m0, m1)[0]


def _norm_rows_bwd(src_ref, gain2, dy_ref, dst_ref, m0, m1):
    n = src_ref.shape[0]
    step = min(NORM_ROWS, n)
    dgain = jnp.zeros((1, LANES), F32)
    for r0 in range(0, n, step):
        t = src_ref[r0:r0 + step, :]
        _, r = _head_norm(t, gain2, m0, m1)
        dt, dg = _head_norm_bwd(t, r, gain2, dy_ref[r0:r0 + step, :], m0, m1)
        dst_ref[r0:r0 + step, :] = dt
        dgain = dgain + dg
    return dgain


def _attn_operands(it, d, first_step, q_ref, kc_ref, kp_ref, vc_ref, vp_ref, band, is_cur):
    j = it // d
    start = (it - j * d) + (d * BLK) * j
    before = jnp.maximum(start - d * BLK, 0)
    inside = j > 0
    q2 = _class_rows(q_ref, start, d)
    kc2 = _class_rows(kc_ref, start, d)
    vc2 = _class_rows(vc_ref, start, d)
    kp2 = jnp.where(inside, _class_rows(kc_ref, before, d), _class_rows(kp_ref, it - j * d, d))
    vp2 = jnp.where(inside, _class_rows(vc_ref, before, d), _class_rows(vp_ref, it - j * d, d))
    has_prev = inside | jnp.logical_not(first_step)
    return start, q2, kp2, kc2, vp2, vc2, band & (is_cur | has_prev)


def _attn_specs(d, step_of):
    nq = N_ATTN_ITERS // d

    def cur(c):
        return pl.BlockSpec((ATTN_ROWS, LANES), lambda hp, n: (step_of(n), c + hp))

    def prev(c):
        return pl.BlockSpec((d * BLK, LANES), lambda hp, n: (jnp.maximum(step_of(n) * nq - 1, 0), c + hp))

    return cur, prev, pl.BlockSpec((1, LANES), lambda hp, n: (0, 0))


def _attn_fwd(z, gq2, gk2, group):
    L = z.shape[0]
    d = DILATIONS[group]
    nsb = L // ATTN_ROWS
    cq, ck, cv = group * 4, 12 + group * 4, 24 + group * 4

    def body(q_ref, kc_ref, kp_ref, vc_ref, vp_ref, gq_ref, gk_ref, o_ref, l_ref, qn_scr, kn_scr, kpn_scr):
        first_step = pl.program_id(1) == 0
        band, is_cur = _band_masks()
        m0, m1 = _head_masks()
        _norm_rows(q_ref, gq_ref[...], qn_scr, m0, m1)
        _norm_rows(kc_ref, gk_ref[...], kn_scr, m0, m1)
        _norm_rows(kp_ref, gk_ref[...], kpn_scr, m0, m1)

        def per_block(it, carry):
            start, qn, kpn, kcn, vp2, vc2, ok = _attn_operands(
                it, d, first_step, qn_scr, kn_scr, kpn_scr, vc_ref, vp_ref, band, is_cur)
            kw = jnp.concatenate([kpn, kcn], axis=0).astype(BF16)
            vw = jnp.concatenate([vp2, vc2], axis=0).astype(BF16)
            o2 = jnp.zeros((BLK, LANES), F32)
            l2 = jnp.zeros((BLK, LANES), F32)
            for mh in (m0, m1):
                p, den, mx = _attn_probs((qn * (mh * ATTN_SCALE)).astype(BF16), kw, ok)
                o2 = o2 + mh * (_nn(p.astype(BF16), vw) / den)
                l2 = l2 + mh * (mx + jnp.log(den))
            _set_class_rows(o_ref, start, d, o2)
            _set_class_rows(l_ref, start, d, l2)
            return carry

        lax.fori_loop(0, N_ATTN_ITERS, per_block, 0, unroll=2)

    cur, prev, vec = _attn_specs(d, lambda n: n)
    out = pl.BlockSpec((ATTN_ROWS, LANES), lambda hp, n: (n, hp))
    sds = jax.ShapeDtypeStruct((L, ATTN_W), F32)
    return pl.pallas_call(
        body, grid=(4, nsb),
        in_specs=[cur(cq), cur(ck), prev(ck), cur(cv), prev(cv), vec, vec],
        out_specs=[out, out], out_shape=[sds, sds],
        scratch_shapes=[pltpu.VMEM((ATTN_ROWS, LANES), F32), pltpu.VMEM((ATTN_ROWS, LANES), F32),
                        pltpu.VMEM((d * BLK, LANES), F32)],
        compiler_params=_cp(("parallel", "arbitrary"), 48), name=f"attn_fwd_g{group}")(z, z, z, z, z, gq2, gk2)


def _attn_bwd(z, gq2, gk2, o, lse, do, dl, group):
    L = z.shape[0]
    d = DILATIONS[group]
    nsb = L // ATTN_ROWS
    cq, ck, cv = group * 4, 12 + group * 4, 24 + group * 4

    def body(q_ref, kc_ref, kp_ref, vc_ref, vp_ref, gq_ref, gk_ref, o_ref, l_ref, do_ref, dl_ref,
             dq_ref, dk_ref, dv_ref, dgq_ref, dgk_ref, ck_scr, cv_scr, qn_scr, kn_scr, kpn_scr, dqn_scr, dkn_scr):
        hp, n = pl.program_id(0), pl.program_id(1)
        first_step = n == nsb - 1
        band, is_cur = _band_masks()
        m0, m1 = _head_masks()
        gq, gk = gq_ref[...], gk_ref[...]
        _norm_rows(q_ref, gq, qn_scr, m0, m1)
        _norm_rows(kc_ref, gk, kn_scr, m0, m1)
        _norm_rows(kp_ref, gk, kpn_scr, m0, m1)

        @pl.when((hp == 0) & (n == 0))
        def _():
            dgq_ref[...] = jnp.zeros_like(dgq_ref)
            dgk_ref[...] = jnp.zeros_like(dgk_ref)

        @pl.when(n == 0)
        def _():
            ck_scr[...] = jnp.zeros_like(ck_scr)
            cv_scr[...] = jnp.zeros_like(cv_scr)

        def per_block(i, carry):
            it = N_ATTN_ITERS - 1 - i
            start, qn, kpn, kcn, vp2, vc2, ok = _attn_operands(
                it, d, first_step, qn_scr, kn_scr, kpn_scr, vc_ref, vp_ref, band, is_cur)
            r = it - (it // d) * d
            kw = jnp.concatenate([kpn, kcn], axis=0).astype(BF16)
            vw = jnp.concatenate([vp2, vc2], axis=0).astype(BF16)
            l2 = _class_rows(l_ref, start, d)
            c2 = _class_rows(dl_ref, start, d) - _class_rows(do_ref, start, d) * _class_rows(o_ref, start, d)
            do2 = _class_rows(do_ref, start, d)
            dqn = jnp.zeros((BLK, LANES), F32)
            dkw = jnp.zeros((2 * BLK, LANES), F32)
            dvw = jnp.zeros((2 * BLK, LANES), F32)
            for mh in (m0, m1):
                qm = (qn * (mh * ATTN_SCALE)).astype(BF16)
                lse = jnp.max(jnp.where(mh > 0.5, l2, -3e38), axis=-1, keepdims=True)
                pn = jnp.exp(_attn_scores(qm, kw, ok) - lse)
                dohb = (do2 * mh).astype(BF16)
                dvw = dvw + _tn_dot(pn.astype(BF16), dohb)
                ds = (pn * (_nt(dohb, vw) + jnp.sum(c2 * mh, axis=-1, keepdims=True))).astype(BF16)
                dqn = dqn + (mh * ATTN_SCALE) * _nn(ds, kw)
                dkw = dkw + _tn_dot(ds, qm)
            _set_class_rows(dqn_scr, start, d, dqn)
            _set_class_rows(dkn_scr, start, d, ck_scr[r] + dkw[BLK:])
            _set_class_rows(dv_ref, start, d, cv_scr[r] + dvw[BLK:])
            ck_scr[r] = dkw[:BLK]
            cv_scr[r] = dvw[:BLK]
            return carry

        lax.fori_loop(0, N_ATTN_ITERS, per_block, 0, unroll=2)
        dgq_ref[...] += _norm_rows_bwd(q_ref, gq, dqn_scr, dq_ref, m0, m1)
        dgk_ref[...] += _norm_rows_bwd(kc_ref, gk, dkn_scr, dk_ref, m0, m1)

    cur, prev, vec = _attn_specs(d, lambda n: nsb - 1 - n)
    sds = jax.ShapeDtypeStruct((L, ATTN_W), F32)
    vsd = jax.ShapeDtypeStruct((1, LANES), F32)
    return pl.pallas_call(
        body, grid=(4, nsb),
        in_specs=[cur(cq), cur(ck), prev(ck), cur(cv), prev(cv), vec, vec, cur(0), cur(0), cur(0), cur(0)],
        out_specs=[cur(0), cur(0), cur(0), vec, vec], out_shape=[sds, sds, sds, vsd, vsd],
        scratch_shapes=[pltpu.VMEM((d, BLK, LANES), F32), pltpu.VMEM((d, BLK, LANES), F32),
                        pltpu.VMEM((ATTN_ROWS, LANES), F32), pltpu.VMEM((ATTN_ROWS, LANES), F32),
                        pltpu.VMEM((d * BLK, LANES), F32),
                        pltpu.VMEM((ATTN_ROWS, LANES), F32), pltpu.VMEM((ATTN_ROWS, LANES), F32)],
        compiler_params=_cp(("arbitrary", "arbitrary"), 56),
        name=f"attn_bwd_g{group}")(z, z, z, z, z, gq2, gk2, o, lse, do, dl)


BLOCK_STATES = SSM_STATES // SSM_BLOCKS
BLOCK_CH = SSM_W // SSM_BLOCKS
SLABS_PER_BLOCK = BLOCK_STATES // LANES


SCAN_STEPS = 4


def _store_block(bufs, b, val, tm):
    for s in range(SLABS_PER_BLOCK):
        k = SLABS_PER_BLOCK * b + s
        bufs[k % 2][pl.ds(8 + k // 2, tm, stride=8), :] = val[:, s * LANES:(s + 1) * LANES]


def _load_block(bufs, b, tm):
    tiles = []
    for s in range(SLABS_PER_BLOCK):
        k = SLABS_PER_BLOCK * b + s
        tiles.append(bufs[k % 2][pl.ds(8 + k // 2, tm, stride=8), :])
    return jnp.concatenate(tiles, axis=1).astype(BF16)


def _ssm_project_in(ub, bdr_ref, bdi_ref, sr, si, tm):
    for b in range(SSM_BLOCKS):
        ubb = ub[:, b * BLOCK_CH:(b + 1) * BLOCK_CH]
        _store_block(sr, b, _nn(ubb, bdr_ref[b]), tm)
        _store_block(si, b, _nn(ubb, bdi_ref[b]), tm)


def _ssm_scan(a, x0, sr, si, tm):
    ar0, ar1, ai0, ai1 = a
    sr[0][0:8, :], sr[1][0:8, :], si[0][0:8, :], si[1][0:8, :] = x0

    def steps(it, c):
        xr0, xr1, xi0, xi1 = c
        base = it * (8 * SCAN_STEPS) + 8
        for q in range(SCAN_STEPS):
            rows = pl.ds(pl.multiple_of(base + 8 * q, 8), 8)
            nr0 = ar0 * xr0 - ai0 * xi0 + sr[0][rows, :]
            ni0 = ar0 * xi0 + ai0 * xr0 + si[0][rows, :]
            nr1 = ar1 * xr1 - ai1 * xi1 + sr[1][rows, :]
            ni1 = ar1 * xi1 + ai1 * xr1 + si[1][rows, :]
            sr[0][rows, :] = nr0
            si[0][rows, :] = ni0
            sr[1][rows, :] = nr1
            si[1][rows, :] = ni1
            xr0, xr1, xi0, xi1 = nr0, nr1, ni0, ni1
        return xr0, xr1, xi0, xi1

    return lax.fori_loop(0, tm // SCAN_STEPS, steps, x0)


def _load_a(ar_ref, ai_ref):
    return ar_ref[:, :LANES], ar_ref[:, LANES:], ai_ref[:, :LANES], ai_ref[:, LANES:]


def _ssm_fwd(z, ar8, ai8, bdr, bdi, cdr, cdi, dsk):
    L = z.shape[0]
    tm = SSM_TM
    nc = L // tm

    def body(u_ref, ar_ref, ai_ref, bdr_ref, bdi_ref, cdr_ref, cdi_ref, dsk_ref, y_ref, cin_ref,
             sr0, sr1, si0, si1, car):
        sr, si = (sr0, sr1), (si0, si1)

        @pl.when(pl.program_id(0) == 0)
        def _():
            car[...] = jnp.zeros_like(car)

        u = u_ref[...]
        _ssm_project_in(u.astype(BF16), bdr_ref, bdi_ref, sr, si, tm)
        cin_ref[0] = car[...]
        xr0, xr1, xi0, xi1 = _ssm_scan(_load_a(ar_ref, ai_ref), (car[0], car[1], car[2], car[3]), sr, si, tm)
        car[0], car[1], car[2], car[3] = xr0, xr1, xi0, xi1
        for b in range(SSM_BLOCKS):
            cols = slice(b * BLOCK_CH, (b + 1) * BLOCK_CH)
            y_ref[:, cols] = (dsk_ref[:, cols] * u[:, cols] + _nn(_load_block(sr, b, tm), cdr_ref[b])
                              - _nn(_load_block(si, b, tm), cdi_ref[b]))

    def const(shape):
        return pl.BlockSpec(shape, lambda i: (0,) * len(shape))

    state = pltpu.VMEM(((tm + 1) * 8, LANES), F32)
    wb = const((SSM_BLOCKS, BLOCK_CH, BLOCK_STATES))
    wc = const((SSM_BLOCKS, BLOCK_STATES, BLOCK_CH))
    return pl.pallas_call(
        body, grid=(nc,),
        in_specs=[pl.BlockSpec((tm, SSM_W), lambda i: (i, COL_U)), const((8, 256)), const((8, 256)),
                  wb, wb, wc, wc, const((1, SSM_W))],
        out_specs=[pl.BlockSpec((tm, SSM_W), lambda i: (i, 0)), pl.BlockSpec((1, 4, 8, LANES), lambda i: (i, 0, 0, 0))],
        out_shape=[jax.ShapeDtypeStruct((L, SSM_W), F32), jax.ShapeDtypeStruct((nc, 4, 8, LANES), F32)],
        scratch_shapes=[state, state, state, state, pltpu.VMEM((4, 8, LANES), F32)],
        compiler_params=_cp(("arbitrary",), 48), name="ssm_fwd")(z, ar8, ai8, bdr, bdi, cdr, cdi, dsk)


def _ssm_bwd(z, dy, cin, ar8, ai8, bdr, bdi, cdr, cdi, dsk):
    L = z.shape[0]
    tm = SSM_TM
    nc = L // tm

    def body(u_ref, dy_ref, cin_ref, ar_ref, ai_ref, dsk_ref, bdr_ref, bdi_ref, cdr_ref, cdi_ref,
             du_ref, da_ref, dds_ref, dbdr_ref, dbdi_ref, dcdr_ref, dcdi_ref,
             sr0, sr1, si0, si1, gr0, gr1, gi0, gi1, carg):
        sr, si, gr, gi = (sr0, sr1), (si0, si1), (gr0, gr1), (gi0, gi1)

        @pl.when(pl.program_id(0) == 0)
        def _():
            carg[...] = jnp.zeros_like(carg)
            for ref in (da_ref, dds_ref, dbdr_ref, dbdi_ref, dcdr_ref, dcdi_ref):
                ref[...] = jnp.zeros_like(ref)

        u = u_ref[...]
        ub = u.astype(BF16)
        dyv = dy_ref[...]
        dyb = dyv.astype(BF16)
        a = _load_a(ar_ref, ai_ref)
        ar0, ar1, ai0, ai1 = a
        x_in = (cin_ref[0, 0], cin_ref[0, 1], cin_ref[0, 2], cin_ref[0, 3])
        _ssm_project_in(ub, bdr_ref, bdi_ref, sr, si, tm)
        _ssm_scan(a, x_in, sr, si, tm)
        for b in range(SSM_BLOCKS):
            dyb_b = dyb[:, b * BLOCK_CH:(b + 1) * BLOCK_CH]
            _store_block(gr, b, _nt(dyb_b, cdr_ref[b]), tm)
            _store_block(gi, b, -_nt(dyb_b, cdi_ref[b]), tm)

        def grad_steps(it, c):
            (nr0, nr1, ni0, ni1), (d_r0, d_r1, d_i0, d_i1) = c
            base = (tm - SCAN_STEPS * (it + 1)) * 8
            for q in reversed(range(SCAN_STEPS)):
                prev = pl.ds(pl.multiple_of(base + 8 * q, 8), 8)
                rows = pl.ds(pl.multiple_of(base + 8 * q + 8, 8), 8)
                g_r0 = gr[0][rows, :] + ar0 * nr0 + ai0 * ni0
                g_i0 = gi[0][rows, :] + ar0 * ni0 - ai0 * nr0
                g_r1 = gr[1][rows, :] + ar1 * nr1 + ai1 * ni1
                g_i1 = gi[1][rows, :] + ar1 * ni1 - ai1 * nr1
                gr[0][rows, :] = g_r0
                gi[0][rows, :] = g_i0
                gr[1][rows, :] = g_r1
                gi[1][rows, :] = g_i1
                pr0, pr1, pi0, pi1 = sr[0][prev, :], sr[1][prev, :], si[0][prev, :], si[1][prev, :]
                d_r0 = d_r0 + pr0 * g_r0 + pi0 * g_i0
                d_r1 = d_r1 + pr1 * g_r1 + pi1 * g_i1
                d_i0 = d_i0 + pr0 * g_i0 - pi0 * g_r0
                d_i1 = d_i1 + pr1 * g_i1 - pi1 * g_r1
                nr0, nr1, ni0, ni1 = g_r0, g_r1, g_i0, g_i1
            return (nr0, nr1, ni0, ni1), (d_r0, d_r1, d_i0, d_i1)

        acc0 = (da_ref[0], da_ref[1], da_ref[2], da_ref[3])
        g_first, acc = lax.fori_loop(0, tm // SCAN_STEPS, grad_steps,
                                     ((carg[0], carg[1], carg[2], carg[3]), acc0))
        carg[0], carg[1], carg[2], carg[3] = g_first
        da_ref[0], da_ref[1], da_ref[2], da_ref[3] = acc

        for b in range(SSM_BLOCKS):
            cols = slice(b * BLOCK_CH, (b + 1) * BLOCK_CH)
            grb, gib = _load_block(gr, b, tm), _load_block(gi, b, tm)
            du_ref[:, cols] = dsk_ref[:, cols] * dyv[:, cols] + _nt(grb, bdr_ref[b]) + _nt(gib, bdi_ref[b])
            dbdr_ref[b] += _tn_dot(ub[:, cols], grb)
            dbdi_ref[b] += _tn_dot(ub[:, cols], gib)
            dcdr_ref[b] += _tn_dot(_load_block(sr, b, tm), dyb[:, cols])
            dcdi_ref[b] -= _tn_dot(_load_block(si, b, tm), dyb[:, cols])
        dds_ref[...] += jnp.sum(dyv * u, axis=0, keepdims=True)

    def const(shape):
        return pl.BlockSpec(shape, lambda i: (0,) * len(shape))

    state = pltpu.VMEM(((tm + 1) * 8, LANES), F32)
    wb = const((SSM_BLOCKS, BLOCK_CH, BLOCK_STATES))
    wc = const((SSM_BLOCKS, BLOCK_STATES, BLOCK_CH))
    return pl.pallas_call(
        body, grid=(nc,),
        in_specs=[pl.BlockSpec((tm, SSM_W), lambda i: (nc - 1 - i, COL_U)),
                  pl.BlockSpec((tm, SSM_W), lambda i: (nc - 1 - i, 0)),
                  pl.BlockSpec((1, 4, 8, LANES), lambda i: (nc - 1 - i, 0, 0, 0)),
                  const((8, 256)), const((8, 256)), const((1, SSM_W)), wb, wb, wc, wc],
        out_specs=[pl.BlockSpec((tm, SSM_W), lambda i: (nc - 1 - i, 0)), const((4, 8, LANES)), const((1, SSM_W)),
                   wb, wb, wc, wc],
        out_shape=[jax.ShapeDtypeStruct((L, SSM_W), F32), jax.ShapeDtypeStruct((4, 8, LANES), F32),
                   jax.ShapeDtypeStruct((1, SSM_W), F32),
                   jax.ShapeDtypeStruct((SSM_BLOCKS, BLOCK_CH, BLOCK_STATES), F32),
                   jax.ShapeDtypeStruct((SSM_BLOCKS, BLOCK_CH, BLOCK_STATES), F32),
                   jax.ShapeDtypeStruct((SSM_BLOCKS, BLOCK_STATES, BLOCK_CH), F32),
                   jax.ShapeDtypeStruct((SSM_BLOCKS, BLOCK_STATES, BLOCK_CH), F32)],
        scratch_shapes=[state] * 8 + [pltpu.VMEM((4, 8, LANES), F32)],
        compiler_params=_cp(("arbitrary",), 56), name="ssm_bwd")(z, dy, cin, ar8, ai8, dsk, bdr, bdi, cdr, cdi)


def _discretise(lam_re, lam_im, log_dt, b_re, b_im):
    dt = jnp.exp(log_dt)[:, None]
    mag = jnp.exp(lam_re * dt)
    ang = lam_im * dt
    abar_re = mag * jnp.cos(ang)
    abar_im = mag * jnp.sin(ang)
    nr = abar_re - 1.0
    ni = abar_im
    den = lam_re * lam_re + lam_im * lam_im
    cr = ((nr * lam_re + ni * lam_im) / den)[..., None]
    ci = ((ni * lam_re - nr * lam_im) / den)[..., None]
    return abar_re, abar_im, cr * b_re - ci * b_im, cr * b_im + ci * b_re


GROUPS_PER_BLOCK = 8


def _block_diag_in(bbar):
    eye = jnp.eye(GROUPS_PER_BLOCK, dtype=F32)
    return jnp.einsum("igpc,gh->igchp", bbar.reshape(SSM_BLOCKS, GROUPS_PER_BLOCK, 64, 16), eye).reshape(
        SSM_BLOCKS, BLOCK_CH, BLOCK_STATES)


def _block_diag_in_t(blocks):
    eye = jnp.eye(GROUPS_PER_BLOCK, dtype=F32)
    return jnp.einsum("igchp,gh->igpc", blocks.reshape(SSM_BLOCKS, GROUPS_PER_BLOCK, 16, GROUPS_PER_BLOCK, 64),
                      eye).reshape(32, 64, 16)


def _block_diag_out(c):
    eye = jnp.eye(GROUPS_PER_BLOCK, dtype=F32)
    return jnp.einsum("igcp,gh->igphc", c.reshape(SSM_BLOCKS, GROUPS_PER_BLOCK, 16, 64), eye).reshape(
        SSM_BLOCKS, BLOCK_STATES, BLOCK_CH)


def _block_diag_out_t(blocks):
    eye = jnp.eye(GROUPS_PER_BLOCK, dtype=F32)
    return jnp.einsum("igphc,gh->igcp", blocks.reshape(SSM_BLOCKS, GROUPS_PER_BLOCK, 64, GROUPS_PER_BLOCK, 16),
                      eye).reshape(32, 16, 64)


SMALL_NAMES = ("g_mix", "g_q", "g_k", "lambda_re", "lambda_im", "log_dt", "b_re", "b_im", "c_re", "c_im",
               "d_skip", "g_ffn")


def _pack_small(parts):
    flat = jnp.concatenate([parts[n].reshape(-1) for n in SMALL_NAMES])
    pad = (-flat.shape[0]) % (8 * LANES * SMALL_TILES)
    return jnp.pad(flat, (0, pad)).reshape(-1, LANES)


def _unpack_small(packed, like):
    flat = packed.reshape(-1)
    out, off = {}, 0
    for n in SMALL_NAMES:
        size = like[n].size
        out[n] = flat[off:off + size].reshape(like[n].shape)
        off += size
    return out


BIG_NAMES = ("w_in", "w_attn_proj", "w_glu_a", "w_glu_b", "w_out", "w_ffn_gate", "w_ffn_up", "w_ffn_down")
BIG_SHARD_AXIS = {"w_in": 2, "w_attn_proj": 2, "w_glu_a": 2, "w_glu_b": 2, "w_out": 1,
                  "w_ffn_gate": 2, "w_ffn_up": 2, "w_ffn_down": 1}
ADAMW_ROWS = {"w_in": 256, "w_attn_proj": 512, "w_glu_a": 512, "w_glu_b": 512, "w_out": 128,
              "w_ffn_gate": 256, "w_ffn_up": 256, "w_ffn_down": 176}


def kernel(x, g_mix, w_in, g_q, g_k, w_attn_proj, lambda_re, lambda_im, log_dt, b_re, b_im, c_re, c_im, d_skip, w_glu_a, w_glu_b, w_out, g_ffn, w_ffn_gate, w_ffn_up, w_ffn_down, loss_target, m_g_mix, m_w_in, m_g_q, m_g_k, m_w_attn_proj, m_lambda_re, m_lambda_im, m_log_dt, m_b_re, m_b_im, m_c_re, m_c_im, m_d_skip, m_w_glu_a, m_w_glu_b, m_w_out, m_g_ffn, m_w_ffn_gate, m_w_ffn_up, m_w_ffn_down, v_g_mix, v_w_in, v_g_q, v_g_k, v_w_attn_proj, v_lambda_re, v_lambda_im, v_log_dt, v_b_re, v_b_im, v_c_re, v_c_im, v_d_skip, v_w_glu_a, v_w_glu_b, v_w_out, v_g_ffn, v_w_ffn_gate, v_w_ffn_up, v_w_ffn_down):
    args = dict(locals())
    weights = {n: args[n] for n in BIG_NAMES + SMALL_NAMES}
    moments_m = {n: args["m_" + n] for n in BIG_NAMES + SMALL_NAMES}
    moments_v = {n: args["v_" + n] for n in BIG_NAMES + SMALL_NAMES}
    x0 = x[0]
    target = loss_target[0]

    shards = []
    for n in BIG_NAMES:
        w = weights[n]
        rows_to, cols_to = w.shape[1], w.shape[2]
        if n in ("w_ffn_gate", "w_ffn_up"):
            cols_to = FF_SHARD_PAD
        if n == "w_ffn_down":
            rows_to = FF_SHARD_PAD
        shards.append(_prep_weight(w, rows_to, cols_to, "prep_" + n))
    full = dict(zip(BIG_NAMES, _all_gather(shards, [BIG_SHARD_AXIS[n] for n in BIG_NAMES])))

    saved = []
    xl = x0
    for l in range(DEPTH):
        abar_re, abar_im, bb_re, bb_im = _discretise(lambda_re[l], lambda_im[l], log_dt[l], b_re[l], b_im[l])
        ssm = dict(ar8=abar_re.reshape(8, 256), ai8=abar_im.reshape(8, 256),
                   bdr=_block_diag_in(bb_re).astype(BF16), bdi=_block_diag_in(bb_im).astype(BF16),
                   cdr=_block_diag_out(c_re[l]).astype(BF16), cdi=_block_diag_out(c_im[l]).astype(BF16),
                   dsk=d_skip[l][None])
        gq2 = jnp.tile(g_q[l], 2)[None]
        gk2 = jnp.tile(g_k[l], 2)[None]
        z, h = _in_proj(xl, g_mix[l][None], full["w_in"][l])
        ols = []
        for g in range(N_GROUPS):
            ols.extend(_attn_fwd(z, gq2, gk2, g))
        y, cin = _ssm_fwd(z, **ssm)
        xm = _mix_fwd(ols, y, z, xl, full["w_attn_proj"][l], full["w_glu_a"][l], full["w_glu_b"][l], full["w_out"][l])
        xo = _ffn_fwd(xm, g_ffn[l][None], full["w_ffn_gate"][l], full["w_ffn_up"][l], full["w_ffn_down"][l])
        saved.append(dict(x=xl, z=z, h=h, ols=ols, y=y, cin=cin, xm=xm, ssm=ssm, gq2=gq2, gk2=gk2))
        xl = xo

    dxo, loss_local = _loss_grad(xl, target)
    loss = lax.psum(loss_local[0, 0], MESH_AXES)
    big_grads = {n: [None] * DEPTH for n in BIG_NAMES}
    small_grads = {n: [None] * DEPTH for n in SMALL_NAMES}
    for l in reversed(range(DEPTH)):
        s = saved[l]
        dxm, h2, hid, dgate, dup, dgffn = _ffn_bwd(s["xm"], g_ffn[l][None], full["w_ffn_gate"][l],
                                                   full["w_ffn_up"][l], full["w_ffn_down"][l], dxo)
        big_grads["w_ffn_down"][l] = _tn(hid, dxo, "grad_w_ffn_down")
        big_grads["w_ffn_gate"][l] = _tn(h2, dgate, "grad_w_ffn_gate")
        big_grads["w_ffn_up"][l] = _tn(h2, dup, "grad_w_ffn_up")
        (do0, dl0, do1, dl1, do2, dl2, dy, dga, dgs, a_b, yg_b, mix_b, dao_b, dpa_b, dpb_b) = _mix_bwd(
            dxm, s["ols"], s["y"], s["z"], full["w_attn_proj"][l], full["w_glu_a"][l], full["w_glu_b"][l],
            full["w_out"][l])
        big_grads["w_out"][l] = _tn(mix_b, dxm, "grad_w_out")
        big_grads["w_attn_proj"][l] = _tn(a_b, dao_b, "grad_w_attn_proj")
        big_grads["w_glu_a"][l] = _tn(yg_b, dpa_b, "grad_w_glu_a")
        big_grads["w_glu_b"][l] = _tn(yg_b, dpb_b, "grad_w_glu_b")
        du, da4, ddsk, dbdr, dbdi, dcdr, dcdi = _ssm_bwd(s["z"], dy, s["cin"], **s["ssm"])
        dqkv = []
        dgq = jnp.zeros((1, LANES), F32)
        dgk = jnp.zeros((1, LANES), F32)
        for g, (do_g, dl_g) in enumerate(((do0, dl0), (do1, dl1), (do2, dl2))):
            dq, dk, dv, dgq_g, dgk_g = _attn_bwd(s["z"], s["gq2"], s["gk2"], s["ols"][2 * g], s["ols"][2 * g + 1],
                                                 do_g, dl_g, g)
            dqkv.append((dq, dk, dv))
            dgq, dgk = dgq + dgq_g, dgk + dgk_g
        pieces = [dqkv[g][j] for j in range(3) for g in range(N_GROUPS)] + [du, dga, dgs]
        dxo, dgmix = _in_proj_bwd(pieces, full["w_in"][l], s["x"], g_mix[l][None], dxm)
        big_grads["w_in"][l] = _tn_pieces(s["h"], pieces, "grad_w_in")
        _, disc_vjp = jax.vjp(_discretise, lambda_re[l], lambda_im[l], log_dt[l], b_re[l], b_im[l])
        dar = jnp.concatenate([da4[0], da4[1]], axis=1).reshape(32, 64)
        dai = jnp.concatenate([da4[2], da4[3]], axis=1).reshape(32, 64)
        dlr, dli, dldt, dbre, dbim = disc_vjp((dar, dai, _block_diag_in_t(dbdr), _block_diag_in_t(dbdi)))
        small_grads["g_mix"][l] = dgmix[0]
        small_grads["g_q"][l] = dgq[0, :HEAD_DIM] + dgq[0, HEAD_DIM:]
        small_grads["g_k"][l] = dgk[0, :HEAD_DIM] + dgk[0, HEAD_DIM:]
        small_grads["lambda_re"][l] = dlr
        small_grads["lambda_im"][l] = dli
        small_grads["log_dt"][l] = dldt
        small_grads["b_re"][l] = dbre
        small_grads["b_im"][l] = dbim
        small_grads["c_re"][l] = _block_diag_out_t(dcdr)
        small_grads["c_im"][l] = _block_diag_out_t(dcdi)
        small_grads["d_skip"][l] = ddsk[0]
        small_grads["g_ffn"][l] = dgffn[0]
    grad_x = dxo[None]

    small_local = {n: jnp.stack(small_grads[n]) for n in SMALL_NAMES}
    rs_axes = [BIG_SHARD_AXIS[n] - 1 for n in BIG_NAMES]
    got = _exchange_with_sibling([big_grads[n] for n in BIG_NAMES], rs_axes)
    core = lax.axis_index("c").astype(jnp.int32).reshape(1)
    sums = [[_chip_sum(big_grads[n][l], got[t], l, rs_axes[t], core, "chip_sum_" + n) for l in range(DEPTH)]
            for t, n in enumerate(BIG_NAMES)]
    recv = _exchange_chip_sums(sums, _pack_small(small_local))
    out_g, out_d, out_m, out_v = {}, {}, {}, {}
    for n, r in zip(BIG_NAMES, recv[:-1]):
        out_g[n], out_d[n], out_m[n], out_v[n] = _adamw_big(r, weights[n], moments_m[n], moments_v[n],
                                                            ADAMW_ROWS[n], "adamw_" + n)
    like = {n: weights[n] for n in SMALL_NAMES}
    packed = _adamw_small(recv[-1], _pack_small(like), _pack_small({n: moments_m[n] for n in SMALL_NAMES}),
                          _pack_small({n: moments_v[n] for n in SMALL_NAMES}))
    for dst, p in zip((out_g, out_d, out_m, out_v), packed):
        dst.update(_unpack_small(p, like))

    order = ("g_mix", "w_in", "g_q", "g_k", "w_attn_proj", "lambda_re", "lambda_im", "log_dt", "b_re", "b_im",
             "c_re", "c_im", "d_skip", "w_glu_a", "w_glu_b", "w_out", "g_ffn", "w_ffn_gate", "w_ffn_up",
             "w_ffn_down")
    return (loss, grad_x, *[out_g[n] for n in order], *[out_d[n] for n in order],
            *[out_m[n] for n in order], *[out_v[n] for n in order])
```

```python
import functools
import math

import jax
import jax.numpy as jnp
from jax import lax
from jax.experimental import pallas as pl
from jax.experimental.pallas import tpu as pltpu

F32 = jnp.float32
BF16 = jnp.bfloat16

D_MODEL = 1024
DEPTH = 4
N_DEV = 8
N_CHIPS = 4
HEAD_DIM = 64
BLK = 128
LANES = 128
ATTN_W = 512
N_GROUPS = 3
DILATIONS = (1, 4, 16)
ATTN_ROWS = 2048
SSM_W = 512
SSM_STATES = 2048
SSM_BLOCKS = 4
IN_COLS = 7168
COL_U = 9
D_FF = 2816
FF_SHARD = D_FF // N_DEV
FF_SHARD_PAD = 384
FF_PAD = FF_SHARD_PAD * N_DEV
FF_CHUNK = 512
EPS = 1e-6
SSM_TM = 512
SMALL_TILES = 4

ADAM_LR = 0.001
ADAM_B1 = 0.9
ADAM_B2 = 0.999
ADAM_EPS = 1e-08
ADAM_WD = 0.01
ADAM_STEP = 10

MESH_AXES = ("x", "y", "c")
MIB = 1024 * 1024


def _cp(sem=None, vmem_mib=None):
    kw = {}
    if sem is not None:
        kw["dimension_semantics"] = sem
    if vmem_mib is not None:
        kw["vmem_limit_bytes"] = vmem_mib * MIB
    return pltpu.CompilerParams(**kw)


def _nt(a, b):
    return lax.dot_general(a, b, (((1,), (1,)), ((), ())), preferred_element_type=F32)


def _tn_dot(a, b):
    return lax.dot_general(a, b, (((0,), (0,)), ((), ())), preferred_element_type=F32)


def _nn(a, b):
    return jnp.dot(a, b, preferred_element_type=F32)


def _sigmoid(t):
    return 0.5 * jnp.tanh(0.5 * t) + 0.5


def _prep_weight(w, rows_to, cols_to, name):
    _, k, n = w.shape

    def body(w_ref, o_ref):
        if rows_to != k or cols_to != n:
            o_ref[...] = jnp.zeros(o_ref.shape, BF16)
        o_ref[0, :k, :n] = w_ref[0].astype(BF16)

    return pl.pallas_call(
        body, grid=(DEPTH,),
        in_specs=[pl.BlockSpec((1, k, n), lambda l: (l, 0, 0))],
        out_specs=pl.BlockSpec((1, rows_to, cols_to), lambda l: (l, 0, 0)),
        out_shape=jax.ShapeDtypeStruct((DEPTH, rows_to, cols_to), BF16),
        compiler_params=_cp(("parallel",), 40), name=name)(w)


def _my_index():
    return 4 * lax.axis_index("x") + 2 * lax.axis_index("y") + lax.axis_index("c")


def _my_chip():
    return 2 * lax.axis_index("x") + lax.axis_index("y")


def _sibling():
    return (lax.axis_index("x"), lax.axis_index("y"), 1 - lax.axis_index("c"))


def _other_chip(j):
    return (jnp.bitwise_xor(lax.axis_index("x"), (j >> 1) & 1), jnp.bitwise_xor(lax.axis_index("y"), j & 1))


def _slab(ref, idx, width, axis):
    start = pl.multiple_of(idx * width, width)
    sl = [slice(None)] * len(ref.shape)
    sl[axis] = pl.ds(start, width)
    return ref.at[tuple(sl)]


def _remote(src, dst, ssem, rsem, device):
    return pltpu.make_async_remote_copy(src_ref=src, dst_ref=dst, send_sem=ssem, recv_sem=rsem,
                                        device_id=device, device_id_type=pl.DeviceIdType.MESH)


def _two_level_gather(srcs, blocks, ssem, rsem, lsem):
    nt = len(srcs)
    x, y, c = lax.axis_index("x"), lax.axis_index("y"), lax.axis_index("c")
    me = _my_index()
    local, sends = [], []
    for t in range(nt):
        mine = blocks[t](me)
        loc = pltpu.make_async_copy(srcs[t], mine, lsem.at[t])
        loc.start()
        local.append(loc)
        first = [_remote(srcs[t], mine, ssem.at[t, 0], rsem.at[t, 0], _sibling())]
        for j in range(1, N_CHIPS):
            first.append(_remote(srcs[t], mine, ssem.at[t, j], rsem.at[t, j], (*_other_chip(j), c)))
        for cp in first:
            cp.start()
        sends.extend(first)
    for t in range(nt):
        for j in range(1, N_CHIPS):
            ox, oy = _other_chip(j)
            landed = blocks[t](4 * ox + 2 * oy + c)
            _remote(landed, landed, ssem.at[t, j], rsem.at[t, j], _sibling()).wait_recv()
            fwd = _remote(landed, landed, ssem.at[t, 3 + j], rsem.at[t, 3 + j], _sibling())
            fwd.start()
            sends.append(fwd)
    for t in range(nt):
        got = blocks[t](4 * x + 2 * y + (1 - c))
        _remote(got, got, ssem.at[t, 0], rsem.at[t, 0], _sibling()).wait_recv()
        for j in range(1, N_CHIPS):
            ox, oy = _other_chip(j)
            got = blocks[t](4 * ox + 2 * oy + (1 - c))
            _remote(got, got, ssem.at[t, 3 + j], rsem.at[t, 3 + j], _sibling()).wait_recv()
    for cp in sends:
        cp.wait_send()
    for cp in local:
        cp.wait()


def _gather_sems(nt):
    return [pltpu.SemaphoreType.DMA((nt, N_DEV - 1)), pltpu.SemaphoreType.DMA((nt, N_DEV - 1)),
            pltpu.SemaphoreType.DMA((nt,))]


def _all_gather(shards, axes):
    nt = len(shards)

    def body(*refs):
        ins, outs = refs[:nt], refs[nt:2 * nt]
        ssem, rsem, lsem = refs[2 * nt:]
        blocks = [functools.partial(_slab, outs[t], width=shards[t].shape[axes[t]], axis=axes[t]) for t in range(nt)]
        _two_level_gather(ins, blocks, ssem, rsem, lsem)

    out_shape = []
    for t in range(nt):
        s = list(shards[t].shape)
        s[axes[t]] *= N_DEV
        out_shape.append(jax.ShapeDtypeStruct(tuple(s), shards[t].dtype))
    return pl.pallas_call(
        body,
        in_specs=[pl.BlockSpec(memory_space=pltpu.HBM)] * nt,
        out_specs=[pl.BlockSpec(memory_space=pltpu.HBM)] * nt,
        out_shape=out_shape, scratch_shapes=_gather_sems(nt),
        name="all_gather_weights")(*shards)


def _exchange_with_sibling(grads, axes):
    nt = len(grads)

    def body(*refs):
        ins = [refs[t * DEPTH:(t + 1) * DEPTH] for t in range(nt)]
        outs = refs[nt * DEPTH: nt * DEPTH + nt]
        ssem, rsem = refs[nt * DEPTH + nt:]
        c = lax.axis_index("c")
        for t in range(nt):
            width = grads[t][0].shape[axes[t]] // N_DEV
            for q in range(N_CHIPS):
                for l in range(DEPTH):
                    _remote(_slab(ins[t][l], 2 * q + (1 - c), width, axes[t]), outs[t].at[q, l],
                            ssem.at[t], rsem.at[t], _sibling()).start()
        for t in range(nt):
            _remote(outs[t], outs[t], ssem.at[t], rsem.at[t], _sibling()).wait()

    out_shape = []
    for t in range(nt):
        s = list(grads[t][0].shape)
        s[axes[t]] //= N_DEV
        out_shape.append(jax.ShapeDtypeStruct((N_CHIPS, DEPTH, s[0], s[1]), F32))
    flat = [g for per_type in grads for g in per_type]
    return pl.pallas_call(
        body,
        in_specs=[pl.BlockSpec(memory_space=pltpu.HBM)] * len(flat),
        out_specs=[pl.BlockSpec(memory_space=pltpu.HBM)] * nt,
        out_shape=out_shape,
        scratch_shapes=[pltpu.SemaphoreType.DMA((nt,)), pltpu.SemaphoreType.DMA((nt,))],
        name="grads_to_sibling")(*flat)


def _chip_sum(grad, got, layer, axis, core, name):
    _, _, r, c = got.shape
    tr = min(r, 512)

    def body(core_ref, g_ref, s_ref, o_ref):
        o_ref[0] = (g_ref[...] + s_ref[0, 0]).astype(BF16)

    if axis == 1:
        g_spec = pl.BlockSpec((tr, c), lambda q, i, core_ref: (i, 2 * q + core_ref[0]))
    else:
        g_spec = pl.BlockSpec((tr, c), lambda q, i, core_ref: ((2 * q + core_ref[0]) * (r // tr) + i, 0))
    return pl.pallas_call(
        body,
        grid_spec=pltpu.PrefetchScalarGridSpec(
            num_scalar_prefetch=1, grid=(N_CHIPS, r // tr),
            in_specs=[g_spec, pl.BlockSpec((1, 1, tr, c), lambda q, i, core_ref: (q, layer, i, 0))],
            out_specs=pl.BlockSpec((1, tr, c), lambda q, i, core_ref: (q, i, 0))),
        out_shape=jax.ShapeDtypeStruct((N_CHIPS, r, c), BF16),
        compiler_params=_cp(("parallel", "parallel"), 40), name=name)(core, grad, got)


def _exchange_chip_sums(sums, small):
    nt = len(sums)

    def body(*refs):
        ins = [refs[t * DEPTH:(t + 1) * DEPTH] for t in range(nt)]
        small_ref = refs[nt * DEPTH]
        outs = refs[nt * DEPTH + 1: nt * DEPTH + 1 + nt]
        small_out = refs[nt * DEPTH + 1 + nt]
        ssem, rsem, lsem, g_ssem, g_rsem, g_lsem = refs[nt * DEPTH + 2 + nt:]
        c = lax.axis_index("c")
        chip = _my_chip()
        for t in range(nt):
            for l in range(DEPTH):
                pltpu.make_async_copy(ins[t][l].at[chip], outs[t].at[chip, l], lsem.at[t]).start()
            for j in range(1, N_CHIPS):
                other = jnp.bitwise_xor(chip, j)
                for l in range(DEPTH):
                    _remote(ins[t][l].at[other], outs[t].at[chip, l], ssem.at[t, j - 1], rsem.at[t, j - 1],
                            (*_other_chip(j), c)).start()
        _two_level_gather([small_ref], [lambda idx: small_out.at[idx]], g_ssem, g_rsem, g_lsem)
        for t in range(nt):
            pltpu.make_async_copy(outs[t].at[chip], outs[t].at[chip], lsem.at[t]).wait()
            for j in range(1, N_CHIPS):
                other = jnp.bitwise_xor(chip, j)
                _remote(outs[t].at[other], outs[t].at[other], ssem.at[t, j - 1], rsem.at[t, j - 1],
                        (*_other_chip(j), c)).wait()

    out_shape = []
    for t in range(nt):
        _, r, c = sums[t][0].shape
        out_shape.append(jax.ShapeDtypeStruct((N_CHIPS, DEPTH, r, c), BF16))
    out_shape.append(jax.ShapeDtypeStruct((N_DEV,) + small.shape, F32))
    flat = [s for per_type in sums for s in per_type]
    return pl.pallas_call(
        body,
        in_specs=[pl.BlockSpec(memory_space=pltpu.HBM)] * (len(flat) + 1),
        out_specs=[pl.BlockSpec(memory_space=pltpu.HBM)] * (nt + 1),
        out_shape=out_shape,
        scratch_shapes=[pltpu.SemaphoreType.DMA((nt, N_CHIPS - 1)), pltpu.SemaphoreType.DMA((nt, N_CHIPS - 1)),
                        pltpu.SemaphoreType.DMA((nt,))] + _gather_sems(1),
        name="chip_sums_over_ici")(*flat, small)


def _adamw_math(w, g, m, v):
    m = ADAM_B1 * m + (1.0 - ADAM_B1) * g
    v = ADAM_B2 * v + (1.0 - ADAM_B2) * (g * g)
    m_hat = m / (1.0 - ADAM_B1 ** ADAM_STEP)
    v_hat = v / (1.0 - ADAM_B2 ** ADAM_STEP)
    delta = -ADAM_LR * (m_hat / (jnp.sqrt(v_hat) + ADAM_EPS) + ADAM_WD * w)
    return delta, m, v


def _adamw_big(recv, w, m, v, tk, name):
    _, k, n = w.shape
    npad = recv.shape[3]

    def body(r_ref, w_ref, m_ref, v_ref, g_out, d_out, m_out, v_out):
        g = r_ref[0, 0].astype(F32)
        for s in range(1, N_CHIPS):
            g = g + r_ref[s, 0].astype(F32)
        g = g[:, :n]
        delta, mn, vn = _adamw_math(w_ref[0], g, m_ref[0], v_ref[0])
        g_out[0] = g
        d_out[0] = delta
        m_out[0] = mn
        v_out[0] = vn

    blk = pl.BlockSpec((1, tk, n), lambda l, i: (l, i, 0))
    sds = jax.ShapeDtypeStruct(w.shape, F32)
    return pl.pallas_call(
        body, grid=(DEPTH, k // tk),
        in_specs=[pl.BlockSpec((N_CHIPS, 1, tk, npad), lambda l, i: (0, l, i, 0)), blk, blk, blk],
        out_specs=[blk, blk, blk, blk], out_shape=[sds, sds, sds, sds],
        compiler_params=_cp(("parallel", "parallel"), 48), name=name)(recv, w, m, v)


def _adamw_small(recv, w, m, v):
    rows = w.shape[0]
    tr = rows // SMALL_TILES

    def body(r_ref, w_ref, m_ref, v_ref, g_out, d_out, m_out, v_out):
        g = r_ref[0]
        for s in range(1, N_DEV):
            g = g + r_ref[s]
        delta, mn, vn = _adamw_math(w_ref[...], g, m_ref[...], v_ref[...])
        g_out[...] = g
        d_out[...] = delta
        m_out[...] = mn
        v_out[...] = vn

    blk = pl.BlockSpec((tr, LANES), lambda i: (i, 0))
    sds = jax.ShapeDtypeStruct(w.shape, F32)
    return pl.pallas_call(
        body, grid=(SMALL_TILES,),
        in_specs=[pl.BlockSpec((N_DEV, tr, LANES), lambda i: (0, i, 0)), blk, blk, blk],
        out_specs=[blk, blk, blk, blk], out_shape=[sds, sds, sds, sds],
        compiler_params=_cp(("parallel",), 40), name="adamw_small")(recv, w, m, v)


def _rms(t):
    return lax.rsqrt(jnp.mean(t * t, axis=-1, keepdims=True) + EPS)


def _rms_bwd(t, r, gain, dh, dres):
    u = dh * gain
    dt = dres + r * u - t * ((r * r * r) * (1.0 / D_MODEL) * jnp.sum(t * u, axis=-1, keepdims=True))
    return dt, dh * t * r


def _in_proj(x, gain, w):
    L = x.shape[0]
    n = w.shape[1]
    tm, tn = 1024, 1024

    def body(x_ref, g_ref, w_ref, z_ref, h_ref):
        @pl.when(pl.program_id(1) == 0)
        def _():
            t = x_ref[...]
            h_ref[...] = (t * _rms(t) * g_ref[...]).astype(BF16)
        z_ref[...] = _nn(h_ref[...], w_ref[...])

    return pl.pallas_call(
        body, grid=(L // tm, n // tn),
        in_specs=[pl.BlockSpec((tm, D_MODEL), lambda i, j: (i, 0)), pl.BlockSpec((1, D_MODEL), lambda i, j: (0, 0)),
                  pl.BlockSpec((D_MODEL, tn), lambda i, j: (0, j))],
        out_specs=[pl.BlockSpec((tm, tn), lambda i, j: (i, j)), pl.BlockSpec((tm, D_MODEL), lambda i, j: (i, 0))],
        out_shape=[jax.ShapeDtypeStruct((L, n), F32), jax.ShapeDtypeStruct((L, D_MODEL), BF16)],
        compiler_params=_cp(("parallel", "arbitrary"), 40), name="in_proj")(x, gain, w)


PIECE_W = 512


def _piece_columns(pieces):
    cols = []
    for p, arr in enumerate(pieces):
        cols.extend((p, off) for off in range(0, arr.shape[1], PIECE_W))
    return cols


def _in_proj_bwd(pieces, w, x, gain, dres):
    L = x.shape[0]
    tm = 512
    npc = len(pieces)
    cols = _piece_columns(pieces)
    per_step = D_MODEL // PIECE_W
    nk = len(cols) // per_step

    def body(*refs):
        dz_refs = refs[:npc]
        w_ref, x_ref, g_ref, dr_ref, dx_ref, dg_ref, acc = refs[npc:]
        i, k = pl.program_id(0), pl.program_id(1)

        @pl.when(k == 0)
        def _():
            acc[...] = jnp.zeros_like(acc)

        for kk in range(nk):
            @pl.when(k == kk)
            def _(kk=kk):
                parts = [dz_refs[p][:, off:off + PIECE_W].astype(BF16)
                         for p, off in cols[kk * per_step:(kk + 1) * per_step]]
                acc[...] += _nt(jnp.concatenate(parts, axis=1), w_ref[...])

        @pl.when(k == nk - 1)
        def _():
            t = x_ref[...]
            dt, dgt = _rms_bwd(t, _rms(t), g_ref[...], acc[...], dr_ref[...])
            dx_ref[...] = dt

            @pl.when(i == 0)
            def _():
                dg_ref[...] = jnp.zeros_like(dg_ref)
            dg_ref[...] += jnp.sum(dgt, axis=0, keepdims=True)

    row = pl.BlockSpec((tm, D_MODEL), lambda i, k: (i, 0))
    vec = pl.BlockSpec((1, D_MODEL), lambda i, k: (0, 0))
    piece_specs = [pl.BlockSpec((tm, arr.shape[1]), lambda i, k: (i, 0)) for arr in pieces]
    return pl.pallas_call(
        body, grid=(L // tm, nk),
        in_specs=piece_specs + [pl.BlockSpec((D_MODEL, D_MODEL), lambda i, k: (0, k)), row, vec, row],
        out_specs=[row, vec],
        out_shape=[jax.ShapeDtypeStruct((L, D_MODEL), F32), jax.ShapeDtypeStruct((1, D_MODEL), F32)],
        scratch_shapes=[pltpu.VMEM((tm, D_MODEL), F32)],
        compiler_params=_cp(("arbitrary", "arbitrary"), 56), name="in_proj_bwd")(*pieces, w, x, gain, dres)


def _tn(a, b, name):
    m, na = a.shape
    nb = b.shape[1]
    ta, tb, tm = min(na, 1024), min(nb, 1024), 1024
    nm = m // tm

    def body(a_ref, b_ref, o_ref):
        @pl.when(pl.program_id(2) == 0)
        def _():
            o_ref[...] = jnp.zeros_like(o_ref)
        o_ref[...] += _tn_dot(a_ref[...].astype(BF16), b_ref[...].astype(BF16))

    return pl.pallas_call(
        body, grid=(na // ta, nb // tb, nm),
        in_specs=[pl.BlockSpec((tm, ta), lambda i, j, k: (k, i)), pl.BlockSpec((tm, tb), lambda i, j, k: (k, j))],
        out_specs=pl.BlockSpec((ta, tb), lambda i, j, k: (i, j)),
        out_shape=jax.ShapeDtypeStruct((na, nb), F32),
        compiler_params=_cp(("parallel", "parallel", "arbitrary"), 48), name=name)(a, b)


def _tn_pieces(a, pieces, name):
    m, na = a.shape
    npc = len(pieces)
    cols = _piece_columns(pieces)
    per_block = D_MODEL // PIECE_W
    nj = len(cols) // per_block
    tm = 1024
    nm = m // tm
    block_of_piece = {}
    for c, (p, _) in enumerate(cols):
        block_of_piece[p] = c // per_block

    def body(*refs):
        a_ref = refs[0]
        b_refs = refs[1:1 + npc]
        o_ref = refs[1 + npc]
        j = pl.program_id(0)

        @pl.when(pl.program_id(1) == 0)
        def _():
            o_ref[...] = jnp.zeros_like(o_ref)

        for jj in range(nj):
            @pl.when(j == jj)
            def _(jj=jj):
                parts = [b_refs[p][:, off:off + PIECE_W].astype(BF16)
                         for p, off in cols[jj * per_block:(jj + 1) * per_block]]
                o_ref[...] += _tn_dot(a_ref[...], jnp.concatenate(parts, axis=1))

    piece_specs = [pl.BlockSpec((tm, arr.shape[1]),
                                functools.partial(lambda j, k, jj: (jnp.where(j == jj, k, 0), 0), jj=block_of_piece[p]))
                   for p, arr in enumerate(pieces)]
    return pl.pallas_call(
        body, grid=(nj, nm),
        in_specs=[pl.BlockSpec((tm, na), lambda j, k: (k, 0))] + piece_specs,
        out_specs=pl.BlockSpec((na, D_MODEL), lambda j, k: (0, j)),
        out_shape=jax.ShapeDtypeStruct((na, D_MODEL * nj), F32),
        compiler_params=_cp(("parallel", "arbitrary"), 56), name=name)(a, *pieces)


def _loss_grad(xf, target):
    L = xf.shape[0]
    tm = 1024

    def body(x_ref, t_ref, dy_ref, l_ref):
        e = x_ref[...] - t_ref[...]
        dy_ref[...] = e * (1.0 / D_MODEL)

        @pl.when(pl.program_id(0) == 0)
        def _():
            l_ref[...] = jnp.zeros_like(l_ref)
        l_ref[...] += jnp.sum(jnp.sum(e * e, axis=1, keepdims=True), axis=0, keepdims=True) * (0.5 / D_MODEL)

    row = pl.BlockSpec((tm, D_MODEL), lambda i: (i, 0))
    return pl.pallas_call(
        body, grid=(L // tm,), in_specs=[row, row],
        out_specs=[row, pl.BlockSpec((1, 1), lambda i: (0, 0))],
        out_shape=[jax.ShapeDtypeStruct((L, D_MODEL), F32), jax.ShapeDtypeStruct((1, 1), F32)],
        compiler_params=_cp(("arbitrary",), 40), name="loss_grad")(xf, target)


def _ffn_fwd(x, gain, wg, wu, wd):
    L = x.shape[0]
    ff = wg.shape[1]
    tm, tf = 1024, 2 * FF_CHUNK
    nf = ff // tf

    def body(x_ref, g_ref, wg_ref, wu_ref, wd_ref, o_ref, h_scr, acc):
        c = pl.program_id(1)

        @pl.when(c == 0)
        def _():
            t = x_ref[...]
            h_scr[...] = (t * _rms(t) * g_ref[...]).astype(BF16)
            acc[...] = jnp.zeros_like(acc)

        h = h_scr[...]
        down = []
        for cols in (slice(0, FF_CHUNK), slice(FF_CHUNK, 2 * FF_CHUNK)):
            gate = _nn(h, wg_ref[:, cols])
            up = _nn(h, wu_ref[:, cols])
            hid = gate * _sigmoid(gate) * up
            down.append(_nn(hid.astype(BF16), wd_ref[cols, :]))
        acc[...] += down[0] + down[1]

        @pl.when(c == nf - 1)
        def _():
            o_ref[...] = x_ref[...] + acc[...]

    row = pl.BlockSpec((tm, D_MODEL), lambda i, c: (i, 0))
    return pl.pallas_call(
        body, grid=(L // tm, nf),
        in_specs=[row, pl.BlockSpec((1, D_MODEL), lambda i, c: (0, 0)),
                  pl.BlockSpec((D_MODEL, tf), lambda i, c: (0, c)), pl.BlockSpec((D_MODEL, tf), lambda i, c: (0, c)),
                  pl.BlockSpec((tf, D_MODEL), lambda i, c: (c, 0))],
        out_specs=row, out_shape=jax.ShapeDtypeStruct((L, D_MODEL), F32),
        scratch_shapes=[pltpu.VMEM((tm, D_MODEL), BF16), pltpu.VMEM((tm, D_MODEL), F32)],
        compiler_params=_cp(("parallel", "arbitrary"), 48), name="ffn_fwd")(x, gain, wg, wu, wd)


def _ffn_bwd(x, gain, wg, wu, wd, dxo):
    L = x.shape[0]
    ff = wg.shape[1]
    tm, tf = 512, 2 * FF_CHUNK
    nf = ff // tf

    def body(x_ref, g_ref, wg_ref, wu_ref, wd_ref, dxo_ref, dx_ref, h_ref, hid_ref, dgate_ref, dup_ref, dg_ref,
             acc, dxo_b):
        i, c = pl.program_id(0), pl.program_id(1)

        @pl.when(c == 0)
        def _():
            t = x_ref[...]
            h_ref[...] = (t * _rms(t) * g_ref[...]).astype(BF16)
            acc[...] = jnp.zeros_like(acc)
            dxo_b[...] = dxo_ref[...].astype(BF16)

        h = h_ref[...]
        back = []
        for cols in (slice(0, FF_CHUNK), slice(FF_CHUNK, 2 * FF_CHUNK)):
            gate = _nn(h, wg_ref[:, cols])
            up = _nn(h, wu_ref[:, cols])
            sg = _sigmoid(gate)
            silu = gate * sg
            hid_ref[:, cols] = (silu * up).astype(BF16)
            dhid = _nt(dxo_b[...], wd_ref[cols, :])
            dup = (dhid * silu).astype(BF16)
            dgate = (dhid * up * (sg * (1.0 + gate * (1.0 - sg)))).astype(BF16)
            dup_ref[:, cols] = dup
            dgate_ref[:, cols] = dgate
            back.append(_nt(dgate, wg_ref[:, cols]) + _nt(dup, wu_ref[:, cols]))
        acc[...] += back[0] + back[1]

        @pl.when(c == nf - 1)
        def _():
            t = x_ref[...]
            dt, dgt = _rms_bwd(t, _rms(t), g_ref[...], acc[...], dxo_ref[...])
            dx_ref[...] = dt

            @pl.when(i == 0)
            def _():
                dg_ref[...] = jnp.zeros_like(dg_ref)
            dg_ref[...] += jnp.sum(dgt, axis=0, keepdims=True)

    row = pl.BlockSpec((tm, D_MODEL), lambda i, c: (i, 0))
    vec = pl.BlockSpec((1, D_MODEL), lambda i, c: (0, 0))
    wcol = pl.BlockSpec((D_MODEL, tf), lambda i, c: (0, c))
    hcol = pl.BlockSpec((tm, tf), lambda i, c: (i, c))
    return pl.pallas_call(
        body, grid=(L // tm, nf),
        in_specs=[row, vec, wcol, wcol, pl.BlockSpec((tf, D_MODEL), lambda i, c: (c, 0)), row],
        out_specs=[row, row, hcol, hcol, hcol, vec],
        out_shape=[jax.ShapeDtypeStruct((L, D_MODEL), F32), jax.ShapeDtypeStruct((L, D_MODEL), BF16),
                   jax.ShapeDtypeStruct((L, ff), BF16), jax.ShapeDtypeStruct((L, ff), BF16),
                   jax.ShapeDtypeStruct((L, ff), BF16), jax.ShapeDtypeStruct((1, D_MODEL), F32)],
        scratch_shapes=[pltpu.VMEM((tm, D_MODEL), F32), pltpu.VMEM((tm, D_MODEL), BF16)],
        compiler_params=_cp(("arbitrary", "arbitrary"), 56), name="ffn_bwd")(x, gain, wg, wu, wd, dxo)


GELU_K = math.sqrt(2.0 / math.pi)
GELU_C = 0.044715


def _gelu(y):
    return 0.5 * y * (1.0 + jnp.tanh(GELU_K * (y + GELU_C * (y * y * y))))


def _gelu_grad(y):
    th = jnp.tanh(GELU_K * (y + GELU_C * (y * y * y)))
    return 0.5 * (1.0 + th) + 0.5 * y * (1.0 - th * th) * (GELU_K * (1.0 + 3.0 * GELU_C * (y * y)))


def _merge_groups(o_refs, l_refs):
    ls = [r[...] for r in l_refs]
    os_ = [r[...] for r in o_refs]
    lmax = jnp.maximum(jnp.maximum(ls[0], ls[1]), ls[2])
    es = [jnp.exp(l - lmax) for l in ls]
    inv = 1.0 / (es[0] + es[1] + es[2])
    ws = [e * inv for e in es]
    a = ws[0] * os_[0] + ws[1] * os_[1] + ws[2] * os_[2]
    return ws, os_, a


def _mix_fwd(ols, y, z, x, wp, wa, wb, wo):
    L = x.shape[0]
    tm = 256

    def body(o0, l0, o1, l1, o2, l2, y_ref, ga_ref, gs_ref, x_ref, wp_ref, wa_ref, wb_ref, wo_ref, out_ref):
        _, _, a = _merge_groups((o0, o1, o2), (l0, l1, l2))
        a_out = _nn(a.astype(BF16), wp_ref[...])
        yg = _gelu(y_ref[...]).astype(BF16)
        s_out = _nn(yg, wa_ref[...]) * _sigmoid(_nn(yg, wb_ref[...]))
        mix = _sigmoid(ga_ref[...]) * a_out + _sigmoid(gs_ref[...]) * s_out
        out_ref[...] = x_ref[...] + _nn(mix.astype(BF16), wo_ref[...])

    half = pl.BlockSpec((tm, ATTN_W), lambda i: (i, 0))
    row = pl.BlockSpec((tm, D_MODEL), lambda i: (i, 0))
    w512 = pl.BlockSpec((ATTN_W, D_MODEL), lambda i: (0, 0))
    return pl.pallas_call(
        body, grid=(L // tm,),
        in_specs=[half] * 7 + [pl.BlockSpec((tm, D_MODEL), lambda i: (i, 5)),
                               pl.BlockSpec((tm, D_MODEL), lambda i: (i, 6)), row, w512, w512, w512,
                               pl.BlockSpec((D_MODEL, D_MODEL), lambda i: (0, 0))],
        out_specs=row, out_shape=jax.ShapeDtypeStruct((L, D_MODEL), F32),
        compiler_params=_cp(("parallel",), 48), name="mix_fwd")(*ols, y, z, z, x, wp, wa, wb, wo)


def _mix_bwd(dxm, ols, y, z, wp, wa, wb, wo):
    L = dxm.shape[0]
    tm = 256

    def body(dx_ref, o0, l0, o1, l1, o2, l2, y_ref, ga_ref, gs_ref, wp_ref, wa_ref, wb_ref, wo_ref,
             do0, dl0, do1, dl1, do2, dl2, dy_ref, dga_ref, dgs_ref, a_ref, yg_ref, mix_ref, dao_ref, dpa_ref,
             dpb_ref):
        ws, os_, a = _merge_groups((o0, o1, o2), (l0, l1, l2))
        ab = a.astype(BF16)
        a_out = _nn(ab, wp_ref[...])
        yv = y_ref[...]
        yg = _gelu(yv).astype(BF16)
        pa = _nn(yg, wa_ref[...])
        spb = _sigmoid(_nn(yg, wb_ref[...]))
        s_out = pa * spb
        sga = _sigmoid(ga_ref[...])
        sgs = _sigmoid(gs_ref[...])
        mix = sga * a_out + sgs * s_out
        dmix = _nt(dx_ref[...].astype(BF16), wo_ref[...])
        da_out = (sga * dmix).astype(BF16)
        ds_out = sgs * dmix
        dpa = (ds_out * spb).astype(BF16)
        dpb = (ds_out * pa * spb * (1.0 - spb)).astype(BF16)
        dga_ref[...] = (dmix * a_out * sga * (1.0 - sga)).astype(BF16)
        dgs_ref[...] = (dmix * s_out * sgs * (1.0 - sgs)).astype(BF16)
        dy_ref[...] = (_nt(dpa, wa_ref[...]) + _nt(dpb, wb_ref[...])) * _gelu_grad(yv)
        da = _nt(da_out, wp_ref[...])
        for w, o, do_ref, dl_ref in zip(ws, os_, (do0, do1, do2), (dl0, dl1, dl2)):
            do_ref[...] = w * da
            dl_ref[...] = da * w * (o - a)
        a_ref[...] = ab
        yg_ref[...] = yg
        mix_ref[...] = mix.astype(BF16)
        dao_ref[...] = da_out
        dpa_ref[...] = dpa
        dpb_ref[...] = dpb

    half = pl.BlockSpec((tm, ATTN_W), lambda i: (i, 0))
    row = pl.BlockSpec((tm, D_MODEL), lambda i: (i, 0))
    w512 = pl.BlockSpec((ATTN_W, D_MODEL), lambda i: (0, 0))
    hf = jax.ShapeDtypeStruct((L, ATTN_W), F32)
    hb = jax.ShapeDtypeStruct((L, ATTN_W), BF16)
    rb = jax.ShapeDtypeStruct((L, D_MODEL), BF16)
    return pl.pallas_call(
        body, grid=(L // tm,),
        in_specs=[row] + [half] * 7 + [pl.BlockSpec((tm, D_MODEL), lambda i: (i, 5)),
                                       pl.BlockSpec((tm, D_MODEL), lambda i: (i, 6)), w512, w512, w512,
                                       pl.BlockSpec((D_MODEL, D_MODEL), lambda i: (0, 0))],
        out_specs=[half] * 7 + [row, row, half, half, row, row, row, row],
        out_shape=[hf] * 7 + [rb, rb, hb, hb, rb, rb, rb, rb],
        compiler_params=_cp(("parallel",), 56), name="mix_bwd")(dxm, *ols, y, z, z, wp, wa, wb, wo)


N_ATTN_ITERS = ATTN_ROWS // BLK


def _class_rows(ref, start, d):
    if d == 1:
        return ref[pl.ds(pl.multiple_of(start, BLK), BLK), :]
    return ref[pl.ds(start, BLK, stride=d), :]


def _set_class_rows(ref, start, d, val):
    if d == 1:
        ref[pl.ds(pl.multiple_of(start, BLK), BLK), :] = val
    else:
        ref[pl.ds(start, BLK, stride=d), :] = val


def _head_masks():
    lane = lax.broadcasted_iota(jnp.int32, (1, LANES), 1)
    m0 = (lane < HEAD_DIM).astype(F32)
    return m0, 1.0 - m0


def _head_norm(t, gain2, m0, m1):
    tt = t * t
    r0 = lax.rsqrt(jnp.sum(tt * m0, axis=-1, keepdims=True) * (1.0 / HEAD_DIM) + EPS)
    r1 = lax.rsqrt(jnp.sum(tt * m1, axis=-1, keepdims=True) * (1.0 / HEAD_DIM) + EPS)
    r = m0 * r0 + m1 * r1
    return t * r * gain2, r


def _head_norm_bwd(t, r, gain2, dy, m0, m1):
    u = dy * gain2
    tu = t * u
    s = m0 * jnp.sum(tu * m0, axis=-1, keepdims=True) + m1 * jnp.sum(tu * m1, axis=-1, keepdims=True)
    return r * u - t * (r * r * r) * s * (1.0 / HEAD_DIM), jnp.sum(dy * t * r, axis=0, keepdims=True)


def _band_masks():
    qi = lax.broadcasted_iota(jnp.int32, (BLK, 2 * BLK), 0)
    ki = lax.broadcasted_iota(jnp.int32, (BLK, 2 * BLK), 1)
    dist = BLK + qi - ki
    return (dist >= 0) & (dist <= BLK), ki >= BLK


ATTN_SCALE = HEAD_DIM ** -0.5


def _attn_scores(qm, kw, ok):
    return jnp.where(ok, _nt(qm, kw), -1e30)


def _attn_probs(qm, kw, ok):
    s = _attn_scores(qm, kw, ok)
    mx = jnp.max(s, axis=-1, keepdims=True)
    p = jnp.exp(s - mx)
    den = jnp.sum(p, axis=-1, keepdims=True)
    return p, den, mx


NORM_ROWS = 256


def _norm_rows(src_ref, gain2, dst_ref, m0, m1):
    n = src_ref.shape[0]
    step = min(NORM_ROWS, n)
    for r0 in range(0, n, step):
        dst_ref[r0:r0 + step, :] = _head_norm(src_ref[r0:r0 + step, :], gain2, m0, m1)[0]


def _norm_rows_bwd(src_ref, gain2, dy_ref, dst_ref, m0, m1):
    n = src_ref.shape[0]
    step = min(NORM_ROWS, n)
    dgain = jnp.zeros((1, LANES), F32)
    for r0 in range(0, n, step):
        t = src_ref[r0:r0 + step, :]
        _, r = _head_norm(t, gain2, m0, m1)
        dt, dg = _head_norm_bwd(t, r, gain2, dy_ref[r0:r0 + step, :], m0, m1)
        dst_ref[r0:r0 + step, :] = dt.astype(dst_ref.dtype)
        dgain = dgain + dg
    return dgain


def _attn_operands(it, d, first_step, q_ref, kc_ref, kp_ref, vc_ref, vp_ref, band, is_cur):
    j = it // d
    start = (it - j * d) + (d * BLK) * j
    before = jnp.maximum(start - d * BLK, 0)
    inside = j > 0
    q2 = _class_rows(q_ref, start, d)
    kc2 = _class_rows(kc_ref, start, d)
    vc2 = _class_rows(vc_ref, start, d)
    kp2 = jnp.where(inside, _class_rows(kc_ref, before, d), _class_rows(kp_ref, it - j * d, d))
    vp2 = jnp.where(inside, _class_rows(vc_ref, before, d), _class_rows(vp_ref, it - j * d, d))
    has_prev = inside | jnp.logical_not(first_step)
    return start, q2, kp2, kc2, vp2, vc2, band & (is_cur | has_prev)


def _attn_specs(d, step_of):
    nq = N_ATTN_ITERS // d

    def cur(c):
        return pl.BlockSpec((ATTN_ROWS, LANES), lambda hp, n: (step_of(n), c + hp))

    def prev(c):
        return pl.BlockSpec((d * BLK, LANES), lambda hp, n: (jnp.maximum(step_of(n) * nq - 1, 0), c + hp))

    return cur, prev, pl.BlockSpec((1, LANES), lambda hp, n: (0, 0))


def _attn_fwd(z, gq2, gk2, group):
    L = z.shape[0]
    d = DILATIONS[group]
    nsb = L // ATTN_ROWS
    cq, ck, cv = group * 4, 12 + group * 4, 24 + group * 4

    def body(q_ref, kc_ref, kp_ref, vc_ref, vp_ref, gq_ref, gk_ref, o_ref, l_ref, qn_scr, kn_scr, kpn_scr):
        first_step = pl.program_id(1) == 0
        band, is_cur = _band_masks()
        m0, m1 = _head_masks()
        _norm_rows(q_ref, gq_ref[...], qn_scr, m0, m1)
        if d * BLK == ATTN_ROWS:
            @pl.when(first_step)
            def _():
                _norm_rows(kp_ref, gk_ref[...], kpn_scr, m0, m1)

            @pl.when(jnp.logical_not(first_step))
            def _():
                kpn_scr[...] = kn_scr[...]
        else:
            _norm_rows(kp_ref, gk_ref[...], kpn_scr, m0, m1)
        _norm_rows(kc_ref, gk_ref[...], kn_scr, m0, m1)

        def per_block(it, carry):
            start, qn, kpn, kcn, vp2, vc2, ok = _attn_operands(
                it, d, first_step, qn_scr, kn_scr, kpn_scr, vc_ref, vp_ref, band, is_cur)
            kw = jnp.concatenate([kpn, kcn], axis=0).astype(BF16)
            vw = jnp.concatenate([vp2, vc2], axis=0).astype(BF16)
            o2 = jnp.zeros((BLK, LANES), F32)
            l2 = jnp.zeros((BLK, LANES), F32)
            for mh in (m0, m1):
                p, den, mx = _attn_probs((qn * (mh * ATTN_SCALE)).astype(BF16), kw, ok)
                o2 = o2 + mh * (_nn(p.astype(BF16), vw) / den)
                l2 = l2 + mh * (mx + jnp.log(den))
            _set_class_rows(o_ref, start, d, o2)
            _set_class_rows(l_ref, start, d, l2)
            return carry

        lax.fori_loop(0, N_ATTN_ITERS, per_block, 0, unroll=2)

    cur, prev, vec = _attn_specs(d, lambda n: n)
    out = pl.BlockSpec((ATTN_ROWS, LANES), lambda hp, n: (n, hp))
    sds = jax.ShapeDtypeStruct((L, ATTN_W), F32)
    return pl.pallas_call(
        body, grid=(4, nsb),
        in_specs=[cur(cq), cur(ck), prev(ck), cur(cv), prev(cv), vec, vec],
        out_specs=[out, out], out_shape=[sds, sds],
        scratch_shapes=[pltpu.VMEM((ATTN_ROWS, LANES), F32), pltpu.VMEM((ATTN_ROWS, LANES), F32),
                        pltpu.VMEM((d * BLK, LANES), F32)],
        compiler_params=_cp(("parallel", "arbitrary"), 48), name=f"attn_fwd_g{group}")(z, z, z, z, z, gq2, gk2)


def _attn_bwd(z, gq2, gk2, o, lse, do, dl, group):
    L = z.shape[0]
    d = DILATIONS[group]
    nsb = L // ATTN_ROWS
    cq, ck, cv = group * 4, 12 + group * 4, 24 + group * 4

    def body(q_ref, kc_ref, kp_ref, vc_ref, vp_ref, gq_ref, gk_ref, o_ref, l_ref, do_ref, dl_ref,
             dq_ref, dk_ref, dv_ref, dgq_ref, dgk_ref, ck_scr, cv_scr, qn_scr, kn_scr, kpn_scr, dqn_scr, dkn_scr):
        hp, n = pl.program_id(0), pl.program_id(1)
        first_step = n == nsb - 1
        band, is_cur = _band_masks()
        m0, m1 = _head_masks()
        gq, gk = gq_ref[...], gk_ref[...]
        _norm_rows(q_ref, gq, qn_scr, m0, m1)
        _norm_rows(kc_ref, gk, kn_scr, m0, m1)
        _norm_rows(kp_ref, gk, kpn_scr, m0, m1)

        @pl.when((hp == 0) & (n == 0))
        def _():
            dgq_ref[...] = jnp.zeros_like(dgq_ref)
            dgk_ref[...] = jnp.zeros_like(dgk_ref)

        @pl.when(n == 0)
        def _():
            ck_scr[...] = jnp.zeros_like(ck_scr)
            cv_scr[...] = jnp.zeros_like(cv_scr)

        def per_block(i, carry):
            it = N_ATTN_ITERS - 1 - i
            start, qn, kpn, kcn, vp2, vc2, ok = _attn_operands(
                it, d, first_step, qn_scr, kn_scr, kpn_scr, vc_ref, vp_ref, band, is_cur)
            r = it - (it // d) * d
            kw = jnp.concatenate([kpn, kcn], axis=0).astype(BF16)
            vw = jnp.concatenate([vp2, vc2], axis=0).astype(BF16)
            l2 = _class_rows(l_ref, start, d)
            c2 = _class_rows(dl_ref, start, d) - _class_rows(do_ref, start, d) * _class_rows(o_ref, start, d)
            do2 = _class_rows(do_ref, start, d)
            dqn = jnp.zeros((BLK, LANES), F32)
            dkw = jnp.zeros((2 * BLK, LANES), F32)
            dvw = jnp.zeros((2 * BLK, LANES), F32)
            for mh in (m0, m1):
                qm = (qn * (mh * ATTN_SCALE)).astype(BF16)
                lse = jnp.max(jnp.where(mh > 0.5, l2, -3e38), axis=-1, keepdims=True)
                pn = jnp.exp(_attn_scores(qm, kw, ok) - lse)
                dohb = (do2 * mh).astype(BF16)
                dvw = dvw + _tn_dot(pn.astype(BF16), dohb)
                ds = (pn * (_nt(dohb, vw) + jnp.sum(c2 * mh, axis=-1, keepdims=True))).astype(BF16)
                dqn = dqn + (mh * ATTN_SCALE) * _nn(ds, kw)
                dkw = dkw + _tn_dot(ds, qm)
            _set_class_rows(dqn_scr, start, d, dqn)
            _set_class_rows(dkn_scr, start, d, ck_scr[r] + dkw[BLK:])
            _set_class_rows(dv_ref, start, d, cv_scr[r] + dvw[BLK:])
            ck_scr[r] = dkw[:BLK]
            cv_scr[r] = dvw[:BLK]
            return carry

        lax.fori_loop(0, N_ATTN_ITERS, per_block, 0, unroll=2)
        dgq_ref[...] += _norm_rows_bwd(q_ref, gq, dqn_scr, dq_ref, m0, m1)
        dgk_ref[...] += _norm_rows_bwd(kc_ref, gk, dkn_scr, dk_ref, m0, m1)

    cur, prev, vec = _attn_specs(d, lambda n: nsb - 1 - n)
    sds = jax.ShapeDtypeStruct((L, ATTN_W), F32)
    sdb = jax.ShapeDtypeStruct((L, ATTN_W), BF16)
    vsd = jax.ShapeDtypeStruct((1, LANES), F32)
    return pl.pallas_call(
        body, grid=(4, nsb),
        in_specs=[cur(cq), cur(ck), prev(ck), cur(cv), prev(cv), vec, vec, cur(0), cur(0), cur(0), cur(0)],
        out_specs=[cur(0), cur(0), cur(0), vec, vec], out_shape=[sdb, sdb, sds, vsd, vsd],
        scratch_shapes=[pltpu.VMEM((d, BLK, LANES), F32), pltpu.VMEM((d, BLK, LANES), F32),
                        pltpu.VMEM((ATTN_ROWS, LANES), F32), pltpu.VMEM((ATTN_ROWS, LANES), F32),
                        pltpu.VMEM((d * BLK, LANES), F32),
                        pltpu.VMEM((ATTN_ROWS, LANES), F32), pltpu.VMEM((ATTN_ROWS, LANES), F32)],
        compiler_params=_cp(("arbitrary", "arbitrary"), 56),
        name=f"attn_bwd_g{group}")(z, z, z, z, z, gq2, gk2, o, lse, do, dl)


BLOCK_STATES = SSM_STATES // SSM_BLOCKS
BLOCK_CH = SSM_W // SSM_BLOCKS
SLABS_PER_BLOCK = BLOCK_STATES // LANES


SCAN_STEPS = 4


def _store_block(bufs, b, val, tm):
    for s in range(SLABS_PER_BLOCK):
        k = SLABS_PER_BLOCK * b + s
        bufs[k % 2][pl.ds(8 + k // 2, tm, stride=8), :] = val[:, s * LANES:(s + 1) * LANES]


def _load_block(bufs, b, tm):
    tiles = []
    for s in range(SLABS_PER_BLOCK):
        k = SLABS_PER_BLOCK * b + s
        tiles.append(bufs[k % 2][pl.ds(8 + k // 2, tm, stride=8), :])
    return jnp.concatenate(tiles, axis=1).astype(BF16)


def _ssm_project_in(ub, bdr_ref, bdi_ref, sr, si, tm):
    for b in range(SSM_BLOCKS):
        ubb = ub[:, b * BLOCK_CH:(b + 1) * BLOCK_CH]
        _store_block(sr, b, _nn(ubb, bdr_ref[b]), tm)
        _store_block(si, b, _nn(ubb, bdi_ref[b]), tm)


def _ssm_scan(a, x0, sr, si, tm):
    ar0, ar1, ai0, ai1 = a
    sr[0][0:8, :], sr[1][0:8, :], si[0][0:8, :], si[1][0:8, :] = x0

    def steps(it, c):
        xr0, xr1, xi0, xi1 = c
        base = it * (8 * SCAN_STEPS) + 8
        for q in range(SCAN_STEPS):
            rows = pl.ds(pl.multiple_of(base + 8 * q, 8), 8)
            nr0 = ar0 * xr0 - ai0 * xi0 + sr[0][rows, :]
            ni0 = ar0 * xi0 + ai0 * xr0 + si[0][rows, :]
            nr1 = ar1 * xr1 - ai1 * xi1 + sr[1][rows, :]
            ni1 = ar1 * xi1 + ai1 * xr1 + si[1][rows, :]
            sr[0][rows, :] = nr0
            si[0][rows, :] = ni0
            sr[1][rows, :] = nr1
            si[1][rows, :] = ni1
            xr0, xr1, xi0, xi1 = nr0, nr1, ni0, ni1
        return xr0, xr1, xi0, xi1

    return lax.fori_loop(0, tm // SCAN_STEPS, steps, x0)


def _load_a(ar_ref, ai_ref):
    return ar_ref[:, :LANES], ar_ref[:, LANES:], ai_ref[:, :LANES], ai_ref[:, LANES:]


def _ssm_fwd(z, ar8, ai8, bdr, bdi, cdr, cdi, dsk):
    L = z.shape[0]
    tm = SSM_TM
    nc = L // tm

    def body(u_ref, ar_ref, ai_ref, bdr_ref, bdi_ref, cdr_ref, cdi_ref, dsk_ref, y_ref, cin_ref,
             sr0, sr1, si0, si1, car):
        sr, si = (sr0, sr1), (si0, si1)

        @pl.when(pl.program_id(0) == 0)
        def _():
            car[...] = jnp.zeros_like(car)

        u = u_ref[...]
        _ssm_project_in(u.astype(BF16), bdr_ref, bdi_ref, sr, si, tm)
        cin_ref[0] = car[...]
        xr0, xr1, xi0, xi1 = _ssm_scan(_load_a(ar_ref, ai_ref), (car[0], car[1], car[2], car[3]), sr, si, tm)
        car[0], car[1], car[2], car[3] = xr0, xr1, xi0, xi1
        for b in range(SSM_BLOCKS):
            cols = slice(b * BLOCK_CH, (b + 1) * BLOCK_CH)
            y_ref[:, cols] = (dsk_ref[:, cols] * u[:, cols] + _nn(_load_block(sr, b, tm), cdr_ref[b])
                              - _nn(_load_block(si, b, tm), cdi_ref[b]))

    def const(shape):
        return pl.BlockSpec(shape, lambda i: (0,) * len(shape))

    state = pltpu.VMEM(((tm + 1) * 8, LANES), F32)
    wb = const((SSM_BLOCKS, BLOCK_CH, BLOCK_STATES))
    wc = const((SSM_BLOCKS, BLOCK_STATES, BLOCK_CH))
    return pl.pallas_call(
        body, grid=(nc,),
        in_specs=[pl.BlockSpec((tm, SSM_W), lambda i: (i, COL_U)), const((8, 256)), const((8, 256)),
                  wb, wb, wc, wc, const((1, SSM_W))],
        out_specs=[pl.BlockSpec((tm, SSM_W), lambda i: (i, 0)), pl.BlockSpec((1, 4, 8, LANES), lambda i: (i, 0, 0, 0))],
        out_shape=[jax.ShapeDtypeStruct((L, SSM_W), F32), jax.ShapeDtypeStruct((nc, 4, 8, LANES), F32)],
        scratch_shapes=[state, state, state, state, pltpu.VMEM((4, 8, LANES), F32)],
        compiler_params=_cp(("arbitrary",), 48), name="ssm_fwd")(z, ar8, ai8, bdr, bdi, cdr, cdi, dsk)


def _ssm_bwd(z, dy, cin, ar8, ai8, bdr, bdi, cdr, cdi, dsk):
    L = z.shape[0]
    tm = SSM_TM
    nc = L // tm

    def body(u_ref, dy_ref, cin_ref, ar_ref, ai_ref, dsk_ref, bdr_ref, bdi_ref, cdr_ref, cdi_ref,
             du_ref, da_ref, dds_ref, dbdr_ref, dbdi_ref, dcdr_ref, dcdi_ref,
             sr0, sr1, si0, si1, gr0, gr1, gi0, gi1, carg):
        sr, si, gr, gi = (sr0, sr1), (si0, si1), (gr0, gr1), (gi0, gi1)

        @pl.when(pl.program_id(0) == 0)
        def _():
            carg[...] = jnp.zeros_like(carg)
            for ref in (da_ref, dds_ref, dbdr_ref, dbdi_ref, dcdr_ref, dcdi_ref):
                ref[...] = jnp.zeros_like(ref)

        u = u_ref[...]
        ub = u.astype(BF16)
        dyv = dy_ref[...]
        dyb = dyv.astype(BF16)
        a = _load_a(ar_ref, ai_ref)
        ar0, ar1, ai0, ai1 = a
        x_in = (cin_ref[0, 0], cin_ref[0, 1], cin_ref[0, 2], cin_ref[0, 3])
        _ssm_project_in(ub, bdr_ref, bdi_ref, sr, si, tm)
        _ssm_scan(a, x_in, sr, si, tm)
        for b in range(SSM_BLOCKS):
            dyb_b = dyb[:, b * BLOCK_CH:(b + 1) * BLOCK_CH]
            _store_block(gr, b, _nt(dyb_b, cdr_ref[b]), tm)
            _store_block(gi, b, -_nt(dyb_b, cdi_ref[b]), tm)

        def grad_steps(it, c):
            (nr0, nr1, ni0, ni1), (d_r0, d_r1, d_i0, d_i1) = c
            base = (tm - SCAN_STEPS * (it + 1)) * 8
            for q in reversed(range(SCAN_STEPS)):
                prev = pl.ds(pl.multiple_of(base + 8 * q, 8), 8)
                rows = pl.ds(pl.multiple_of(base + 8 * q + 8, 8), 8)
                g_r0 = gr[0][rows, :] + ar0 * nr0 + ai0 * ni0
                g_i0 = gi[0][rows, :] + ar0 * ni0 - ai0 * nr0
                g_r1 = gr[1][rows, :] + ar1 * nr1 + ai1 * ni1
                g_i1 = gi[1][rows, :] + ar1 * ni1 - ai1 * nr1
                gr[0][rows, :] = g_r0
                gi[0][rows, :] = g_i0
                gr[1][rows, :] = g_r1
                gi[1][rows, :] = g_i1
                pr0, pr1, pi0, pi1 = sr[0][prev, :], sr[1][prev, :], si[0][prev, :], si[1][prev, :]
                d_r0 = d_r0 + pr0 * g_r0 + pi0 * g_i0
                d_r1 = d_r1 + pr1 * g_r1 + pi1 * g_i1
                d_i0 = d_i0 + pr0 * g_i0 - pi0 * g_r0
                d_i1 = d_i1 + pr1 * g_i1 - pi1 * g_r1
                nr0, nr1, ni0, ni1 = g_r0, g_r1, g_i0, g_i1
            return (nr0, nr1, ni0, ni1), (d_r0, d_r1, d_i0, d_i1)

        acc0 = (da_ref[0], da_ref[1], da_ref[2], da_ref[3])
        g_first, acc = lax.fori_loop(0, tm // SCAN_STEPS, grad_steps,
                                     ((carg[0], carg[1], carg[2], carg[3]), acc0))
        carg[0], carg[1], carg[2], carg[3] = g_first
        da_ref[0], da_ref[1], da_ref[2], da_ref[3] = acc

        for b in range(SSM_BLOCKS):
            cols = slice(b * BLOCK_CH, (b + 1) * BLOCK_CH)
            grb, gib = _load_block(gr, b, tm), _load_block(gi, b, tm)
            du_ref[:, cols] = (dsk_ref[:, cols] * dyv[:, cols] + _nt(grb, bdr_ref[b])
                               + _nt(gib, bdi_ref[b])).astype(BF16)
            dbdr_ref[b] += _tn_dot(ub[:, cols], grb)
            dbdi_ref[b] += _tn_dot(ub[:, cols], gib)
            dcdr_ref[b] += _tn_dot(_load_block(sr, b, tm), dyb[:, cols])
            dcdi_ref[b] -= _tn_dot(_load_block(si, b, tm), dyb[:, cols])
        dds_ref[...] += jnp.sum(dyv * u, axis=0, keepdims=True)

    def const(shape):
        return pl.BlockSpec(shape, lambda i: (0,) * len(shape))

    state = pltpu.VMEM(((tm + 1) * 8, LANES), F32)
    wb = const((SSM_BLOCKS, BLOCK_CH, BLOCK_STATES))
    wc = const((SSM_BLOCKS, BLOCK_STATES, BLOCK_CH))
    return pl.pallas_call(
        body, grid=(nc,),
        in_specs=[pl.BlockSpec((tm, SSM_W), lambda i: (nc - 1 - i, COL_U)),
                  pl.BlockSpec((tm, SSM_W), lambda i: (nc - 1 - i, 0)),
                  pl.BlockSpec((1, 4, 8, LANES), lambda i: (nc - 1 - i, 0, 0, 0)),
                  const((8, 256)), const((8, 256)), const((1, SSM_W)), wb, wb, wc, wc],
        out_specs=[pl.BlockSpec((tm, SSM_W), lambda i: (nc - 1 - i, 0)), const((4, 8, LANES)), const((1, SSM_W)),
                   wb, wb, wc, wc],
        out_shape=[jax.ShapeDtypeStruct((L, SSM_W), BF16), jax.ShapeDtypeStruct((4, 8, LANES), F32),
                   jax.ShapeDtypeStruct((1, SSM_W), F32),
                   jax.ShapeDtypeStruct((SSM_BLOCKS, BLOCK_CH, BLOCK_STATES), F32),
                   jax.ShapeDtypeStruct((SSM_BLOCKS, BLOCK_CH, BLOCK_STATES), F32),
                   jax.ShapeDtypeStruct((SSM_BLOCKS, BLOCK_STATES, BLOCK_CH), F32),
                   jax.ShapeDtypeStruct((SSM_BLOCKS, BLOCK_STATES, BLOCK_CH), F32)],
        scratch_shapes=[state] * 8 + [pltpu.VMEM((4, 8, LANES), F32)],
        compiler_params=_cp(("arbitrary",), 56), name="ssm_bwd")(z, dy, cin, ar8, ai8, dsk, bdr, bdi, cdr, cdi)


def _discretise(lam_re, lam_im, log_dt, b_re, b_im):
    dt = jnp.exp(log_dt)[:, None]
    mag = jnp.exp(lam_re * dt)
    ang = lam_im * dt
    abar_re = mag * jnp.cos(ang)
    abar_im = mag * jnp.sin(ang)
    nr = abar_re - 1.0
    ni = abar_im
    den = lam_re * lam_re + lam_im * lam_im
    cr = ((nr * lam_re + ni * lam_im) / den)[..., None]
    ci = ((ni * lam_re - nr * lam_im) / den)[..., None]
    return abar_re, abar_im, cr * b_re - ci * b_im, cr * b_im + ci * b_re


GROUPS_PER_BLOCK = 8


def _block_diag_in(bbar):
    eye = jnp.eye(GROUPS_PER_BLOCK, dtype=F32)
    return jnp.einsum("igpc,gh->igchp", bbar.reshape(SSM_BLOCKS, GROUPS_PER_BLOCK, 64, 16), eye).reshape(
        SSM_BLOCKS, BLOCK_CH, BLOCK_STATES)


def _block_diag_in_t(blocks):
    eye = jnp.eye(GROUPS_PER_BLOCK, dtype=F32)
    return jnp.einsum("igchp,gh->igpc", blocks.reshape(SSM_BLOCKS, GROUPS_PER_BLOCK, 16, GROUPS_PER_BLOCK, 64),
                      eye).reshape(32, 64, 16)


def _block_diag_out(c):
    eye = jnp.eye(GROUPS_PER_BLOCK, dtype=F32)
    return jnp.einsum("igcp,gh->igphc", c.reshape(SSM_BLOCKS, GROUPS_PER_BLOCK, 16, 64), eye).reshape(
        SSM_BLOCKS, BLOCK_STATES, BLOCK_CH)


def _block_diag_out_t(blocks):
    eye = jnp.eye(GROUPS_PER_BLOCK, dtype=F32)
    return jnp.einsum("igphc,gh->igcp", blocks.reshape(SSM_BLOCKS, GROUPS_PER_BLOCK, 64, GROUPS_PER_BLOCK, 16),
                      eye).reshape(32, 16, 64)


SMALL_NAMES = ("g_mix", "g_q", "g_k", "lambda_re", "lambda_im", "log_dt", "b_re", "b_im", "c_re", "c_im",
               "d_skip", "g_ffn")


def _pack_small(parts):
    flat = jnp.concatenate([parts[n].reshape(-1) for n in SMALL_NAMES])
    pad = (-flat.shape[0]) % (8 * LANES * SMALL_TILES)
    return jnp.pad(flat, (0, pad)).reshape(-1, LANES)


def _unpack_small(packed, like):
    flat = packed.reshape(-1)
    out, off = {}, 0
    for n in SMALL_NAMES:
        size = like[n].size
        out[n] = flat[off:off + size].reshape(like[n].shape)
        off += size
    return out


BIG_NAMES = ("w_in", "w_attn_proj", "w_glu_a", "w_glu_b", "w_out", "w_ffn_gate", "w_ffn_up", "w_ffn_down")
BIG_SHARD_AXIS = {"w_in": 2, "w_attn_proj": 2, "w_glu_a": 2, "w_glu_b": 2, "w_out": 1,
                  "w_ffn_gate": 2, "w_ffn_up": 2, "w_ffn_down": 1}
ADAMW_ROWS = {"w_in": 256, "w_attn_proj": 512, "w_glu_a": 512, "w_glu_b": 512, "w_out": 128,
              "w_ffn_gate": 256, "w_ffn_up": 256, "w_ffn_down": 176}


def kernel(x, g_mix, w_in, g_q, g_k, w_attn_proj, lambda_re, lambda_im, log_dt, b_re, b_im, c_re, c_im, d_skip, w_glu_a, w_glu_b, w_out, g_ffn, w_ffn_gate, w_ffn_up, w_ffn_down, loss_target, m_g_mix, m_w_in, m_g_q, m_g_k, m_w_attn_proj, m_lambda_re, m_lambda_im, m_log_dt, m_b_re, m_b_im, m_c_re, m_c_im, m_d_skip, m_w_glu_a, m_w_glu_b, m_w_out, m_g_ffn, m_w_ffn_gate, m_w_ffn_up, m_w_ffn_down, v_g_mix, v_w_in, v_g_q, v_g_k, v_w_attn_proj, v_lambda_re, v_lambda_im, v_log_dt, v_b_re, v_b_im, v_c_re, v_c_im, v_d_skip, v_w_glu_a, v_w_glu_b, v_w_out, v_g_ffn, v_w_ffn_gate, v_w_ffn_up, v_w_ffn_down):
    args = dict(locals())
    weights = {n: args[n] for n in BIG_NAMES + SMALL_NAMES}
    moments_m = {n: args["m_" + n] for n in BIG_NAMES + SMALL_NAMES}
    moments_v = {n: args["v_" + n] for n in BIG_NAMES + SMALL_NAMES}
    x0 = x[0]
    target = loss_target[0]

    shards = []
    for n in BIG_NAMES:
        w = weights[n]
        rows_to, cols_to = w.shape[1], w.shape[2]
        if n in ("w_ffn_gate", "w_ffn_up"):
            cols_to = FF_SHARD_PAD
        if n == "w_ffn_down":
            rows_to = FF_SHARD_PAD
        shards.append(_prep_weight(w, rows_to, cols_to, "prep_" + n))
    full = dict(zip(BIG_NAMES, _all_gather(shards, [BIG_SHARD_AXIS[n] for n in BIG_NAMES])))

    saved = []
    xl = x0
    for l in range(DEPTH):
        abar_re, abar_im, bb_re, bb_im = _discretise(lambda_re[l], lambda_im[l], log_dt[l], b_re[l], b_im[l])
        ssm = dict(ar8=abar_re.reshape(8, 256), ai8=abar_im.reshape(8, 256),
                   bdr=_block_diag_in(bb_re).astype(BF16), bdi=_block_diag_in(bb_im).astype(BF16),
                   cdr=_block_diag_out(c_re[l]).astype(BF16), cdi=_block_diag_out(c_im[l]).astype(BF16),
                   dsk=d_skip[l][None])
        gq2 = jnp.tile(g_q[l], 2)[None]
        gk2 = jnp.tile(g_k[l], 2)[None]
        z, h = _in_proj(xl, g_mix[l][None], full["w_in"][l])
        ols = []
        for g in range(N_GROUPS):
            ols.extend(_attn_fwd(z, gq2, gk2, g))
        y, cin = _ssm_fwd(z, **ssm)
        xm = _mix_fwd(ols, y, z, xl, full["w_attn_proj"][l], full["w_glu_a"][l], full["w_glu_b"][l], full["w_out"][l])
        xo = _ffn_fwd(xm, g_ffn[l][None], full["w_ffn_gate"][l], full["w_ffn_up"][l], full["w_ffn_down"][l])
        saved.append(dict(x=xl, z=z, h=h, ols=ols, y=y, cin=cin, xm=xm, ssm=ssm, gq2=gq2, gk2=gk2))
        xl = xo

    dxo, loss_local = _loss_grad(xl, target)
    loss = lax.psum(loss_local[0, 0], MESH_AXES)
    big_grads = {n: [None] * DEPTH for n in BIG_NAMES}
    small_grads = {n: [None] * DEPTH for n in SMALL_NAMES}
    for l in reversed(range(DEPTH)):
        s = saved[l]
        dxm, h2, hid, dgate, dup, dgffn = _ffn_bwd(s["xm"], g_ffn[l][None], full["w_ffn_gate"][l],
                                                   full["w_ffn_up"][l], full["w_ffn_down"][l], dxo)
        big_grads["w_ffn_down"][l] = _tn(hid, dxo, "grad_w_ffn_down")
        big_grads["w_ffn_gate"][l] = _tn(h2, dgate, "grad_w_ffn_gate")
        big_grads["w_ffn_up"][l] = _tn(h2, dup, "grad_w_ffn_up")
        (do0, dl0, do1, dl1, do2, dl2, dy, dga, dgs, a_b, yg_b, mix_b, dao_b, dpa_b, dpb_b) = _mix_bwd(
            dxm, s["ols"], s["y"], s["z"], full["w_attn_proj"][l], full["w_glu_a"][l], full["w_glu_b"][l],
            full["w_out"][l])
        big_grads["w_out"][l] = _tn(mix_b, dxm, "grad_w_out")
        big_grads["w_attn_proj"][l] = _tn(a_b, dao_b, "grad_w_attn_proj")
        big_grads["w_glu_a"][l] = _tn(yg_b, dpa_b, "grad_w_glu_a")
        big_grads["w_glu_b"][l] = _tn(yg_b, dpb_b, "grad_w_glu_b")
        du, da4, ddsk, dbdr, dbdi, dcdr, dcdi = _ssm_bwd(s["z"], dy, s["cin"], **s["ssm"])
        dqkv = []
        dgq = jnp.zeros((1, LANES), F32)
        dgk = jnp.zeros((1, LANES), F32)
        for g, (do_g, dl_g) in enumerate(((do0, dl0), (do1, dl1), (do2, dl2))):
            dq, dk, dv, dgq_g, dgk_g = _attn_bwd(s["z"], s["gq2"], s["gk2"], s["ols"][2 * g], s["ols"][2 * g + 1],
                                                 do_g, dl_g, g)
            dqkv.append((dq, dk, dv))
            dgq, dgk = dgq + dgq_g, dgk + dgk_g
        pieces = [dqkv[g][j] for j in range(3) for g in range(N_GROUPS)] + [du, dga, dgs]
        dxo, dgmix = _in_proj_bwd(pieces, full["w_in"][l], s["x"], g_mix[l][None], dxm)
        big_grads["w_in"][l] = _tn_pieces(s["h"], pieces, "grad_w_in")
        _, disc_vjp = jax.vjp(_discretise, lambda_re[l], lambda_im[l], log_dt[l], b_re[l], b_im[l])
        dar = jnp.concatenate([da4[0], da4[1]], axis=1).reshape(32, 64)
        dai = jnp.concatenate([da4[2], da4[3]], axis=1).reshape(32, 64)
        dlr, dli, dldt, dbre, dbim = disc_vjp((dar, dai, _block_diag_in_t(dbdr), _block_diag_in_t(dbdi)))
        small_grads["g_mix"][l] = dgmix[0]
        small_grads["g_q"][l] = dgq[0, :HEAD_DIM] + dgq[0, HEAD_DIM:]
        small_grads["g_k"][l] = dgk[0, :HEAD_DIM] + dgk[0, HEAD_DIM:]
        small_grads["lambda_re"][l] = dlr
        small_grads["lambda_im"][l] = dli
        small_grads["log_dt"][l] = dldt
        small_grads["b_re"][l] = dbre
        small_grads["b_im"][l] = dbim
        small_grads["c_re"][l] = _block_diag_out_t(dcdr)
        small_grads["c_im"][l] = _block_diag_out_t(dcdi)
        small_grads["d_skip"][l] = ddsk[0]
        small_grads["g_ffn"][l] = dgffn[0]
    grad_x = dxo[None]

    small_local = {n: jnp.stack(small_grads[n]) for n in SMALL_NAMES}
    rs_axes = [BIG_SHARD_AXIS[n] - 1 for n in BIG_NAMES]
    got = _exchange_with_sibling([big_grads[n] for n in BIG_NAMES], rs_axes)
    core = lax.axis_index("c").astype(jnp.int32).reshape(1)
    sums = [[_chip_sum(big_grads[n][l], got[t], l, rs_axes[t], core, "chip_sum_" + n) for l in range(DEPTH)]
            for t, n in enumerate(BIG_NAMES)]
    recv = _exchange_chip_sums(sums, _pack_small(small_local))
    out_g, out_d, out_m, out_v = {}, {}, {}, {}
    for n, r in zip(BIG_NAMES, recv[:-1]):
        out_g[n], out_d[n], out_m[n], out_v[n] = _adamw_big(r, weights[n], moments_m[n], moments_v[n],
                                                            ADAMW_ROWS[n], "adamw_" + n)
    like = {n: weights[n] for n in SMALL_NAMES}
    packed = _adamw_small(recv[-1], _pack_small(like), _pack_small({n: moments_m[n] for n in SMALL_NAMES}),
                          _pack_small({n: moments_v[n] for n in SMALL_NAMES}))
    for dst, p in zip((out_g, out_d, out_m, out_v), packed):
        dst.update(_unpack_small(p, like))

    order = ("g_mix", "w_in", "g_q", "g_k", "w_attn_proj", "lambda_re", "lambda_im", "log_dt", "b_re", "b_im",
             "c_re", "c_im", "d_skip", "w_glu_a", "w_glu_b", "w_out", "g_ffn", "w_ffn_gate", "w_ffn_up",
             "w_ffn_down")
    return (loss, grad_x, *[out_g[n] for n in order], *[out_d[n] for n in order],
            *[out_m[n] for n in order], *[out_v[n] for n in order])
```

```python
import functools
import math

import jax
import jax.numpy as jnp
from jax import lax
from jax.experimental import pallas as pl
from jax.experimental.pallas import tpu as pltpu

F32 = jnp.float32
BF16 = jnp.bfloat16

D_MODEL = 1024
DEPTH = 4
N_DEV = 8
N_CHIPS = 4
HEAD_DIM = 64
BLK = 128
LANES = 128
ATTN_W = 512
N_GROUPS = 3
DILATIONS = (1, 4, 16)
ATTN_ROWS = 2048
SSM_W = 512
SSM_STATES = 2048
SSM_BLOCKS = 4
IN_COLS = 7168
COL_U = 9
D_FF = 2816
FF_SHARD = D_FF // N_DEV
FF_SHARD_PAD = 384
FF_PAD = FF_SHARD_PAD * N_DEV
FF_CHUNK = 512
EPS = 1e-6
SSM_TM = 512
SMALL_TILES = 4

ADAM_LR = 0.001
ADAM_B1 = 0.9
ADAM_B2 = 0.999
ADAM_EPS = 1e-08
ADAM_WD = 0.01
ADAM_STEP = 10

MESH_AXES = ("x", "y", "c")
MIB = 1024 * 1024


def _cp(sem=None, vmem_mib=None):
    kw = {}
    if sem is not None:
        kw["dimension_semantics"] = sem
    if vmem_mib is not None:
        kw["vmem_limit_bytes"] = vmem_mib * MIB
    return pltpu.CompilerParams(**kw)


def _nt(a, b):
    return lax.dot_general(a, b, (((1,), (1,)), ((), ())), preferred_element_type=F32)


def _tn_dot(a, b):
    return lax.dot_general(a, b, (((0,), (0,)), ((), ())), preferred_element_type=F32)


def _nn(a, b):
    return jnp.dot(a, b, preferred_element_type=F32)


def _sigmoid(t):
    return 0.5 * jnp.tanh(0.5 * t) + 0.5


def _prep_weight(w, rows_to, cols_to, name):
    _, k, n = w.shape

    def body(w_ref, o_ref):
        if rows_to != k or cols_to != n:
            o_ref[...] = jnp.zeros(o_ref.shape, BF16)
        o_ref[0, :k, :n] = w_ref[0].astype(BF16)

    return pl.pallas_call(
        body, grid=(DEPTH,),
        in_specs=[pl.BlockSpec((1, k, n), lambda l: (l, 0, 0))],
        out_specs=pl.BlockSpec((1, rows_to, cols_to), lambda l: (l, 0, 0)),
        out_shape=jax.ShapeDtypeStruct((DEPTH, rows_to, cols_to), BF16),
        compiler_params=_cp(("parallel",), 40), name=name)(w)


def _my_index():
    return 4 * lax.axis_index("x") + 2 * lax.axis_index("y") + lax.axis_index("c")


def _my_chip():
    return 2 * lax.axis_index("x") + lax.axis_index("y")


def _sibling():
    return (lax.axis_index("x"), lax.axis_index("y"), 1 - lax.axis_index("c"))


def _other_chip(j):
    return (jnp.bitwise_xor(lax.axis_index("x"), (j >> 1) & 1), jnp.bitwise_xor(lax.axis_index("y"), j & 1))


def _slab(ref, idx, width, axis):
    start = pl.multiple_of(idx * width, width)
    sl = [slice(None)] * len(ref.shape)
    sl[axis] = pl.ds(start, width)
    return ref.at[tuple(sl)]


def _remote(src, dst, ssem, rsem, device):
    return pltpu.make_async_remote_copy(src_ref=src, dst_ref=dst, send_sem=ssem, recv_sem=rsem,
                                        device_id=device, device_id_type=pl.DeviceIdType.MESH)


def _two_level_gather(srcs, blocks, ssem, rsem, lsem):
    nt = len(srcs)
    x, y, c = lax.axis_index("x"), lax.axis_index("y"), lax.axis_index("c")
    me = _my_index()
    local, sends = [], []
    for t in range(nt):
        mine = blocks[t](me)
        loc = pltpu.make_async_copy(srcs[t], mine, lsem.at[t])
        loc.start()
        local.append(loc)
        first = [_remote(srcs[t], mine, ssem.at[t, 0], rsem.at[t, 0], _sibling())]
        for j in range(1, N_CHIPS):
            first.append(_remote(srcs[t], mine, ssem.at[t, j], rsem.at[t, j], (*_other_chip(j), c)))
        for cp in first:
            cp.start()
        sends.extend(first)
    for t in range(nt):
        for j in range(1, N_CHIPS):
            ox, oy = _other_chip(j)
            landed = blocks[t](4 * ox + 2 * oy + c)
            _remote(landed, landed, ssem.at[t, j], rsem.at[t, j], _sibling()).wait_recv()
            fwd = _remote(landed, landed, ssem.at[t, 3 + j], rsem.at[t, 3 + j], _sibling())
            fwd.start()
            sends.append(fwd)
    for t in range(nt):
        got = blocks[t](4 * x + 2 * y + (1 - c))
        _remote(got, got, ssem.at[t, 0], rsem.at[t, 0], _sibling()).wait_recv()
        for j in range(1, N_CHIPS):
            ox, oy = _other_chip(j)
            got = blocks[t](4 * ox + 2 * oy + (1 - c))
            _remote(got, got, ssem.at[t, 3 + j], rsem.at[t, 3 + j], _sibling()).wait_recv()
    for cp in sends:
        cp.wait_send()
    for cp in local:
        cp.wait()


def _gather_sems(nt):
    return [pltpu.SemaphoreType.DMA((nt, N_DEV - 1)), pltpu.SemaphoreType.DMA((nt, N_DEV - 1)),
            pltpu.SemaphoreType.DMA((nt,))]


def _all_gather(shards, axes):
    nt = len(shards)

    def body(*refs):
        ins, outs = refs[:nt], refs[nt:2 * nt]
        ssem, rsem, lsem = refs[2 * nt:]
        blocks = [functools.partial(_slab, outs[t], width=shards[t].shape[axes[t]], axis=axes[t]) for t in range(nt)]
        _two_level_gather(ins, blocks, ssem, rsem, lsem)

    out_shape = []
    for t in range(nt):
        s = list(shards[t].shape)
        s[axes[t]] *= N_DEV
        out_shape.append(jax.ShapeDtypeStruct(tuple(s), shards[t].dtype))
    return pl.pallas_call(
        body,
        in_specs=[pl.BlockSpec(memory_space=pltpu.HBM)] * nt,
        out_specs=[pl.BlockSpec(memory_space=pltpu.HBM)] * nt,
        out_shape=out_shape, scratch_shapes=_gather_sems(nt),
        name="all_gather_weights")(*shards)


def _exchange_with_sibling(grads, axes):
    nt = len(grads)

    def body(*refs):
        ins = [refs[t * DEPTH:(t + 1) * DEPTH] for t in range(nt)]
        outs = refs[nt * DEPTH: nt * DEPTH + nt]
        ssem, rsem = refs[nt * DEPTH + nt:]
        c = lax.axis_index("c")
        for t in range(nt):
            width = grads[t][0].shape[axes[t]] // N_DEV
            for q in range(N_CHIPS):
                for l in range(DEPTH):
                    _remote(_slab(ins[t][l], 2 * q + (1 - c), width, axes[t]), outs[t].at[q, l],
                            ssem.at[t], rsem.at[t], _sibling()).start()
        for t in range(nt):
            _remote(outs[t], outs[t], ssem.at[t], rsem.at[t], _sibling()).wait()

    out_shape = []
    for t in range(nt):
        s = list(grads[t][0].shape)
        s[axes[t]] //= N_DEV
        out_shape.append(jax.ShapeDtypeStruct((N_CHIPS, DEPTH, s[0], s[1]), F32))
    flat = [g for per_type in grads for g in per_type]
    return pl.pallas_call(
        body,
        in_specs=[pl.BlockSpec(memory_space=pltpu.HBM)] * len(flat),
        out_specs=[pl.BlockSpec(memory_space=pltpu.HBM)] * nt,
        out_shape=out_shape,
        scratch_shapes=[pltpu.SemaphoreType.DMA((nt,)), pltpu.SemaphoreType.DMA((nt,))],
        name="grads_to_sibling")(*flat)


def _chip_sum(grad, got, layer, axis, core, name):
    _, _, r, c = got.shape
    tr = min(r, 512)

    def body(core_ref, g_ref, s_ref, o_ref):
        o_ref[0] = (g_ref[...] + s_ref[0, 0]).astype(BF16)

    if axis == 1:
        g_spec = pl.BlockSpec((tr, c), lambda q, i, core_ref: (i, 2 * q + core_ref[0]))
    else:
        g_spec = pl.BlockSpec((tr, c), lambda q, i, core_ref: ((2 * q + core_ref[0]) * (r // tr) + i, 0))
    return pl.pallas_call(
        body,
        grid_spec=pltpu.PrefetchScalarGridSpec(
            num_scalar_prefetch=1, grid=(N_CHIPS, r // tr),
            in_specs=[g_spec, pl.BlockSpec((1, 1, tr, c), lambda q, i, core_ref: (q, layer, i, 0))],
            out_specs=pl.BlockSpec((1, tr, c), lambda q, i, core_ref: (q, i, 0))),
        out_shape=jax.ShapeDtypeStruct((N_CHIPS, r, c), BF16),
        compiler_params=_cp(("parallel", "parallel"), 40), name=name)(core, grad, got)


def _exchange_chip_sums(sums, small):
    nt = len(sums)

    def body(*refs):
        ins = [refs[t * DEPTH:(t + 1) * DEPTH] for t in range(nt)]
        small_ref = refs[nt * DEPTH]
        outs = refs[nt * DEPTH + 1: nt * DEPTH + 1 + nt]
        small_out = refs[nt * DEPTH + 1 + nt]
        ssem, rsem, lsem, g_ssem, g_rsem, g_lsem = refs[nt * DEPTH + 2 + nt:]
        c = lax.axis_index("c")
        chip = _my_chip()
        for t in range(nt):
            for l in range(DEPTH):
                pltpu.make_async_copy(ins[t][l].at[chip], outs[t].at[chip, l], lsem.at[t]).start()
            for j in range(1, N_CHIPS):
                other = jnp.bitwise_xor(chip, j)
                for l in range(DEPTH):
                    _remote(ins[t][l].at[other], outs[t].at[chip, l], ssem.at[t, j - 1], rsem.at[t, j - 1],
                            (*_other_chip(j), c)).start()
        _two_level_gather([small_ref], [lambda idx: small_out.at[idx]], g_ssem, g_rsem, g_lsem)
        for t in range(nt):
            pltpu.make_async_copy(outs[t].at[chip], outs[t].at[chip], lsem.at[t]).wait()
            for j in range(1, N_CHIPS):
                other = jnp.bitwise_xor(chip, j)
                _remote(outs[t].at[other], outs[t].at[other], ssem.at[t, j - 1], rsem.at[t, j - 1],
                        (*_other_chip(j), c)).wait()

    out_shape = []
    for t in range(nt):
        _, r, c = sums[t][0].shape
        out_shape.append(jax.ShapeDtypeStruct((N_CHIPS, DEPTH, r, c), BF16))
    out_shape.append(jax.ShapeDtypeStruct((N_DEV,) + small.shape, F32))
    flat = [s for per_type in sums for s in per_type]
    return pl.pallas_call(
        body,
        in_specs=[pl.BlockSpec(memory_space=pltpu.HBM)] * (len(flat) + 1),
        out_specs=[pl.BlockSpec(memory_space=pltpu.HBM)] * (nt + 1),
        out_shape=out_shape,
        scratch_shapes=[pltpu.SemaphoreType.DMA((nt, N_CHIPS - 1)), pltpu.SemaphoreType.DMA((nt, N_CHIPS - 1)),
                        pltpu.SemaphoreType.DMA((nt,))] + _gather_sems(1),
        name="chip_sums_over_ici")(*flat, small)


def _adamw_math(w, g, m, v):
    m = ADAM_B1 * m + (1.0 - ADAM_B1) * g
    v = ADAM_B2 * v + (1.0 - ADAM_B2) * (g * g)
    m_hat = m / (1.0 - ADAM_B1 ** ADAM_STEP)
    v_hat = v / (1.0 - ADAM_B2 ** ADAM_STEP)
    delta = -ADAM_LR * (m_hat / (jnp.sqrt(v_hat) + ADAM_EPS) + ADAM_WD * w)
    return delta, m, v


def _adamw_big(recv, w, m, v, tk, name):
    _, k, n = w.shape
    npad = recv.shape[3]

    def body(r_ref, w_ref, m_ref, v_ref, g_out, d_out, m_out, v_out):
        g = r_ref[0, 0].astype(F32)
        for s in range(1, N_CHIPS):
            g = g + r_ref[s, 0].astype(F32)
        g = g[:, :n]
        delta, mn, vn = _adamw_math(w_ref[0], g, m_ref[0], v_ref[0])
        g_out[0] = g
        d_out[0] = delta
        m_out[0] = mn
        v_out[0] = vn

    blk = pl.BlockSpec((1, tk, n), lambda l, i: (l, i, 0))
    sds = jax.ShapeDtypeStruct(w.shape, F32)
    return pl.pallas_call(
        body, grid=(DEPTH, k // tk),
        in_specs=[pl.BlockSpec((N_CHIPS, 1, tk, npad), lambda l, i: (0, l, i, 0)), blk, blk, blk],
        out_specs=[blk, blk, blk, blk], out_shape=[sds, sds, sds, sds],
        compiler_params=_cp(("parallel", "parallel"), 48), name=name)(recv, w, m, v)


def _adamw_small(recv, w, m, v):
    rows = w.shape[0]
    tr = rows // SMALL_TILES

    def body(r_ref, w_ref, m_ref, v_ref, g_out, d_out, m_out, v_out):
        g = r_ref[0]
        for s in range(1, N_DEV):
            g = g + r_ref[s]
        delta, mn, vn = _adamw_math(w_ref[...], g, m_ref[...], v_ref[...])
        g_out[...] = g
        d_out[...] = delta
        m_out[...] = mn
        v_out[...] = vn

    blk = pl.BlockSpec((tr, LANES), lambda i: (i, 0))
    sds = jax.ShapeDtypeStruct(w.shape, F32)
    return pl.pallas_call(
        body, grid=(SMALL_TILES,),
        in_specs=[pl.BlockSpec((N_DEV, tr, LANES), lambda i: (0, i, 0)), blk, blk, blk],
        out_specs=[blk, blk, blk, blk], out_shape=[sds, sds, sds, sds],
        compiler_params=_cp(("parallel",), 40), name="adamw_small")(recv, w, m, v)


def _rms(t):
    return lax.rsqrt(jnp.mean(t * t, axis=-1, keepdims=True) + EPS)


def _rms_bwd(t, r, gain, dh, dres):
    u = dh * gain
    dt = dres + r * u - t * ((r * r * r) * (1.0 / D_MODEL) * jnp.sum(t * u, axis=-1, keepdims=True))
    return dt, dh * t * r


def _in_proj(x, gain, w):
    L = x.shape[0]
    n = w.shape[1]
    tm, tn = 1024, 1024

    def body(x_ref, g_ref, w_ref, z_ref, h_ref):
        @pl.when(pl.program_id(1) == 0)
        def _():
            t = x_ref[...]
            h_ref[...] = (t * _rms(t) * g_ref[...]).astype(BF16)
        z_ref[...] = _nn(h_ref[...], w_ref[...])

    return pl.pallas_call(
        body, grid=(L // tm, n // tn),
        in_specs=[pl.BlockSpec((tm, D_MODEL), lambda i, j: (i, 0)), pl.BlockSpec((1, D_MODEL), lambda i, j: (0, 0)),
                  pl.BlockSpec((D_MODEL, tn), lambda i, j: (0, j))],
        out_specs=[pl.BlockSpec((tm, tn), lambda i, j: (i, j)), pl.BlockSpec((tm, D_MODEL), lambda i, j: (i, 0))],
        out_shape=[jax.ShapeDtypeStruct((L, n), F32), jax.ShapeDtypeStruct((L, D_MODEL), BF16)],
        compiler_params=_cp(("parallel", "arbitrary"), 40), name="in_proj")(x, gain, w)


PIECE_W = 512


def _piece_columns(pieces):
    cols = []
    for p, arr in enumerate(pieces):
        cols.extend((p, off) for off in range(0, arr.shape[1], PIECE_W))
    return cols


def _in_proj_bwd(pieces, w, x, gain, dres):
    L = x.shape[0]
    n = w.shape[1]
    tm = 512
    npc = len(pieces)
    cols = _piece_columns(pieces)
    per_dot = 4

    def body(*refs):
        dz_refs = refs[:npc]
        w_hbm, x_ref, g_ref, dr_ref, dx_ref, dg_ref, w_scr = refs[npc:]

        @pl.when(pl.program_id(0) == 0)
        def _():
            pltpu.sync_copy(w_hbm, w_scr)
            dg_ref[...] = jnp.zeros_like(dg_ref)

        dh = None
        for c0 in range(0, len(cols), per_dot):
            chunk = cols[c0:c0 + per_dot]
            parts = [dz_refs[p][:, off:off + PIECE_W].astype(BF16) for p, off in chunk]
            term = _nt(jnp.concatenate(parts, axis=1), w_scr[:, c0 * PIECE_W:(c0 + len(chunk)) * PIECE_W])
            dh = term if dh is None else dh + term
        t = x_ref[...]
        dt, dgt = _rms_bwd(t, _rms(t), g_ref[...], dh, dr_ref[...])
        dx_ref[...] = dt
        dg_ref[...] += jnp.sum(dgt, axis=0, keepdims=True)

    row = pl.BlockSpec((tm, D_MODEL), lambda i: (i, 0))
    vec = pl.BlockSpec((1, D_MODEL), lambda i: (0, 0))
    piece_specs = [pl.BlockSpec((tm, arr.shape[1]), lambda i: (i, 0)) for arr in pieces]
    return pl.pallas_call(
        body, grid=(L // tm,),
        in_specs=piece_specs + [pl.BlockSpec(memory_space=pltpu.HBM), row, vec, row],
        out_specs=[row, vec],
        out_shape=[jax.ShapeDtypeStruct((L, D_MODEL), F32), jax.ShapeDtypeStruct((1, D_MODEL), F32)],
        scratch_shapes=[pltpu.VMEM((D_MODEL, n), BF16)],
        compiler_params=_cp(("arbitrary",), 60), name="in_proj_bwd")(*pieces, w, x, gain, dres)


def _tn(a, b, name):
    m, na = a.shape
    nb = b.shape[1]
    ta, tb, tm = min(na, 1024), min(nb, 1024), 1024
    nm = m // tm

    def body(a_ref, b_ref, o_ref):
        @pl.when(pl.program_id(2) == 0)
        def _():
            o_ref[...] = jnp.zeros_like(o_ref)
        o_ref[...] += _tn_dot(a_ref[...].astype(BF16), b_ref[...].astype(BF16))

    return pl.pallas_call(
        body, grid=(na // ta, nb // tb, nm),
        in_specs=[pl.BlockSpec((tm, ta), lambda i, j, k: (k, i)), pl.BlockSpec((tm, tb), lambda i, j, k: (k, j))],
        out_specs=pl.BlockSpec((ta, tb), lambda i, j, k: (i, j)),
        out_shape=jax.ShapeDtypeStruct((na, nb), F32),
        compiler_params=_cp(("parallel", "parallel", "arbitrary"), 48), name=name)(a, b)


def _tn_pieces(a, pieces, name):
    m, na = a.shape
    npc = len(pieces)
    cols = _piece_columns(pieces)
    per_block = D_MODEL // PIECE_W
    nj = len(cols) // per_block
    tm = 1024
    nm = m // tm
    block_of_piece = {}
    for c, (p, _) in enumerate(cols):
        block_of_piece[p] = c // per_block

    def body(*refs):
        a_ref = refs[0]
        b_refs = refs[1:1 + npc]
        o_ref = refs[1 + npc]
        j = pl.program_id(0)

        @pl.when(pl.program_id(1) == 0)
        def _():
            o_ref[...] = jnp.zeros_like(o_ref)

        for jj in range(nj):
            @pl.when(j == jj)
            def _(jj=jj):
                parts = [b_refs[p][:, off:off + PIECE_W].astype(BF16)
                         for p, off in cols[jj * per_block:(jj + 1) * per_block]]
                o_ref[...] += _tn_dot(a_ref[...], jnp.concatenate(parts, axis=1))

    piece_specs = [pl.BlockSpec((tm, arr.shape[1]),
                                functools.partial(lambda j, k, jj: (jnp.where(j == jj, k, 0), 0), jj=block_of_piece[p]))
                   for p, arr in enumerate(pieces)]
    return pl.pallas_call(
        body, grid=(nj, nm),
        in_specs=[pl.BlockSpec((tm, na), lambda j, k: (k, 0))] + piece_specs,
        out_specs=pl.BlockSpec((na, D_MODEL), lambda j, k: (0, j)),
        out_shape=jax.ShapeDtypeStruct((na, D_MODEL * nj), F32),
        compiler_params=_cp(("parallel", "arbitrary"), 56), name=name)(a, *pieces)


def _loss_grad(xf, target):
    L = xf.shape[0]
    tm = 1024

    def body(x_ref, t_ref, dy_ref, l_ref):
        e = x_ref[...] - t_ref[...]
        dy_ref[...] = e * (1.0 / D_MODEL)

        @pl.when(pl.program_id(0) == 0)
        def _():
            l_ref[...] = jnp.zeros_like(l_ref)
        l_ref[...] += jnp.sum(jnp.sum(e * e, axis=1, keepdims=True), axis=0, keepdims=True) * (0.5 / D_MODEL)

    row = pl.BlockSpec((tm, D_MODEL), lambda i: (i, 0))
    return pl.pallas_call(
        body, grid=(L // tm,), in_specs=[row, row],
        out_specs=[row, pl.BlockSpec((1, 1), lambda i: (0, 0))],
        out_shape=[jax.ShapeDtypeStruct((L, D_MODEL), F32), jax.ShapeDtypeStruct((1, 1), F32)],
        compiler_params=_cp(("arbitrary",), 40), name="loss_grad")(xf, target)


def _ffn_fwd(x, gain, wg, wu, wd):
    L = x.shape[0]
    ff = wg.shape[1]
    tm, tf = 1024, 2 * FF_CHUNK
    nf = ff // tf

    def body(x_ref, g_ref, wg_ref, wu_ref, wd_ref, o_ref, h_scr, acc):
        c = pl.program_id(1)

        @pl.when(c == 0)
        def _():
            t = x_ref[...]
            h_scr[...] = (t * _rms(t) * g_ref[...]).astype(BF16)
            acc[...] = jnp.zeros_like(acc)

        h = h_scr[...]
        down = []
        for cols in (slice(0, FF_CHUNK), slice(FF_CHUNK, 2 * FF_CHUNK)):
            gate = _nn(h, wg_ref[:, cols])
            up = _nn(h, wu_ref[:, cols])
            hid = gate * _sigmoid(gate) * up
            down.append(_nn(hid.astype(BF16), wd_ref[cols, :]))
        acc[...] += down[0] + down[1]

        @pl.when(c == nf - 1)
        def _():
            o_ref[...] = x_ref[...] + acc[...]

    row = pl.BlockSpec((tm, D_MODEL), lambda i, c: (i, 0))
    return pl.pallas_call(
        body, grid=(L // tm, nf),
        in_specs=[row, pl.BlockSpec((1, D_MODEL), lambda i, c: (0, 0)),
                  pl.BlockSpec((D_MODEL, tf), lambda i, c: (0, c)), pl.BlockSpec((D_MODEL, tf), lambda i, c: (0, c)),
                  pl.BlockSpec((tf, D_MODEL), lambda i, c: (c, 0))],
        out_specs=row, out_shape=jax.ShapeDtypeStruct((L, D_MODEL), F32),
        scratch_shapes=[pltpu.VMEM((tm, D_MODEL), BF16), pltpu.VMEM((tm, D_MODEL), F32)],
        compiler_params=_cp(("parallel", "arbitrary"), 48), name="ffn_fwd")(x, gain, wg, wu, wd)


def _ffn_bwd(x, gain, wg, wu, wd, dxo):
    L = x.shape[0]
    ff = wg.shape[1]
    tm, tf = 512, 2 * FF_CHUNK
    nf = ff // tf

    def body(x_ref, g_ref, wg_ref, wu_ref, wd_ref, dxo_ref, dx_ref, h_ref, hid_ref, dgate_ref, dup_ref, dg_ref,
             acc, dxo_b):
        i, c = pl.program_id(0), pl.program_id(1)

        @pl.when(c == 0)
        def _():
            t = x_ref[...]
            h_ref[...] = (t * _rms(t) * g_ref[...]).astype(BF16)
            acc[...] = jnp.zeros_like(acc)
            dxo_b[...] = dxo_ref[...].astype(BF16)

        h = h_ref[...]
        back = []
        for cols in (slice(0, FF_CHUNK), slice(FF_CHUNK, 2 * FF_CHUNK)):
            gate = _nn(h, wg_ref[:, cols])
            up = _nn(h, wu_ref[:, cols])
            sg = _sigmoid(gate)
            silu = gate * sg
            hid_ref[:, cols] = (silu * up).astype(BF16)
            dhid = _nt(dxo_b[...], wd_ref[cols, :])
            dup = (dhid * silu).astype(BF16)
            dgate = (dhid * up * (sg * (1.0 + gate * (1.0 - sg)))).astype(BF16)
            dup_ref[:, cols] = dup
            dgate_ref[:, cols] = dgate
            back.append(_nt(dgate, wg_ref[:, cols]) + _nt(dup, wu_ref[:, cols]))
        acc[...] += back[0] + back[1]

        @pl.when(c == nf - 1)
        def _():
            t = x_ref[...]
            dt, dgt = _rms_bwd(t, _rms(t), g_ref[...], acc[...], dxo_ref[...])
            dx_ref[...] = dt

            @pl.when(i == 0)
            def _():
                dg_ref[...] = jnp.zeros_like(dg_ref)
            dg_ref[...] += jnp.sum(dgt, axis=0, keepdims=True)

    row = pl.BlockSpec((tm, D_MODEL), lambda i, c: (i, 0))
    vec = pl.BlockSpec((1, D_MODEL), lambda i, c: (0, 0))
    wcol = pl.BlockSpec((D_MODEL, tf), lambda i, c: (0, c))
    hcol = pl.BlockSpec((tm, tf), lambda i, c: (i, c))
    return pl.pallas_call(
        body, grid=(L // tm, nf),
        in_specs=[row, vec, wcol, wcol, pl.BlockSpec((tf, D_MODEL), lambda i, c: (c, 0)), row],
        out_specs=[row, row, hcol, hcol, hcol, vec],
        out_shape=[jax.ShapeDtypeStruct((L, D_MODEL), F32), jax.ShapeDtypeStruct((L, D_MODEL), BF16),
                   jax.ShapeDtypeStruct((L, ff), BF16), jax.ShapeDtypeStruct((L, ff), BF16),
                   jax.ShapeDtypeStruct((L, ff), BF16), jax.ShapeDtypeStruct((1, D_MODEL), F32)],
        scratch_shapes=[pltpu.VMEM((tm, D_MODEL), F32), pltpu.VMEM((tm, D_MODEL), BF16)],
        compiler_params=_cp(("arbitrary", "arbitrary"), 56), name="ffn_bwd")(x, gain, wg, wu, wd, dxo)


GELU_K = math.sqrt(2.0 / math.pi)
GELU_C = 0.044715


def _gelu(y):
    return 0.5 * y * (1.0 + jnp.tanh(GELU_K * (y + GELU_C * (y * y * y))))


def _gelu_grad(y):
    th = jnp.tanh(GELU_K * (y + GELU_C * (y * y * y)))
    return 0.5 * (1.0 + th) + 0.5 * y * (1.0 - th * th) * (GELU_K * (1.0 + 3.0 * GELU_C * (y * y)))


def _merge_groups(o_refs, l_refs):
    ls = [r[...] for r in l_refs]
    os_ = [r[...] for r in o_refs]
    lmax = jnp.maximum(jnp.maximum(ls[0], ls[1]), ls[2])
    es = [jnp.exp(l - lmax) for l in ls]
    inv = 1.0 / (es[0] + es[1] + es[2])
    ws = [e * inv for e in es]
    a = ws[0] * os_[0] + ws[1] * os_[1] + ws[2] * os_[2]
    return ws, os_, a


def _mix_fwd(ols, y, z, x, wp, wa, wb, wo):
    L = x.shape[0]
    tm = 256

    def body(o0, l0, o1, l1, o2, l2, y_ref, ga_ref, gs_ref, x_ref, wp_ref, wa_ref, wb_ref, wo_ref, out_ref):
        _, _, a = _merge_groups((o0, o1, o2), (l0, l1, l2))
        a_out = _nn(a.astype(BF16), wp_ref[...])
        yg = _gelu(y_ref[...]).astype(BF16)
        s_out = _nn(yg, wa_ref[...]) * _sigmoid(_nn(yg, wb_ref[...]))
        mix = _sigmoid(ga_ref[...]) * a_out + _sigmoid(gs_ref[...]) * s_out
        out_ref[...] = x_ref[...] + _nn(mix.astype(BF16), wo_ref[...])

    half = pl.BlockSpec((tm, ATTN_W), lambda i: (i, 0))
    row = pl.BlockSpec((tm, D_MODEL), lambda i: (i, 0))
    w512 = pl.BlockSpec((ATTN_W, D_MODEL), lambda i: (0, 0))
    return pl.pallas_call(
        body, grid=(L // tm,),
        in_specs=[half] * 7 + [pl.BlockSpec((tm, D_MODEL), lambda i: (i, 5)),
                               pl.BlockSpec((tm, D_MODEL), lambda i: (i, 6)), row, w512, w512, w512,
                               pl.BlockSpec((D_MODEL, D_MODEL), lambda i: (0, 0))],
        out_specs=row, out_shape=jax.ShapeDtypeStruct((L, D_MODEL), F32),
        compiler_params=_cp(("parallel",), 48), name="mix_fwd")(*ols, y, z, z, x, wp, wa, wb, wo)


def _mix_bwd(dxm, ols, y, z, wp, wa, wb, wo):
    L = dxm.shape[0]
    tm = 256

    def body(dx_ref, o0, l0, o1, l1, o2, l2, y_ref, ga_ref, gs_ref, wp_ref, wa_ref, wb_ref, wo_ref,
             do0, dl0, do1, dl1, do2, dl2, dy_ref, dga_ref, dgs_ref, a_ref, yg_ref, mix_ref, dao_ref, dpa_ref,
             dpb_ref):
        ws, os_, a = _merge_groups((o0, o1, o2), (l0, l1, l2))
        ab = a.astype(BF16)
        a_out = _nn(ab, wp_ref[...])
        yv = y_ref[...]
        yg = _gelu(yv).astype(BF16)
        pa = _nn(yg, wa_ref[...])
        spb = _sigmoid(_nn(yg, wb_ref[...]))
        s_out = pa * spb
        sga = _sigmoid(ga_ref[...])
        sgs = _sigmoid(gs_ref[...])
        mix = sga * a_out + sgs * s_out
        dmix = _nt(dx_ref[...].astype(BF16), wo_ref[...])
        da_out = (sga * dmix).astype(BF16)
        ds_out = sgs * dmix
        dpa = (ds_out * spb).astype(BF16)
        dpb = (ds_out * pa * spb * (1.0 - spb)).astype(BF16)
        dga_ref[...] = (dmix * a_out * sga * (1.0 - sga)).astype(BF16)
        dgs_ref[...] = (dmix * s_out * sgs * (1.0 - sgs)).astype(BF16)
        dy_ref[...] = (_nt(dpa, wa_ref[...]) + _nt(dpb, wb_ref[...])) * _gelu_grad(yv)
        da = _nt(da_out, wp_ref[...])
        for w, o, do_ref, dl_ref in zip(ws, os_, (do0, do1, do2), (dl0, dl1, dl2)):
            do_ref[...] = w * da
            dl_ref[...] = da * w * (o - a)
        a_ref[...] = ab
        yg_ref[...] = yg
        mix_ref[...] = mix.astype(BF16)
        dao_ref[...] = da_out
        dpa_ref[...] = dpa
        dpb_ref[...] = dpb

    half = pl.BlockSpec((tm, ATTN_W), lambda i: (i, 0))
    row = pl.BlockSpec((tm, D_MODEL), lambda i: (i, 0))
    w512 = pl.BlockSpec((ATTN_W, D_MODEL), lambda i: (0, 0))
    hf = jax.ShapeDtypeStruct((L, ATTN_W), F32)
    hb = jax.ShapeDtypeStruct((L, ATTN_W), BF16)
    rb = jax.ShapeDtypeStruct((L, D_MODEL), BF16)
    return pl.pallas_call(
        body, grid=(L // tm,),
        in_specs=[row] + [half] * 7 + [pl.BlockSpec((tm, D_MODEL), lambda i: (i, 5)),
                                       pl.BlockSpec((tm, D_MODEL), lambda i: (i, 6)), w512, w512, w512,
                                       pl.BlockSpec((D_MODEL, D_MODEL), lambda i: (0, 0))],
        out_specs=[half] * 7 + [row, row, half, half, row, row, row, row],
        out_shape=[hf] * 7 + [rb, rb, hb, hb, rb, rb, rb, rb],
        compiler_params=_cp(("parallel",), 56), name="mix_bwd")(dxm, *ols, y, z, z, wp, wa, wb, wo)


N_ATTN_ITERS = ATTN_ROWS // BLK


def _class_rows(ref, start, d):
    if d == 1:
        return ref[pl.ds(pl.multiple_of(start, BLK), BLK), :]
    return ref[pl.ds(start, BLK, stride=d), :]


def _set_class_rows(ref, start, d, val):
    if d == 1:
        ref[pl.ds(pl.multiple_of(start, BLK), BLK), :] = val
    else:
        ref[pl.ds(start, BLK, stride=d), :] = val


def _head_masks():
    lane = lax.broadcasted_iota(jnp.int32, (1, LANES), 1)
    m0 = (lane < HEAD_DIM).astype(F32)
    return m0, 1.0 - m0


def _head_norm(t, gain2, m0, m1):
    tt = t * t
    r0 = lax.rsqrt(jnp.sum(tt * m0, axis=-1, keepdims=True) * (1.0 / HEAD_DIM) + EPS)
    r1 = lax.rsqrt(jnp.sum(tt * m1, axis=-1, keepdims=True) * (1.0 / HEAD_DIM) + EPS)
    r = m0 * r0 + m1 * r1
    return t * r * gain2, r


def _head_norm_bwd(t, r, gain2, dy, m0, m1):
    u = dy * gain2
    tu = t * u
    s = m0 * jnp.sum(tu * m0, axis=-1, keepdims=True) + m1 * jnp.sum(tu * m1, axis=-1, keepdims=True)
    return r * u - t * (r * r * r) * s * (1.0 / HEAD_DIM), jnp.sum(dy * t * r, axis=0, keepdims=True)


def _band_masks():
    qi = lax.broadcasted_iota(jnp.int32, (BLK, 2 * BLK), 0)
    ki = lax.broadcasted_iota(jnp.int32, (BLK, 2 * BLK), 1)
    dist = BLK + qi - ki
    return (dist >= 0) & (dist <= BLK), ki >= BLK


ATTN_SCALE = HEAD_DIM ** -0.5


def _attn_scores(qm, kw, ok):
    return jnp.where(ok, _nt(qm, kw), -1e30)


def _attn_probs(qm, kw, ok):
    s = _attn_scores(qm, kw, ok)
    mx = jnp.max(s, axis=-1, keepdims=True)
    p = jnp.exp(s - mx)
    den = jnp.sum(p, axis=-1, keepdims=True)
    return p, den, mx


NORM_ROWS = 256


def _norm_rows(src_ref, gain2, dst_ref, m0, m1):
    n = src_ref.shape[0]
    step = min(NORM_ROWS, n)
    for r0 in range(0, n, step):
        dst_ref[r0:r0 + step, :] = _head_norm(src_ref[r0:r0 + step, :], gain2, m0, m1)[0]


def _norm_rows_bwd(src_ref, gain2, dy_ref, dst_ref, m0, m1):
    n = src_ref.shape[0]
    step = min(NORM_ROWS, n)
    dgain = jnp.zeros((1, LANES), F32)
    for r0 in range(0, n, step):
        t = src_ref[r0:r0 + step, :]
        _, r = _head_norm(t, gain2, m0, m1)
        dt, dg = _head_norm_bwd(t, r, gain2, dy_ref[r0:r0 + step, :], m0, m1)
        dst_ref[r0:r0 + step, :] = dt.astype(dst_ref.dtype)
        dgain = dgain + dg
    return dgain


def _attn_operands(it, d, first_step, q_ref, kc_ref, kp_ref, vc_ref, vp_ref, band, is_cur):
    j = it // d
    start = (it - j * d) + (d * BLK) * j
    before = jnp.maximum(start - d * BLK, 0)
    inside = j > 0
    q2 = _class_rows(q_ref, start, d)
    kc2 = _class_rows(kc_ref, start, d)
    vc2 = _class_rows(vc_ref, start, d)
    kp2 = jnp.where(inside, _class_rows(kc_ref, before, d), _class_rows(kp_ref, it - j * d, d))
    vp2 = jnp.where(inside, _class_rows(vc_ref, before, d), _class_rows(vp_ref, it - j * d, d))
    has_prev = inside | jnp.logical_not(first_step)
    return start, q2, kp2, kc2, vp2, vc2, band & (is_cur | has_prev)


def _attn_specs(d, step_of):
    nq = N_ATTN_ITERS // d

    def cur(c):
        return pl.BlockSpec((ATTN_ROWS, LANES), lambda hp, n: (step_of(n), c + hp))

    def prev(c):
        return pl.BlockSpec((d * BLK, LANES), lambda hp, n: (jnp.maximum(step_of(n) * nq - 1, 0), c + hp))

    return cur, prev, pl.BlockSpec((1, LANES), lambda hp, n: (0, 0))


def _attn_fwd(z, gq2, gk2, group):
    L = z.shape[0]
    d = DILATIONS[group]
    nsb = L // ATTN_ROWS
    cq, ck, cv = group * 4, 12 + group * 4, 24 + group * 4

    def body(q_ref, kc_ref, kp_ref, vc_ref, vp_ref, gq_ref, gk_ref, o_ref, l_ref, qn_scr, kn_scr, kpn_scr):
        first_step = pl.program_id(1) == 0
        band, is_cur = _band_masks()
        m0, m1 = _head_masks()
        _norm_rows(q_ref, gq_ref[...], qn_scr, m0, m1)
        if d * BLK == ATTN_ROWS:
            @pl.when(first_step)
            def _():
                _norm_rows(kp_ref, gk_ref[...], kpn_scr, m0, m1)

            @pl.when(jnp.logical_not(first_step))
            def _():
                kpn_scr[...] = kn_scr[...]
        else:
            _norm_rows(kp_ref, gk_ref[...], kpn_scr, m0, m1)
        _norm_rows(kc_ref, gk_ref[...], kn_scr, m0, m1)

        def per_block(it, carry):
            start, qn, kpn, kcn, vp2, vc2, ok = _attn_operands(
                it, d, first_step, qn_scr, kn_scr, kpn_scr, vc_ref, vp_ref, band, is_cur)
            kw = jnp.concatenate([kpn, kcn], axis=0).astype(BF16)
            vw = jnp.concatenate([vp2, vc2], axis=0).astype(BF16)
            o2 = jnp.zeros((BLK, LANES), F32)
            l2 = jnp.zeros((BLK, LANES), F32)
            for mh in (m0, m1):
                p, den, mx = _attn_probs((qn * (mh * ATTN_SCALE)).astype(BF16), kw, ok)
                o2 = o2 + mh * (_nn(p.astype(BF16), vw) / den)
                l2 = l2 + mh * (mx + jnp.log(den))
            _set_class_rows(o_ref, start, d, o2)
            _set_class_rows(l_ref, start, d, l2)
            return carry

        lax.fori_loop(0, N_ATTN_ITERS, per_block, 0, unroll=2)

    cur, prev, vec = _attn_specs(d, lambda n: n)
    out = pl.BlockSpec((ATTN_ROWS, LANES), lambda hp, n: (n, hp))
    sds = jax.ShapeDtypeStruct((L, ATTN_W), F32)
    return pl.pallas_call(
        body, grid=(4, nsb),
        in_specs=[cur(cq), cur(ck), prev(ck), cur(cv), prev(cv), vec, vec],
        out_specs=[out, out], out_shape=[sds, sds],
        scratch_shapes=[pltpu.VMEM((ATTN_ROWS, LANES), F32), pltpu.VMEM((ATTN_ROWS, LANES), F32),
                        pltpu.VMEM((d * BLK, LANES), F32)],
        compiler_params=_cp(("parallel", "arbitrary"), 48), name=f"attn_fwd_g{group}")(z, z, z, z, z, gq2, gk2)


def _attn_bwd(z, gq2, gk2, o, lse, do, dl, group):
    L = z.shape[0]
    d = DILATIONS[group]
    nsb = L // ATTN_ROWS
    cq, ck, cv = group * 4, 12 + group * 4, 24 + group * 4

    def body(q_ref, kc_ref, kp_ref, vc_ref, vp_ref, gq_ref, gk_ref, o_ref, l_ref, do_ref, dl_ref,
             dq_ref, dk_ref, dv_ref, dgq_ref, dgk_ref, ck_scr, cv_scr, qn_scr, kn_scr, kpn_scr, dqn_scr, dkn_scr):
        hp, n = pl.program_id(0), pl.program_id(1)
        first_step = n == nsb - 1
        band, is_cur = _band_masks()
        m0, m1 = _head_masks()
        gq, gk = gq_ref[...], gk_ref[...]
        _norm_rows(q_ref, gq, qn_scr, m0, m1)
        _norm_rows(kc_ref, gk, kn_scr, m0, m1)
        _norm_rows(kp_ref, gk, kpn_scr, m0, m1)

        @pl.when((hp == 0) & (n == 0))
        def _():
            dgq_ref[...] = jnp.zeros_like(dgq_ref)
            dgk_ref[...] = jnp.zeros_like(dgk_ref)

        @pl.when(n == 0)
        def _():
            ck_scr[...] = jnp.zeros_like(ck_scr)
            cv_scr[...] = jnp.zeros_like(cv_scr)

        def per_block(i, carry):
            it = N_ATTN_ITERS - 1 - i
            start, qn, kpn, kcn, vp2, vc2, ok = _attn_operands(
                it, d, first_step, qn_scr, kn_scr, kpn_scr, vc_ref, vp_ref, band, is_cur)
            r = it - (it // d) * d
            kw = jnp.concatenate([kpn, kcn], axis=0).astype(BF16)
            vw = jnp.concatenate([vp2, vc2], axis=0).astype(BF16)
            l2 = _class_rows(l_ref, start, d)
            c2 = _class_rows(dl_ref, start, d) - _class_rows(do_ref, start, d) * _class_rows(o_ref, start, d)
            do2 = _class_rows(do_ref, start, d)
            dqn = jnp.zeros((BLK, LANES), F32)
            dkw = jnp.zeros((2 * BLK, LANES), F32)
            dvw = jnp.zeros((2 * BLK, LANES), F32)
            for mh in (m0, m1):
                qm = (qn * (mh * ATTN_SCALE)).astype(BF16)
                lse = jnp.max(jnp.where(mh > 0.5, l2, -3e38), axis=-1, keepdims=True)
                pn = jnp.exp(_attn_scores(qm, kw, ok) - lse)
                dohb = (do2 * mh).astype(BF16)
                dvw = dvw + _tn_dot(pn.astype(BF16), dohb)
                ds = (pn * (_nt(dohb, vw) + jnp.sum(c2 * mh, axis=-1, keepdims=True))).astype(BF16)
                dqn = dqn + (mh * ATTN_SCALE) * _nn(ds, kw)
                dkw = dkw + _tn_dot(ds, qm)
            _set_class_rows(dqn_scr, start, d, dqn)
            _set_class_rows(dkn_scr, start, d, ck_scr[r] + dkw[BLK:])
            _set_class_rows(dv_ref, start, d, cv_scr[r] + dvw[BLK:])
            ck_scr[r] = dkw[:BLK]
            cv_scr[r] = dvw[:BLK]
            return carry

        lax.fori_loop(0, N_ATTN_ITERS, per_block, 0, unroll=2)
        dgq_ref[...] += _norm_rows_bwd(q_ref, gq, dqn_scr, dq_ref, m0, m1)
        dgk_ref[...] += _norm_rows_bwd(kc_ref, gk, dkn_scr, dk_ref, m0, m1)

    cur, prev, vec = _attn_specs(d, lambda n: nsb - 1 - n)
    sds = jax.ShapeDtypeStruct((L, ATTN_W), F32)
    sdb = jax.ShapeDtypeStruct((L, ATTN_W), BF16)
    vsd = jax.ShapeDtypeStruct((1, LANES), F32)
    return pl.pallas_call(
        body, grid=(4, nsb),
        in_specs=[cur(cq), cur(ck), prev(ck), cur(cv), prev(cv), vec, vec, cur(0), cur(0), cur(0), cur(0)],
        out_specs=[cur(0), cur(0), cur(0), vec, vec], out_shape=[sdb, sdb, sds, vsd, vsd],
        scratch_shapes=[pltpu.VMEM((d, BLK, LANES), F32), pltpu.VMEM((d, BLK, LANES), F32),
                        pltpu.VMEM((ATTN_ROWS, LANES), F32), pltpu.VMEM((ATTN_ROWS, LANES), F32),
                        pltpu.VMEM((d * BLK, LANES), F32),
                        pltpu.VMEM((ATTN_ROWS, LANES), F32), pltpu.VMEM((ATTN_ROWS, LANES), F32)],
        compiler_params=_cp(("arbitrary", "arbitrary"), 56),
        name=f"attn_bwd_g{group}")(z, z, z, z, z, gq2, gk2, o, lse, do, dl)


BLOCK_STATES = SSM_STATES // SSM_BLOCKS
BLOCK_CH = SSM_W // SSM_BLOCKS
SLABS_PER_BLOCK = BLOCK_STATES // LANES


SCAN_STEPS = 4


def _store_block(bufs, b, val, tm):
    for s in range(SLABS_PER_BLOCK):
        k = SLABS_PER_BLOCK * b + s
        bufs[k % 2][pl.ds(8 + k // 2, tm, stride=8), :] = val[:, s * LANES:(s + 1) * LANES]


def _load_block(bufs, b, tm):
    tiles = []
    for s in range(SLABS_PER_BLOCK):
        k = SLABS_PER_BLOCK * b + s
        tiles.append(bufs[k % 2][pl.ds(8 + k // 2, tm, stride=8), :])
    return jnp.concatenate(tiles, axis=1).astype(BF16)


def _ssm_project_in(ub, bdr_ref, bdi_ref, sr, si, tm):
    for b in range(SSM_BLOCKS):
        ubb = ub[:, b * BLOCK_CH:(b + 1) * BLOCK_CH]
        _store_block(sr, b, _nn(ubb, bdr_ref[b]), tm)
        _store_block(si, b, _nn(ubb, bdi_ref[b]), tm)


def _ssm_scan(a, x0, sr, si, tm):
    ar0, ar1, ai0, ai1 = a
    sr[0][0:8, :], sr[1][0:8, :], si[0][0:8, :], si[1][0:8, :] = x0

    def steps(it, c):
        xr0, xr1, xi0, xi1 = c
        base = it * (8 * SCAN_STEPS) + 8
        for q in range(SCAN_STEPS):
            rows = pl.ds(pl.multiple_of(base + 8 * q, 8), 8)
            nr0 = ar0 * xr0 - ai0 * xi0 + sr[0][rows, :]
            ni0 = ar0 * xi0 + ai0 * xr0 + si[0][rows, :]
            nr1 = ar1 * xr1 - ai1 * xi1 + sr[1][rows, :]
            ni1 = ar1 * xi1 + ai1 * xr1 + si[1][rows, :]
            sr[0][rows, :] = nr0
            si[0][rows, :] = ni0
            sr[1][rows, :] = nr1
            si[1][rows, :] = ni1
            xr0, xr1, xi0, xi1 = nr0, nr1, ni0, ni1
        return xr0, xr1, xi0, xi1

    return lax.fori_loop(0, tm // SCAN_STEPS, steps, x0)


def _load_a(ar_ref, ai_ref):
    return ar_ref[:, :LANES], ar_ref[:, LANES:], ai_ref[:, :LANES], ai_ref[:, LANES:]


def _ssm_fwd(z, ar8, ai8, bdr, bdi, cdr, cdi, dsk):
    L = z.shape[0]
    tm = SSM_TM
    nc = L // tm

    def body(u_ref, ar_ref, ai_ref, bdr_ref, bdi_ref, cdr_ref, cdi_ref, dsk_ref, y_ref, cin_ref,
             sr0, sr1, si0, si1, car):
        sr, si = (sr0, sr1), (si0, si1)

        @pl.when(pl.program_id(0) == 0)
        def _():
            car[...] = jnp.zeros_like(car)

        u = u_ref[...]
        _ssm_project_in(u.astype(BF16), bdr_ref, bdi_ref, sr, si, tm)
        cin_ref[0] = car[...]
        xr0, xr1, xi0, xi1 = _ssm_scan(_load_a(ar_ref, ai_ref), (car[0], car[1], car[2], car[3]), sr, si, tm)
        car[0], car[1], car[2], car[3] = xr0, xr1, xi0, xi1
        for b in range(SSM_BLOCKS):
            cols = slice(b * BLOCK_CH, (b + 1) * BLOCK_CH)
            y_ref[:, cols] = (dsk_ref[:, cols] * u[:, cols] + _nn(_load_block(sr, b, tm), cdr_ref[b])
                              - _nn(_load_block(si, b, tm), cdi_ref[b]))

    def const(shape):
        return pl.BlockSpec(shape, lambda i: (0,) * len(shape))

    state = pltpu.VMEM(((tm + 1) * 8, LANES), F32)
    wb = const((SSM_BLOCKS, BLOCK_CH, BLOCK_STATES))
    wc = const((SSM_BLOCKS, BLOCK_STATES, BLOCK_CH))
    return pl.pallas_call(
        body, grid=(nc,),
        in_specs=[pl.BlockSpec((tm, SSM_W), lambda i: (i, COL_U)), const((8, 256)), const((8, 256)),
                  wb, wb, wc, wc, const((1, SSM_W))],
        out_specs=[pl.BlockSpec((tm, SSM_W), lambda i: (i, 0)), pl.BlockSpec((1, 4, 8, LANES), lambda i: (i, 0, 0, 0))],
        out_shape=[jax.ShapeDtypeStruct((L, SSM_W), F32), jax.ShapeDtypeStruct((nc, 4, 8, LANES), F32)],
        scratch_shapes=[state, state, state, state, pltpu.VMEM((4, 8, LANES), F32)],
        compiler_params=_cp(("arbitrary",), 48), name="ssm_fwd")(z, ar8, ai8, bdr, bdi, cdr, cdi, dsk)


def _ssm_bwd(z, dy, cin, ar8, ai8, bdr, bdi, cdr, cdi, dsk):
    L = z.shape[0]
    tm = SSM_TM
    nc = L // tm

    def body(u_ref, dy_ref, cin_ref, ar_ref, ai_ref, dsk_ref, bdr_ref, bdi_ref, cdr_ref, cdi_ref,
             du_ref, da_ref, dds_ref, dbdr_ref, dbdi_ref, dcdr_ref, dcdi_ref,
             sr0, sr1, si0, si1, gr0, gr1, gi0, gi1, carg):
        sr, si, gr, gi = (sr0, sr1), (si0, si1), (gr0, gr1), (gi0, gi1)

        @pl.when(pl.program_id(0) == 0)
        def _():
            carg[...] = jnp.zeros_like(carg)
            for ref in (da_ref, dds_ref, dbdr_ref, dbdi_ref, dcdr_ref, dcdi_ref):
                ref[...] = jnp.zeros_like(ref)

        u = u_ref[...]
        ub = u.astype(BF16)
        dyv = dy_ref[...]
        dyb = dyv.astype(BF16)
        a = _load_a(ar_ref, ai_ref)
        ar0, ar1, ai0, ai1 = a
        x_in = (cin_ref[0, 0], cin_ref[0, 1], cin_ref[0, 2], cin_ref[0, 3])
        _ssm_project_in(ub, bdr_ref, bdi_ref, sr, si, tm)
        _ssm_scan(a, x_in, sr, si, tm)
        for b in range(SSM_BLOCKS):
            dyb_b = dyb[:, b * BLOCK_CH:(b + 1) * BLOCK_CH]
            _store_block(gr, b, _nt(dyb_b, cdr_ref[b]), tm)
            _store_block(gi, b, -_nt(dyb_b, cdi_ref[b]), tm)

        def grad_steps(it, c):
            (nr0, nr1, ni0, ni1), (d_r0, d_r1, d_i0, d_i1) = c
            base = (tm - SCAN_STEPS * (it + 1)) * 8
            for q in reversed(range(SCAN_STEPS)):
                prev = pl.ds(pl.multiple_of(base + 8 * q, 8), 8)
                rows = pl.ds(pl.multiple_of(base + 8 * q + 8, 8), 8)
                g_r0 = gr[0][rows, :] + ar0 * nr0 + ai0 * ni0
                g_i0 = gi[0][rows, :] + ar0 * ni0 - ai0 * nr0
                g_r1 = gr[1][rows, :] + ar1 * nr1 + ai1 * ni1
                g_i1 = gi[1][rows, :] + ar1 * ni1 - ai1 * nr1
                gr[0][rows, :] = g_r0
                gi[0][rows, :] = g_i0
                gr[1][rows, :] = g_r1
                gi[1][rows, :] = g_i1
                pr0, pr1, pi0, pi1 = sr[0][prev, :], sr[1][prev, :], si[0][prev, :], si[1][prev, :]
                d_r0 = d_r0 + pr0 * g_r0 + pi0 * g_i0
                d_r1 = d_r1 + pr1 * g_r1 + pi1 * g_i1
                d_i0 = d_i0 + pr0 * g_i0 - pi0 * g_r0
                d_i1 = d_i1 + pr1 * g_i1 - pi1 * g_r1
                nr0, nr1, ni0, ni1 = g_r0, g_r1, g_i0, g_i1
            return (nr0, nr1, ni0, ni1), (d_r0, d_r1, d_i0, d_i1)

        acc0 = (da_ref[0], da_ref[1], da_ref[2], da_ref[3])
        g_first, acc = lax.fori_loop(0, tm // SCAN_STEPS, grad_steps,
                                     ((carg[0], carg[1], carg[2], carg[3]), acc0))
        carg[0], carg[1], carg[2], carg[3] = g_first
        da_ref[0], da_ref[1], da_ref[2], da_ref[3] = acc

        for b in range(SSM_BLOCKS):
            cols = slice(b * BLOCK_CH, (b + 1) * BLOCK_CH)
            grb, gib = _load_block(gr, b, tm), _load_block(gi, b, tm)
            du_ref[:, cols] = (dsk_ref[:, cols] * dyv[:, cols] + _nt(grb, bdr_ref[b])
                               + _nt(gib, bdi_ref[b])).astype(BF16)
            dbdr_ref[b] += _tn_dot(ub[:, cols], grb)
            dbdi_ref[b] += _tn_dot(ub[:, cols], gib)
            dcdr_ref[b] += _tn_dot(_load_block(sr, b, tm), dyb[:, cols])
            dcdi_ref[b] -= _tn_dot(_load_block(si, b, tm), dyb[:, cols])
        dds_ref[...] += jnp.sum(dyv * u, axis=0, keepdims=True)

    def const(shape):
        return pl.BlockSpec(shape, lambda i: (0,) * len(shape))

    state = pltpu.VMEM(((tm + 1) * 8, LANES), F32)
    wb = const((SSM_BLOCKS, BLOCK_CH, BLOCK_STATES))
    wc = const((SSM_BLOCKS, BLOCK_STATES, BLOCK_CH))
    return pl.pallas_call(
        body, grid=(nc,),
        in_specs=[pl.BlockSpec((tm, SSM_W), lambda i: (nc - 1 - i, COL_U)),
                  pl.BlockSpec((tm, SSM_W), lambda i: (nc - 1 - i, 0)),
                  pl.BlockSpec((1, 4, 8, LANES), lambda i: (nc - 1 - i, 0, 0, 0)),
                  const((8, 256)), const((8, 256)), const((1, SSM_W)), wb, wb, wc, wc],
        out_specs=[pl.BlockSpec((tm, SSM_W), lambda i: (nc - 1 - i, 0)), const((4, 8, LANES)), const((1, SSM_W)),
                   wb, wb, wc, wc],
        out_shape=[jax.ShapeDtypeStruct((L, SSM_W), BF16), jax.ShapeDtypeStruct((4, 8, LANES), F32),
                   jax.ShapeDtypeStruct((1, SSM_W), F32),
                   jax.ShapeDtypeStruct((SSM_BLOCKS, BLOCK_CH, BLOCK_STATES), F32),
                   jax.ShapeDtypeStruct((SSM_BLOCKS, BLOCK_CH, BLOCK_STATES), F32),
                   jax.ShapeDtypeStruct((SSM_BLOCKS, BLOCK_STATES, BLOCK_CH), F32),
                   jax.ShapeDtypeStruct((SSM_BLOCKS, BLOCK_STATES, BLOCK_CH), F32)],
        scratch_shapes=[state] * 8 + [pltpu.VMEM((4, 8, LANES), F32)],
        compiler_params=_cp(("arbitrary",), 56), name="ssm_bwd")(z, dy, cin, ar8, ai8, dsk, bdr, bdi, cdr, cdi)


def _discretise(lam_re, lam_im, log_dt, b_re, b_im):
    dt = jnp.exp(log_dt)[:, None]
    mag = jnp.exp(lam_re * dt)
    ang = lam_im * dt
    abar_re = mag * jnp.cos(ang)
    abar_im = mag * jnp.sin(ang)
    nr = abar_re - 1.0
    ni = abar_im
    den = lam_re * lam_re + lam_im * lam_im
    cr = ((nr * lam_re + ni * lam_im) / den)[..., None]
    ci = ((ni * lam_re - nr * lam_im) / den)[..., None]
    return abar_re, abar_im, cr * b_re - ci * b_im, cr * b_im + ci * b_re


GROUPS_PER_BLOCK = 8


def _block_diag_in(bbar):
    eye = jnp.eye(GROUPS_PER_BLOCK, dtype=F32)
    return jnp.einsum("igpc,gh->igchp", bbar.reshape(SSM_BLOCKS, GROUPS_PER_BLOCK, 64, 16), eye).reshape(
        SSM_BLOCKS, BLOCK_CH, BLOCK_STATES)


def _block_diag_in_t(blocks):
    eye = jnp.eye(GROUPS_PER_BLOCK, dtype=F32)
    return jnp.einsum("igchp,gh->igpc", blocks.reshape(SSM_BLOCKS, GROUPS_PER_BLOCK, 16, GROUPS_PER_BLOCK, 64),
                      eye).reshape(32, 64, 16)


def _block_diag_out(c):
    eye = jnp.eye(GROUPS_PER_BLOCK, dtype=F32)
    return jnp.einsum("igcp,gh->igphc", c.reshape(SSM_BLOCKS, GROUPS_PER_BLOCK, 16, 64), eye).reshape(
        SSM_BLOCKS, BLOCK_STATES, BLOCK_CH)


def _block_diag_out_t(blocks):
    eye = jnp.eye(GROUPS_PER_BLOCK, dtype=F32)
    return jnp.einsum("igphc,gh->igcp", blocks.reshape(SSM_BLOCKS, GROUPS_PER_BLOCK, 64, GROUPS_PER_BLOCK, 16),
                      eye).reshape(32, 16, 64)


SMALL_NAMES = ("g_mix", "g_q", "g_k", "lambda_re", "lambda_im", "log_dt", "b_re", "b_im", "c_re", "c_im",
               "d_skip", "g_ffn")


def _pack_small(parts):
    flat = jnp.concatenate([parts[n].reshape(-1) for n in SMALL_NAMES])
    pad = (-flat.shape[0]) % (8 * LANES * SMALL_TILES)
    return jnp.pad(flat, (0, pad)).reshape(-1, LANES)


def _unpack_small(packed, like):
    flat = packed.reshape(-1)
    out, off = {}, 0
    for n in SMALL_NAMES:
        size = like[n].size
        out[n] = flat[off:off + size].reshape(like[n].shape)
        off += size
    return out


BIG_NAMES = ("w_in", "w_attn_proj", "w_glu_a", "w_glu_b", "w_out", "w_ffn_gate", "w_ffn_up", "w_ffn_down")
BIG_SHARD_AXIS = {"w_in": 2, "w_attn_proj": 2, "w_glu_a": 2, "w_glu_b": 2, "w_out": 1,
                  "w_ffn_gate": 2, "w_ffn_up": 2, "w_ffn_down": 1}
ADAMW_ROWS = {"w_in": 256, "w_attn_proj": 512, "w_glu_a": 512, "w_glu_b": 512, "w_out": 128,
              "w_ffn_gate": 256, "w_ffn_up": 256, "w_ffn_down": 176}


def kernel(x, g_mix, w_in, g_q, g_k, w_attn_proj, lambda_re, lambda_im, log_dt, b_re, b_im, c_re, c_im, d_skip, w_glu_a, w_glu_b, w_out, g_ffn, w_ffn_gate, w_ffn_up, w_ffn_down, loss_target, m_g_mix, m_w_in, m_g_q, m_g_k, m_w_attn_proj, m_lambda_re, m_lambda_im, m_log_dt, m_b_re, m_b_im, m_c_re, m_c_im, m_d_skip, m_w_glu_a, m_w_glu_b, m_w_out, m_g_ffn, m_w_ffn_gate, m_w_ffn_up, m_w_ffn_down, v_g_mix, v_w_in, v_g_q, v_g_k, v_w_attn_proj, v_lambda_re, v_lambda_im, v_log_dt, v_b_re, v_b_im, v_c_re, v_c_im, v_d_skip, v_w_glu_a, v_w_glu_b, v_w_out, v_g_ffn, v_w_ffn_gate, v_w_ffn_up, v_w_ffn_down):
    args = dict(locals())
    weights = {n: args[n] for n in BIG_NAMES + SMALL_NAMES}
    moments_m = {n: args["m_" + n] for n in BIG_NAMES + SMALL_NAMES}
    moments_v = {n: args["v_" + n] for n in BIG_NAMES + SMALL_NAMES}
    x0 = x[0]
    target = loss_target[0]

    shards = []
    for n in BIG_NAMES:
        w = weights[n]
        rows_to, cols_to = w.shape[1], w.shape[2]
        if n in ("w_ffn_gate", "w_ffn_up"):
            cols_to = FF_SHARD_PAD
        if n == "w_ffn_down":
            rows_to = FF_SHARD_PAD
        shards.append(_prep_weight(w, rows_to, cols_to, "prep_" + n))
    full = dict(zip(BIG_NAMES, _all_gather(shards, [BIG_SHARD_AXIS[n] for n in BIG_NAMES])))

    saved = []
    xl = x0
    for l in range(DEPTH):
        abar_re, abar_im, bb_re, bb_im = _discretise(lambda_re[l], lambda_im[l], log_dt[l], b_re[l], b_im[l])
        ssm = dict(ar8=abar_re.reshape(8, 256), ai8=abar_im.reshape(8, 256),
                   bdr=_block_diag_in(bb_re).astype(BF16), bdi=_block_diag_in(bb_im).astype(BF16),
                   cdr=_block_diag_out(c_re[l]).astype(BF16), cdi=_block_diag_out(c_im[l]).astype(BF16),
                   dsk=d_skip[l][None])
        gq2 = jnp.tile(g_q[l], 2)[None]
        gk2 = jnp.tile(g_k[l], 2)[None]
        z, h = _in_proj(xl, g_mix[l][None], full["w_in"][l])
        ols = []
        for g in range(N_GROUPS):
            ols.extend(_attn_fwd(z, gq2, gk2, g))
        y, cin = _ssm_fwd(z, **ssm)
        xm = _mix_fwd(ols, y, z, xl, full["w_attn_proj"][l], full["w_glu_a"][l], full["w_glu_b"][l], full["w_out"][l])
        xo = _ffn_fwd(xm, g_ffn[l][None], full["w_ffn_gate"][l], full["w_ffn_up"][l], full["w_ffn_down"][l])
        saved.append(dict(x=xl, z=z, h=h, ols=ols, y=y, cin=cin, xm=xm, ssm=ssm, gq2=gq2, gk2=gk2))
        xl = xo

    dxo, loss_local = _loss_grad(xl, target)
    loss = lax.psum(loss_local[0, 0], MESH_AXES)
    big_grads = {n: [None] * DEPTH for n in BIG_NAMES}
    small_grads = {n: [None] * DEPTH for n in SMALL_NAMES}
    for l in reversed(range(DEPTH)):
        s = saved[l]
        dxm, h2, hid, dgate, dup, dgffn = _ffn_bwd(s["xm"], g_ffn[l][None], full["w_ffn_gate"][l],
                                                   full["w_ffn_up"][l], full["w_ffn_down"][l], dxo)
        big_grads["w_ffn_down"][l] = _tn(hid, dxo, "grad_w_ffn_down")
        big_grads["w_ffn_gate"][l] = _tn(h2, dgate, "grad_w_ffn_gate")
        big_grads["w_ffn_up"][l] = _tn(h2, dup, "grad_w_ffn_up")
        (do0, dl0, do1, dl1, do2, dl2, dy, dga, dgs, a_b, yg_b, mix_b, dao_b, dpa_b, dpb_b) = _mix_bwd(
            dxm, s["ols"], s["y"], s["z"], full["w_attn_proj"][l], full["w_glu_a"][l], full["w_glu_b"][l],
            full["w_out"][l])
        big_grads["w_out"][l] = _tn(mix_b, dxm, "grad_w_out")
        big_grads["w_attn_proj"][l] = _tn(a_b, dao_b, "grad_w_attn_proj")
        big_grads["w_glu_a"][l] = _tn(yg_b, dpa_b, "grad_w_glu_a")
        big_grads["w_glu_b"][l] = _tn(yg_b, dpb_b, "grad_w_glu_b")
        du, da4, ddsk, dbdr, dbdi, dcdr, dcdi = _ssm_bwd(s["z"], dy, s["cin"], **s["ssm"])
        dqkv = []
        dgq = jnp.zeros((1, LANES), F32)
        dgk = jnp.zeros((1, LANES), F32)
        for g, (do_g, dl_g) in enumerate(((do0, dl0), (do1, dl1), (do2, dl2))):
            dq, dk, dv, dgq_g, dgk_g = _attn_bwd(s["z"], s["gq2"], s["gk2"], s["ols"][2 * g], s["ols"][2 * g + 1],
                                                 do_g, dl_g, g)
            dqkv.append((dq, dk, dv))
            dgq, dgk = dgq + dgq_g, dgk + dgk_g
        pieces = [dqkv[g][j] for j in range(3) for g in range(N_GROUPS)] + [du, dga, dgs]
        dxo, dgmix = _in_proj_bwd(pieces, full["w_in"][l], s["x"], g_mix[l][None], dxm)
        big_grads["w_in"][l] = _tn_pieces(s["h"], pieces, "grad_w_in")
        _, disc_vjp = jax.vjp(_discretise, lambda_re[l], lambda_im[l], log_dt[l], b_re[l], b_im[l])
        dar = jnp.concatenate([da4[0], da4[1]], axis=1).reshape(32, 64)
        dai = jnp.concatenate([da4[2], da4[3]], axis=1).reshape(32, 64)
        dlr, dli, dldt, dbre, dbim = disc_vjp((dar, dai, _block_diag_in_t(dbdr), _block_diag_in_t(dbdi)))
        small_grads["g_mix"][l] = dgmix[0]
        small_grads["g_q"][l] = dgq[0, :HEAD_DIM] + dgq[0, HEAD_DIM:]
        small_grads["g_k"][l] = dgk[0, :HEAD_DIM] + dgk[0, HEAD_DIM:]
        small_grads["lambda_re"][l] = dlr
        small_grads["lambda_im"][l] = dli
        small_grads["log_dt"][l] = dldt
        small_grads["b_re"][l] = dbre
        small_grads["b_im"][l] = dbim
        small_grads["c_re"][l] = _block_diag_out_t(dcdr)
        small_grads["c_im"][l] = _block_diag_out_t(dcdi)
        small_grads["d_skip"][l] = ddsk[0]
        small_grads["g_ffn"][l] = dgffn[0]
    grad_x = dxo[None]

    small_local = {n: jnp.stack(small_grads[n]) for n in SMALL_NAMES}
    rs_axes = [BIG_SHARD_AXIS[n] - 1 for n in BIG_NAMES]
    got = _exchange_with_sibling([big_grads[n] for n in BIG_NAMES], rs_axes)
    core = lax.axis_index("c").astype(jnp.int32).reshape(1)
    sums = [[_chip_sum(big_grads[n][l], got[t], l, rs_axes[t], core, "chip_sum_" + n) for l in range(DEPTH)]
            for t, n in enumerate(BIG_NAMES)]
    recv = _exchange_chip_sums(sums, _pack_small(small_local))
    out_g, out_d, out_m, out_v = {}, {}, {}, {}
    for n, r in zip(BIG_NAMES, recv[:-1]):
        out_g[n], out_d[n], out_m[n], out_v[n] = _adamw_big(r, weights[n], moments_m[n], moments_v[n],
                                                            ADAMW_ROWS[n], "adamw_" + n)
    like = {n: weights[n] for n in SMALL_NAMES}
    packed = _adamw_small(recv[-1], _pack_small(like), _pack_small({n: moments_m[n] for n in SMALL_NAMES}),
                          _pack_small({n: moments_v[n] for n in SMALL_NAMES}))
    for dst, p in zip((out_g, out_d, out_m, out_v), packed):
        dst.update(_unpack_small(p, like))

    order = ("g_mix", "w_in", "g_q", "g_k", "w_attn_proj", "lambda_re", "lambda_im", "log_dt", "b_re", "b_im",
             "c_re", "c_im", "d_skip", "w_glu_a", "w_glu_b", "w_out", "g_ffn", "w_ffn_gate", "w_ffn_up",
             "w_ffn_down")
    return (loss, grad_x, *[out_g[n] for n in order], *[out_d[n] for n in order],
            *[out_m[n] for n in order], *[out_v[n] for n in order])
```

```python
import functools
import math

import jax
import jax.numpy as jnp
from jax import lax
from jax.experimental import pallas as pl
from jax.experimental.pallas import tpu as pltpu

F32 = jnp.float32
BF16 = jnp.bfloat16

D_MODEL = 1024
DEPTH = 4
N_DEV = 8
N_CHIPS = 4
HEAD_DIM = 64
BLK = 128
LANES = 128
ATTN_W = 512
N_GROUPS = 3
DILATIONS = (1, 4, 16)
ATTN_ROWS = 2048
SSM_W = 512
SSM_STATES = 2048
SSM_BLOCKS = 4
IN_COLS = 7168
COL_U = 9
D_FF = 2816
FF_SHARD = D_FF // N_DEV
FF_SHARD_PAD = 384
FF_PAD = FF_SHARD_PAD * N_DEV
FF_CHUNK = 512
EPS = 1e-6
SSM_TM = 512
SMALL_TILES = 4

ADAM_LR = 0.001
ADAM_B1 = 0.9
ADAM_B2 = 0.999
ADAM_EPS = 1e-08
ADAM_WD = 0.01
ADAM_STEP = 10

MESH_AXES = ("x", "y", "c")
MIB = 1024 * 1024


def _cp(sem=None, vmem_mib=None):
    kw = {}
    if sem is not None:
        kw["dimension_semantics"] = sem
    if vmem_mib is not None:
        kw["vmem_limit_bytes"] = vmem_mib * MIB
    return pltpu.CompilerParams(**kw)


def _nt(a, b):
    return lax.dot_general(a, b, (((1,), (1,)), ((), ())), preferred_element_type=F32)


def _tn_dot(a, b):
    return lax.dot_general(a, b, (((0,), (0,)), ((), ())), preferred_element_type=F32)


def _nn(a, b):
    return jnp.dot(a, b, preferred_element_type=F32)


def _sigmoid(t):
    return 0.5 * jnp.tanh(0.5 * t) + 0.5


def _prep_weight(w, rows_to, cols_to, name):
    _, k, n = w.shape

    def body(w_ref, o_ref):
        if rows_to != k or cols_to != n:
            o_ref[...] = jnp.zeros(o_ref.shape, BF16)
        o_ref[0, :k, :n] = w_ref[0].astype(BF16)

    return pl.pallas_call(
        body, grid=(DEPTH,),
        in_specs=[pl.BlockSpec((1, k, n), lambda l: (l, 0, 0))],
        out_specs=pl.BlockSpec((1, rows_to, cols_to), lambda l: (l, 0, 0)),
        out_shape=jax.ShapeDtypeStruct((DEPTH, rows_to, cols_to), BF16),
        compiler_params=_cp(("parallel",), 40), name=name)(w)


def _my_index():
    return 4 * lax.axis_index("x") + 2 * lax.axis_index("y") + lax.axis_index("c")


def _my_chip():
    return 2 * lax.axis_index("x") + lax.axis_index("y")


def _sibling():
    return (lax.axis_index("x"), lax.axis_index("y"), 1 - lax.axis_index("c"))


def _other_chip(j):
    return (jnp.bitwise_xor(lax.axis_index("x"), (j >> 1) & 1), jnp.bitwise_xor(lax.axis_index("y"), j & 1))


def _slab(ref, idx, width, axis):
    start = pl.multiple_of(idx * width, width)
    sl = [slice(None)] * len(ref.shape)
    sl[axis] = pl.ds(start, width)
    return ref.at[tuple(sl)]


def _remote(src, dst, ssem, rsem, device):
    return pltpu.make_async_remote_copy(src_ref=src, dst_ref=dst, send_sem=ssem, recv_sem=rsem,
                                        device_id=device, device_id_type=pl.DeviceIdType.MESH)


def _two_level_gather(srcs, blocks, ssem, rsem, lsem):
    nt = len(srcs)
    x, y, c = lax.axis_index("x"), lax.axis_index("y"), lax.axis_index("c")
    me = _my_index()
    local, sends = [], []
    for t in range(nt):
        mine = blocks[t](me)
        loc = pltpu.make_async_copy(srcs[t], mine, lsem.at[t])
        loc.start()
        local.append(loc)
        first = [_remote(srcs[t], mine, ssem.at[t, 0], rsem.at[t, 0], _sibling())]
        for j in range(1, N_CHIPS):
            first.append(_remote(srcs[t], mine, ssem.at[t, j], rsem.at[t, j], (*_other_chip(j), c)))
        for cp in first:
            cp.start()
        sends.extend(first)
    for t in range(nt):
        for j in range(1, N_CHIPS):
            ox, oy = _other_chip(j)
            landed = blocks[t](4 * ox + 2 * oy + c)
            _remote(landed, landed, ssem.at[t, j], rsem.at[t, j], _sibling()).wait_recv()
            fwd = _remote(landed, landed, ssem.at[t, 3 + j], rsem.at[t, 3 + j], _sibling())
            fwd.start()
            sends.append(fwd)
    for t in range(nt):
        got = blocks[t](4 * x + 2 * y + (1 - c))
        _remote(got, got, ssem.at[t, 0], rsem.at[t, 0], _sibling()).wait_recv()
        for j in range(1, N_CHIPS):
            ox, oy = _other_chip(j)
            got = blocks[t](4 * ox + 2 * oy + (1 - c))
            _remote(got, got, ssem.at[t, 3 + j], rsem.at[t, 3 + j], _sibling()).wait_recv()
    for cp in sends:
        cp.wait_send()
    for cp in local:
        cp.wait()


def _gather_sems(nt):
    return [pltpu.SemaphoreType.DMA((nt, N_DEV - 1)), pltpu.SemaphoreType.DMA((nt, N_DEV - 1)),
            pltpu.SemaphoreType.DMA((nt,))]


def _all_gather(shards, axes):
    nt = len(shards)

    def body(*refs):
        ins, outs = refs[:nt], refs[nt:2 * nt]
        ssem, rsem, lsem = refs[2 * nt:]
        blocks = [functools.partial(_slab, outs[t], width=shards[t].shape[axes[t]], axis=axes[t]) for t in range(nt)]
        _two_level_gather(ins, blocks, ssem, rsem, lsem)

    out_shape = []
    for t in range(nt):
        s = list(shards[t].shape)
        s[axes[t]] *= N_DEV
        out_shape.append(jax.ShapeDtypeStruct(tuple(s), shards[t].dtype))
    return pl.pallas_call(
        body,
        in_specs=[pl.BlockSpec(memory_space=pltpu.HBM)] * nt,
        out_specs=[pl.BlockSpec(memory_space=pltpu.HBM)] * nt,
        out_shape=out_shape, scratch_shapes=_gather_sems(nt),
        name="all_gather_weights")(*shards)


def _exchange_with_sibling(grads, axes):
    nt = len(grads)

    def body(*refs):
        ins = [refs[t * DEPTH:(t + 1) * DEPTH] for t in range(nt)]
        outs = refs[nt * DEPTH: nt * DEPTH + nt]
        ssem, rsem = refs[nt * DEPTH + nt:]
        c = lax.axis_index("c")
        for t in range(nt):
            width = grads[t][0].shape[axes[t]] // N_DEV
            for q in range(N_CHIPS):
                for l in range(DEPTH):
                    _remote(_slab(ins[t][l], 2 * q + (1 - c), width, axes[t]), outs[t].at[q, l],
                            ssem.at[t], rsem.at[t], _sibling()).start()
        for t in range(nt):
            _remote(outs[t], outs[t], ssem.at[t], rsem.at[t], _sibling()).wait()

    out_shape = []
    for t in range(nt):
        s = list(grads[t][0].shape)
        s[axes[t]] //= N_DEV
        out_shape.append(jax.ShapeDtypeStruct((N_CHIPS, DEPTH, s[0], s[1]), F32))
    flat = [g for per_type in grads for g in per_type]
    return pl.pallas_call(
        body,
        in_specs=[pl.BlockSpec(memory_space=pltpu.HBM)] * len(flat),
        out_specs=[pl.BlockSpec(memory_space=pltpu.HBM)] * nt,
        out_shape=out_shape,
        scratch_shapes=[pltpu.SemaphoreType.DMA((nt,)), pltpu.SemaphoreType.DMA((nt,))],
        name="grads_to_sibling")(*flat)


def _chip_sum(grad, got, layer, axis, core, name):
    _, _, r, c = got.shape
    tr = min(r, 512)

    def body(core_ref, g_ref, s_ref, o_ref):
        o_ref[0] = (g_ref[...] + s_ref[0, 0]).astype(BF16)

    if axis == 1:
        g_spec = pl.BlockSpec((tr, c), lambda q, i, core_ref: (i, 2 * q + core_ref[0]))
    else:
        g_spec = pl.BlockSpec((tr, c), lambda q, i, core_ref: ((2 * q + core_ref[0]) * (r // tr) + i, 0))
    return pl.pallas_call(
        body,
        grid_spec=pltpu.PrefetchScalarGridSpec(
            num_scalar_prefetch=1, grid=(N_CHIPS, r // tr),
            in_specs=[g_spec, pl.BlockSpec((1, 1, tr, c), lambda q, i, core_ref: (q, layer, i, 0))],
            out_specs=pl.BlockSpec((1, tr, c), lambda q, i, core_ref: (q, i, 0))),
        out_shape=jax.ShapeDtypeStruct((N_CHIPS, r, c), BF16),
        compiler_params=_cp(("parallel", "parallel"), 40), name=name)(core, grad, got)


def _exchange_chip_sums(sums, small):
    nt = len(sums)

    def body(*refs):
        ins = [refs[t * DEPTH:(t + 1) * DEPTH] for t in range(nt)]
        small_ref = refs[nt * DEPTH]
        outs = refs[nt * DEPTH + 1: nt * DEPTH + 1 + nt]
        small_out = refs[nt * DEPTH + 1 + nt]
        ssem, rsem, lsem, g_ssem, g_rsem, g_lsem = refs[nt * DEPTH + 2 + nt:]
        c = lax.axis_index("c")
        chip = _my_chip()
        for t in range(nt):
            for l in range(DEPTH):
                pltpu.make_async_copy(ins[t][l].at[chip], outs[t].at[chip, l], lsem.at[t]).start()
            for j in range(1, N_CHIPS):
                other = jnp.bitwise_xor(chip, j)
                for l in range(DEPTH):
                    _remote(ins[t][l].at[other], outs[t].at[chip, l], ssem.at[t, j - 1], rsem.at[t, j - 1],
                            (*_other_chip(j), c)).start()
        _two_level_gather([small_ref], [lambda idx: small_out.at[idx]], g_ssem, g_rsem, g_lsem)
        for t in range(nt):
            pltpu.make_async_copy(outs[t].at[chip], outs[t].at[chip], lsem.at[t]).wait()
            for j in range(1, N_CHIPS):
                other = jnp.bitwise_xor(chip, j)
                _remote(outs[t].at[other], outs[t].at[other], ssem.at[t, j - 1], rsem.at[t, j - 1],
                        (*_other_chip(j), c)).wait()

    out_shape = []
    for t in range(nt):
        _, r, c = sums[t][0].shape
        out_shape.append(jax.ShapeDtypeStruct((N_CHIPS, DEPTH, r, c), BF16))
    out_shape.append(jax.ShapeDtypeStruct((N_DEV,) + small.shape, F32))
    flat = [s for per_type in sums for s in per_type]
    return pl.pallas_call(
        body,
        in_specs=[pl.BlockSpec(memory_space=pltpu.HBM)] * (len(flat) + 1),
        out_specs=[pl.BlockSpec(memory_space=pltpu.HBM)] * (nt + 1),
        out_shape=out_shape,
        scratch_shapes=[pltpu.SemaphoreType.DMA((nt, N_CHIPS - 1)), pltpu.SemaphoreType.DMA((nt, N_CHIPS - 1)),
                        pltpu.SemaphoreType.DMA((nt,))] + _gather_sems(1),
        name="chip_sums_over_ici")(*flat, small)


def _adamw_math(w, g, m, v):
    m = ADAM_B1 * m + (1.0 - ADAM_B1) * g
    v = ADAM_B2 * v + (1.0 - ADAM_B2) * (g * g)
    m_hat = m / (1.0 - ADAM_B1 ** ADAM_STEP)
    v_hat = v / (1.0 - ADAM_B2 ** ADAM_STEP)
    delta = -ADAM_LR * (m_hat / (jnp.sqrt(v_hat) + ADAM_EPS) + ADAM_WD * w)
    return delta, m, v


def _adamw_big(recv, w, m, v, tk, name):
    _, k, n = w.shape
    npad = recv.shape[3]

    def body(r_ref, w_ref, m_ref, v_ref, g_out, d_out, m_out, v_out):
        g = r_ref[0, 0].astype(F32)
        for s in range(1, N_CHIPS):
            g = g + r_ref[s, 0].astype(F32)
        g = g[:, :n]
        delta, mn, vn = _adamw_math(w_ref[0], g, m_ref[0], v_ref[0])
        g_out[0] = g
        d_out[0] = delta
        m_out[0] = mn
        v_out[0] = vn

    blk = pl.BlockSpec((1, tk, n), lambda l, i: (l, i, 0))
    sds = jax.ShapeDtypeStruct(w.shape, F32)
    return pl.pallas_call(
        body, grid=(DEPTH, k // tk),
        in_specs=[pl.BlockSpec((N_CHIPS, 1, tk, npad), lambda l, i: (0, l, i, 0)), blk, blk, blk],
        out_specs=[blk, blk, blk, blk], out_shape=[sds, sds, sds, sds],
        compiler_params=_cp(("parallel", "parallel"), 48), name=name)(recv, w, m, v)


def _adamw_small(recv, w, m, v):
    rows = w.shape[0]
    tr = rows // SMALL_TILES

    def body(r_ref, w_ref, m_ref, v_ref, g_out, d_out, m_out, v_out):
        g = r_ref[0]
        for s in range(1, N_DEV):
            g = g + r_ref[s]
        delta, mn, vn = _adamw_math(w_ref[...], g, m_ref[...], v_ref[...])
        g_out[...] = g
        d_out[...] = delta
        m_out[...] = mn
        v_out[...] = vn

    blk = pl.BlockSpec((tr, LANES), lambda i: (i, 0))
    sds = jax.ShapeDtypeStruct(w.shape, F32)
    return pl.pallas_call(
        body, grid=(SMALL_TILES,),
        in_specs=[pl.BlockSpec((N_DEV, tr, LANES), lambda i: (0, i, 0)), blk, blk, blk],
        out_specs=[blk, blk, blk, blk], out_shape=[sds, sds, sds, sds],
        compiler_params=_cp(("parallel",), 40), name="adamw_small")(recv, w, m, v)


def _rms(t):
    return lax.rsqrt(jnp.mean(t * t, axis=-1, keepdims=True) + EPS)


def _rms_bwd(t, r, gain, dh, dres):
    u = dh * gain
    dt = dres + r * u - t * ((r * r * r) * (1.0 / D_MODEL) * jnp.sum(t * u, axis=-1, keepdims=True))
    return dt, dh * t * r


def _in_proj(x, gain, w):
    L = x.shape[0]
    n = w.shape[1]
    tm, tn = 1024, 1024

    def body(x_ref, g_ref, w_ref, z_ref, h_ref):
        @pl.when(pl.program_id(1) == 0)
        def _():
            t = x_ref[...]
            h_ref[...] = (t * _rms(t) * g_ref[...]).astype(BF16)
        z_ref[...] = _nn(h_ref[...], w_ref[...])

    return pl.pallas_call(
        body, grid=(L // tm, n // tn),
        in_specs=[pl.BlockSpec((tm, D_MODEL), lambda i, j: (i, 0)), pl.BlockSpec((1, D_MODEL), lambda i, j: (0, 0)),
                  pl.BlockSpec((D_MODEL, tn), lambda i, j: (0, j))],
        out_specs=[pl.BlockSpec((tm, tn), lambda i, j: (i, j)), pl.BlockSpec((tm, D_MODEL), lambda i, j: (i, 0))],
        out_shape=[jax.ShapeDtypeStruct((L, n), F32), jax.ShapeDtypeStruct((L, D_MODEL), BF16)],
        compiler_params=_cp(("parallel", "arbitrary"), 40), name="in_proj")(x, gain, w)


PIECE_W = 512


def _piece_columns(pieces):
    cols = []
    for p, arr in enumerate(pieces):
        cols.extend((p, off) for off in range(0, arr.shape[1], PIECE_W))
    return cols


def _in_proj_bwd(pieces, w, x, gain, dres):
    L = x.shape[0]
    n = w.shape[1]
    tm = 512
    npc = len(pieces)
    cols = _piece_columns(pieces)
    per_dot = 4

    def body(*refs):
        dz_refs = refs[:npc]
        w_hbm, x_ref, g_ref, dr_ref, dx_ref, dg_ref, w_scr = refs[npc:]

        @pl.when(pl.program_id(0) == 0)
        def _():
            pltpu.sync_copy(w_hbm, w_scr)
            dg_ref[...] = jnp.zeros_like(dg_ref)

        dh = None
        for c0 in range(0, len(cols), per_dot):
            chunk = cols[c0:c0 + per_dot]
            parts = [dz_refs[p][:, off:off + PIECE_W].astype(BF16) for p, off in chunk]
            term = _nt(jnp.concatenate(parts, axis=1), w_scr[:, c0 * PIECE_W:(c0 + len(chunk)) * PIECE_W])
            dh = term if dh is None else dh + term
        t = x_ref[...]
        dt, dgt = _rms_bwd(t, _rms(t), g_ref[...], dh, dr_ref[...])
        dx_ref[...] = dt
        dg_ref[...] += jnp.sum(dgt, axis=0, keepdims=True)

    row = pl.BlockSpec((tm, D_MODEL), lambda i: (i, 0))
    vec = pl.BlockSpec((1, D_MODEL), lambda i: (0, 0))
    piece_specs = [pl.BlockSpec((tm, arr.shape[1]), lambda i: (i, 0)) for arr in pieces]
    return pl.pallas_call(
        body, grid=(L // tm,),
        in_specs=piece_specs + [pl.BlockSpec(memory_space=pltpu.HBM), row, vec, row],
        out_specs=[row, vec],
        out_shape=[jax.ShapeDtypeStruct((L, D_MODEL), F32), jax.ShapeDtypeStruct((1, D_MODEL), F32)],
        scratch_shapes=[pltpu.VMEM((D_MODEL, n), BF16)],
        compiler_params=_cp(("arbitrary",), 60), name="in_proj_bwd")(*pieces, w, x, gain, dres)


def _tn(a, b, name):
    m, na = a.shape
    nb = b.shape[1]
    ta, tb, tm = min(na, 1024), min(nb, 1024), 2048
    nm = m // tm

    def body(a_ref, b_ref, o_ref):
        @pl.when(pl.program_id(2) == 0)
        def _():
            o_ref[...] = jnp.zeros_like(o_ref)
        o_ref[...] += _tn_dot(a_ref[...].astype(BF16), b_ref[...].astype(BF16))

    return pl.pallas_call(
        body, grid=(na // ta, nb // tb, nm),
        in_specs=[pl.BlockSpec((tm, ta), lambda i, j, k: (k, i)), pl.BlockSpec((tm, tb), lambda i, j, k: (k, j))],
        out_specs=pl.BlockSpec((ta, tb), lambda i, j, k: (i, j)),
        out_shape=jax.ShapeDtypeStruct((na, nb), F32),
        compiler_params=_cp(("parallel", "parallel", "arbitrary"), 48), name=name)(a, b)


def _tn_pieces(a, pieces, name):
    m, na = a.shape
    npc = len(pieces)
    cols = _piece_columns(pieces)
    per_block = D_MODEL // PIECE_W
    nj = len(cols) // per_block
    tm = 1024
    nm = m // tm
    block_of_piece = {}
    for c, (p, _) in enumerate(cols):
        block_of_piece[p] = c // per_block

    def body(*refs):
        a_ref = refs[0]
        b_refs = refs[1:1 + npc]
        o_ref = refs[1 + npc]
        j = pl.program_id(0)

        @pl.when(pl.program_id(1) == 0)
        def _():
            o_ref[...] = jnp.zeros_like(o_ref)

        for jj in range(nj):
            @pl.when(j == jj)
            def _(jj=jj):
                parts = [b_refs[p][:, off:off + PIECE_W].astype(BF16)
                         for p, off in cols[jj * per_block:(jj + 1) * per_block]]
                o_ref[...] += _tn_dot(a_ref[...], jnp.concatenate(parts, axis=1))

    piece_specs = [pl.BlockSpec((tm, arr.shape[1]),
                                functools.partial(lambda j, k, jj: (jnp.where(j == jj, k, 0), 0), jj=block_of_piece[p]))
                   for p, arr in enumerate(pieces)]
    return pl.pallas_call(
        body, grid=(nj, nm),
        in_specs=[pl.BlockSpec((tm, na), lambda j, k: (k, 0))] + piece_specs,
        out_specs=pl.BlockSpec((na, D_MODEL), lambda j, k: (0, j)),
        out_shape=jax.ShapeDtypeStruct((na, D_MODEL * nj), F32),
        compiler_params=_cp(("parallel", "arbitrary"), 56), name=name)(a, *pieces)


def _loss_grad(xf, target):
    L = xf.shape[0]
    tm = 1024

    def body(x_ref, t_ref, dy_ref, l_ref):
        e = x_ref[...] - t_ref[...]
        dy_ref[...] = e * (1.0 / D_MODEL)

        @pl.when(pl.program_id(0) == 0)
        def _():
            l_ref[...] = jnp.zeros_like(l_ref)
        l_ref[...] += jnp.sum(jnp.sum(e * e, axis=1, keepdims=True), axis=0, keepdims=True) * (0.5 / D_MODEL)

    row = pl.BlockSpec((tm, D_MODEL), lambda i: (i, 0))
    return pl.pallas_call(
        body, grid=(L // tm,), in_specs=[row, row],
        out_specs=[row, pl.BlockSpec((1, 1), lambda i: (0, 0))],
        out_shape=[jax.ShapeDtypeStruct((L, D_MODEL), F32), jax.ShapeDtypeStruct((1, 1), F32)],
        compiler_params=_cp(("arbitrary",), 40), name="loss_grad")(xf, target)


def _ffn_fwd(x, gain, wg, wu, wd):
    L = x.shape[0]
    ff = wg.shape[1]
    tm, tf = 1024, 2 * FF_CHUNK
    nf = ff // tf

    def body(x_ref, g_ref, wg_ref, wu_ref, wd_ref, o_ref, h_scr, acc):
        c = pl.program_id(1)

        @pl.when(c == 0)
        def _():
            t = x_ref[...]
            h_scr[...] = (t * _rms(t) * g_ref[...]).astype(BF16)
            acc[...] = jnp.zeros_like(acc)

        h = h_scr[...]
        down = []
        for cols in (slice(0, FF_CHUNK), slice(FF_CHUNK, 2 * FF_CHUNK)):
            gate = _nn(h, wg_ref[:, cols])
            up = _nn(h, wu_ref[:, cols])
            hid = gate * _sigmoid(gate) * up
            down.append(_nn(hid.astype(BF16), wd_ref[cols, :]))
        acc[...] += down[0] + down[1]

        @pl.when(c == nf - 1)
        def _():
            o_ref[...] = x_ref[...] + acc[...]

    row = pl.BlockSpec((tm, D_MODEL), lambda i, c: (i, 0))
    return pl.pallas_call(
        body, grid=(L // tm, nf),
        in_specs=[row, pl.BlockSpec((1, D_MODEL), lambda i, c: (0, 0)),
                  pl.BlockSpec((D_MODEL, tf), lambda i, c: (0, c)), pl.BlockSpec((D_MODEL, tf), lambda i, c: (0, c)),
                  pl.BlockSpec((tf, D_MODEL), lambda i, c: (c, 0))],
        out_specs=row, out_shape=jax.ShapeDtypeStruct((L, D_MODEL), F32),
        scratch_shapes=[pltpu.VMEM((tm, D_MODEL), BF16), pltpu.VMEM((tm, D_MODEL), F32)],
        compiler_params=_cp(("parallel", "arbitrary"), 48), name="ffn_fwd")(x, gain, wg, wu, wd)


def _ffn_bwd(x, gain, wg, wu, wd, dxo):
    L = x.shape[0]
    ff = wg.shape[1]
    tm, tf = 512, 2 * FF_CHUNK
    nf = ff // tf

    def body(x_ref, g_ref, wg_ref, wu_ref, wd_ref, dxo_ref, dx_ref, h_ref, hid_ref, dgate_ref, dup_ref, dg_ref,
             acc, dxo_b):
        i, c = pl.program_id(0), pl.program_id(1)

        @pl.when(c == 0)
        def _():
            t = x_ref[...]
            h_ref[...] = (t * _rms(t) * g_ref[...]).astype(BF16)
            acc[...] = jnp.zeros_like(acc)
            dxo_b[...] = dxo_ref[...].astype(BF16)

        h = h_ref[...]
        back = []
        for cols in (slice(0, FF_CHUNK), slice(FF_CHUNK, 2 * FF_CHUNK)):
            gate = _nn(h, wg_ref[:, cols])
            up = _nn(h, wu_ref[:, cols])
            sg = _sigmoid(gate)
            silu = gate * sg
            hid_ref[:, cols] = (silu * up).astype(BF16)
            dhid = _nt(dxo_b[...], wd_ref[cols, :])
            dup = (dhid * silu).astype(BF16)
            dgate = (dhid * up * (sg * (1.0 + gate * (1.0 - sg)))).astype(BF16)
            dup_ref[:, cols] = dup
            dgate_ref[:, cols] = dgate
            back.append(_nt(dgate, wg_ref[:, cols]) + _nt(dup, wu_ref[:, cols]))
        acc[...] += back[0] + back[1]

        @pl.when(c == nf - 1)
        def _():
            t = x_ref[...]
            dt, dgt = _rms_bwd(t, _rms(t), g_ref[...], acc[...], dxo_ref[...])
            dx_ref[...] = dt

            @pl.when(i == 0)
            def _():
                dg_ref[...] = jnp.zeros_like(dg_ref)
            dg_ref[...] += jnp.sum(dgt, axis=0, keepdims=True)

    row = pl.BlockSpec((tm, D_MODEL), lambda i, c: (i, 0))
    vec = pl.BlockSpec((1, D_MODEL), lambda i, c: (0, 0))
    wcol = pl.BlockSpec((D_MODEL, tf), lambda i, c: (0, c))
    hcol = pl.BlockSpec((tm, tf), lambda i, c: (i, c))
    return pl.pallas_call(
        body, grid=(L // tm, nf),
        in_specs=[row, vec, wcol, wcol, pl.BlockSpec((tf, D_MODEL), lambda i, c: (c, 0)), row],
        out_specs=[row, row, hcol, hcol, hcol, vec],
        out_shape=[jax.ShapeDtypeStruct((L, D_MODEL), F32), jax.ShapeDtypeStruct((L, D_MODEL), BF16),
                   jax.ShapeDtypeStruct((L, ff), BF16), jax.ShapeDtypeStruct((L, ff), BF16),
                   jax.ShapeDtypeStruct((L, ff), BF16), jax.ShapeDtypeStruct((1, D_MODEL), F32)],
        scratch_shapes=[pltpu.VMEM((tm, D_MODEL), F32), pltpu.VMEM((tm, D_MODEL), BF16)],
        compiler_params=_cp(("arbitrary", "arbitrary"), 56), name="ffn_bwd")(x, gain, wg, wu, wd, dxo)


GELU_K = math.sqrt(2.0 / math.pi)
GELU_C = 0.044715


def _gelu(y):
    return 0.5 * y * (1.0 + jnp.tanh(GELU_K * (y + GELU_C * (y * y * y))))


def _gelu_grad(y):
    th = jnp.tanh(GELU_K * (y + GELU_C * (y * y * y)))
    return 0.5 * (1.0 + th) + 0.5 * y * (1.0 - th * th) * (GELU_K * (1.0 + 3.0 * GELU_C * (y * y)))


def _merge_groups(o_refs, l_refs):
    ls = [r[...] for r in l_refs]
    os_ = [r[...] for r in o_refs]
    lmax = jnp.maximum(jnp.maximum(ls[0], ls[1]), ls[2])
    es = [jnp.exp(l - lmax) for l in ls]
    inv = 1.0 / (es[0] + es[1] + es[2])
    ws = [e * inv for e in es]
    a = ws[0] * os_[0] + ws[1] * os_[1] + ws[2] * os_[2]
    return ws, os_, a


def _mix_fwd(ols, y, z, x, wp, wa, wb, wo):
    L = x.shape[0]
    tm = 256

    def body(o0, l0, o1, l1, o2, l2, y_ref, ga_ref, gs_ref, x_ref, wp_ref, wa_ref, wb_ref, wo_ref, out_ref):
        _, _, a = _merge_groups((o0, o1, o2), (l0, l1, l2))
        a_out = _nn(a.astype(BF16), wp_ref[...])
        yg = _gelu(y_ref[...]).astype(BF16)
        s_out = _nn(yg, wa_ref[...]) * _sigmoid(_nn(yg, wb_ref[...]))
        mix = _sigmoid(ga_ref[...]) * a_out + _sigmoid(gs_ref[...]) * s_out
        out_ref[...] = x_ref[...] + _nn(mix.astype(BF16), wo_ref[...])

    half = pl.BlockSpec((tm, ATTN_W), lambda i: (i, 0))
    row = pl.BlockSpec((tm, D_MODEL), lambda i: (i, 0))
    w512 = pl.BlockSpec((ATTN_W, D_MODEL), lambda i: (0, 0))
    return pl.pallas_call(
        body, grid=(L // tm,),
        in_specs=[half] * 7 + [pl.BlockSpec((tm, D_MODEL), lambda i: (i, 5)),
                               pl.BlockSpec((tm, D_MODEL), lambda i: (i, 6)), row, w512, w512, w512,
                               pl.BlockSpec((D_MODEL, D_MODEL), lambda i: (0, 0))],
        out_specs=row, out_shape=jax.ShapeDtypeStruct((L, D_MODEL), F32),
        compiler_params=_cp(("parallel",), 48), name="mix_fwd")(*ols, y, z, z, x, wp, wa, wb, wo)


def _mix_bwd(dxm, ols, y, z, wp, wa, wb, wo):
    L = dxm.shape[0]
    tm = 256

    def body(dx_ref, o0, l0, o1, l1, o2, l2, y_ref, ga_ref, gs_ref, wp_ref, wa_ref, wb_ref, wo_ref,
             do0, dl0, do1, dl1, do2, dl2, dy_ref, dga_ref, dgs_ref, a_ref, yg_ref, mix_ref, dao_ref, dpa_ref,
             dpb_ref):
        ws, os_, a = _merge_groups((o0, o1, o2), (l0, l1, l2))
        ab = a.astype(BF16)
        a_out = _nn(ab, wp_ref[...])
        yv = y_ref[...]
        yg = _gelu(yv).astype(BF16)
        pa = _nn(yg, wa_ref[...])
        spb = _sigmoid(_nn(yg, wb_ref[...]))
        s_out = pa * spb
        sga = _sigmoid(ga_ref[...])
        sgs = _sigmoid(gs_ref[...])
        mix = sga * a_out + sgs * s_out
        dmix = _nt(dx_ref[...].astype(BF16), wo_ref[...])
        da_out = (sga * dmix).astype(BF16)
        ds_out = sgs * dmix
        dpa = (ds_out * spb).astype(BF16)
        dpb = (ds_out * pa * spb * (1.0 - spb)).astype(BF16)
        dga_ref[...] = (dmix * a_out * sga * (1.0 - sga)).astype(BF16)
        dgs_ref[...] = (dmix * s_out * sgs * (1.0 - sgs)).astype(BF16)
        dy_ref[...] = (_nt(dpa, wa_ref[...]) + _nt(dpb, wb_ref[...])) * _gelu_grad(yv)
        da = _nt(da_out, wp_ref[...])
        for w, o, do_ref, dl_ref in zip(ws, os_, (do0, do1, do2), (dl0, dl1, dl2)):
            do_ref[...] = w * da
            dl_ref[...] = -(w * da) * a
        a_ref[...] = ab
        yg_ref[...] = yg
        mix_ref[...] = mix.astype(BF16)
        dao_ref[...] = da_out
        dpa_ref[...] = dpa
        dpb_ref[...] = dpb

    half = pl.BlockSpec((tm, ATTN_W), lambda i: (i, 0))
    row = pl.BlockSpec((tm, D_MODEL), lambda i: (i, 0))
    w512 = pl.BlockSpec((ATTN_W, D_MODEL), lambda i: (0, 0))
    hf = jax.ShapeDtypeStruct((L, ATTN_W), F32)
    hb = jax.ShapeDtypeStruct((L, ATTN_W), BF16)
    rb = jax.ShapeDtypeStruct((L, D_MODEL), BF16)
    return pl.pallas_call(
        body, grid=(L // tm,),
        in_specs=[row] + [half] * 7 + [pl.BlockSpec((tm, D_MODEL), lambda i: (i, 5)),
                                       pl.BlockSpec((tm, D_MODEL), lambda i: (i, 6)), w512, w512, w512,
                                       pl.BlockSpec((D_MODEL, D_MODEL), lambda i: (0, 0))],
        out_specs=[half] * 7 + [row, row, half, half, row, row, row, row],
        out_shape=[hf] * 7 + [rb, rb, hb, hb, rb, rb, rb, rb],
        compiler_params=_cp(("parallel",), 56), name="mix_bwd")(dxm, *ols, y, z, z, wp, wa, wb, wo)


N_ATTN_ITERS = ATTN_ROWS // BLK


def _class_rows(ref, start, d):
    if d == 1:
        return ref[pl.ds(pl.multiple_of(start, BLK), BLK), :]
    return ref[pl.ds(start, BLK, stride=d), :]


def _set_class_rows(ref, start, d, val):
    if d == 1:
        ref[pl.ds(pl.multiple_of(start, BLK), BLK), :] = val
    else:
        ref[pl.ds(start, BLK, stride=d), :] = val


def _head_masks():
    lane = lax.broadcasted_iota(jnp.int32, (1, LANES), 1)
    m0 = (lane < HEAD_DIM).astype(F32)
    return m0, 1.0 - m0


def _head_norm(t, gain2, m0, m1):
    tt = t * t
    r0 = lax.rsqrt(jnp.sum(tt * m0, axis=-1, keepdims=True) * (1.0 / HEAD_DIM) + EPS)
    r1 = lax.rsqrt(jnp.sum(tt * m1, axis=-1, keepdims=True) * (1.0 / HEAD_DIM) + EPS)
    r = m0 * r0 + m1 * r1
    return t * r * gain2, r


def _head_norm_bwd(t, r, gain2, dy, m0, m1):
    u = dy * gain2
    tu = t * u
    s = m0 * jnp.sum(tu * m0, axis=-1, keepdims=True) + m1 * jnp.sum(tu * m1, axis=-1, keepdims=True)
    return r * u - t * (r * r * r) * s * (1.0 / HEAD_DIM), jnp.sum(dy * t * r, axis=0, keepdims=True)


def _band_masks():
    qi = lax.broadcasted_iota(jnp.int32, (BLK, 2 * BLK), 0)
    ki = lax.broadcasted_iota(jnp.int32, (BLK, 2 * BLK), 1)
    dist = BLK + qi - ki
    return (dist >= 0) & (dist <= BLK), ki >= BLK


ATTN_SCALE = HEAD_DIM ** -0.5


def _attn_scores(qm, kw, ok):
    return jnp.where(ok, _nt(qm, kw), -1e30)


def _attn_probs(qm, kw, ok):
    s = _attn_scores(qm, kw, ok)
    mx = jnp.max(s, axis=-1, keepdims=True)
    p = jnp.exp(s - mx)
    den = jnp.sum(p, axis=-1, keepdims=True)
    return p, den, mx


NORM_ROWS = 256


def _norm_rows(src_ref, gain2, dst_ref, m0, m1):
    n = src_ref.shape[0]
    step = min(NORM_ROWS, n)
    for r0 in range(0, n, step):
        dst_ref[r0:r0 + step, :] = _head_norm(src_ref[r0:r0 + step, :], gain2, m0, m1)[0]


def _norm_rows_bwd(src_ref, gain2, dy_ref, dst_ref, m0, m1):
    n = src_ref.shape[0]
    step = min(NORM_ROWS, n)
    dgain = jnp.zeros((1, LANES), F32)
    for r0 in range(0, n, step):
        t = src_ref[r0:r0 + step, :]
        _, r = _head_norm(t, gain2, m0, m1)
        dt, dg = _head_norm_bwd(t, r, gain2, dy_ref[r0:r0 + step, :], m0, m1)
        dst_ref[r0:r0 + step, :] = dt.astype(dst_ref.dtype)
        dgain = dgain + dg
    return dgain


def _attn_operands(it, d, first_step, q_ref, kc_ref, kp_ref, vc_ref, vp_ref, band, is_cur):
    j = it // d
    start = (it - j * d) + (d * BLK) * j
    before = jnp.maximum(start - d * BLK, 0)
    inside = j > 0
    q2 = _class_rows(q_ref, start, d)
    kc2 = _class_rows(kc_ref, start, d)
    vc2 = _class_rows(vc_ref, start, d)
    kp2 = jnp.where(inside, _class_rows(kc_ref, before, d), _class_rows(kp_ref, it - j * d, d))
    vp2 = jnp.where(inside, _class_rows(vc_ref, before, d), _class_rows(vp_ref, it - j * d, d))
    has_prev = inside | jnp.logical_not(first_step)
    return start, q2, kp2, kc2, vp2, vc2, band & (is_cur | has_prev)


def _attn_specs(d, step_of):
    nq = N_ATTN_ITERS // d

    def cur(c):
        return pl.BlockSpec((ATTN_ROWS, LANES), lambda hp, n: (step_of(n), c + hp))

    def prev(c):
        return pl.BlockSpec((d * BLK, LANES), lambda hp, n: (jnp.maximum(step_of(n) * nq - 1, 0), c + hp))

    return cur, prev, pl.BlockSpec((1, LANES), lambda hp, n: (0, 0))


def _attn_fwd(z, gq2, gk2, group):
    L = z.shape[0]
    d = DILATIONS[group]
    nsb = L // ATTN_ROWS
    cq, ck, cv = group * 4, 12 + group * 4, 24 + group * 4

    def body(q_ref, kc_ref, kp_ref, vc_ref, vp_ref, gq_ref, gk_ref, o_ref, l_ref, qn_scr, kn_scr, kpn_scr):
        first_step = pl.program_id(1) == 0
        band, is_cur = _band_masks()
        m0, m1 = _head_masks()
        _norm_rows(q_ref, gq_ref[...], qn_scr, m0, m1)
        if d * BLK == ATTN_ROWS:
            @pl.when(first_step)
            def _():
                _norm_rows(kp_ref, gk_ref[...], kpn_scr, m0, m1)

            @pl.when(jnp.logical_not(first_step))
            def _():
                kpn_scr[...] = kn_scr[...]
        else:
            _norm_rows(kp_ref, gk_ref[...], kpn_scr, m0, m1)
        _norm_rows(kc_ref, gk_ref[...], kn_scr, m0, m1)

        def per_block(it, carry):
            start, qn, kpn, kcn, vp2, vc2, ok = _attn_operands(
                it, d, first_step, qn_scr, kn_scr, kpn_scr, vc_ref, vp_ref, band, is_cur)
            kw = jnp.concatenate([kpn, kcn], axis=0).astype(BF16)
            vw = jnp.concatenate([vp2, vc2], axis=0).astype(BF16)
            o2 = jnp.zeros((BLK, LANES), F32)
            l2 = jnp.zeros((BLK, LANES), F32)
            for mh in (m0, m1):
                p, den, mx = _attn_probs((qn * (mh * ATTN_SCALE)).astype(BF16), kw, ok)
                o2 = o2 + mh * (_nn(p.astype(BF16), vw) / den)
                l2 = l2 + mh * (mx + jnp.log(den))
            _set_class_rows(o_ref, start, d, o2)
            _set_class_rows(l_ref, start, d, l2)
            return carry

        lax.fori_loop(0, N_ATTN_ITERS, per_block, 0, unroll=2)

    cur, prev, vec = _attn_specs(d, lambda n: n)
    out = pl.BlockSpec((ATTN_ROWS, LANES), lambda hp, n: (n, hp))
    sds = jax.ShapeDtypeStruct((L, ATTN_W), F32)
    return pl.pallas_call(
        body, grid=(4, nsb),
        in_specs=[cur(cq), cur(ck), prev(ck), cur(cv), prev(cv), vec, vec],
        out_specs=[out, out], out_shape=[sds, sds],
        scratch_shapes=[pltpu.VMEM((ATTN_ROWS, LANES), F32), pltpu.VMEM((ATTN_ROWS, LANES), F32),
                        pltpu.VMEM((d * BLK, LANES), F32)],
        compiler_params=_cp(("parallel", "arbitrary"), 48), name=f"attn_fwd_g{group}")(z, z, z, z, z, gq2, gk2)


def _attn_bwd(z, gq2, gk2, lse, do, c, group):
    L = z.shape[0]
    d = DILATIONS[group]
    nsb = L // ATTN_ROWS
    cq, ck, cv = group * 4, 12 + group * 4, 24 + group * 4

    def body(q_ref, kc_ref, kp_ref, vc_ref, vp_ref, gq_ref, gk_ref, l_ref, do_ref, c_ref,
             dq_ref, dk_ref, dv_ref, dgq_ref, dgk_ref, ck_scr, cv_scr, qn_scr, kn_scr, kpn_scr, dqn_scr, dkn_scr):
        hp, n = pl.program_id(0), pl.program_id(1)
        first_step = n == nsb - 1
        band, is_cur = _band_masks()
        m0, m1 = _head_masks()
        gq, gk = gq_ref[...], gk_ref[...]
        _norm_rows(q_ref, gq, qn_scr, m0, m1)
        _norm_rows(kc_ref, gk, kn_scr, m0, m1)
        _norm_rows(kp_ref, gk, kpn_scr, m0, m1)

        @pl.when((hp == 0) & (n == 0))
        def _():
            dgq_ref[...] = jnp.zeros_like(dgq_ref)
            dgk_ref[...] = jnp.zeros_like(dgk_ref)

        @pl.when(n == 0)
        def _():
            ck_scr[...] = jnp.zeros_like(ck_scr)
            cv_scr[...] = jnp.zeros_like(cv_scr)

        def per_block(i, carry):
            it = N_ATTN_ITERS - 1 - i
            start, qn, kpn, kcn, vp2, vc2, ok = _attn_operands(
                it, d, first_step, qn_scr, kn_scr, kpn_scr, vc_ref, vp_ref, band, is_cur)
            r = it - (it // d) * d
            kw = jnp.concatenate([kpn, kcn], axis=0).astype(BF16)
            vw = jnp.concatenate([vp2, vc2], axis=0).astype(BF16)
            l2 = _class_rows(l_ref, start, d)
            c2 = _class_rows(c_ref, start, d)
            do2 = _class_rows(do_ref, start, d)
            dqn = jnp.zeros((BLK, LANES), F32)
            dkw = jnp.zeros((2 * BLK, LANES), F32)
            dvw = jnp.zeros((2 * BLK, LANES), F32)
            for mh in (m0, m1):
                qm = (qn * (mh * ATTN_SCALE)).astype(BF16)
                lse = jnp.max(jnp.where(mh > 0.5, l2, -3e38), axis=-1, keepdims=True)
                pn = jnp.exp(_attn_scores(qm, kw, ok) - lse)
                dohb = (do2 * mh).astype(BF16)
                dvw = dvw + _tn_dot(pn.astype(BF16), dohb)
                ds = (pn * (_nt(dohb, vw) + jnp.sum(c2 * mh, axis=-1, keepdims=True))).astype(BF16)
                dqn = dqn + (mh * ATTN_SCALE) * _nn(ds, kw)
                dkw = dkw + _tn_dot(ds, qm)
            _set_class_rows(dqn_scr, start, d, dqn)
            _set_class_rows(dkn_scr, start, d, ck_scr[r] + dkw[BLK:])
            _set_class_rows(dv_ref, start, d, cv_scr[r] + dvw[BLK:])
            ck_scr[r] = dkw[:BLK]
            cv_scr[r] = dvw[:BLK]
            return carry

        lax.fori_loop(0, N_ATTN_ITERS, per_block, 0, unroll=2)
        dgq_ref[...] += _norm_rows_bwd(q_ref, gq, dqn_scr, dq_ref, m0, m1)
        dgk_ref[...] += _norm_rows_bwd(kc_ref, gk, dkn_scr, dk_ref, m0, m1)

    cur, prev, vec = _attn_specs(d, lambda n: nsb - 1 - n)
    sds = jax.ShapeDtypeStruct((L, ATTN_W), F32)
    sdb = jax.ShapeDtypeStruct((L, ATTN_W), BF16)
    vsd = jax.ShapeDtypeStruct((1, LANES), F32)
    return pl.pallas_call(
        body, grid=(4, nsb),
        in_specs=[cur(cq), cur(ck), prev(ck), cur(cv), prev(cv), vec, vec, cur(0), cur(0), cur(0)],
        out_specs=[cur(0), cur(0), cur(0), vec, vec], out_shape=[sdb, sdb, sds, vsd, vsd],
        scratch_shapes=[pltpu.VMEM((d, BLK, LANES), F32), pltpu.VMEM((d, BLK, LANES), F32),
                        pltpu.VMEM((ATTN_ROWS, LANES), F32), pltpu.VMEM((ATTN_ROWS, LANES), F32),
                        pltpu.VMEM((d * BLK, LANES), F32),
                        pltpu.VMEM((ATTN_ROWS, LANES), F32), pltpu.VMEM((ATTN_ROWS, LANES), F32)],
        compiler_params=_cp(("arbitrary", "arbitrary"), 56),
        name=f"attn_bwd_g{group}")(z, z, z, z, z, gq2, gk2, lse, do, c)


BLOCK_STATES = SSM_STATES // SSM_BLOCKS
BLOCK_CH = SSM_W // SSM_BLOCKS
SLABS_PER_BLOCK = BLOCK_STATES // LANES


SCAN_STEPS = 4


def _store_block(bufs, b, val, tm):
    for s in range(SLABS_PER_BLOCK):
        k = SLABS_PER_BLOCK * b + s
        bufs[k % 2][pl.ds(8 + k // 2, tm, stride=8), :] = val[:, s * LANES:(s + 1) * LANES]


def _load_block(bufs, b, tm):
    tiles = []
    for s in range(SLABS_PER_BLOCK):
        k = SLABS_PER_BLOCK * b + s
        tiles.append(bufs[k % 2][pl.ds(8 + k // 2, tm, stride=8), :])
    return jnp.concatenate(tiles, axis=1).astype(BF16)


def _ssm_project_in(ub, bdr_ref, bdi_ref, sr, si, tm):
    for b in range(SSM_BLOCKS):
        ubb = ub[:, b * BLOCK_CH:(b + 1) * BLOCK_CH]
        _store_block(sr, b, _nn(ubb, bdr_ref[b]), tm)
        _store_block(si, b, _nn(ubb, bdi_ref[b]), tm)


def _ssm_scan(a, x0, sr, si, tm):
    ar0, ar1, ai0, ai1 = a
    sr[0][0:8, :], sr[1][0:8, :], si[0][0:8, :], si[1][0:8, :] = x0

    def steps(it, c):
        xr0, xr1, xi0, xi1 = c
        base = it * (8 * SCAN_STEPS) + 8
        for q in range(SCAN_STEPS):
            rows = pl.ds(pl.multiple_of(base + 8 * q, 8), 8)
            nr0 = ar0 * xr0 - ai0 * xi0 + sr[0][rows, :]
            ni0 = ar0 * xi0 + ai0 * xr0 + si[0][rows, :]
            nr1 = ar1 * xr1 - ai1 * xi1 + sr[1][rows, :]
            ni1 = ar1 * xi1 + ai1 * xr1 + si[1][rows, :]
            sr[0][rows, :] = nr0
            si[0][rows, :] = ni0
            sr[1][rows, :] = nr1
            si[1][rows, :] = ni1
            xr0, xr1, xi0, xi1 = nr0, nr1, ni0, ni1
        return xr0, xr1, xi0, xi1

    return lax.fori_loop(0, tm // SCAN_STEPS, steps, x0)


def _load_a(ar_ref, ai_ref):
    return ar_ref[:, :LANES], ar_ref[:, LANES:], ai_ref[:, :LANES], ai_ref[:, LANES:]


def _ssm_fwd(z, ar8, ai8, bdr, bdi, cdr, cdi, dsk):
    L = z.shape[0]
    tm = SSM_TM
    nc = L // tm

    def body(u_ref, ar_ref, ai_ref, bdr_ref, bdi_ref, cdr_ref, cdi_ref, dsk_ref, y_ref, cin_ref,
             sr0, sr1, si0, si1, car):
        sr, si = (sr0, sr1), (si0, si1)

        @pl.when(pl.program_id(0) == 0)
        def _():
            car[...] = jnp.zeros_like(car)

        u = u_ref[...]
        _ssm_project_in(u.astype(BF16), bdr_ref, bdi_ref, sr, si, tm)
        cin_ref[0] = car[...]
        xr0, xr1, xi0, xi1 = _ssm_scan(_load_a(ar_ref, ai_ref), (car[0], car[1], car[2], car[3]), sr, si, tm)
        car[0], car[1], car[2], car[3] = xr0, xr1, xi0, xi1
        for b in range(SSM_BLOCKS):
            cols = slice(b * BLOCK_CH, (b + 1) * BLOCK_CH)
            y_ref[:, cols] = (dsk_ref[:, cols] * u[:, cols] + _nn(_load_block(sr, b, tm), cdr_ref[b])
                              - _nn(_load_block(si, b, tm), cdi_ref[b]))

    def const(shape):
        return pl.BlockSpec(shape, lambda i: (0,) * len(shape))

    state = pltpu.VMEM(((tm + 1) * 8, LANES), F32)
    wb = const((SSM_BLOCKS, BLOCK_CH, BLOCK_STATES))
    wc = const((SSM_BLOCKS, BLOCK_STATES, BLOCK_CH))
    return pl.pallas_call(
        body, grid=(nc,),
        in_specs=[pl.BlockSpec((tm, SSM_W), lambda i: (i, COL_U)), const((8, 256)), const((8, 256)),
                  wb, wb, wc, wc, const((1, SSM_W))],
        out_specs=[pl.BlockSpec((tm, SSM_W), lambda i: (i, 0)), pl.BlockSpec((1, 4, 8, LANES), lambda i: (i, 0, 0, 0))],
        out_shape=[jax.ShapeDtypeStruct((L, SSM_W), F32), jax.ShapeDtypeStruct((nc, 4, 8, LANES), F32)],
        scratch_shapes=[state, state, state, state, pltpu.VMEM((4, 8, LANES), F32)],
        compiler_params=_cp(("arbitrary",), 48), name="ssm_fwd")(z, ar8, ai8, bdr, bdi, cdr, cdi, dsk)


def _ssm_bwd(z, dy, cin, ar8, ai8, bdr, bdi, cdr, cdi, dsk):
    L = z.shape[0]
    tm = SSM_TM
    nc = L // tm

    def body(u_ref, dy_ref, cin_ref, ar_ref, ai_ref, dsk_ref, bdr_ref, bdi_ref, cdr_ref, cdi_ref,
             du_ref, da_ref, dds_ref, dbdr_ref, dbdi_ref, dcdr_ref, dcdi_ref,
             sr0, sr1, si0, si1, gr0, gr1, gi0, gi1, carg):
        sr, si, gr, gi = (sr0, sr1), (si0, si1), (gr0, gr1), (gi0, gi1)

        @pl.when(pl.program_id(0) == 0)
        def _():
            carg[...] = jnp.zeros_like(carg)
            for ref in (da_ref, dds_ref, dbdr_ref, dbdi_ref, dcdr_ref, dcdi_ref):
                ref[...] = jnp.zeros_like(ref)

        u = u_ref[...]
        ub = u.astype(BF16)
        dyv = dy_ref[...]
        dyb = dyv.astype(BF16)
        a = _load_a(ar_ref, ai_ref)
        ar0, ar1, ai0, ai1 = a
        x_in = (cin_ref[0, 0], cin_ref[0, 1], cin_ref[0, 2], cin_ref[0, 3])
        _ssm_project_in(ub, bdr_ref, bdi_ref, sr, si, tm)
        _ssm_scan(a, x_in, sr, si, tm)
        for b in range(SSM_BLOCKS):
            dyb_b = dyb[:, b * BLOCK_CH:(b + 1) * BLOCK_CH]
            _store_block(gr, b, _nt(dyb_b, cdr_ref[b]), tm)
            _store_block(gi, b, -_nt(dyb_b, cdi_ref[b]), tm)

        def grad_steps(it, c):
            (nr0, nr1, ni0, ni1), (d_r0, d_r1, d_i0, d_i1) = c
            base = (tm - SCAN_STEPS * (it + 1)) * 8
            for q in reversed(range(SCAN_STEPS)):
                prev = pl.ds(pl.multiple_of(base + 8 * q, 8), 8)
                rows = pl.ds(pl.multiple_of(base + 8 * q + 8, 8), 8)
                g_r0 = gr[0][rows, :] + ar0 * nr0 + ai0 * ni0
                g_i0 = gi[0][rows, :] + ar0 * ni0 - ai0 * nr0
                g_r1 = gr[1][rows, :] + ar1 * nr1 + ai1 * ni1
                g_i1 = gi[1][rows, :] + ar1 * ni1 - ai1 * nr1
                gr[0][rows, :] = g_r0
                gi[0][rows, :] = g_i0
                gr[1][rows, :] = g_r1
                gi[1][rows, :] = g_i1
                pr0, pr1, pi0, pi1 = sr[0][prev, :], sr[1][prev, :], si[0][prev, :], si[1][prev, :]
                d_r0 = d_r0 + pr0 * g_r0 + pi0 * g_i0
                d_r1 = d_r1 + pr1 * g_r1 + pi1 * g_i1
                d_i0 = d_i0 + pr0 * g_i0 - pi0 * g_r0
                d_i1 = d_i1 + pr1 * g_i1 - pi1 * g_r1
                nr0, nr1, ni0, ni1 = g_r0, g_r1, g_i0, g_i1
            return (nr0, nr1, ni0, ni1), (d_r0, d_r1, d_i0, d_i1)

        acc0 = (da_ref[0], da_ref[1], da_ref[2], da_ref[3])
        g_first, acc = lax.fori_loop(0, tm // SCAN_STEPS, grad_steps,
                                     ((carg[0], carg[1], carg[2], carg[3]), acc0))
        carg[0], carg[1], carg[2], carg[3] = g_first
        da_ref[0], da_ref[1], da_ref[2], da_ref[3] = acc

        for b in range(SSM_BLOCKS):
            cols = slice(b * BLOCK_CH, (b + 1) * BLOCK_CH)
            grb, gib = _load_block(gr, b, tm), _load_block(gi, b, tm)
            du_ref[:, cols] = (dsk_ref[:, cols] * dyv[:, cols] + _nt(grb, bdr_ref[b])
                               + _nt(gib, bdi_ref[b])).astype(BF16)
            dbdr_ref[b] += _tn_dot(ub[:, cols], grb)
            dbdi_ref[b] += _tn_dot(ub[:, cols], gib)
            dcdr_ref[b] += _tn_dot(_load_block(sr, b, tm), dyb[:, cols])
            dcdi_ref[b] -= _tn_dot(_load_block(si, b, tm), dyb[:, cols])
        dds_ref[...] += jnp.sum(dyv * u, axis=0, keepdims=True)

    def const(shape):
        return pl.BlockSpec(shape, lambda i: (0,) * len(shape))

    state = pltpu.VMEM(((tm + 1) * 8, LANES), F32)
    wb = const((SSM_BLOCKS, BLOCK_CH, BLOCK_STATES))
    wc = const((SSM_BLOCKS, BLOCK_STATES, BLOCK_CH))
    return pl.pallas_call(
        body, grid=(nc,),
        in_specs=[pl.BlockSpec((tm, SSM_W), lambda i: (nc - 1 - i, COL_U)),
                  pl.BlockSpec((tm, SSM_W), lambda i: (nc - 1 - i, 0)),
                  pl.BlockSpec((1, 4, 8, LANES), lambda i: (nc - 1 - i, 0, 0, 0)),
                  const((8, 256)), const((8, 256)), const((1, SSM_W)), wb, wb, wc, wc],
        out_specs=[pl.BlockSpec((tm, SSM_W), lambda i: (nc - 1 - i, 0)), const((4, 8, LANES)), const((1, SSM_W)),
                   wb, wb, wc, wc],
        out_shape=[jax.ShapeDtypeStruct((L, SSM_W), BF16), jax.ShapeDtypeStruct((4, 8, LANES), F32),
                   jax.ShapeDtypeStruct((1, SSM_W), F32),
                   jax.ShapeDtypeStruct((SSM_BLOCKS, BLOCK_CH, BLOCK_STATES), F32),
                   jax.ShapeDtypeStruct((SSM_BLOCKS, BLOCK_CH, BLOCK_STATES), F32),
                   jax.ShapeDtypeStruct((SSM_BLOCKS, BLOCK_STATES, BLOCK_CH), F32),
                   jax.ShapeDtypeStruct((SSM_BLOCKS, BLOCK_STATES, BLOCK_CH), F32)],
        scratch_shapes=[state] * 8 + [pltpu.VMEM((4, 8, LANES), F32)],
        compiler_params=_cp(("arbitrary",), 56), name="ssm_bwd")(z, dy, cin, ar8, ai8, dsk, bdr, bdi, cdr, cdi)


def _discretise(lam_re, lam_im, log_dt, b_re, b_im):
    dt = jnp.exp(log_dt)[:, None]
    mag = jnp.exp(lam_re * dt)
    ang = lam_im * dt
    abar_re = mag * jnp.cos(ang)
    abar_im = mag * jnp.sin(ang)
    nr = abar_re - 1.0
    ni = abar_im
    den = lam_re * lam_re + lam_im * lam_im
    cr = ((nr * lam_re + ni * lam_im) / den)[..., None]
    ci = ((ni * lam_re - nr * lam_im) / den)[..., None]
    return abar_re, abar_im, cr * b_re - ci * b_im, cr * b_im + ci * b_re


GROUPS_PER_BLOCK = 8


def _block_diag_in(bbar):
    eye = jnp.eye(GROUPS_PER_BLOCK, dtype=F32)
    return jnp.einsum("igpc,gh->igchp", bbar.reshape(SSM_BLOCKS, GROUPS_PER_BLOCK, 64, 16), eye).reshape(
        SSM_BLOCKS, BLOCK_CH, BLOCK_STATES)


def _block_diag_in_t(blocks):
    eye = jnp.eye(GROUPS_PER_BLOCK, dtype=F32)
    return jnp.einsum("igchp,gh->igpc", blocks.reshape(SSM_BLOCKS, GROUPS_PER_BLOCK, 16, GROUPS_PER_BLOCK, 64),
                      eye).reshape(32, 64, 16)


def _block_diag_out(c):
    eye = jnp.eye(GROUPS_PER_BLOCK, dtype=F32)
    return jnp.einsum("igcp,gh->igphc", c.reshape(SSM_BLOCKS, GROUPS_PER_BLOCK, 16, 64), eye).reshape(
        SSM_BLOCKS, BLOCK_STATES, BLOCK_CH)


def _block_diag_out_t(blocks):
    eye = jnp.eye(GROUPS_PER_BLOCK, dtype=F32)
    return jnp.einsum("igphc,gh->igcp", blocks.reshape(SSM_BLOCKS, GROUPS_PER_BLOCK, 64, GROUPS_PER_BLOCK, 16),
                      eye).reshape(32, 16, 64)


SMALL_NAMES = ("g_mix", "g_q", "g_k", "lambda_re", "lambda_im", "log_dt", "b_re", "b_im", "c_re", "c_im",
               "d_skip", "g_ffn")


def _pack_small(parts):
    flat = jnp.concatenate([parts[n].reshape(-1) for n in SMALL_NAMES])
    pad = (-flat.shape[0]) % (8 * LANES * SMALL_TILES)
    return jnp.pad(flat, (0, pad)).reshape(-1, LANES)


def _unpack_small(packed, like):
    flat = packed.reshape(-1)
    out, off = {}, 0
    for n in SMALL_NAMES:
        size = like[n].size
        out[n] = flat[off:off + size].reshape(like[n].shape)
        off += size
    return out


BIG_NAMES = ("w_in", "w_attn_proj", "w_glu_a", "w_glu_b", "w_out", "w_ffn_gate", "w_ffn_up", "w_ffn_down")
BIG_SHARD_AXIS = {"w_in": 2, "w_attn_proj": 2, "w_glu_a": 2, "w_glu_b": 2, "w_out": 1,
                  "w_ffn_gate": 2, "w_ffn_up": 2, "w_ffn_down": 1}
ADAMW_ROWS = {"w_in": 256, "w_attn_proj": 512, "w_glu_a": 512, "w_glu_b": 512, "w_out": 128,
              "w_ffn_gate": 256, "w_ffn_up": 256, "w_ffn_down": 176}


def kernel(x, g_mix, w_in, g_q, g_k, w_attn_proj, lambda_re, lambda_im, log_dt, b_re, b_im, c_re, c_im, d_skip, w_glu_a, w_glu_b, w_out, g_ffn, w_ffn_gate, w_ffn_up, w_ffn_down, loss_target, m_g_mix, m_w_in, m_g_q, m_g_k, m_w_attn_proj, m_lambda_re, m_lambda_im, m_log_dt, m_b_re, m_b_im, m_c_re, m_c_im, m_d_skip, m_w_glu_a, m_w_glu_b, m_w_out, m_g_ffn, m_w_ffn_gate, m_w_ffn_up, m_w_ffn_down, v_g_mix, v_w_in, v_g_q, v_g_k, v_w_attn_proj, v_lambda_re, v_lambda_im, v_log_dt, v_b_re, v_b_im, v_c_re, v_c_im, v_d_skip, v_w_glu_a, v_w_glu_b, v_w_out, v_g_ffn, v_w_ffn_gate, v_w_ffn_up, v_w_ffn_down):
    args = dict(locals())
    weights = {n: args[n] for n in BIG_NAMES + SMALL_NAMES}
    moments_m = {n: args["m_" + n] for n in BIG_NAMES + SMALL_NAMES}
    moments_v = {n: args["v_" + n] for n in BIG_NAMES + SMALL_NAMES}
    x0 = x[0]
    target = loss_target[0]

    shards = []
    for n in BIG_NAMES:
        w = weights[n]
        rows_to, cols_to = w.shape[1], w.shape[2]
        if n in ("w_ffn_gate", "w_ffn_up"):
            cols_to = FF_SHARD_PAD
        if n == "w_ffn_down":
            rows_to = FF_SHARD_PAD
        shards.append(_prep_weight(w, rows_to, cols_to, "prep_" + n))
    full = dict(zip(BIG_NAMES, _all_gather(shards, [BIG_SHARD_AXIS[n] for n in BIG_NAMES])))

    saved = []
    xl = x0
    for l in range(DEPTH):
        abar_re, abar_im, bb_re, bb_im = _discretise(lambda_re[l], lambda_im[l], log_dt[l], b_re[l], b_im[l])
        ssm = dict(ar8=abar_re.reshape(8, 256), ai8=abar_im.reshape(8, 256),
                   bdr=_block_diag_in(bb_re).astype(BF16), bdi=_block_diag_in(bb_im).astype(BF16),
                   cdr=_block_diag_out(c_re[l]).astype(BF16), cdi=_block_diag_out(c_im[l]).astype(BF16),
                   dsk=d_skip[l][None])
        gq2 = jnp.tile(g_q[l], 2)[None]
        gk2 = jnp.tile(g_k[l], 2)[None]
        z, h = _in_proj(xl, g_mix[l][None], full["w_in"][l])
        ols = []
        for g in range(N_GROUPS):
            ols.extend(_attn_fwd(z, gq2, gk2, g))
        y, cin = _ssm_fwd(z, **ssm)
        xm = _mix_fwd(ols, y, z, xl, full["w_attn_proj"][l], full["w_glu_a"][l], full["w_glu_b"][l], full["w_out"][l])
        xo = _ffn_fwd(xm, g_ffn[l][None], full["w_ffn_gate"][l], full["w_ffn_up"][l], full["w_ffn_down"][l])
        saved.append(dict(x=xl, z=z, h=h, ols=ols, y=y, cin=cin, xm=xm, ssm=ssm, gq2=gq2, gk2=gk2))
        xl = xo

    dxo, loss_local = _loss_grad(xl, target)
    loss = lax.psum(loss_local[0, 0], MESH_AXES)
    big_grads = {n: [None] * DEPTH for n in BIG_NAMES}
    small_grads = {n: [None] * DEPTH for n in SMALL_NAMES}
    for l in reversed(range(DEPTH)):
        s = saved[l]
        dxm, h2, hid, dgate, dup, dgffn = _ffn_bwd(s["xm"], g_ffn[l][None], full["w_ffn_gate"][l],
                                                   full["w_ffn_up"][l], full["w_ffn_down"][l], dxo)
        big_grads["w_ffn_down"][l] = _tn(hid, dxo, "grad_w_ffn_down")
        big_grads["w_ffn_gate"][l] = _tn(h2, dgate, "grad_w_ffn_gate")
        big_grads["w_ffn_up"][l] = _tn(h2, dup, "grad_w_ffn_up")
        (do0, c0, do1, c1, do2, c2, dy, dga, dgs, a_b, yg_b, mix_b, dao_b, dpa_b, dpb_b) = _mix_bwd(
            dxm, s["ols"], s["y"], s["z"], full["w_attn_proj"][l], full["w_glu_a"][l], full["w_glu_b"][l],
            full["w_out"][l])
        big_grads["w_out"][l] = _tn(mix_b, dxm, "grad_w_out")
        big_grads["w_attn_proj"][l] = _tn(a_b, dao_b, "grad_w_attn_proj")
        big_grads["w_glu_a"][l] = _tn(yg_b, dpa_b, "grad_w_glu_a")
        big_grads["w_glu_b"][l] = _tn(yg_b, dpb_b, "grad_w_glu_b")
        du, da4, ddsk, dbdr, dbdi, dcdr, dcdi = _ssm_bwd(s["z"], dy, s["cin"], **s["ssm"])
        dqkv = []
        dgq = jnp.zeros((1, LANES), F32)
        dgk = jnp.zeros((1, LANES), F32)
        for g, (do_g, c_g) in enumerate(((do0, c0), (do1, c1), (do2, c2))):
            dq, dk, dv, dgq_g, dgk_g = _attn_bwd(s["z"], s["gq2"], s["gk2"], s["ols"][2 * g + 1], do_g, c_g, g)
            dqkv.append((dq, dk, dv))
            dgq, dgk = dgq + dgq_g, dgk + dgk_g
        pieces = [dqkv[g][j] for j in range(3) for g in range(N_GROUPS)] + [du, dga, dgs]
        dxo, dgmix = _in_proj_bwd(pieces, full["w_in"][l], s["x"], g_mix[l][None], dxm)
        big_grads["w_in"][l] = _tn_pieces(s["h"], pieces, "grad_w_in")
        _, disc_vjp = jax.vjp(_discretise, lambda_re[l], lambda_im[l], log_dt[l], b_re[l], b_im[l])
        dar = jnp.concatenate([da4[0], da4[1]], axis=1).reshape(32, 64)
        dai = jnp.concatenate([da4[2], da4[3]], axis=1).reshape(32, 64)
        dlr, dli, dldt, dbre, dbim = disc_vjp((dar, dai, _block_diag_in_t(dbdr), _block_diag_in_t(dbdi)))
        small_grads["g_mix"][l] = dgmix[0]
        small_grads["g_q"][l] = dgq[0, :HEAD_DIM] + dgq[0, HEAD_DIM:]
        small_grads["g_k"][l] = dgk[0, :HEAD_DIM] + dgk[0, HEAD_DIM:]
        small_grads["lambda_re"][l] = dlr
        small_grads["lambda_im"][l] = dli
        small_grads["log_dt"][l] = dldt
        small_grads["b_re"][l] = dbre
        small_grads["b_im"][l] = dbim
        small_grads["c_re"][l] = _block_diag_out_t(dcdr)
        small_grads["c_im"][l] = _block_diag_out_t(dcdi)
        small_grads["d_skip"][l] = ddsk[0]
        small_grads["g_ffn"][l] = dgffn[0]
    grad_x = dxo[None]

    small_local = {n: jnp.stack(small_grads[n]) for n in SMALL_NAMES}
    rs_axes = [BIG_SHARD_AXIS[n] - 1 for n in BIG_NAMES]
    got = _exchange_with_sibling([big_grads[n] for n in BIG_NAMES], rs_axes)
    core = lax.axis_index("c").astype(jnp.int32).reshape(1)
    sums = [[_chip_sum(big_grads[n][l], got[t], l, rs_axes[t], core, "chip_sum_" + n) for l in range(DEPTH)]
            for t, n in enumerate(BIG_NAMES)]
    recv = _exchange_chip_sums(sums, _pack_small(small_local))
    out_g, out_d, out_m, out_v = {}, {}, {}, {}
    for n, r in zip(BIG_NAMES, recv[:-1]):
        out_g[n], out_d[n], out_m[n], out_v[n] = _adamw_big(r, weights[n], moments_m[n], moments_v[n],
                                                            ADAMW_ROWS[n], "adamw_" + n)
    like = {n: weights[n] for n in SMALL_NAMES}
    packed = _adamw_small(recv[-1], _pack_small(like), _pack_small({n: moments_m[n] for n in SMALL_NAMES}),
                          _pack_small({n: moments_v[n] for n in SMALL_NAMES}))
    for dst, p in zip((out_g, out_d, out_m, out_v), packed):
        dst.update(_unpack_small(p, like))

    order = ("g_mix", "w_in", "g_q", "g_k", "w_attn_proj", "lambda_re", "lambda_im", "log_dt", "b_re", "b_im",
             "c_re", "c_im", "d_skip", "w_glu_a", "w_glu_b", "w_out", "g_ffn", "w_ffn_gate", "w_ffn_up",
             "w_ffn_down")
    return (loss, grad_x, *[out_g[n] for n in order], *[out_d[n] for n in order],
            *[out_m[n] for n in order], *[out_v[n] for n in order])
```

```python
import functools
import math

import jax
import jax.numpy as jnp
from jax import lax
from jax.experimental import pallas as pl
from jax.experimental.pallas import tpu as pltpu

F32 = jnp.float32
BF16 = jnp.bfloat16

D_MODEL = 1024
DEPTH = 4
N_DEV = 8
N_CHIPS = 4
HEAD_DIM = 64
BLK = 128
LANES = 128
ATTN_W = 512
N_GROUPS = 3
DILATIONS = (1, 4, 16)
ATTN_ROWS = 2048
SSM_W = 512
SSM_STATES = 2048
SSM_BLOCKS = 4
IN_COLS = 7168
COL_U = 9
D_FF = 2816
FF_SHARD = D_FF // N_DEV
FF_SHARD_PAD = 384
FF_PAD = FF_SHARD_PAD * N_DEV
FF_CHUNK = 512
EPS = 1e-6
SSM_TM = 512
SMALL_TILES = 4

ADAM_LR = 0.001
ADAM_B1 = 0.9
ADAM_B2 = 0.999
ADAM_EPS = 1e-08
ADAM_WD = 0.01
ADAM_STEP = 10

MESH_AXES = ("x", "y", "c")
MIB = 1024 * 1024


def _cp(sem=None, vmem_mib=None):
    kw = {}
    if sem is not None:
        kw["dimension_semantics"] = sem
    if vmem_mib is not None:
        kw["vmem_limit_bytes"] = vmem_mib * MIB
    return pltpu.CompilerParams(**kw)


def _nt(a, b):
    return lax.dot_general(a, b, (((1,), (1,)), ((), ())), preferred_element_type=F32)


def _tn_dot(a, b):
    return lax.dot_general(a, b, (((0,), (0,)), ((), ())), preferred_element_type=F32)


def _nn(a, b):
    return jnp.dot(a, b, preferred_element_type=F32)


def _sigmoid(t):
    return 0.5 * jnp.tanh(0.5 * t) + 0.5


def _prep_weight(w, rows_to, cols_to, name):
    _, k, n = w.shape

    def body(w_ref, o_ref):
        if rows_to != k or cols_to != n:
            o_ref[...] = jnp.zeros(o_ref.shape, BF16)
        o_ref[0, :k, :n] = w_ref[0].astype(BF16)

    return pl.pallas_call(
        body, grid=(DEPTH,),
        in_specs=[pl.BlockSpec((1, k, n), lambda l: (l, 0, 0))],
        out_specs=pl.BlockSpec((1, rows_to, cols_to), lambda l: (l, 0, 0)),
        out_shape=jax.ShapeDtypeStruct((DEPTH, rows_to, cols_to), BF16),
        compiler_params=_cp(("parallel",), 40), name=name)(w)


def _my_index():
    return 4 * lax.axis_index("x") + 2 * lax.axis_index("y") + lax.axis_index("c")


def _my_chip():
    return 2 * lax.axis_index("x") + lax.axis_index("y")


def _sibling():
    return (lax.axis_index("x"), lax.axis_index("y"), 1 - lax.axis_index("c"))


def _other_chip(j):
    return (jnp.bitwise_xor(lax.axis_index("x"), (j >> 1) & 1), jnp.bitwise_xor(lax.axis_index("y"), j & 1))


def _slab(ref, idx, width, axis):
    start = pl.multiple_of(idx * width, width)
    sl = [slice(None)] * len(ref.shape)
    sl[axis] = pl.ds(start, width)
    return ref.at[tuple(sl)]


def _remote(src, dst, ssem, rsem, device):
    return pltpu.make_async_remote_copy(src_ref=src, dst_ref=dst, send_sem=ssem, recv_sem=rsem,
                                        device_id=device, device_id_type=pl.DeviceIdType.MESH)


def _two_level_gather(srcs, blocks, ssem, rsem, lsem):
    nt = len(srcs)
    x, y, c = lax.axis_index("x"), lax.axis_index("y"), lax.axis_index("c")
    me = _my_index()
    local, sends = [], []
    for t in range(nt):
        mine = blocks[t](me)
        loc = pltpu.make_async_copy(srcs[t], mine, lsem.at[t])
        loc.start()
        local.append(loc)
        first = [_remote(srcs[t], mine, ssem.at[t, 0], rsem.at[t, 0], _sibling())]
        for j in range(1, N_CHIPS):
            first.append(_remote(srcs[t], mine, ssem.at[t, j], rsem.at[t, j], (*_other_chip(j), c)))
        for cp in first:
            cp.start()
        sends.extend(first)
    for t in range(nt):
        for j in range(1, N_CHIPS):
            ox, oy = _other_chip(j)
            landed = blocks[t](4 * ox + 2 * oy + c)
            _remote(landed, landed, ssem.at[t, j], rsem.at[t, j], _sibling()).wait_recv()
            fwd = _remote(landed, landed, ssem.at[t, 3 + j], rsem.at[t, 3 + j], _sibling())
            fwd.start()
            sends.append(fwd)
    for t in range(nt):
        got = blocks[t](4 * x + 2 * y + (1 - c))
        _remote(got, got, ssem.at[t, 0], rsem.at[t, 0], _sibling()).wait_recv()
        for j in range(1, N_CHIPS):
            ox, oy = _other_chip(j)
            got = blocks[t](4 * ox + 2 * oy + (1 - c))
            _remote(got, got, ssem.at[t, 3 + j], rsem.at[t, 3 + j], _sibling()).wait_recv()
    for cp in sends:
        cp.wait_send()
    for cp in local:
        cp.wait()


def _gather_sems(nt):
    return [pltpu.SemaphoreType.DMA((nt, N_DEV - 1)), pltpu.SemaphoreType.DMA((nt, N_DEV - 1)),
            pltpu.SemaphoreType.DMA((nt,))]


def _all_gather(shards, axes):
    nt = len(shards)

    def body(*refs):
        ins, outs = refs[:nt], refs[nt:2 * nt]
        ssem, rsem, lsem = refs[2 * nt:]
        blocks = [functools.partial(_slab, outs[t], width=shards[t].shape[axes[t]], axis=axes[t]) for t in range(nt)]
        _two_level_gather(ins, blocks, ssem, rsem, lsem)

    out_shape = []
    for t in range(nt):
        s = list(shards[t].shape)
        s[axes[t]] *= N_DEV
        out_shape.append(jax.ShapeDtypeStruct(tuple(s), shards[t].dtype))
    return pl.pallas_call(
        body,
        in_specs=[pl.BlockSpec(memory_space=pltpu.HBM)] * nt,
        out_specs=[pl.BlockSpec(memory_space=pltpu.HBM)] * nt,
        out_shape=out_shape, scratch_shapes=_gather_sems(nt),
        name="all_gather_weights")(*shards)


def _exchange_with_sibling(grads, axes):
    nt = len(grads)

    def body(*refs):
        ins = [refs[t * DEPTH:(t + 1) * DEPTH] for t in range(nt)]
        outs = refs[nt * DEPTH: nt * DEPTH + nt]
        ssem, rsem = refs[nt * DEPTH + nt:]
        c = lax.axis_index("c")
        for t in range(nt):
            width = grads[t][0].shape[axes[t]] // N_DEV
            for q in range(N_CHIPS):
                for l in range(DEPTH):
                    _remote(_slab(ins[t][l], 2 * q + (1 - c), width, axes[t]), outs[t].at[q, l],
                            ssem.at[t], rsem.at[t], _sibling()).start()
        for t in range(nt):
            _remote(outs[t], outs[t], ssem.at[t], rsem.at[t], _sibling()).wait()

    out_shape = []
    for t in range(nt):
        s = list(grads[t][0].shape)
        s[axes[t]] //= N_DEV
        out_shape.append(jax.ShapeDtypeStruct((N_CHIPS, DEPTH, s[0], s[1]), F32))
    flat = [g for per_type in grads for g in per_type]
    return pl.pallas_call(
        body,
        in_specs=[pl.BlockSpec(memory_space=pltpu.HBM)] * len(flat),
        out_specs=[pl.BlockSpec(memory_space=pltpu.HBM)] * nt,
        out_shape=out_shape,
        scratch_shapes=[pltpu.SemaphoreType.DMA((nt,)), pltpu.SemaphoreType.DMA((nt,))],
        name="grads_to_sibling")(*flat)


def _chip_sum(grad, got, layer, axis, core, name):
    _, _, r, c = got.shape
    tr = min(r, 512)

    def body(core_ref, g_ref, s_ref, o_ref):
        o_ref[0] = (g_ref[...] + s_ref[0, 0]).astype(BF16)

    if axis == 1:
        g_spec = pl.BlockSpec((tr, c), lambda q, i, core_ref: (i, 2 * q + core_ref[0]))
    else:
        g_spec = pl.BlockSpec((tr, c), lambda q, i, core_ref: ((2 * q + core_ref[0]) * (r // tr) + i, 0))
    return pl.pallas_call(
        body,
        grid_spec=pltpu.PrefetchScalarGridSpec(
            num_scalar_prefetch=1, grid=(N_CHIPS, r // tr),
            in_specs=[g_spec, pl.BlockSpec((1, 1, tr, c), lambda q, i, core_ref: (q, layer, i, 0))],
            out_specs=pl.BlockSpec((1, tr, c), lambda q, i, core_ref: (q, i, 0))),
        out_shape=jax.ShapeDtypeStruct((N_CHIPS, r, c), BF16),
        compiler_params=_cp(("parallel", "parallel"), 40), name=name)(core, grad, got)


def _exchange_chip_sums(sums, small):
    nt = len(sums)

    def body(*refs):
        ins = [refs[t * DEPTH:(t + 1) * DEPTH] for t in range(nt)]
        small_ref = refs[nt * DEPTH]
        outs = refs[nt * DEPTH + 1: nt * DEPTH + 1 + nt]
        small_out = refs[nt * DEPTH + 1 + nt]
        ssem, rsem, lsem, g_ssem, g_rsem, g_lsem = refs[nt * DEPTH + 2 + nt:]
        c = lax.axis_index("c")
        chip = _my_chip()
        for t in range(nt):
            for l in range(DEPTH):
                pltpu.make_async_copy(ins[t][l].at[chip], outs[t].at[chip, l], lsem.at[t]).start()
            for j in range(1, N_CHIPS):
                other = jnp.bitwise_xor(chip, j)
                for l in range(DEPTH):
                    _remote(ins[t][l].at[other], outs[t].at[chip, l], ssem.at[t, j - 1], rsem.at[t, j - 1],
                            (*_other_chip(j), c)).start()
        _two_level_gather([small_ref], [lambda idx: small_out.at[idx]], g_ssem, g_rsem, g_lsem)
        for t in range(nt):
            pltpu.make_async_copy(outs[t].at[chip], outs[t].at[chip], lsem.at[t]).wait()
            for j in range(1, N_CHIPS):
                other = jnp.bitwise_xor(chip, j)
                _remote(outs[t].at[other], outs[t].at[other], ssem.at[t, j - 1], rsem.at[t, j - 1],
                        (*_other_chip(j), c)).wait()

    out_shape = []
    for t in range(nt):
        _, r, c = sums[t][0].shape
        out_shape.append(jax.ShapeDtypeStruct((N_CHIPS, DEPTH, r, c), BF16))
    out_shape.append(jax.ShapeDtypeStruct((N_DEV,) + small.shape, F32))
    flat = [s for per_type in sums for s in per_type]
    return pl.pallas_call(
        body,
        in_specs=[pl.BlockSpec(memory_space=pltpu.HBM)] * (len(flat) + 1),
        out_specs=[pl.BlockSpec(memory_space=pltpu.HBM)] * (nt + 1),
        out_shape=out_shape,
        scratch_shapes=[pltpu.SemaphoreType.DMA((nt, N_CHIPS - 1)), pltpu.SemaphoreType.DMA((nt, N_CHIPS - 1)),
                        pltpu.SemaphoreType.DMA((nt,))] + _gather_sems(1),
        name="chip_sums_over_ici")(*flat, small)


def _adamw_math(w, g, m, v):
    m = ADAM_B1 * m + (1.0 - ADAM_B1) * g
    v = ADAM_B2 * v + (1.0 - ADAM_B2) * (g * g)
    m_hat = m / (1.0 - ADAM_B1 ** ADAM_STEP)
    v_hat = v / (1.0 - ADAM_B2 ** ADAM_STEP)
    delta = -ADAM_LR * (m_hat / (jnp.sqrt(v_hat) + ADAM_EPS) + ADAM_WD * w)
    return delta, m, v


def _adamw_big(recv, w, m, v, tk, name):
    _, k, n = w.shape
    npad = recv.shape[3]

    def body(r_ref, w_ref, m_ref, v_ref, g_out, d_out, m_out, v_out):
        g = r_ref[0, 0].astype(F32)
        for s in range(1, N_CHIPS):
            g = g + r_ref[s, 0].astype(F32)
        g = g[:, :n]
        delta, mn, vn = _adamw_math(w_ref[0], g, m_ref[0], v_ref[0])
        g_out[0] = g
        d_out[0] = delta
        m_out[0] = mn
        v_out[0] = vn

    blk = pl.BlockSpec((1, tk, n), lambda l, i: (l, i, 0))
    sds = jax.ShapeDtypeStruct(w.shape, F32)
    return pl.pallas_call(
        body, grid=(DEPTH, k // tk),
        in_specs=[pl.BlockSpec((N_CHIPS, 1, tk, npad), lambda l, i: (0, l, i, 0)), blk, blk, blk],
        out_specs=[blk, blk, blk, blk], out_shape=[sds, sds, sds, sds],
        compiler_params=_cp(("parallel", "parallel"), 48), name=name)(recv, w, m, v)


def _adamw_small(recv, w, m, v):
    rows = w.shape[0]
    tr = rows // SMALL_TILES

    def body(r_ref, w_ref, m_ref, v_ref, g_out, d_out, m_out, v_out):
        g = r_ref[0]
        for s in range(1, N_DEV):
            g = g + r_ref[s]
        delta, mn, vn = _adamw_math(w_ref[...], g, m_ref[...], v_ref[...])
        g_out[...] = g
        d_out[...] = delta
        m_out[...] = mn
        v_out[...] = vn

    blk = pl.BlockSpec((tr, LANES), lambda i: (i, 0))
    sds = jax.ShapeDtypeStruct(w.shape, F32)
    return pl.pallas_call(
        body, grid=(SMALL_TILES,),
        in_specs=[pl.BlockSpec((N_DEV, tr, LANES), lambda i: (0, i, 0)), blk, blk, blk],
        out_specs=[blk, blk, blk, blk], out_shape=[sds, sds, sds, sds],
        compiler_params=_cp(("parallel",), 40), name="adamw_small")(recv, w, m, v)


def _rms(t):
    return lax.rsqrt(jnp.mean(t * t, axis=-1, keepdims=True) + EPS)


def _rms_bwd(t, r, gain, dh, dres):
    u = dh * gain
    dt = dres + r * u - t * ((r * r * r) * (1.0 / D_MODEL) * jnp.sum(t * u, axis=-1, keepdims=True))
    return dt, dh * t * r


def _in_proj(x, gain, w):
    L = x.shape[0]
    n = w.shape[1]
    tm, tn = 1024, 1024

    def body(x_ref, g_ref, w_ref, z_ref, h_ref):
        @pl.when(pl.program_id(1) == 0)
        def _():
            t = x_ref[...]
            h_ref[...] = (t * _rms(t) * g_ref[...]).astype(BF16)
        z_ref[...] = _nn(h_ref[...], w_ref[...])

    return pl.pallas_call(
        body, grid=(L // tm, n // tn),
        in_specs=[pl.BlockSpec((tm, D_MODEL), lambda i, j: (i, 0)), pl.BlockSpec((1, D_MODEL), lambda i, j: (0, 0)),
                  pl.BlockSpec((D_MODEL, tn), lambda i, j: (0, j))],
        out_specs=[pl.BlockSpec((tm, tn), lambda i, j: (i, j)), pl.BlockSpec((tm, D_MODEL), lambda i, j: (i, 0))],
        out_shape=[jax.ShapeDtypeStruct((L, n), F32), jax.ShapeDtypeStruct((L, D_MODEL), BF16)],
        compiler_params=_cp(("parallel", "arbitrary"), 40), name="in_proj")(x, gain, w)


PIECE_W = 512


def _piece_columns(pieces):
    cols = []
    for p, arr in enumerate(pieces):
        cols.extend((p, off) for off in range(0, arr.shape[1], PIECE_W))
    return cols


def _in_proj_bwd(pieces, w, x, gain, dres):
    L = x.shape[0]
    n = w.shape[1]
    tm = 512
    npc = len(pieces)
    cols = _piece_columns(pieces)
    per_dot = 4

    def body(*refs):
        dz_refs = refs[:npc]
        w_hbm, x_ref, g_ref, dr_ref, dx_ref, dg_ref, w_scr = refs[npc:]

        @pl.when(pl.program_id(0) == 0)
        def _():
            pltpu.sync_copy(w_hbm, w_scr)
            dg_ref[...] = jnp.zeros_like(dg_ref)

        dh = None
        for c0 in range(0, len(cols), per_dot):
            chunk = cols[c0:c0 + per_dot]
            parts = [dz_refs[p][:, off:off + PIECE_W].astype(BF16) for p, off in chunk]
            term = _nt(jnp.concatenate(parts, axis=1), w_scr[:, c0 * PIECE_W:(c0 + len(chunk)) * PIECE_W])
            dh = term if dh is None else dh + term
        t = x_ref[...]
        dt, dgt = _rms_bwd(t, _rms(t), g_ref[...], dh, dr_ref[...])
        dx_ref[...] = dt
        dg_ref[...] += jnp.sum(dgt, axis=0, keepdims=True)

    row = pl.BlockSpec((tm, D_MODEL), lambda i: (i, 0))
    vec = pl.BlockSpec((1, D_MODEL), lambda i: (0, 0))
    piece_specs = [pl.BlockSpec((tm, arr.shape[1]), lambda i: (i, 0)) for arr in pieces]
    return pl.pallas_call(
        body, grid=(L // tm,),
        in_specs=piece_specs + [pl.BlockSpec(memory_space=pltpu.HBM), row, vec, row],
        out_specs=[row, vec],
        out_shape=[jax.ShapeDtypeStruct((L, D_MODEL), F32), jax.ShapeDtypeStruct((1, D_MODEL), F32)],
        scratch_shapes=[pltpu.VMEM((D_MODEL, n), BF16)],
        compiler_params=_cp(("arbitrary",), 60), name="in_proj_bwd")(*pieces, w, x, gain, dres)


def _tn(a, b, name):
    m, na = a.shape
    nb = b.shape[1]
    ta, tb, tm = min(na, 1024), min(nb, 1024), 2048
    nm = m // tm

    def body(a_ref, b_ref, o_ref):
        @pl.when(pl.program_id(2) == 0)
        def _():
            o_ref[...] = jnp.zeros_like(o_ref)
        o_ref[...] += _tn_dot(a_ref[...].astype(BF16), b_ref[...].astype(BF16))

    return pl.pallas_call(
        body, grid=(na // ta, nb // tb, nm),
        in_specs=[pl.BlockSpec((tm, ta), lambda i, j, k: (k, i)), pl.BlockSpec((tm, tb), lambda i, j, k: (k, j))],
        out_specs=pl.BlockSpec((ta, tb), lambda i, j, k: (i, j)),
        out_shape=jax.ShapeDtypeStruct((na, nb), F32),
        compiler_params=_cp(("parallel", "parallel", "arbitrary"), 48), name=name)(a, b)


def _tn_pieces(a, pieces, name):
    m, na = a.shape
    npc = len(pieces)
    cols = _piece_columns(pieces)
    per_block = D_MODEL // PIECE_W
    nj = len(cols) // per_block
    tm = 1024
    nm = m // tm
    block_of_piece = {}
    for c, (p, _) in enumerate(cols):
        block_of_piece[p] = c // per_block

    def body(*refs):
        a_ref = refs[0]
        b_refs = refs[1:1 + npc]
        o_ref = refs[1 + npc]
        j = pl.program_id(0)

        @pl.when(pl.program_id(1) == 0)
        def _():
            o_ref[...] = jnp.zeros_like(o_ref)

        for jj in range(nj):
            @pl.when(j == jj)
            def _(jj=jj):
                parts = [b_refs[p][:, off:off + PIECE_W].astype(BF16)
                         for p, off in cols[jj * per_block:(jj + 1) * per_block]]
                o_ref[...] += _tn_dot(a_ref[...], jnp.concatenate(parts, axis=1))

    piece_specs = [pl.BlockSpec((tm, arr.shape[1]),
                                functools.partial(lambda j, k, jj: (jnp.where(j == jj, k, 0), 0), jj=block_of_piece[p]))
                   for p, arr in enumerate(pieces)]
    return pl.pallas_call(
        body, grid=(nj, nm),
        in_specs=[pl.BlockSpec((tm, na), lambda j, k: (k, 0))] + piece_specs,
        out_specs=pl.BlockSpec((na, D_MODEL), lambda j, k: (0, j)),
        out_shape=jax.ShapeDtypeStruct((na, D_MODEL * nj), F32),
        compiler_params=_cp(("parallel", "arbitrary"), 56), name=name)(a, *pieces)


def _loss_grad(xf, target):
    L = xf.shape[0]
    tm = 1024

    def body(x_ref, t_ref, dy_ref, l_ref):
        e = x_ref[...] - t_ref[...]
        dy_ref[...] = e * (1.0 / D_MODEL)

        @pl.when(pl.program_id(0) == 0)
        def _():
            l_ref[...] = jnp.zeros_like(l_ref)
        l_ref[...] += jnp.sum(jnp.sum(e * e, axis=1, keepdims=True), axis=0, keepdims=True) * (0.5 / D_MODEL)

    row = pl.BlockSpec((tm, D_MODEL), lambda i: (i, 0))
    return pl.pallas_call(
        body, grid=(L // tm,), in_specs=[row, row],
        out_specs=[row, pl.BlockSpec((1, 1), lambda i: (0, 0))],
        out_shape=[jax.ShapeDtypeStruct((L, D_MODEL), F32), jax.ShapeDtypeStruct((1, 1), F32)],
        compiler_params=_cp(("arbitrary",), 40), name="loss_grad")(xf, target)


def _ffn_fwd(x, gain, wg, wu, wd):
    L = x.shape[0]
    ff = wg.shape[1]
    tm, tf = 1024, 2 * FF_CHUNK
    nf = ff // tf

    def body(x_ref, g_ref, wg_ref, wu_ref, wd_ref, o_ref, h_scr, acc):
        c = pl.program_id(1)

        @pl.when(c == 0)
        def _():
            t = x_ref[...]
            h_scr[...] = (t * _rms(t) * g_ref[...]).astype(BF16)
            acc[...] = jnp.zeros_like(acc)

        h = h_scr[...]
        down = []
        for cols in (slice(0, FF_CHUNK), slice(FF_CHUNK, 2 * FF_CHUNK)):
            gate = _nn(h, wg_ref[:, cols])
            up = _nn(h, wu_ref[:, cols])
            hid = gate * _sigmoid(gate) * up
            down.append(_nn(hid.astype(BF16), wd_ref[cols, :]))
        acc[...] += down[0] + down[1]

        @pl.when(c == nf - 1)
        def _():
            o_ref[...] = x_ref[...] + acc[...]

    row = pl.BlockSpec((tm, D_MODEL), lambda i, c: (i, 0))
    return pl.pallas_call(
        body, grid=(L // tm, nf),
        in_specs=[row, pl.BlockSpec((1, D_MODEL), lambda i, c: (0, 0)),
                  pl.BlockSpec((D_MODEL, tf), lambda i, c: (0, c)), pl.BlockSpec((D_MODEL, tf), lambda i, c: (0, c)),
                  pl.BlockSpec((tf, D_MODEL), lambda i, c: (c, 0))],
        out_specs=row, out_shape=jax.ShapeDtypeStruct((L, D_MODEL), F32),
        scratch_shapes=[pltpu.VMEM((tm, D_MODEL), BF16), pltpu.VMEM((tm, D_MODEL), F32)],
        compiler_params=_cp(("parallel", "arbitrary"), 48), name="ffn_fwd")(x, gain, wg, wu, wd)


def _ffn_bwd(x, gain, wg, wu, wd, dxo):
    L = x.shape[0]
    ff = wg.shape[1]
    tm, tf = 512, 2 * FF_CHUNK
    nf = ff // tf

    def body(x_ref, g_ref, wg_ref, wu_ref, wd_ref, dxo_ref, dx_ref, h_ref, hid_ref, dgate_ref, dup_ref, dg_ref,
             acc, dxo_b):
        i, c = pl.program_id(0), pl.program_id(1)

        @pl.when(c == 0)
        def _():
            t = x_ref[...]
            h_ref[...] = (t * _rms(t) * g_ref[...]).astype(BF16)
            acc[...] = jnp.zeros_like(acc)
            dxo_b[...] = dxo_ref[...].astype(BF16)

        h = h_ref[...]
        back = []
        for cols in (slice(0, FF_CHUNK), slice(FF_CHUNK, 2 * FF_CHUNK)):
            gate = _nn(h, wg_ref[:, cols])
            up = _nn(h, wu_ref[:, cols])
            sg = _sigmoid(gate)
            silu = gate * sg
            hid_ref[:, cols] = (silu * up).astype(BF16)
            dhid = _nt(dxo_b[...], wd_ref[cols, :])
            dup = (dhid * silu).astype(BF16)
            dgate = (dhid * up * (sg * (1.0 + gate * (1.0 - sg)))).astype(BF16)
            dup_ref[:, cols] = dup
            dgate_ref[:, cols] = dgate
            back.append(_nt(dgate, wg_ref[:, cols]) + _nt(dup, wu_ref[:, cols]))
        acc[...] += back[0] + back[1]

        @pl.when(c == nf - 1)
        def _():
            t = x_ref[...]
            dt, dgt = _rms_bwd(t, _rms(t), g_ref[...], acc[...], dxo_ref[...])
            dx_ref[...] = dt

            @pl.when(i == 0)
            def _():
                dg_ref[...] = jnp.zeros_like(dg_ref)
            dg_ref[...] += jnp.sum(dgt, axis=0, keepdims=True)

    row = pl.BlockSpec((tm, D_MODEL), lambda i, c: (i, 0))
    vec = pl.BlockSpec((1, D_MODEL), lambda i, c: (0, 0))
    wcol = pl.BlockSpec((D_MODEL, tf), lambda i, c: (0, c))
    hcol = pl.BlockSpec((tm, tf), lambda i, c: (i, c))
    return pl.pallas_call(
        body, grid=(L // tm, nf),
        in_specs=[row, vec, wcol, wcol, pl.BlockSpec((tf, D_MODEL), lambda i, c: (c, 0)), row],
        out_specs=[row, row, hcol, hcol, hcol, vec],
        out_shape=[jax.ShapeDtypeStruct((L, D_MODEL), F32), jax.ShapeDtypeStruct((L, D_MODEL), BF16),
                   jax.ShapeDtypeStruct((L, ff), BF16), jax.ShapeDtypeStruct((L, ff), BF16),
                   jax.ShapeDtypeStruct((L, ff), BF16), jax.ShapeDtypeStruct((1, D_MODEL), F32)],
        scratch_shapes=[pltpu.VMEM((tm, D_MODEL), F32), pltpu.VMEM((tm, D_MODEL), BF16)],
        compiler_params=_cp(("arbitrary", "arbitrary"), 56), name="ffn_bwd")(x, gain, wg, wu, wd, dxo)


GELU_K = math.sqrt(2.0 / math.pi)
GELU_C = 0.044715


def _gelu(y):
    return 0.5 * y * (1.0 + jnp.tanh(GELU_K * (y + GELU_C * (y * y * y))))


def _gelu_grad(y):
    th = jnp.tanh(GELU_K * (y + GELU_C * (y * y * y)))
    return 0.5 * (1.0 + th) + 0.5 * y * (1.0 - th * th) * (GELU_K * (1.0 + 3.0 * GELU_C * (y * y)))


def _merge_groups(o_refs, l_refs):
    ls = [r[...] for r in l_refs]
    os_ = [r[...] for r in o_refs]
    lmax = jnp.maximum(jnp.maximum(ls[0], ls[1]), ls[2])
    es = [jnp.exp(l - lmax) for l in ls]
    inv = 1.0 / (es[0] + es[1] + es[2])
    ws = [e * inv for e in es]
    a = ws[0] * os_[0] + ws[1] * os_[1] + ws[2] * os_[2]
    return ws, os_, a


def _mix_fwd(ols, y, z, x, wp, wa, wb, wo):
    L = x.shape[0]
    tm = 256

    def body(o0, l0, o1, l1, o2, l2, y_ref, ga_ref, gs_ref, x_ref, wp_ref, wa_ref, wb_ref, wo_ref, out_ref):
        _, _, a = _merge_groups((o0, o1, o2), (l0, l1, l2))
        a_out = _nn(a.astype(BF16), wp_ref[...])
        yg = _gelu(y_ref[...]).astype(BF16)
        s_out = _nn(yg, wa_ref[...]) * _sigmoid(_nn(yg, wb_ref[...]))
        mix = _sigmoid(ga_ref[...]) * a_out + _sigmoid(gs_ref[...]) * s_out
        out_ref[...] = x_ref[...] + _nn(mix.astype(BF16), wo_ref[...])

    half = pl.BlockSpec((tm, ATTN_W), lambda i: (i, 0))
    row = pl.BlockSpec((tm, D_MODEL), lambda i: (i, 0))
    w512 = pl.BlockSpec((ATTN_W, D_MODEL), lambda i: (0, 0))
    return pl.pallas_call(
        body, grid=(L // tm,),
        in_specs=[half] * 7 + [pl.BlockSpec((tm, D_MODEL), lambda i: (i, 5)),
                               pl.BlockSpec((tm, D_MODEL), lambda i: (i, 6)), row, w512, w512, w512,
                               pl.BlockSpec((D_MODEL, D_MODEL), lambda i: (0, 0))],
        out_specs=row, out_shape=jax.ShapeDtypeStruct((L, D_MODEL), F32),
        compiler_params=_cp(("parallel",), 48), name="mix_fwd")(*ols, y, z, z, x, wp, wa, wb, wo)


def _mix_bwd(dxm, ols, y, z, wp, wa, wb, wo):
    L = dxm.shape[0]
    tm = 256

    def body(dx_ref, o0, l0, o1, l1, o2, l2, y_ref, ga_ref, gs_ref, wp_ref, wa_ref, wb_ref, wo_ref,
             do0, dl0, do1, dl1, do2, dl2, dy_ref, dga_ref, dgs_ref, a_ref, yg_ref, mix_ref, dao_ref, dpa_ref,
             dpb_ref):
        ws, os_, a = _merge_groups((o0, o1, o2), (l0, l1, l2))
        ab = a.astype(BF16)
        a_out = _nn(ab, wp_ref[...])
        yv = y_ref[...]
        yg = _gelu(yv).astype(BF16)
        pa = _nn(yg, wa_ref[...])
        spb = _sigmoid(_nn(yg, wb_ref[...]))
        s_out = pa * spb
        sga = _sigmoid(ga_ref[...])
        sgs = _sigmoid(gs_ref[...])
        mix = sga * a_out + sgs * s_out
        dmix = _nt(dx_ref[...].astype(BF16), wo_ref[...])
        da_out = (sga * dmix).astype(BF16)
        ds_out = sgs * dmix
        dpa = (ds_out * spb).astype(BF16)
        dpb = (ds_out * pa * spb * (1.0 - spb)).astype(BF16)
        dga_ref[...] = (dmix * a_out * sga * (1.0 - sga)).astype(BF16)
        dgs_ref[...] = (dmix * s_out * sgs * (1.0 - sgs)).astype(BF16)
        dy_ref[...] = (_nt(dpa, wa_ref[...]) + _nt(dpb, wb_ref[...])) * _gelu_grad(yv)
        da = _nt(da_out, wp_ref[...])
        for w, o, do_ref, dl_ref in zip(ws, os_, (do0, do1, do2), (dl0, dl1, dl2)):
            do_ref[...] = w * da
            dl_ref[...] = -(w * da) * a
        a_ref[...] = ab
        yg_ref[...] = yg
        mix_ref[...] = mix.astype(BF16)
        dao_ref[...] = da_out
        dpa_ref[...] = dpa
        dpb_ref[...] = dpb

    half = pl.BlockSpec((tm, ATTN_W), lambda i: (i, 0))
    row = pl.BlockSpec((tm, D_MODEL), lambda i: (i, 0))
    w512 = pl.BlockSpec((ATTN_W, D_MODEL), lambda i: (0, 0))
    hf = jax.ShapeDtypeStruct((L, ATTN_W), F32)
    hb = jax.ShapeDtypeStruct((L, ATTN_W), BF16)
    rb = jax.ShapeDtypeStruct((L, D_MODEL), BF16)
    return pl.pallas_call(
        body, grid=(L // tm,),
        in_specs=[row] + [half] * 7 + [pl.BlockSpec((tm, D_MODEL), lambda i: (i, 5)),
                                       pl.BlockSpec((tm, D_MODEL), lambda i: (i, 6)), w512, w512, w512,
                                       pl.BlockSpec((D_MODEL, D_MODEL), lambda i: (0, 0))],
        out_specs=[half] * 7 + [row, row, half, half, row, row, row, row],
        out_shape=[hf] * 7 + [rb, rb, hb, hb, rb, rb, rb, rb],
        compiler_params=_cp(("parallel",), 56), name="mix_bwd")(dxm, *ols, y, z, z, wp, wa, wb, wo)


N_ATTN_ITERS = ATTN_ROWS // BLK


def _class_rows(ref, start, d):
    if d == 1:
        return ref[pl.ds(pl.multiple_of(start, BLK), BLK), :]
    return ref[pl.ds(start, BLK, stride=d), :]


def _set_class_rows(ref, start, d, val):
    if d == 1:
        ref[pl.ds(pl.multiple_of(start, BLK), BLK), :] = val
    else:
        ref[pl.ds(start, BLK, stride=d), :] = val


def _head_masks():
    lane = lax.broadcasted_iota(jnp.int32, (1, LANES), 1)
    m0 = (lane < HEAD_DIM).astype(F32)
    return m0, 1.0 - m0


def _head_norm(t, gain2, m0, m1):
    tt = t * t
    r0 = lax.rsqrt(jnp.sum(tt * m0, axis=-1, keepdims=True) * (1.0 / HEAD_DIM) + EPS)
    r1 = lax.rsqrt(jnp.sum(tt * m1, axis=-1, keepdims=True) * (1.0 / HEAD_DIM) + EPS)
    r = m0 * r0 + m1 * r1
    return t * r * gain2, r


def _head_norm_bwd(t, r, gain2, dy, m0, m1):
    u = dy * gain2
    tu = t * u
    s = m0 * jnp.sum(tu * m0, axis=-1, keepdims=True) + m1 * jnp.sum(tu * m1, axis=-1, keepdims=True)
    return r * u - t * (r * r * r) * s * (1.0 / HEAD_DIM), jnp.sum(dy * t * r, axis=0, keepdims=True)


def _band_masks():
    qi = lax.broadcasted_iota(jnp.int32, (BLK, 2 * BLK), 0)
    ki = lax.broadcasted_iota(jnp.int32, (BLK, 2 * BLK), 1)
    dist = BLK + qi - ki
    return (dist >= 0) & (dist <= BLK), ki >= BLK


ATTN_SCALE = HEAD_DIM ** -0.5


def _attn_scores(qm, kw, ok):
    return jnp.where(ok, _nt(qm, kw), -1e30)


def _attn_probs(qm, kw, ok):
    s = _attn_scores(qm, kw, ok)
    mx = jnp.max(s, axis=-1, keepdims=True)
    p = jnp.exp(s - mx)
    den = jnp.sum(p, axis=-1, keepdims=True)
    return p, den, mx


NORM_ROWS = 256


def _norm_rows(src_ref, gain2, dst_ref, m0, m1):
    n = src_ref.shape[0]
    step = min(NORM_ROWS, n)
    for r0 in range(0, n, step):
        dst_ref[r0:r0 + step, :] = _head_norm(src_ref[r0:r0 + step, :], gain2, m0, m1)[0]


def _norm_rows_bwd(src_ref, gain2, dy_ref, dst_ref, m0, m1):
    n = src_ref.shape[0]
    step = min(NORM_ROWS, n)
    dgain = jnp.zeros((1, LANES), F32)
    for r0 in range(0, n, step):
        t = src_ref[r0:r0 + step, :]
        _, r = _head_norm(t, gain2, m0, m1)
        dt, dg = _head_norm_bwd(t, r, gain2, dy_ref[r0:r0 + step, :], m0, m1)
        dst_ref[r0:r0 + step, :] = dt.astype(dst_ref.dtype)
        dgain = dgain + dg
    return dgain


def _attn_operands(it, d, first_step, q_ref, kc_ref, kp_ref, vc_ref, vp_ref, band, is_cur):
    j = it // d
    start = (it - j * d) + (d * BLK) * j
    before = jnp.maximum(start - d * BLK, 0)
    inside = j > 0
    q2 = _class_rows(q_ref, start, d)
    kc2 = _class_rows(kc_ref, start, d)
    vc2 = _class_rows(vc_ref, start, d)
    kp2 = jnp.where(inside, _class_rows(kc_ref, before, d), _class_rows(kp_ref, it - j * d, d))
    vp2 = jnp.where(inside, _class_rows(vc_ref, before, d), _class_rows(vp_ref, it - j * d, d))
    has_prev = inside | jnp.logical_not(first_step)
    return start, q2, kp2, kc2, vp2, vc2, band & (is_cur | has_prev)


def _attn_specs(d, step_of):
    nq = N_ATTN_ITERS // d

    def cur(c):
        return pl.BlockSpec((ATTN_ROWS, LANES), lambda hp, n: (step_of(n), c + hp))

    def prev(c):
        return pl.BlockSpec((d * BLK, LANES), lambda hp, n: (jnp.maximum(step_of(n) * nq - 1, 0), c + hp))

    return cur, prev, pl.BlockSpec((1, LANES), lambda hp, n: (0, 0))


def _attn_fwd(z, gq2, gk2, group):
    L = z.shape[0]
    d = DILATIONS[group]
    nsb = L // ATTN_ROWS
    cq, ck, cv = group * 4, 12 + group * 4, 24 + group * 4

    def body(q_ref, kc_ref, kp_ref, vc_ref, vp_ref, gq_ref, gk_ref, o_ref, l_ref, qn_scr, kn_scr, kpn_scr):
        first_step = pl.program_id(1) == 0
        band, is_cur = _band_masks()
        m0, m1 = _head_masks()
        _norm_rows(q_ref, gq_ref[...], qn_scr, m0, m1)
        if d * BLK == ATTN_ROWS:
            @pl.when(first_step)
            def _():
                _norm_rows(kp_ref, gk_ref[...], kpn_scr, m0, m1)

            @pl.when(jnp.logical_not(first_step))
            def _():
                kpn_scr[...] = kn_scr[...]
        else:
            _norm_rows(kp_ref, gk_ref[...], kpn_scr, m0, m1)
        _norm_rows(kc_ref, gk_ref[...], kn_scr, m0, m1)

        def per_block(it, carry):
            start, qn, kpn, kcn, vp2, vc2, ok = _attn_operands(
                it, d, first_step, qn_scr, kn_scr, kpn_scr, vc_ref, vp_ref, band, is_cur)
            kw = jnp.concatenate([kpn, kcn], axis=0).astype(BF16)
            vw = jnp.concatenate([vp2, vc2], axis=0).astype(BF16)
            o2 = jnp.zeros((BLK, LANES), F32)
            l2 = jnp.zeros((BLK, LANES), F32)
            for mh in (m0, m1):
                p, den, mx = _attn_probs((qn * (mh * ATTN_SCALE)).astype(BF16), kw, ok)
                o2 = o2 + mh * (_nn(p.astype(BF16), vw) / den)
                l2 = l2 + mh * (mx + jnp.log(den))
            _set_class_rows(o_ref, start, d, o2)
            _set_class_rows(l_ref, start, d, l2)
            return carry

        lax.fori_loop(0, N_ATTN_ITERS, per_block, 0, unroll=2)

    cur, prev, vec = _attn_specs(d, lambda n: n)
    out = pl.BlockSpec((ATTN_ROWS, LANES), lambda hp, n: (n, hp))
    sds = jax.ShapeDtypeStruct((L, ATTN_W), F32)
    return pl.pallas_call(
        body, grid=(4, nsb),
        in_specs=[cur(cq), cur(ck), prev(ck), cur(cv), prev(cv), vec, vec],
        out_specs=[out, out], out_shape=[sds, sds],
        scratch_shapes=[pltpu.VMEM((ATTN_ROWS, LANES), F32), pltpu.VMEM((ATTN_ROWS, LANES), F32),
                        pltpu.VMEM((d * BLK, LANES), F32)],
        compiler_params=_cp(("parallel", "arbitrary"), 48), name=f"attn_fwd_g{group}")(z, z, z, z, z, gq2, gk2)


def _attn_bwd(z, gq2, gk2, lse, do, c, group):
    L = z.shape[0]
    d = DILATIONS[group]
    nsb = L // ATTN_ROWS
    cq, ck, cv = group * 4, 12 + group * 4, 24 + group * 4

    def body(q_ref, kc_ref, kp_ref, vc_ref, vp_ref, gq_ref, gk_ref, l_ref, do_ref, c_ref,
             dq_ref, dk_ref, dv_ref, dgq_ref, dgk_ref, ck_scr, cv_scr, qn_scr, kn_scr, kpn_scr, dqn_scr, dkn_scr):
        hp, n = pl.program_id(0), pl.program_id(1)
        first_step = n == nsb - 1
        band, is_cur = _band_masks()
        m0, m1 = _head_masks()
        gq, gk = gq_ref[...], gk_ref[...]
        _norm_rows(q_ref, gq, qn_scr, m0, m1)
        _norm_rows(kc_ref, gk, kn_scr, m0, m1)
        _norm_rows(kp_ref, gk, kpn_scr, m0, m1)

        @pl.when((hp == 0) & (n == 0))
        def _():
            dgq_ref[...] = jnp.zeros_like(dgq_ref)
            dgk_ref[...] = jnp.zeros_like(dgk_ref)

        @pl.when(n == 0)
        def _():
            ck_scr[...] = jnp.zeros_like(ck_scr)
            cv_scr[...] = jnp.zeros_like(cv_scr)

        def per_block(i, carry):
            it = N_ATTN_ITERS - 1 - i
            start, qn, kpn, kcn, vp2, vc2, ok = _attn_operands(
                it, d, first_step, qn_scr, kn_scr, kpn_scr, vc_ref, vp_ref, band, is_cur)
            r = it - (it // d) * d
            kw = jnp.concatenate([kpn, kcn], axis=0).astype(BF16)
            vw = jnp.concatenate([vp2, vc2], axis=0).astype(BF16)
            l2 = _class_rows(l_ref, start, d)
            c2 = _class_rows(c_ref, start, d)
            do2 = _class_rows(do_ref, start, d)
            dqn = jnp.zeros((BLK, LANES), F32)
            dkw = jnp.zeros((2 * BLK, LANES), F32)
            dvw = jnp.zeros((2 * BLK, LANES), F32)
            for mh in (m0, m1):
                qm = (qn * (mh * ATTN_SCALE)).astype(BF16)
                lse = jnp.max(jnp.where(mh > 0.5, l2, -3e38), axis=-1, keepdims=True)
                pn = jnp.exp(_attn_scores(qm, kw, ok) - lse)
                dohb = (do2 * mh).astype(BF16)
                dvw = dvw + _tn_dot(pn.astype(BF16), dohb)
                ds = (pn * (_nt(dohb, vw) + jnp.sum(c2 * mh, axis=-1, keepdims=True))).astype(BF16)
                dqn = dqn + (mh * ATTN_SCALE) * _nn(ds, kw)
                dkw = dkw + _tn_dot(ds, qm)
            _set_class_rows(dqn_scr, start, d, dqn)
            _set_class_rows(dkn_scr, start, d, ck_scr[r] + dkw[BLK:])
            _set_class_rows(dv_ref, start, d, cv_scr[r] + dvw[BLK:])
            ck_scr[r] = dkw[:BLK]
            cv_scr[r] = dvw[:BLK]
            return carry

        lax.fori_loop(0, N_ATTN_ITERS, per_block, 0, unroll=2)
        dgq_ref[...] += _norm_rows_bwd(q_ref, gq, dqn_scr, dq_ref, m0, m1)
        dgk_ref[...] += _norm_rows_bwd(kc_ref, gk, dkn_scr, dk_ref, m0, m1)

    cur, prev, vec = _attn_specs(d, lambda n: nsb - 1 - n)
    sds = jax.ShapeDtypeStruct((L, ATTN_W), F32)
    sdb = jax.ShapeDtypeStruct((L, ATTN_W), BF16)
    vsd = jax.ShapeDtypeStruct((1, LANES), F32)
    return pl.pallas_call(
        body, grid=(4, nsb),
        in_specs=[cur(cq), cur(ck), prev(ck), cur(cv), prev(cv), vec, vec, cur(0), cur(0), cur(0)],
        out_specs=[cur(0), cur(0), cur(0), vec, vec], out_shape=[sdb, sdb, sds, vsd, vsd],
        scratch_shapes=[pltpu.VMEM((d, BLK, LANES), F32), pltpu.VMEM((d, BLK, LANES), F32),
                        pltpu.VMEM((ATTN_ROWS, LANES), F32), pltpu.VMEM((ATTN_ROWS, LANES), F32),
                        pltpu.VMEM((d * BLK, LANES), F32),
                        pltpu.VMEM((ATTN_ROWS, LANES), F32), pltpu.VMEM((ATTN_ROWS, LANES), F32)],
        compiler_params=_cp(("arbitrary", "arbitrary"), 56),
        name=f"attn_bwd_g{group}")(z, z, z, z, z, gq2, gk2, lse, do, c)


BLOCK_STATES = SSM_STATES // SSM_BLOCKS
BLOCK_CH = SSM_W // SSM_BLOCKS
SLABS_PER_BLOCK = BLOCK_STATES // LANES


SCAN_STEPS = 8


def _store_block(bufs, b, val, tm):
    for s in range(SLABS_PER_BLOCK):
        k = SLABS_PER_BLOCK * b + s
        bufs[k % 2][pl.ds(8 + k // 2, tm, stride=8), :] = val[:, s * LANES:(s + 1) * LANES]


def _load_block(bufs, b, tm):
    tiles = []
    for s in range(SLABS_PER_BLOCK):
        k = SLABS_PER_BLOCK * b + s
        tiles.append(bufs[k % 2][pl.ds(8 + k // 2, tm, stride=8), :])
    return jnp.concatenate(tiles, axis=1).astype(BF16)


def _ssm_project_in(ub, bdr_ref, bdi_ref, sr, si, tm):
    for b in range(SSM_BLOCKS):
        ubb = ub[:, b * BLOCK_CH:(b + 1) * BLOCK_CH]
        _store_block(sr, b, _nn(ubb, bdr_ref[b]), tm)
        _store_block(si, b, _nn(ubb, bdi_ref[b]), tm)


def _ssm_scan(a, x0, sr, si, tm):
    ar0, ar1, ai0, ai1 = a
    sr[0][0:8, :], sr[1][0:8, :], si[0][0:8, :], si[1][0:8, :] = x0

    def steps(it, c):
        xr0, xr1, xi0, xi1 = c
        base = it * (8 * SCAN_STEPS) + 8
        for q in range(SCAN_STEPS):
            rows = pl.ds(pl.multiple_of(base + 8 * q, 8), 8)
            nr0 = ar0 * xr0 - ai0 * xi0 + sr[0][rows, :]
            ni0 = ar0 * xi0 + ai0 * xr0 + si[0][rows, :]
            nr1 = ar1 * xr1 - ai1 * xi1 + sr[1][rows, :]
            ni1 = ar1 * xi1 + ai1 * xr1 + si[1][rows, :]
            sr[0][rows, :] = nr0
            si[0][rows, :] = ni0
            sr[1][rows, :] = nr1
            si[1][rows, :] = ni1
            xr0, xr1, xi0, xi1 = nr0, nr1, ni0, ni1
        return xr0, xr1, xi0, xi1

    return lax.fori_loop(0, tm // SCAN_STEPS, steps, x0)


def _load_a(ar_ref, ai_ref):
    return ar_ref[:, :LANES], ar_ref[:, LANES:], ai_ref[:, :LANES], ai_ref[:, LANES:]


def _ssm_fwd(z, ar8, ai8, bdr, bdi, cdr, cdi, dsk):
    L = z.shape[0]
    tm = SSM_TM
    nc = L // tm

    def body(u_ref, ar_ref, ai_ref, bdr_ref, bdi_ref, cdr_ref, cdi_ref, dsk_ref, y_ref, cin_ref,
             sr0, sr1, si0, si1, car):
        sr, si = (sr0, sr1), (si0, si1)

        @pl.when(pl.program_id(0) == 0)
        def _():
            car[...] = jnp.zeros_like(car)

        u = u_ref[...]
        _ssm_project_in(u.astype(BF16), bdr_ref, bdi_ref, sr, si, tm)
        cin_ref[0] = car[...]
        xr0, xr1, xi0, xi1 = _ssm_scan(_load_a(ar_ref, ai_ref), (car[0], car[1], car[2], car[3]), sr, si, tm)
        car[0], car[1], car[2], car[3] = xr0, xr1, xi0, xi1
        for b in range(SSM_BLOCKS):
            cols = slice(b * BLOCK_CH, (b + 1) * BLOCK_CH)
            y_ref[:, cols] = (dsk_ref[:, cols] * u[:, cols] + _nn(_load_block(sr, b, tm), cdr_ref[b])
                              - _nn(_load_block(si, b, tm), cdi_ref[b]))

    def const(shape):
        return pl.BlockSpec(shape, lambda i: (0,) * len(shape))

    state = pltpu.VMEM(((tm + 1) * 8, LANES), F32)
    wb = const((SSM_BLOCKS, BLOCK_CH, BLOCK_STATES))
    wc = const((SSM_BLOCKS, BLOCK_STATES, BLOCK_CH))
    return pl.pallas_call(
        body, grid=(nc,),
        in_specs=[pl.BlockSpec((tm, SSM_W), lambda i: (i, COL_U)), const((8, 256)), const((8, 256)),
                  wb, wb, wc, wc, const((1, SSM_W))],
        out_specs=[pl.BlockSpec((tm, SSM_W), lambda i: (i, 0)), pl.BlockSpec((1, 4, 8, LANES), lambda i: (i, 0, 0, 0))],
        out_shape=[jax.ShapeDtypeStruct((L, SSM_W), F32), jax.ShapeDtypeStruct((nc, 4, 8, LANES), F32)],
        scratch_shapes=[state, state, state, state, pltpu.VMEM((4, 8, LANES), F32)],
        compiler_params=_cp(("arbitrary",), 48), name="ssm_fwd")(z, ar8, ai8, bdr, bdi, cdr, cdi, dsk)


def _ssm_bwd(z, dy, cin, ar8, ai8, bdr, bdi, cdr, cdi, dsk):
    L = z.shape[0]
    tm = SSM_TM
    nc = L // tm

    def body(u_ref, dy_ref, cin_ref, ar_ref, ai_ref, dsk_ref, bdr_ref, bdi_ref, cdr_ref, cdi_ref,
             du_ref, da_ref, dds_ref, dbdr_ref, dbdi_ref, dcdr_ref, dcdi_ref,
             sr0, sr1, si0, si1, gr0, gr1, gi0, gi1, carg):
        sr, si, gr, gi = (sr0, sr1), (si0, si1), (gr0, gr1), (gi0, gi1)

        @pl.when(pl.program_id(0) == 0)
        def _():
            carg[...] = jnp.zeros_like(carg)
            for ref in (da_ref, dds_ref, dbdr_ref, dbdi_ref, dcdr_ref, dcdi_ref):
                ref[...] = jnp.zeros_like(ref)

        u = u_ref[...]
        ub = u.astype(BF16)
        dyv = dy_ref[...]
        dyb = dyv.astype(BF16)
        a = _load_a(ar_ref, ai_ref)
        ar0, ar1, ai0, ai1 = a
        x_in = (cin_ref[0, 0], cin_ref[0, 1], cin_ref[0, 2], cin_ref[0, 3])
        _ssm_project_in(ub, bdr_ref, bdi_ref, sr, si, tm)
        _ssm_scan(a, x_in, sr, si, tm)
        for b in range(SSM_BLOCKS):
            dyb_b = dyb[:, b * BLOCK_CH:(b + 1) * BLOCK_CH]
            _store_block(gr, b, _nt(dyb_b, cdr_ref[b]), tm)
            _store_block(gi, b, -_nt(dyb_b, cdi_ref[b]), tm)

        def grad_steps(it, c):
            (nr0, nr1, ni0, ni1), (d_r0, d_r1, d_i0, d_i1) = c
            base = (tm - SCAN_STEPS * (it + 1)) * 8
            for q in reversed(range(SCAN_STEPS)):
                prev = pl.ds(pl.multiple_of(base + 8 * q, 8), 8)
                rows = pl.ds(pl.multiple_of(base + 8 * q + 8, 8), 8)
                g_r0 = gr[0][rows, :] + ar0 * nr0 + ai0 * ni0
                g_i0 = gi[0][rows, :] + ar0 * ni0 - ai0 * nr0
                g_r1 = gr[1][rows, :] + ar1 * nr1 + ai1 * ni1
                g_i1 = gi[1][rows, :] + ar1 * ni1 - ai1 * nr1
                gr[0][rows, :] = g_r0
                gi[0][rows, :] = g_i0
                gr[1][rows, :] = g_r1
                gi[1][rows, :] = g_i1
                pr0, pr1, pi0, pi1 = sr[0][prev, :], sr[1][prev, :], si[0][prev, :], si[1][prev, :]
                d_r0 = d_r0 + pr0 * g_r0 + pi0 * g_i0
                d_r1 = d_r1 + pr1 * g_r1 + pi1 * g_i1
                d_i0 = d_i0 + pr0 * g_i0 - pi0 * g_r0
                d_i1 = d_i1 + pr1 * g_i1 - pi1 * g_r1
                nr0, nr1, ni0, ni1 = g_r0, g_r1, g_i0, g_i1
            return (nr0, nr1, ni0, ni1), (d_r0, d_r1, d_i0, d_i1)

        acc0 = (da_ref[0], da_ref[1], da_ref[2], da_ref[3])
        g_first, acc = lax.fori_loop(0, tm // SCAN_STEPS, grad_steps,
                                     ((carg[0], carg[1], carg[2], carg[3]), acc0))
        carg[0], carg[1], carg[2], carg[3] = g_first
        da_ref[0], da_ref[1], da_ref[2], da_ref[3] = acc

        for b in range(SSM_BLOCKS):
            cols = slice(b * BLOCK_CH, (b + 1) * BLOCK_CH)
            grb, gib = _load_block(gr, b, tm), _load_block(gi, b, tm)
            du_ref[:, cols] = (dsk_ref[:, cols] * dyv[:, cols] + _nt(grb, bdr_ref[b])
                               + _nt(gib, bdi_ref[b])).astype(BF16)
            dbdr_ref[b] += _tn_dot(ub[:, cols], grb)
            dbdi_ref[b] += _tn_dot(ub[:, cols], gib)
            dcdr_ref[b] += _tn_dot(_load_block(sr, b, tm), dyb[:, cols])
            dcdi_ref[b] -= _tn_dot(_load_block(si, b, tm), dyb[:, cols])
        dds_ref[...] += jnp.sum(dyv * u, axis=0, keepdims=True)

    def const(shape):
        return pl.BlockSpec(shape, lambda i: (0,) * len(shape))

    state = pltpu.VMEM(((tm + 1) * 8, LANES), F32)
    wb = const((SSM_BLOCKS, BLOCK_CH, BLOCK_STATES))
    wc = const((SSM_BLOCKS, BLOCK_STATES, BLOCK_CH))
    return pl.pallas_call(
        body, grid=(nc,),
        in_specs=[pl.BlockSpec((tm, SSM_W), lambda i: (nc - 1 - i, COL_U)),
                  pl.BlockSpec((tm, SSM_W), lambda i: (nc - 1 - i, 0)),
                  pl.BlockSpec((1, 4, 8, LANES), lambda i: (nc - 1 - i, 0, 0, 0)),
                  const((8, 256)), const((8, 256)), const((1, SSM_W)), wb, wb, wc, wc],
        out_specs=[pl.BlockSpec((tm, SSM_W), lambda i: (nc - 1 - i, 0)), const((4, 8, LANES)), const((1, SSM_W)),
                   wb, wb, wc, wc],
        out_shape=[jax.ShapeDtypeStruct((L, SSM_W), BF16), jax.ShapeDtypeStruct((4, 8, LANES), F32),
                   jax.ShapeDtypeStruct((1, SSM_W), F32),
                   jax.ShapeDtypeStruct((SSM_BLOCKS, BLOCK_CH, BLOCK_STATES), F32),
                   jax.ShapeDtypeStruct((SSM_BLOCKS, BLOCK_CH, BLOCK_STATES), F32),
                   jax.ShapeDtypeStruct((SSM_BLOCKS, BLOCK_STATES, BLOCK_CH), F32),
                   jax.ShapeDtypeStruct((SSM_BLOCKS, BLOCK_STATES, BLOCK_CH), F32)],
        scratch_shapes=[state] * 8 + [pltpu.VMEM((4, 8, LANES), F32)],
        compiler_params=_cp(("arbitrary",), 56), name="ssm_bwd")(z, dy, cin, ar8, ai8, dsk, bdr, bdi, cdr, cdi)


def _discretise(lam_re, lam_im, log_dt, b_re, b_im):
    dt = jnp.exp(log_dt)[:, None]
    mag = jnp.exp(lam_re * dt)
    ang = lam_im * dt
    abar_re = mag * jnp.cos(ang)
    abar_im = mag * jnp.sin(ang)
    nr = abar_re - 1.0
    ni = abar_im
    den = lam_re * lam_re + lam_im * lam_im
    cr = ((nr * lam_re + ni * lam_im) / den)[..., None]
    ci = ((ni * lam_re - nr * lam_im) / den)[..., None]
    return abar_re, abar_im, cr * b_re - ci * b_im, cr * b_im + ci * b_re


GROUPS_PER_BLOCK = 8


def _block_diag_in(bbar):
    eye = jnp.eye(GROUPS_PER_BLOCK, dtype=F32)
    return jnp.einsum("igpc,gh->igchp", bbar.reshape(SSM_BLOCKS, GROUPS_PER_BLOCK, 64, 16), eye).reshape(
        SSM_BLOCKS, BLOCK_CH, BLOCK_STATES)


def _block_diag_in_t(blocks):
    eye = jnp.eye(GROUPS_PER_BLOCK, dtype=F32)
    return jnp.einsum("igchp,gh->igpc", blocks.reshape(SSM_BLOCKS, GROUPS_PER_BLOCK, 16, GROUPS_PER_BLOCK, 64),
                      eye).reshape(32, 64, 16)


def _block_diag_out(c):
    eye = jnp.eye(GROUPS_PER_BLOCK, dtype=F32)
    return jnp.einsum("igcp,gh->igphc", c.reshape(SSM_BLOCKS, GROUPS_PER_BLOCK, 16, 64), eye).reshape(
        SSM_BLOCKS, BLOCK_STATES, BLOCK_CH)


def _block_diag_out_t(blocks):
    eye = jnp.eye(GROUPS_PER_BLOCK, dtype=F32)
    return jnp.einsum("igphc,gh->igcp", blocks.reshape(SSM_BLOCKS, GROUPS_PER_BLOCK, 64, GROUPS_PER_BLOCK, 16),
                      eye).reshape(32, 16, 64)


SMALL_NAMES = ("g_mix", "g_q", "g_k", "lambda_re", "lambda_im", "log_dt", "b_re", "b_im", "c_re", "c_im",
               "d_skip", "g_ffn")


def _pack_small(parts):
    flat = jnp.concatenate([parts[n].reshape(-1) for n in SMALL_NAMES])
    pad = (-flat.shape[0]) % (8 * LANES * SMALL_TILES)
    return jnp.pad(flat, (0, pad)).reshape(-1, LANES)


def _unpack_small(packed, like):
    flat = packed.reshape(-1)
    out, off = {}, 0
    for n in SMALL_NAMES:
        size = like[n].size
        out[n] = flat[off:off + size].reshape(like[n].shape)
        off += size
    return out


BIG_NAMES = ("w_in", "w_attn_proj", "w_glu_a", "w_glu_b", "w_out", "w_ffn_gate", "w_ffn_up", "w_ffn_down")
BIG_SHARD_AXIS = {"w_in": 2, "w_attn_proj": 2, "w_glu_a": 2, "w_glu_b": 2, "w_out": 1,
                  "w_ffn_gate": 2, "w_ffn_up": 2, "w_ffn_down": 1}
ADAMW_ROWS = {"w_in": 256, "w_attn_proj": 512, "w_glu_a": 512, "w_glu_b": 512, "w_out": 128,
              "w_ffn_gate": 256, "w_ffn_up": 256, "w_ffn_down": 176}


def kernel(x, g_mix, w_in, g_q, g_k, w_attn_proj, lambda_re, lambda_im, log_dt, b_re, b_im, c_re, c_im, d_skip, w_glu_a, w_glu_b, w_out, g_ffn, w_ffn_gate, w_ffn_up, w_ffn_down, loss_target, m_g_mix, m_w_in, m_g_q, m_g_k, m_w_attn_proj, m_lambda_re, m_lambda_im, m_log_dt, m_b_re, m_b_im, m_c_re, m_c_im, m_d_skip, m_w_glu_a, m_w_glu_b, m_w_out, m_g_ffn, m_w_ffn_gate, m_w_ffn_up, m_w_ffn_down, v_g_mix, v_w_in, v_g_q, v_g_k, v_w_attn_proj, v_lambda_re, v_lambda_im, v_log_dt, v_b_re, v_b_im, v_c_re, v_c_im, v_d_skip, v_w_glu_a, v_w_glu_b, v_w_out, v_g_ffn, v_w_ffn_gate, v_w_ffn_up, v_w_ffn_down):
    args = dict(locals())
    weights = {n: args[n] for n in BIG_NAMES + SMALL_NAMES}
    moments_m = {n: args["m_" + n] for n in BIG_NAMES + SMALL_NAMES}
    moments_v = {n: args["v_" + n] for n in BIG_NAMES + SMALL_NAMES}
    x0 = x[0]
    target = loss_target[0]

    shards = []
    for n in BIG_NAMES:
        w = weights[n]
        rows_to, cols_to = w.shape[1], w.shape[2]
        if n in ("w_ffn_gate", "w_ffn_up"):
            cols_to = FF_SHARD_PAD
        if n == "w_ffn_down":
            rows_to = FF_SHARD_PAD
        shards.append(_prep_weight(w, rows_to, cols_to, "prep_" + n))
    full = dict(zip(BIG_NAMES, _all_gather(shards, [BIG_SHARD_AXIS[n] for n in BIG_NAMES])))

    saved = []
    xl = x0
    for l in range(DEPTH):
        abar_re, abar_im, bb_re, bb_im = _discretise(lambda_re[l], lambda_im[l], log_dt[l], b_re[l], b_im[l])
        ssm = dict(ar8=abar_re.reshape(8, 256), ai8=abar_im.reshape(8, 256),
                   bdr=_block_diag_in(bb_re).astype(BF16), bdi=_block_diag_in(bb_im).astype(BF16),
                   cdr=_block_diag_out(c_re[l]).astype(BF16), cdi=_block_diag_out(c_im[l]).astype(BF16),
                   dsk=d_skip[l][None])
        gq2 = jnp.tile(g_q[l], 2)[None]
        gk2 = jnp.tile(g_k[l], 2)[None]
        z, h = _in_proj(xl, g_mix[l][None], full["w_in"][l])
        ols = []
        for g in range(N_GROUPS):
            ols.extend(_attn_fwd(z, gq2, gk2, g))
        y, cin = _ssm_fwd(z, **ssm)
        xm = _mix_fwd(ols, y, z, xl, full["w_attn_proj"][l], full["w_glu_a"][l], full["w_glu_b"][l], full["w_out"][l])
        xo = _ffn_fwd(xm, g_ffn[l][None], full["w_ffn_gate"][l], full["w_ffn_up"][l], full["w_ffn_down"][l])
        saved.append(dict(x=xl, z=z, h=h, ols=ols, y=y, cin=cin, xm=xm, ssm=ssm, gq2=gq2, gk2=gk2))
        xl = xo

    dxo, loss_local = _loss_grad(xl, target)
    loss = lax.psum(loss_local[0, 0], MESH_AXES)
    big_grads = {n: [None] * DEPTH for n in BIG_NAMES}
    small_grads = {n: [None] * DEPTH for n in SMALL_NAMES}
    for l in reversed(range(DEPTH)):
        s = saved[l]
        dxm, h2, hid, dgate, dup, dgffn = _ffn_bwd(s["xm"], g_ffn[l][None], full["w_ffn_gate"][l],
                                                   full["w_ffn_up"][l], full["w_ffn_down"][l], dxo)
        big_grads["w_ffn_down"][l] = _tn(hid, dxo, "grad_w_ffn_down")
        big_grads["w_ffn_gate"][l] = _tn(h2, dgate, "grad_w_ffn_gate")
        big_grads["w_ffn_up"][l] = _tn(h2, dup, "grad_w_ffn_up")
        (do0, c0, do1, c1, do2, c2, dy, dga, dgs, a_b, yg_b, mix_b, dao_b, dpa_b, dpb_b) = _mix_bwd(
            dxm, s["ols"], s["y"], s["z"], full["w_attn_proj"][l], full["w_glu_a"][l], full["w_glu_b"][l],
            full["w_out"][l])
        big_grads["w_out"][l] = _tn(mix_b, dxm, "grad_w_out")
        big_grads["w_attn_proj"][l] = _tn(a_b, dao_b, "grad_w_attn_proj")
        big_grads["w_glu_a"][l] = _tn(yg_b, dpa_b, "grad_w_glu_a")
        big_grads["w_glu_b"][l] = _tn(yg_b, dpb_b, "grad_w_glu_b")
        du, da4, ddsk, dbdr, dbdi, dcdr, dcdi = _ssm_bwd(s["z"], dy, s["cin"], **s["ssm"])
        dqkv = []
        dgq = jnp.zeros((1, LANES), F32)
        dgk = jnp.zeros((1, LANES), F32)
        for g, (do_g, c_g) in enumerate(((do0, c0), (do1, c1), (do2, c2))):
            dq, dk, dv, dgq_g, dgk_g = _attn_bwd(s["z"], s["gq2"], s["gk2"], s["ols"][2 * g + 1], do_g, c_g, g)
            dqkv.append((dq, dk, dv))
            dgq, dgk = dgq + dgq_g, dgk + dgk_g
        pieces = [dqkv[g][j] for j in range(3) for g in range(N_GROUPS)] + [du, dga, dgs]
        dxo, dgmix = _in_proj_bwd(pieces, full["w_in"][l], s["x"], g_mix[l][None], dxm)
        big_grads["w_in"][l] = _tn_pieces(s["h"], pieces, "grad_w_in")
        _, disc_vjp = jax.vjp(_discretise, lambda_re[l], lambda_im[l], log_dt[l], b_re[l], b_im[l])
        dar = jnp.concatenate([da4[0], da4[1]], axis=1).reshape(32, 64)
        dai = jnp.concatenate([da4[2], da4[3]], axis=1).reshape(32, 64)
        dlr, dli, dldt, dbre, dbim = disc_vjp((dar, dai, _block_diag_in_t(dbdr), _block_diag_in_t(dbdi)))
        small_grads["g_mix"][l] = dgmix[0]
        small_grads["g_q"][l] = dgq[0, :HEAD_DIM] + dgq[0, HEAD_DIM:]
        small_grads["g_k"][l] = dgk[0, :HEAD_DIM] + dgk[0, HEAD_DIM:]
        small_grads["lambda_re"][l] = dlr
        small_grads["lambda_im"][l] = dli
        small_grads["log_dt"][l] = dldt
        small_grads["b_re"][l] = dbre
        small_grads["b_im"][l] = dbim
        small_grads["c_re"][l] = _block_diag_out_t(dcdr)
        small_grads["c_im"][l] = _block_diag_out_t(dcdi)
        small_grads["d_skip"][l] = ddsk[0]
        small_grads["g_ffn"][l] = dgffn[0]
    grad_x = dxo[None]

    small_local = {n: jnp.stack(small_grads[n]) for n in SMALL_NAMES}
    rs_axes = [BIG_SHARD_AXIS[n] - 1 for n in BIG_NAMES]
    got = _exchange_with_sibling([big_grads[n] for n in BIG_NAMES], rs_axes)
    core = lax.axis_index("c").astype(jnp.int32).reshape(1)
    sums = [[_chip_sum(big_grads[n][l], got[t], l, rs_axes[t], core, "chip_sum_" + n) for l in range(DEPTH)]
            for t, n in enumerate(BIG_NAMES)]
    recv = _exchange_chip_sums(sums, _pack_small(small_local))
    out_g, out_d, out_m, out_v = {}, {}, {}, {}
    for n, r in zip(BIG_NAMES, recv[:-1]):
        out_g[n], out_d[n], out_m[n], out_v[n] = _adamw_big(r, weights[n], moments_m[n], moments_v[n],
                                                            ADAMW_ROWS[n], "adamw_" + n)
    like = {n: weights[n] for n in SMALL_NAMES}
    packed = _adamw_small(recv[-1], _pack_small(like), _pack_small({n: moments_m[n] for n in SMALL_NAMES}),
                          _pack_small({n: moments_v[n] for n in SMALL_NAMES}))
    for dst, p in zip((out_g, out_d, out_m, out_v), packed):
        dst.update(_unpack_small(p, like))

    order = ("g_mix", "w_in", "g_q", "g_k", "w_attn_proj", "lambda_re", "lambda_im", "log_dt", "b_re", "b_im",
             "c_re", "c_im", "d_skip", "w_glu_a", "w_glu_b", "w_out", "g_ffn", "w_ffn_gate", "w_ffn_up",
             "w_ffn_down")
    return (loss, grad_x, *[out_g[n] for n in order], *[out_d[n] for n in order],
            *[out_m[n] for n in order], *[out_v[n] for n in order])
```

```python
import functools
import math

import jax
import jax.numpy as jnp
from jax import lax
from jax.experimental import pallas as pl
from jax.experimental.pallas import tpu as pltpu

F32 = jnp.float32
BF16 = jnp.bfloat16

D_MODEL = 1024
DEPTH = 4
N_DEV = 8
N_CHIPS = 4
HEAD_DIM = 64
BLK = 128
LANES = 128
ATTN_W = 512
N_GROUPS = 3
DILATIONS = (1, 4, 16)
ATTN_ROWS = 2048
SSM_W = 512
SSM_STATES = 2048
SSM_BLOCKS = 4
IN_COLS = 7168
COL_U = 9
D_FF = 2816
FF_SHARD = D_FF // N_DEV
FF_SHARD_PAD = 384
FF_PAD = FF_SHARD_PAD * N_DEV
FF_CHUNK = 512
EPS = 1e-6
SSM_TM = 512
SMALL_TILES = 4

ADAM_LR = 0.001
ADAM_B1 = 0.9
ADAM_B2 = 0.999
ADAM_EPS = 1e-08
ADAM_WD = 0.01
ADAM_STEP = 10

MESH_AXES = ("x", "y", "c")
MIB = 1024 * 1024


def _cp(sem=None, vmem_mib=None):
    kw = {}
    if sem is not None:
        kw["dimension_semantics"] = sem
    if vmem_mib is not None:
        kw["vmem_limit_bytes"] = vmem_mib * MIB
    return pltpu.CompilerParams(**kw)


def _nt(a, b):
    return lax.dot_general(a, b, (((1,), (1,)), ((), ())), preferred_element_type=F32)


def _tn_dot(a, b):
    return lax.dot_general(a, b, (((0,), (0,)), ((), ())), preferred_element_type=F32)


def _nn(a, b):
    return jnp.dot(a, b, preferred_element_type=F32)


def _sigmoid(t):
    return 0.5 * jnp.tanh(0.5 * t) + 0.5


def _prep_weight(w, rows_to, cols_to, name):
    _, k, n = w.shape

    def body(w_ref, o_ref):
        if rows_to != k or cols_to != n:
            o_ref[...] = jnp.zeros(o_ref.shape, BF16)
        o_ref[0, :k, :n] = w_ref[0].astype(BF16)

    return pl.pallas_call(
        body, grid=(DEPTH,),
        in_specs=[pl.BlockSpec((1, k, n), lambda l: (l, 0, 0))],
        out_specs=pl.BlockSpec((1, rows_to, cols_to), lambda l: (l, 0, 0)),
        out_shape=jax.ShapeDtypeStruct((DEPTH, rows_to, cols_to), BF16),
        compiler_params=_cp(("parallel",), 40), name=name)(w)


def _my_index():
    return 4 * lax.axis_index("x") + 2 * lax.axis_index("y") + lax.axis_index("c")


def _my_chip():
    return 2 * lax.axis_index("x") + lax.axis_index("y")


def _sibling():
    return (lax.axis_index("x"), lax.axis_index("y"), 1 - lax.axis_index("c"))


def _other_chip(j):
    return (jnp.bitwise_xor(lax.axis_index("x"), (j >> 1) & 1), jnp.bitwise_xor(lax.axis_index("y"), j & 1))


def _slab(ref, idx, width, axis):
    start = pl.multiple_of(idx * width, width)
    sl = [slice(None)] * len(ref.shape)
    sl[axis] = pl.ds(start, width)
    return ref.at[tuple(sl)]


def _remote(src, dst, ssem, rsem, device):
    return pltpu.make_async_remote_copy(src_ref=src, dst_ref=dst, send_sem=ssem, recv_sem=rsem,
                                        device_id=device, device_id_type=pl.DeviceIdType.MESH)


def _two_level_gather(srcs, blocks, ssem, rsem, lsem):
    nt = len(srcs)
    x, y, c = lax.axis_index("x"), lax.axis_index("y"), lax.axis_index("c")
    me = _my_index()
    local, sends = [], []
    for t in range(nt):
        mine = blocks[t](me)
        loc = pltpu.make_async_copy(srcs[t], mine, lsem.at[t])
        loc.start()
        local.append(loc)
        first = [_remote(srcs[t], mine, ssem.at[t, 0], rsem.at[t, 0], _sibling())]
        for j in range(1, N_CHIPS):
            first.append(_remote(srcs[t], mine, ssem.at[t, j], rsem.at[t, j], (*_other_chip(j), c)))
        for cp in first:
            cp.start()
        sends.extend(first)
    for t in range(nt):
        for j in range(1, N_CHIPS):
            ox, oy = _other_chip(j)
            landed = blocks[t](4 * ox + 2 * oy + c)
            _remote(landed, landed, ssem.at[t, j], rsem.at[t, j], _sibling()).wait_recv()
            fwd = _remote(landed, landed, ssem.at[t, 3 + j], rsem.at[t, 3 + j], _sibling())
            fwd.start()
            sends.append(fwd)
    for t in range(nt):
        got = blocks[t](4 * x + 2 * y + (1 - c))
        _remote(got, got, ssem.at[t, 0], rsem.at[t, 0], _sibling()).wait_recv()
        for j in range(1, N_CHIPS):
            ox, oy = _other_chip(j)
            got = blocks[t](4 * ox + 2 * oy + (1 - c))
            _remote(got, got, ssem.at[t, 3 + j], rsem.at[t, 3 + j], _sibling()).wait_recv()
    for cp in sends:
        cp.wait_send()
    for cp in local:
        cp.wait()


def _gather_sems(nt):
    return [pltpu.SemaphoreType.DMA((nt, N_DEV - 1)), pltpu.SemaphoreType.DMA((nt, N_DEV - 1)),
            pltpu.SemaphoreType.DMA((nt,))]


def _all_gather(shards, axes):
    nt = len(shards)

    def body(*refs):
        ins, outs = refs[:nt], refs[nt:2 * nt]
        ssem, rsem, lsem = refs[2 * nt:]
        blocks = [functools.partial(_slab, outs[t], width=shards[t].shape[axes[t]], axis=axes[t]) for t in range(nt)]
        _two_level_gather(ins, blocks, ssem, rsem, lsem)

    out_shape = []
    for t in range(nt):
        s = list(shards[t].shape)
        s[axes[t]] *= N_DEV
        out_shape.append(jax.ShapeDtypeStruct(tuple(s), shards[t].dtype))
    return pl.pallas_call(
        body,
        in_specs=[pl.BlockSpec(memory_space=pltpu.HBM)] * nt,
        out_specs=[pl.BlockSpec(memory_space=pltpu.HBM)] * nt,
        out_shape=out_shape, scratch_shapes=_gather_sems(nt),
        name="all_gather_weights")(*shards)


def _exchange_with_sibling(grads, axes):
    nt = len(grads)

    def body(*refs):
        ins = [refs[t * DEPTH:(t + 1) * DEPTH] for t in range(nt)]
        outs = refs[nt * DEPTH: nt * DEPTH + nt]
        ssem, rsem = refs[nt * DEPTH + nt:]
        c = lax.axis_index("c")
        for t in range(nt):
            width = grads[t][0].shape[axes[t]] // N_DEV
            for q in range(N_CHIPS):
                for l in range(DEPTH):
                    _remote(_slab(ins[t][l], 2 * q + (1 - c), width, axes[t]), outs[t].at[q, l],
                            ssem.at[t], rsem.at[t], _sibling()).start()
        for t in range(nt):
            _remote(outs[t], outs[t], ssem.at[t], rsem.at[t], _sibling()).wait()

    out_shape = []
    for t in range(nt):
        s = list(grads[t][0].shape)
        s[axes[t]] //= N_DEV
        out_shape.append(jax.ShapeDtypeStruct((N_CHIPS, DEPTH, s[0], s[1]), F32))
    flat = [g for per_type in grads for g in per_type]
    return pl.pallas_call(
        body,
        in_specs=[pl.BlockSpec(memory_space=pltpu.HBM)] * len(flat),
        out_specs=[pl.BlockSpec(memory_space=pltpu.HBM)] * nt,
        out_shape=out_shape,
        scratch_shapes=[pltpu.SemaphoreType.DMA((nt,)), pltpu.SemaphoreType.DMA((nt,))],
        name="grads_to_sibling")(*flat)


def _chip_sum(grad, got, layer, axis, core, name):
    _, _, r, c = got.shape
    tr = min(r, 512)

    def body(core_ref, g_ref, s_ref, o_ref):
        o_ref[0] = (g_ref[...] + s_ref[0, 0]).astype(BF16)

    if axis == 1:
        g_spec = pl.BlockSpec((tr, c), lambda q, i, core_ref: (i, 2 * q + core_ref[0]))
    else:
        g_spec = pl.BlockSpec((tr, c), lambda q, i, core_ref: ((2 * q + core_ref[0]) * (r // tr) + i, 0))
    return pl.pallas_call(
        body,
        grid_spec=pltpu.PrefetchScalarGridSpec(
            num_scalar_prefetch=1, grid=(N_CHIPS, r // tr),
            in_specs=[g_spec, pl.BlockSpec((1, 1, tr, c), lambda q, i, core_ref: (q, layer, i, 0))],
            out_specs=pl.BlockSpec((1, tr, c), lambda q, i, core_ref: (q, i, 0))),
        out_shape=jax.ShapeDtypeStruct((N_CHIPS, r, c), BF16),
        compiler_params=_cp(("parallel", "parallel"), 40), name=name)(core, grad, got)


def _exchange_chip_sums(sums, small):
    nt = len(sums)

    def body(*refs):
        ins = [refs[t * DEPTH:(t + 1) * DEPTH] for t in range(nt)]
        small_ref = refs[nt * DEPTH]
        outs = refs[nt * DEPTH + 1: nt * DEPTH + 1 + nt]
        small_out = refs[nt * DEPTH + 1 + nt]
        ssem, rsem, lsem, g_ssem, g_rsem, g_lsem = refs[nt * DEPTH + 2 + nt:]
        c = lax.axis_index("c")
        chip = _my_chip()
        for t in range(nt):
            for l in range(DEPTH):
                pltpu.make_async_copy(ins[t][l].at[chip], outs[t].at[chip, l], lsem.at[t]).start()
            for j in range(1, N_CHIPS):
                other = jnp.bitwise_xor(chip, j)
                for l in range(DEPTH):
                    _remote(ins[t][l].at[other], outs[t].at[chip, l], ssem.at[t, j - 1], rsem.at[t, j - 1],
                            (*_other_chip(j), c)).start()
        _two_level_gather([small_ref], [lambda idx: small_out.at[idx]], g_ssem, g_rsem, g_lsem)
        for t in range(nt):
            pltpu.make_async_copy(outs[t].at[chip], outs[t].at[chip], lsem.at[t]).wait()
            for j in range(1, N_CHIPS):
                other = jnp.bitwise_xor(chip, j)
                _remote(outs[t].at[other], outs[t].at[other], ssem.at[t, j - 1], rsem.at[t, j - 1],
                        (*_other_chip(j), c)).wait()

    out_shape = []
    for t in range(nt):
        _, r, c = sums[t][0].shape
        out_shape.append(jax.ShapeDtypeStruct((N_CHIPS, DEPTH, r, c), BF16))
    out_shape.append(jax.ShapeDtypeStruct((N_DEV,) + small.shape, F32))
    flat = [s for per_type in sums for s in per_type]
    return pl.pallas_call(
        body,
        in_specs=[pl.BlockSpec(memory_space=pltpu.HBM)] * (len(flat) + 1),
        out_specs=[pl.BlockSpec(memory_space=pltpu.HBM)] * (nt + 1),
        out_shape=out_shape,
        scratch_shapes=[pltpu.SemaphoreType.DMA((nt, N_CHIPS - 1)), pltpu.SemaphoreType.DMA((nt, N_CHIPS - 1)),
                        pltpu.SemaphoreType.DMA((nt,))] + _gather_sems(1),
        name="chip_sums_over_ici")(*flat, small)


def _adamw_math(w, g, m, v):
    m = ADAM_B1 * m + (1.0 - ADAM_B1) * g
    v = ADAM_B2 * v + (1.0 - ADAM_B2) * (g * g)
    m_hat = m / (1.0 - ADAM_B1 ** ADAM_STEP)
    v_hat = v / (1.0 - ADAM_B2 ** ADAM_STEP)
    delta = -ADAM_LR * (m_hat / (jnp.sqrt(v_hat) + ADAM_EPS) + ADAM_WD * w)
    return delta, m, v


def _adamw_big(recv, w, m, v, tk, name):
    _, k, n = w.shape
    npad = recv.shape[3]

    def body(r_ref, w_ref, m_ref, v_ref, g_out, d_out, m_out, v_out):
        g = r_ref[0, 0].astype(F32)
        for s in range(1, N_CHIPS):
            g = g + r_ref[s, 0].astype(F32)
        g = g[:, :n]
        delta, mn, vn = _adamw_math(w_ref[0], g, m_ref[0], v_ref[0])
        g_out[0] = g
        d_out[0] = delta
        m_out[0] = mn
        v_out[0] = vn

    blk = pl.BlockSpec((1, tk, n), lambda l, i: (l, i, 0))
    sds = jax.ShapeDtypeStruct(w.shape, F32)
    return pl.pallas_call(
        body, grid=(DEPTH, k // tk),
        in_specs=[pl.BlockSpec((N_CHIPS, 1, tk, npad), lambda l, i: (0, l, i, 0)), blk, blk, blk],
        out_specs=[blk, blk, blk, blk], out_shape=[sds, sds, sds, sds],
        compiler_params=_cp(("parallel", "parallel"), 48), name=name)(recv, w, m, v)


def _adamw_small(recv, w, m, v):
    rows = w.shape[0]
    tr = rows // SMALL_TILES

    def body(r_ref, w_ref, m_ref, v_ref, g_out, d_out, m_out, v_out):
        g = r_ref[0]
        for s in range(1, N_DEV):
            g = g + r_ref[s]
        delta, mn, vn = _adamw_math(w_ref[...], g, m_ref[...], v_ref[...])
        g_out[...] = g
        d_out[...] = delta
        m_out[...] = mn
        v_out[...] = vn

    blk = pl.BlockSpec((tr, LANES), lambda i: (i, 0))
    sds = jax.ShapeDtypeStruct(w.shape, F32)
    return pl.pallas_call(
        body, grid=(SMALL_TILES,),
        in_specs=[pl.BlockSpec((N_DEV, tr, LANES), lambda i: (0, i, 0)), blk, blk, blk],
        out_specs=[blk, blk, blk, blk], out_shape=[sds, sds, sds, sds],
        compiler_params=_cp(("parallel",), 40), name="adamw_small")(recv, w, m, v)


def _rms(t):
    return lax.rsqrt(jnp.mean(t * t, axis=-1, keepdims=True) + EPS)


def _rms_bwd(t, r, gain, dh, dres):
    u = dh * gain
    dt = dres + r * u - t * ((r * r * r) * (1.0 / D_MODEL) * jnp.sum(t * u, axis=-1, keepdims=True))
    return dt, dh * t * r


def _in_proj(x, gain, w):
    L = x.shape[0]
    n = w.shape[1]
    tm, halves = 512, 2
    tn = n // halves

    def body(x_ref, g_ref, w_hbm, z_ref, h_ref, w_scr):
        i, j = pl.program_id(0), pl.program_id(1)

        @pl.when((i == 0) & (j == 0))
        def _():
            pltpu.sync_copy(w_hbm, w_scr)

        @pl.when(j == 0)
        def _():
            t = x_ref[...]
            h_ref[...] = (t * _rms(t) * g_ref[...]).astype(BF16)

        for jj in range(halves):
            @pl.when(j == jj)
            def _(jj=jj):
                z_ref[...] = _nn(h_ref[...], w_scr[:, jj * tn:(jj + 1) * tn])

    return pl.pallas_call(
        body, grid=(L // tm, halves),
        in_specs=[pl.BlockSpec((tm, D_MODEL), lambda i, j: (i, 0)), pl.BlockSpec((1, D_MODEL), lambda i, j: (0, 0)),
                  pl.BlockSpec(memory_space=pltpu.HBM)],
        out_specs=[pl.BlockSpec((tm, tn), lambda i, j: (i, j)), pl.BlockSpec((tm, D_MODEL), lambda i, j: (i, 0))],
        out_shape=[jax.ShapeDtypeStruct((L, n), F32), jax.ShapeDtypeStruct((L, D_MODEL), BF16)],
        scratch_shapes=[pltpu.VMEM((D_MODEL, n), BF16)],
        compiler_params=_cp(("arbitrary", "arbitrary"), 56), name="in_proj")(x, gain, w)


PIECE_W = 512


def _piece_columns(pieces):
    cols = []
    for p, arr in enumerate(pieces):
        cols.extend((p, off) for off in range(0, arr.shape[1], PIECE_W))
    return cols


def _in_proj_bwd(pieces, w, x, gain, dres):
    L = x.shape[0]
    n = w.shape[1]
    tm = 512
    npc = len(pieces)
    cols = _piece_columns(pieces)
    per_dot = 4

    def body(*refs):
        dz_refs = refs[:npc]
        w_hbm, x_ref, g_ref, dr_ref, dx_ref, dg_ref, w_scr = refs[npc:]

        @pl.when(pl.program_id(0) == 0)
        def _():
            pltpu.sync_copy(w_hbm, w_scr)
            dg_ref[...] = jnp.zeros_like(dg_ref)

        dh = None
        for c0 in range(0, len(cols), per_dot):
            chunk = cols[c0:c0 + per_dot]
            parts = [dz_refs[p][:, off:off + PIECE_W].astype(BF16) for p, off in chunk]
            term = _nt(jnp.concatenate(parts, axis=1), w_scr[:, c0 * PIECE_W:(c0 + len(chunk)) * PIECE_W])
            dh = term if dh is None else dh + term
        t = x_ref[...]
        dt, dgt = _rms_bwd(t, _rms(t), g_ref[...], dh, dr_ref[...])
        dx_ref[...] = dt
        dg_ref[...] += jnp.sum(dgt, axis=0, keepdims=True)

    row = pl.BlockSpec((tm, D_MODEL), lambda i: (i, 0))
    vec = pl.BlockSpec((1, D_MODEL), lambda i: (0, 0))
    piece_specs = [pl.BlockSpec((tm, arr.shape[1]), lambda i: (i, 0)) for arr in pieces]
    return pl.pallas_call(
        body, grid=(L // tm,),
        in_specs=piece_specs + [pl.BlockSpec(memory_space=pltpu.HBM), row, vec, row],
        out_specs=[row, vec],
        out_shape=[jax.ShapeDtypeStruct((L, D_MODEL), F32), jax.ShapeDtypeStruct((1, D_MODEL), F32)],
        scratch_shapes=[pltpu.VMEM((D_MODEL, n), BF16)],
        compiler_params=_cp(("arbitrary",), 60), name="in_proj_bwd")(*pieces, w, x, gain, dres)


def _tn(a, b, name):
    m, na = a.shape
    nb = b.shape[1]
    ta, tb, tm = min(na, 1024), min(nb, 1024), 2048
    nm = m // tm

    def body(a_ref, b_ref, o_ref):
        @pl.when(pl.program_id(2) == 0)
        def _():
            o_ref[...] = jnp.zeros_like(o_ref)
        o_ref[...] += _tn_dot(a_ref[...].astype(BF16), b_ref[...].astype(BF16))

    return pl.pallas_call(
        body, grid=(na // ta, nb // tb, nm),
        in_specs=[pl.BlockSpec((tm, ta), lambda i, j, k: (k, i)), pl.BlockSpec((tm, tb), lambda i, j, k: (k, j))],
        out_specs=pl.BlockSpec((ta, tb), lambda i, j, k: (i, j)),
        out_shape=jax.ShapeDtypeStruct((na, nb), F32),
        compiler_params=_cp(("parallel", "parallel", "arbitrary"), 48), name=name)(a, b)


def _tn_pieces(a, pieces, name):
    m, na = a.shape
    npc = len(pieces)
    cols = _piece_columns(pieces)
    per_block = D_MODEL // PIECE_W
    nj = len(cols) // per_block
    tm = 1024
    nm = m // tm
    block_of_piece = {}
    for c, (p, _) in enumerate(cols):
        block_of_piece[p] = c // per_block

    def body(*refs):
        a_ref = refs[0]
        b_refs = refs[1:1 + npc]
        o_ref = refs[1 + npc]
        j = pl.program_id(0)

        @pl.when(pl.program_id(1) == 0)
        def _():
            o_ref[...] = jnp.zeros_like(o_ref)

        for jj in range(nj):
            @pl.when(j == jj)
            def _(jj=jj):
                parts = [b_refs[p][:, off:off + PIECE_W].astype(BF16)
                         for p, off in cols[jj * per_block:(jj + 1) * per_block]]
                o_ref[...] += _tn_dot(a_ref[...], jnp.concatenate(parts, axis=1))

    piece_specs = [pl.BlockSpec((tm, arr.shape[1]),
                                functools.partial(lambda j, k, jj: (jnp.where(j == jj, k, 0), 0), jj=block_of_piece[p]))
                   for p, arr in enumerate(pieces)]
    return pl.pallas_call(
        body, grid=(nj, nm),
        in_specs=[pl.BlockSpec((tm, na), lambda j, k: (k, 0))] + piece_specs,
        out_specs=pl.BlockSpec((na, D_MODEL), lambda j, k: (0, j)),
        out_shape=jax.ShapeDtypeStruct((na, D_MODEL * nj), F32),
        compiler_params=_cp(("parallel", "arbitrary"), 56), name=name)(a, *pieces)


def _loss_grad(xf, target):
    L = xf.shape[0]
    tm = 1024

    def body(x_ref, t_ref, dy_ref, l_ref):
        e = x_ref[...] - t_ref[...]
        dy_ref[...] = e * (1.0 / D_MODEL)

        @pl.when(pl.program_id(0) == 0)
        def _():
            l_ref[...] = jnp.zeros_like(l_ref)
        l_ref[...] += jnp.sum(jnp.sum(e * e, axis=1, keepdims=True), axis=0, keepdims=True) * (0.5 / D_MODEL)

    row = pl.BlockSpec((tm, D_MODEL), lambda i: (i, 0))
    return pl.pallas_call(
        body, grid=(L // tm,), in_specs=[row, row],
        out_specs=[row, pl.BlockSpec((1, 1), lambda i: (0, 0))],
        out_shape=[jax.ShapeDtypeStruct((L, D_MODEL), F32), jax.ShapeDtypeStruct((1, 1), F32)],
        compiler_params=_cp(("arbitrary",), 40), name="loss_grad")(xf, target)


def _ffn_fwd(x, gain, wg, wu, wd):
    L = x.shape[0]
    ff = wg.shape[1]
    tm, tf = 1024, 2 * FF_CHUNK
    nf = ff // tf

    def body(x_ref, g_ref, wg_ref, wu_ref, wd_ref, o_ref, h_scr, acc):
        c = pl.program_id(1)

        @pl.when(c == 0)
        def _():
            t = x_ref[...]
            h_scr[...] = (t * _rms(t) * g_ref[...]).astype(BF16)
            acc[...] = jnp.zeros_like(acc)

        h = h_scr[...]
        down = []
        for cols in (slice(0, FF_CHUNK), slice(FF_CHUNK, 2 * FF_CHUNK)):
            gate = _nn(h, wg_ref[:, cols])
            up = _nn(h, wu_ref[:, cols])
            hid = gate * _sigmoid(gate) * up
            down.append(_nn(hid.astype(BF16), wd_ref[cols, :]))
        acc[...] += down[0] + down[1]

        @pl.when(c == nf - 1)
        def _():
            o_ref[...] = x_ref[...] + acc[...]

    row = pl.BlockSpec((tm, D_MODEL), lambda i, c: (i, 0))
    return pl.pallas_call(
        body, grid=(L // tm, nf),
        in_specs=[row, pl.BlockSpec((1, D_MODEL), lambda i, c: (0, 0)),
                  pl.BlockSpec((D_MODEL, tf), lambda i, c: (0, c)), pl.BlockSpec((D_MODEL, tf), lambda i, c: (0, c)),
                  pl.BlockSpec((tf, D_MODEL), lambda i, c: (c, 0))],
        out_specs=row, out_shape=jax.ShapeDtypeStruct((L, D_MODEL), F32),
        scratch_shapes=[pltpu.VMEM((tm, D_MODEL), BF16), pltpu.VMEM((tm, D_MODEL), F32)],
        compiler_params=_cp(("parallel", "arbitrary"), 48), name="ffn_fwd")(x, gain, wg, wu, wd)


def _ffn_bwd(x, gain, wg, wu, wd, dxo):
    L = x.shape[0]
    ff = wg.shape[1]
    tm, tf = 512, 2 * FF_CHUNK
    nf = ff // tf

    def body(x_ref, g_ref, wg_ref, wu_ref, wd_ref, dxo_ref, dx_ref, h_ref, hid_ref, dgate_ref, dup_ref, dg_ref,
             acc, dxo_b):
        i, c = pl.program_id(0), pl.program_id(1)

        @pl.when(c == 0)
        def _():
            t = x_ref[...]
            h_ref[...] = (t * _rms(t) * g_ref[...]).astype(BF16)
            acc[...] = jnp.zeros_like(acc)
            dxo_b[...] = dxo_ref[...].astype(BF16)

        h = h_ref[...]
        back = []
        for cols in (slice(0, FF_CHUNK), slice(FF_CHUNK, 2 * FF_CHUNK)):
            gate = _nn(h, wg_ref[:, cols])
            up = _nn(h, wu_ref[:, cols])
            sg = _sigmoid(gate)
            silu = gate * sg
            hid_ref[:, cols] = (silu * up).astype(BF16)
            dhid = _nt(dxo_b[...], wd_ref[cols, :])
            dup = (dhid * silu).astype(BF16)
            dgate = (dhid * up * (sg * (1.0 + gate * (1.0 - sg)))).astype(BF16)
            dup_ref[:, cols] = dup
            dgate_ref[:, cols] = dgate
            back.append(_nt(dgate, wg_ref[:, cols]) + _nt(dup, wu_ref[:, cols]))
        acc[...] += back[0] + back[1]

        @pl.when(c == nf - 1)
        def _():
            t = x_ref[...]
            dt, dgt = _rms_bwd(t, _rms(t), g_ref[...], acc[...], dxo_ref[...])
            dx_ref[...] = dt

            @pl.when(i == 0)
            def _():
                dg_ref[...] = jnp.zeros_like(dg_ref)
            dg_ref[...] += jnp.sum(dgt, axis=0, keepdims=True)

    row = pl.BlockSpec((tm, D_MODEL), lambda i, c: (i, 0))
    vec = pl.BlockSpec((1, D_MODEL), lambda i, c: (0, 0))
    wcol = pl.BlockSpec((D_MODEL, tf), lambda i, c: (0, c))
    hcol = pl.BlockSpec((tm, tf), lambda i, c: (i, c))
    return pl.pallas_call(
        body, grid=(L // tm, nf),
        in_specs=[row, vec, wcol, wcol, pl.BlockSpec((tf, D_MODEL), lambda i, c: (c, 0)), row],
        out_specs=[row, row, hcol, hcol, hcol, vec],
        out_shape=[jax.ShapeDtypeStruct((L, D_MODEL), F32), jax.ShapeDtypeStruct((L, D_MODEL), BF16),
                   jax.ShapeDtypeStruct((L, ff), BF16), jax.ShapeDtypeStruct((L, ff), BF16),
                   jax.ShapeDtypeStruct((L, ff), BF16), jax.ShapeDtypeStruct((1, D_MODEL), F32)],
        scratch_shapes=[pltpu.VMEM((tm, D_MODEL), F32), pltpu.VMEM((tm, D_MODEL), BF16)],
        compiler_params=_cp(("arbitrary", "arbitrary"), 56), name="ffn_bwd")(x, gain, wg, wu, wd, dxo)


GELU_K = math.sqrt(2.0 / math.pi)
GELU_C = 0.044715


def _gelu(y):
    return 0.5 * y * (1.0 + jnp.tanh(GELU_K * (y + GELU_C * (y * y * y))))


def _gelu_grad(y):
    th = jnp.tanh(GELU_K * (y + GELU_C * (y * y * y)))
    return 0.5 * (1.0 + th) + 0.5 * y * (1.0 - th * th) * (GELU_K * (1.0 + 3.0 * GELU_C * (y * y)))


def _merge_groups(o_refs, l_refs):
    ls = [r[...] for r in l_refs]
    os_ = [r[...] for r in o_refs]
    lmax = jnp.maximum(jnp.maximum(ls[0], ls[1]), ls[2])
    es = [jnp.exp(l - lmax) for l in ls]
    inv = 1.0 / (es[0] + es[1] + es[2])
    ws = [e * inv for e in es]
    a = ws[0] * os_[0] + ws[1] * os_[1] + ws[2] * os_[2]
    return ws, os_, a


def _mix_fwd(ols, y, z, x, wp, wa, wb, wo):
    L = x.shape[0]
    tm = 256

    def body(o0, l0, o1, l1, o2, l2, y_ref, ga_ref, gs_ref, x_ref, wp_ref, wa_ref, wb_ref, wo_ref, out_ref):
        _, _, a = _merge_groups((o0, o1, o2), (l0, l1, l2))
        a_out = _nn(a.astype(BF16), wp_ref[...])
        yg = _gelu(y_ref[...]).astype(BF16)
        s_out = _nn(yg, wa_ref[...]) * _sigmoid(_nn(yg, wb_ref[...]))
        mix = _sigmoid(ga_ref[...]) * a_out + _sigmoid(gs_ref[...]) * s_out
        out_ref[...] = x_ref[...] + _nn(mix.astype(BF16), wo_ref[...])

    half = pl.BlockSpec((tm, ATTN_W), lambda i: (i, 0))
    row = pl.BlockSpec((tm, D_MODEL), lambda i: (i, 0))
    w512 = pl.BlockSpec((ATTN_W, D_MODEL), lambda i: (0, 0))
    return pl.pallas_call(
        body, grid=(L // tm,),
        in_specs=[half] * 7 + [pl.BlockSpec((tm, D_MODEL), lambda i: (i, 5)),
                               pl.BlockSpec((tm, D_MODEL), lambda i: (i, 6)), row, w512, w512, w512,
                               pl.BlockSpec((D_MODEL, D_MODEL), lambda i: (0, 0))],
        out_specs=row, out_shape=jax.ShapeDtypeStruct((L, D_MODEL), F32),
        compiler_params=_cp(("parallel",), 48), name="mix_fwd")(*ols, y, z, z, x, wp, wa, wb, wo)


def _mix_bwd(dxm, ols, y, z, wp, wa, wb, wo):
    L = dxm.shape[0]
    tm = 256

    def body(dx_ref, o0, l0, o1, l1, o2, l2, y_ref, ga_ref, gs_ref, wp_ref, wa_ref, wb_ref, wo_ref,
             do0, dl0, do1, dl1, do2, dl2, dy_ref, dga_ref, dgs_ref, a_ref, yg_ref, mix_ref, dao_ref, dpa_ref,
             dpb_ref):
        ws, os_, a = _merge_groups((o0, o1, o2), (l0, l1, l2))
        ab = a.astype(BF16)
        a_out = _nn(ab, wp_ref[...])
        yv = y_ref[...]
        yg = _gelu(yv).astype(BF16)
        pa = _nn(yg, wa_ref[...])
        spb = _sigmoid(_nn(yg, wb_ref[...]))
        s_out = pa * spb
        sga = _sigmoid(ga_ref[...])
        sgs = _sigmoid(gs_ref[...])
        mix = sga * a_out + sgs * s_out
        dmix = _nt(dx_ref[...].astype(BF16), wo_ref[...])
        da_out = (sga * dmix).astype(BF16)
        ds_out = sgs * dmix
        dpa = (ds_out * spb).astype(BF16)
        dpb = (ds_out * pa * spb * (1.0 - spb)).astype(BF16)
        dga_ref[...] = (dmix * a_out * sga * (1.0 - sga)).astype(BF16)
        dgs_ref[...] = (dmix * s_out * sgs * (1.0 - sgs)).astype(BF16)
        dy_ref[...] = (_nt(dpa, wa_ref[...]) + _nt(dpb, wb_ref[...])) * _gelu_grad(yv)
        da = _nt(da_out, wp_ref[...])
        for w, o, do_ref, dl_ref in zip(ws, os_, (do0, do1, do2), (dl0, dl1, dl2)):
            do_ref[...] = w * da
            dl_ref[...] = -(w * da) * a
        a_ref[...] = ab
        yg_ref[...] = yg
        mix_ref[...] = mix.astype(BF16)
        dao_ref[...] = da_out
        dpa_ref[...] = dpa
        dpb_ref[...] = dpb

    half = pl.BlockSpec((tm, ATTN_W), lambda i: (i, 0))
    row = pl.BlockSpec((tm, D_MODEL), lambda i: (i, 0))
    w512 = pl.BlockSpec((ATTN_W, D_MODEL), lambda i: (0, 0))
    hf = jax.ShapeDtypeStruct((L, ATTN_W), F32)
    hb = jax.ShapeDtypeStruct((L, ATTN_W), BF16)
    rb = jax.ShapeDtypeStruct((L, D_MODEL), BF16)
    return pl.pallas_call(
        body, grid=(L // tm,),
        in_specs=[row] + [half] * 7 + [pl.BlockSpec((tm, D_MODEL), lambda i: (i, 5)),
                                       pl.BlockSpec((tm, D_MODEL), lambda i: (i, 6)), w512, w512, w512,
                                       pl.BlockSpec((D_MODEL, D_MODEL), lambda i: (0, 0))],
        out_specs=[half] * 7 + [row, row, half, half, row, row, row, row],
        out_shape=[hf] * 7 + [rb, rb, hb, hb, rb, rb, rb, rb],
        compiler_params=_cp(("parallel",), 56), name="mix_bwd")(dxm, *ols, y, z, z, wp, wa, wb, wo)


N_ATTN_ITERS = ATTN_ROWS // BLK


def _class_rows(ref, start, d):
    if d == 1:
        return ref[pl.ds(pl.multiple_of(start, BLK), BLK), :]
    return ref[pl.ds(start, BLK, stride=d), :]


def _set_class_rows(ref, start, d, val):
    if d == 1:
        ref[pl.ds(pl.multiple_of(start, BLK), BLK), :] = val
    else:
        ref[pl.ds(start, BLK, stride=d), :] = val


def _head_masks():
    lane = lax.broadcasted_iota(jnp.int32, (1, LANES), 1)
    m0 = (lane < HEAD_DIM).astype(F32)
    return m0, 1.0 - m0


def _head_norm(t, gain2, m0, m1):
    tt = t * t
    r0 = lax.rsqrt(jnp.sum(tt * m0, axis=-1, keepdims=True) * (1.0 / HEAD_DIM) + EPS)
    r1 = lax.rsqrt(jnp.sum(tt * m1, axis=-1, keepdims=True) * (1.0 / HEAD_DIM) + EPS)
    r = m0 * r0 + m1 * r1
    return t * r * gain2, r


def _head_norm_bwd(t, r, gain2, dy, m0, m1):
    u = dy * gain2
    tu = t * u
    s = m0 * jnp.sum(tu * m0, axis=-1, keepdims=True) + m1 * jnp.sum(tu * m1, axis=-1, keepdims=True)
    return r * u - t * (r * r * r) * s * (1.0 / HEAD_DIM), jnp.sum(dy * t * r, axis=0, keepdims=True)


def _band_masks():
    qi = lax.broadcasted_iota(jnp.int32, (BLK, 2 * BLK), 0)
    ki = lax.broadcasted_iota(jnp.int32, (BLK, 2 * BLK), 1)
    dist = BLK + qi - ki
    return (dist >= 0) & (dist <= BLK), ki >= BLK


ATTN_SCALE = HEAD_DIM ** -0.5


def _attn_scores(qm, kw, ok):
    return jnp.where(ok, _nt(qm, kw), -1e30)


def _attn_probs(qm, kw, ok):
    s = _attn_scores(qm, kw, ok)
    mx = jnp.max(s, axis=-1, keepdims=True)
    p = jnp.exp(s - mx)
    den = jnp.sum(p, axis=-1, keepdims=True)
    return p, den, mx


NORM_ROWS = 256


def _norm_rows(src_ref, gain2, dst_ref, m0, m1):
    n = src_ref.shape[0]
    step = min(NORM_ROWS, n)
    for r0 in range(0, n, step):
        dst_ref[r0:r0 + step, :] = _head_norm(src_ref[r0:r0 + step, :], gain2, m0, m1)[0]


def _norm_rows_bwd(src_ref, gain2, dy_ref, dst_ref, m0, m1):
    n = src_ref.shape[0]
    step = min(NORM_ROWS, n)
    dgain = jnp.zeros((1, LANES), F32)
    for r0 in range(0, n, step):
        t = src_ref[r0:r0 + step, :]
        _, r = _head_norm(t, gain2, m0, m1)
        dt, dg = _head_norm_bwd(t, r, gain2, dy_ref[r0:r0 + step, :], m0, m1)
        dst_ref[r0:r0 + step, :] = dt.astype(dst_ref.dtype)
        dgain = dgain + dg
    return dgain


def _attn_operands(it, d, first_step, q_ref, kc_ref, kp_ref, vc_ref, vp_ref, band, is_cur):
    j = it // d
    start = (it - j * d) + (d * BLK) * j
    before = jnp.maximum(start - d * BLK, 0)
    inside = j > 0
    q2 = _class_rows(q_ref, start, d)
    kc2 = _class_rows(kc_ref, start, d)
    vc2 = _class_rows(vc_ref, start, d)
    kp2 = jnp.where(inside, _class_rows(kc_ref, before, d), _class_rows(kp_ref, it - j * d, d))
    vp2 = jnp.where(inside, _class_rows(vc_ref, before, d), _class_rows(vp_ref, it - j * d, d))
    has_prev = inside | jnp.logical_not(first_step)
    return start, q2, kp2, kc2, vp2, vc2, band & (is_cur | has_prev)


def _attn_specs(d, step_of):
    nq = N_ATTN_ITERS // d

    def cur(c):
        return pl.BlockSpec((ATTN_ROWS, LANES), lambda hp, n: (step_of(n), c + hp))

    def prev(c):
        return pl.BlockSpec((d * BLK, LANES), lambda hp, n: (jnp.maximum(step_of(n) * nq - 1, 0), c + hp))

    return cur, prev, pl.BlockSpec((1, LANES), lambda hp, n: (0, 0))


def _attn_fwd(z, gq2, gk2, group):
    L = z.shape[0]
    d = DILATIONS[group]
    nsb = L // ATTN_ROWS
    cq, ck, cv = group * 4, 12 + group * 4, 24 + group * 4

    def body(q_ref, kc_ref, kp_ref, vc_ref, vp_ref, gq_ref, gk_ref, o_ref, l_ref, qn_scr, kn_scr, kpn_scr):
        first_step = pl.program_id(1) == 0
        band, is_cur = _band_masks()
        m0, m1 = _head_masks()
        _norm_rows(q_ref, gq_ref[...], qn_scr, m0, m1)
        if d * BLK == ATTN_ROWS:
            @pl.when(first_step)
            def _():
                _norm_rows(kp_ref, gk_ref[...], kpn_scr, m0, m1)

            @pl.when(jnp.logical_not(first_step))
            def _():
                kpn_scr[...] = kn_scr[...]
        else:
            _norm_rows(kp_ref, gk_ref[...], kpn_scr, m0, m1)
        _norm_rows(kc_ref, gk_ref[...], kn_scr, m0, m1)

        def per_block(it, carry):
            start, qn, kpn, kcn, vp2, vc2, ok = _attn_operands(
                it, d, first_step, qn_scr, kn_scr, kpn_scr, vc_ref, vp_ref, band, is_cur)
            kw = jnp.concatenate([kpn, kcn], axis=0).astype(BF16)
            vw = jnp.concatenate([vp2, vc2], axis=0).astype(BF16)
            o2 = jnp.zeros((BLK, LANES), F32)
            l2 = jnp.zeros((BLK, LANES), F32)
            for mh in (m0, m1):
                p, den, mx = _attn_probs((qn * (mh * ATTN_SCALE)).astype(BF16), kw, ok)
                o2 = o2 + mh * (_nn(p.astype(BF16), vw) / den)
                l2 = l2 + mh * (mx + jnp.log(den))
            _set_class_rows(o_ref, start, d, o2)
            _set_class_rows(l_ref, start, d, l2)
            return carry

        lax.fori_loop(0, N_ATTN_ITERS, per_block, 0, unroll=2)

    cur, prev, vec = _attn_specs(d, lambda n: n)
    out = pl.BlockSpec((ATTN_ROWS, LANES), lambda hp, n: (n, hp))
    sds = jax.ShapeDtypeStruct((L, ATTN_W), F32)
    return pl.pallas_call(
        body, grid=(4, nsb),
        in_specs=[cur(cq), cur(ck), prev(ck), cur(cv), prev(cv), vec, vec],
        out_specs=[out, out], out_shape=[sds, sds],
        scratch_shapes=[pltpu.VMEM((ATTN_ROWS, LANES), F32), pltpu.VMEM((ATTN_ROWS, LANES), F32),
                        pltpu.VMEM((d * BLK, LANES), F32)],
        compiler_params=_cp(("parallel", "arbitrary"), 48), name=f"attn_fwd_g{group}")(z, z, z, z, z, gq2, gk2)


def _attn_bwd(z, gq2, gk2, lse, do, c, group):
    L = z.shape[0]
    d = DILATIONS[group]
    nsb = L // ATTN_ROWS
    cq, ck, cv = group * 4, 12 + group * 4, 24 + group * 4

    def body(q_ref, kc_ref, kp_ref, vc_ref, vp_ref, gq_ref, gk_ref, l_ref, do_ref, c_ref,
             dq_ref, dk_ref, dv_ref, dgq_ref, dgk_ref, ck_scr, cv_scr, qn_scr, kn_scr, kpn_scr, dqn_scr, dkn_scr):
        hp, n = pl.program_id(0), pl.program_id(1)
        first_step = n == nsb - 1
        band, is_cur = _band_masks()
        m0, m1 = _head_masks()
        gq, gk = gq_ref[...], gk_ref[...]
        _norm_rows(q_ref, gq, qn_scr, m0, m1)
        _norm_rows(kc_ref, gk, kn_scr, m0, m1)
        _norm_rows(kp_ref, gk, kpn_scr, m0, m1)

        @pl.when((hp == 0) & (n == 0))
        def _():
            dgq_ref[...] = jnp.zeros_like(dgq_ref)
            dgk_ref[...] = jnp.zeros_like(dgk_ref)

        @pl.when(n == 0)
        def _():
            ck_scr[...] = jnp.zeros_like(ck_scr)
            cv_scr[...] = jnp.zeros_like(cv_scr)

        def per_block(i, carry):
            it = N_ATTN_ITERS - 1 - i
            start, qn, kpn, kcn, vp2, vc2, ok = _attn_operands(
                it, d, first_step, qn_scr, kn_scr, kpn_scr, vc_ref, vp_ref, band, is_cur)
            r = it - (it // d) * d
            kw = jnp.concatenate([kpn, kcn], axis=0).astype(BF16)
            vw = jnp.concatenate([vp2, vc2], axis=0).astype(BF16)
            l2 = _class_rows(l_ref, start, d)
            c2 = _class_rows(c_ref, start, d)
            do2 = _class_rows(do_ref, start, d)
            dqn = jnp.zeros((BLK, LANES), F32)
            dkw = jnp.zeros((2 * BLK, LANES), F32)
            dvw = jnp.zeros((2 * BLK, LANES), F32)
            for mh in (m0, m1):
                qm = (qn * (mh * ATTN_SCALE)).astype(BF16)
                lse = jnp.max(jnp.where(mh > 0.5, l2, -3e38), axis=-1, keepdims=True)
                pn = jnp.exp(_attn_scores(qm, kw, ok) - lse)
                dohb = (do2 * mh).astype(BF16)
                dvw = dvw + _tn_dot(pn.astype(BF16), dohb)
                ds = (pn * (_nt(dohb, vw) + jnp.sum(c2 * mh, axis=-1, keepdims=True))).astype(BF16)
                dqn = dqn + (mh * ATTN_SCALE) * _nn(ds, kw)
                dkw = dkw + _tn_dot(ds, qm)
            _set_class_rows(dqn_scr, start, d, dqn)
            _set_class_rows(dkn_scr, start, d, ck_scr[r] + dkw[BLK:])
            _set_class_rows(dv_ref, start, d, cv_scr[r] + dvw[BLK:])
            ck_scr[r] = dkw[:BLK]
            cv_scr[r] = dvw[:BLK]
            return carry

        lax.fori_loop(0, N_ATTN_ITERS, per_block, 0, unroll=2)
        dgq_ref[...] += _norm_rows_bwd(q_ref, gq, dqn_scr, dq_ref, m0, m1)
        dgk_ref[...] += _norm_rows_bwd(kc_ref, gk, dkn_scr, dk_ref, m0, m1)

    cur, prev, vec = _attn_specs(d, lambda n: nsb - 1 - n)
    sds = jax.ShapeDtypeStruct((L, ATTN_W), F32)
    sdb = jax.ShapeDtypeStruct((L, ATTN_W), BF16)
    vsd = jax.ShapeDtypeStruct((1, LANES), F32)
    return pl.pallas_call(
        body, grid=(4, nsb),
        in_specs=[cur(cq), cur(ck), prev(ck), cur(cv), prev(cv), vec, vec, cur(0), cur(0), cur(0)],
        out_specs=[cur(0), cur(0), cur(0), vec, vec], out_shape=[sdb, sdb, sds, vsd, vsd],
        scratch_shapes=[pltpu.VMEM((d, BLK, LANES), F32), pltpu.VMEM((d, BLK, LANES), F32),
                        pltpu.VMEM((ATTN_ROWS, LANES), F32), pltpu.VMEM((ATTN_ROWS, LANES), F32),
                        pltpu.VMEM((d * BLK, LANES), F32),
                        pltpu.VMEM((ATTN_ROWS, LANES), F32), pltpu.VMEM((ATTN_ROWS, LANES), F32)],
        compiler_params=_cp(("arbitrary", "arbitrary"), 56),
        name=f"attn_bwd_g{group}")(z, z, z, z, z, gq2, gk2, lse, do, c)


BLOCK_STATES = SSM_STATES // SSM_BLOCKS
BLOCK_CH = SSM_W // SSM_BLOCKS
SLABS_PER_BLOCK = BLOCK_STATES // LANES


SCAN_STEPS = 8


def _store_block(bufs, b, val, tm):
    for s in range(SLABS_PER_BLOCK):
        k = SLABS_PER_BLOCK * b + s
        bufs[k % 2][pl.ds(8 + k // 2, tm, stride=8), :] = val[:, s * LANES:(s + 1) * LANES]


def _load_block(bufs, b, tm):
    tiles = []
    for s in range(SLABS_PER_BLOCK):
        k = SLABS_PER_BLOCK * b + s
        tiles.append(bufs[k % 2][pl.ds(8 + k // 2, tm, stride=8), :])
    return jnp.concatenate(tiles, axis=1).astype(BF16)


def _ssm_project_in(ub, bdr_ref, bdi_ref, sr, si, tm):
    for b in range(SSM_BLOCKS):
        ubb = ub[:, b * BLOCK_CH:(b + 1) * BLOCK_CH]
        _store_block(sr, b, _nn(ubb, bdr_ref[b]), tm)
        _store_block(si, b, _nn(ubb, bdi_ref[b]), tm)


def _ssm_scan(a, x0, sr, si, tm):
    ar0, ar1, ai0, ai1 = a
    sr[0][0:8, :], sr[1][0:8, :], si[0][0:8, :], si[1][0:8, :] = x0

    def steps(it, c):
        xr0, xr1, xi0, xi1 = c
        base = it * (8 * SCAN_STEPS) + 8
        for q in range(SCAN_STEPS):
            rows = pl.ds(pl.multiple_of(base + 8 * q, 8), 8)
            nr0 = ar0 * xr0 - ai0 * xi0 + sr[0][rows, :]
            ni0 = ar0 * xi0 + ai0 * xr0 + si[0][rows, :]
            nr1 = ar1 * xr1 - ai1 * xi1 + sr[1][rows, :]
            ni1 = ar1 * xi1 + ai1 * xr1 + si[1][rows, :]
            sr[0][rows, :] = nr0
            si[0][rows, :] = ni0
            sr[1][rows, :] = nr1
            si[1][rows, :] = ni1
            xr0, xr1, xi0, xi1 = nr0, nr1, ni0, ni1
        return xr0, xr1, xi0, xi1

    return lax.fori_loop(0, tm // SCAN_STEPS, steps, x0)


def _load_a(ar_ref, ai_ref):
    return ar_ref[:, :LANES], ar_ref[:, LANES:], ai_ref[:, :LANES], ai_ref[:, LANES:]


def _ssm_fwd(z, ar8, ai8, bdr, bdi, cdr, cdi, dsk):
    L = z.shape[0]
    tm = SSM_TM
    nc = L // tm

    def body(u_ref, ar_ref, ai_ref, bdr_ref, bdi_ref, cdr_ref, cdi_ref, dsk_ref, y_ref, cin_ref,
             sr0, sr1, si0, si1, car):
        sr, si = (sr0, sr1), (si0, si1)

        @pl.when(pl.program_id(0) == 0)
        def _():
            car[...] = jnp.zeros_like(car)

        u = u_ref[...]
        _ssm_project_in(u.astype(BF16), bdr_ref, bdi_ref, sr, si, tm)
        cin_ref[0] = car[...]
        xr0, xr1, xi0, xi1 = _ssm_scan(_load_a(ar_ref, ai_ref), (car[0], car[1], car[2], car[3]), sr, si, tm)
        car[0], car[1], car[2], car[3] = xr0, xr1, xi0, xi1
        for b in range(SSM_BLOCKS):
            cols = slice(b * BLOCK_CH, (b + 1) * BLOCK_CH)
            y_ref[:, cols] = (dsk_ref[:, cols] * u[:, cols] + _nn(_load_block(sr, b, tm), cdr_ref[b])
                              - _nn(_load_block(si, b, tm), cdi_ref[b]))

    def const(shape):
        return pl.BlockSpec(shape, lambda i: (0,) * len(shape))

    state = pltpu.VMEM(((tm + 1) * 8, LANES), F32)
    wb = const((SSM_BLOCKS, BLOCK_CH, BLOCK_STATES))
    wc = const((SSM_BLOCKS, BLOCK_STATES, BLOCK_CH))
    return pl.pallas_call(
        body, grid=(nc,),
        in_specs=[pl.BlockSpec((tm, SSM_W), lambda i: (i, COL_U)), const((8, 256)), const((8, 256)),
                  wb, wb, wc, wc, const((1, SSM_W))],
        out_specs=[pl.BlockSpec((tm, SSM_W), lambda i: (i, 0)), pl.BlockSpec((1, 4, 8, LANES), lambda i: (i, 0, 0, 0))],
        out_shape=[jax.ShapeDtypeStruct((L, SSM_W), F32), jax.ShapeDtypeStruct((nc, 4, 8, LANES), F32)],
        scratch_shapes=[state, state, state, state, pltpu.VMEM((4, 8, LANES), F32)],
        compiler_params=_cp(("arbitrary",), 48), name="ssm_fwd")(z, ar8, ai8, bdr, bdi, cdr, cdi, dsk)


def _ssm_bwd(z, dy, cin, ar8, ai8, bdr, bdi, cdr, cdi, dsk):
    L = z.shape[0]
    tm = SSM_TM
    nc = L // tm

    def body(u_ref, dy_ref, cin_ref, ar_ref, ai_ref, dsk_ref, bdr_ref, bdi_ref, cdr_ref, cdi_ref,
             du_ref, da_ref, dds_ref, dbdr_ref, dbdi_ref, dcdr_ref, dcdi_ref,
             sr0, sr1, si0, si1, gr0, gr1, gi0, gi1, carg):
        sr, si, gr, gi = (sr0, sr1), (si0, si1), (gr0, gr1), (gi0, gi1)

        @pl.when(pl.program_id(0) == 0)
        def _():
            carg[...] = jnp.zeros_like(carg)
            for ref in (da_ref, dds_ref, dbdr_ref, dbdi_ref, dcdr_ref, dcdi_ref):
                ref[...] = jnp.zeros_like(ref)

        u = u_ref[...]
        ub = u.astype(BF16)
        dyv = dy_ref[...]
        dyb = dyv.astype(BF16)
        a = _load_a(ar_ref, ai_ref)
        ar0, ar1, ai0, ai1 = a
        x_in = (cin_ref[0, 0], cin_ref[0, 1], cin_ref[0, 2], cin_ref[0, 3])
        _ssm_project_in(ub, bdr_ref, bdi_ref, sr, si, tm)
        _ssm_scan(a, x_in, sr, si, tm)
        for b in range(SSM_BLOCKS):
            dyb_b = dyb[:, b * BLOCK_CH:(b + 1) * BLOCK_CH]
            _store_block(gr, b, _nt(dyb_b, cdr_ref[b]), tm)
            _store_block(gi, b, -_nt(dyb_b, cdi_ref[b]), tm)

        def grad_steps(it, c):
            (nr0, nr1, ni0, ni1), (d_r0, d_r1, d_i0, d_i1) = c
            base = (tm - SCAN_STEPS * (it + 1)) * 8
            for q in reversed(range(SCAN_STEPS)):
                prev = pl.ds(pl.multiple_of(base + 8 * q, 8), 8)
                rows = pl.ds(pl.multiple_of(base + 8 * q + 8, 8), 8)
                g_r0 = gr[0][rows, :] + ar0 * nr0 + ai0 * ni0
                g_i0 = gi[0][rows, :] + ar0 * ni0 - ai0 * nr0
                g_r1 = gr[1][rows, :] + ar1 * nr1 + ai1 * ni1
                g_i1 = gi[1][rows, :] + ar1 * ni1 - ai1 * nr1
                gr[0][rows, :] = g_r0
                gi[0][rows, :] = g_i0
                gr[1][rows, :] = g_r1
                gi[1][rows, :] = g_i1
                pr0, pr1, pi0, pi1 = sr[0][prev, :], sr[1][prev, :], si[0][prev, :], si[1][prev, :]
                d_r0 = d_r0 + pr0 * g_r0 + pi0 * g_i0
                d_r1 = d_r1 + pr1 * g_r1 + pi1 * g_i1
                d_i0 = d_i0 + pr0 * g_i0 - pi0 * g_r0
                d_i1 = d_i1 + pr1 * g_i1 - pi1 * g_r1
                nr0, nr1, ni0, ni1 = g_r0, g_r1, g_i0, g_i1
            return (nr0, nr1, ni0, ni1), (d_r0, d_r1, d_i0, d_i1)

        acc0 = (da_ref[0], da_ref[1], da_ref[2], da_ref[3])
        g_first, acc = lax.fori_loop(0, tm // SCAN_STEPS, grad_steps,
                                     ((carg[0], carg[1], carg[2], carg[3]), acc0))
        carg[0], carg[1], carg[2], carg[3] = g_first
        da_ref[0], da_ref[1], da_ref[2], da_ref[3] = acc

        for b in range(SSM_BLOCKS):
            cols = slice(b * BLOCK_CH, (b + 1) * BLOCK_CH)
            grb, gib = _load_block(gr, b, tm), _load_block(gi, b, tm)
            du_ref[:, cols] = (dsk_ref[:, cols] * dyv[:, cols] + _nt(grb, bdr_ref[b])
                               + _nt(gib, bdi_ref[b])).astype(BF16)
            dbdr_ref[b] += _tn_dot(ub[:, cols], grb)
            dbdi_ref[b] += _tn_dot(ub[:, cols], gib)
            dcdr_ref[b] += _tn_dot(_load_block(sr, b, tm), dyb[:, cols])
            dcdi_ref[b] -= _tn_dot(_load_block(si, b, tm), dyb[:, cols])
        dds_ref[...] += jnp.sum(dyv * u, axis=0, keepdims=True)

    def const(shape):
        return pl.BlockSpec(shape, lambda i: (0,) * len(shape))

    state = pltpu.VMEM(((tm + 1) * 8, LANES), F32)
    wb = const((SSM_BLOCKS, BLOCK_CH, BLOCK_STATES))
    wc = const((SSM_BLOCKS, BLOCK_STATES, BLOCK_CH))
    return pl.pallas_call(
        body, grid=(nc,),
        in_specs=[pl.BlockSpec((tm, SSM_W), lambda i: (nc - 1 - i, COL_U)),
                  pl.BlockSpec((tm, SSM_W), lambda i: (nc - 1 - i, 0)),
                  pl.BlockSpec((1, 4, 8, LANES), lambda i: (nc - 1 - i, 0, 0, 0)),
                  const((8, 256)), const((8, 256)), const((1, SSM_W)), wb, wb, wc, wc],
        out_specs=[pl.BlockSpec((tm, SSM_W), lambda i: (nc - 1 - i, 0)), const((4, 8, LANES)), const((1, SSM_W)),
                   wb, wb, wc, wc],
        out_shape=[jax.ShapeDtypeStruct((L, SSM_W), BF16), jax.ShapeDtypeStruct((4, 8, LANES), F32),
                   jax.ShapeDtypeStruct((1, SSM_W), F32),
                   jax.ShapeDtypeStruct((SSM_BLOCKS, BLOCK_CH, BLOCK_STATES), F32),
                   jax.ShapeDtypeStruct((SSM_BLOCKS, BLOCK_CH, BLOCK_STATES), F32),
                   jax.ShapeDtypeStruct((SSM_BLOCKS, BLOCK_STATES, BLOCK_CH), F32),
                   jax.ShapeDtypeStruct((SSM_BLOCKS, BLOCK_STATES, BLOCK_CH), F32)],
        scratch_shapes=[state] * 8 + [pltpu.VMEM((4, 8, LANES), F32)],
        compiler_params=_cp(("arbitrary",), 56), name="ssm_bwd")(z, dy, cin, ar8, ai8, dsk, bdr, bdi, cdr, cdi)


def _discretise(lam_re, lam_im, log_dt, b_re, b_im):
    dt = jnp.exp(log_dt)[:, None]
    mag = jnp.exp(lam_re * dt)
    ang = lam_im * dt
    abar_re = mag * jnp.cos(ang)
    abar_im = mag * jnp.sin(ang)
    nr = abar_re - 1.0
    ni = abar_im
    den = lam_re * lam_re + lam_im * lam_im
    cr = ((nr * lam_re + ni * lam_im) / den)[..., None]
    ci = ((ni * lam_re - nr * lam_im) / den)[..., None]
    return abar_re, abar_im, cr * b_re - ci * b_im, cr * b_im + ci * b_re


GROUPS_PER_BLOCK = 8


def _block_diag_in(bbar):
    eye = jnp.eye(GROUPS_PER_BLOCK, dtype=F32)
    return jnp.einsum("igpc,gh->igchp", bbar.reshape(SSM_BLOCKS, GROUPS_PER_BLOCK, 64, 16), eye).reshape(
        SSM_BLOCKS, BLOCK_CH, BLOCK_STATES)


def _block_diag_in_t(blocks):
    eye = jnp.eye(GROUPS_PER_BLOCK, dtype=F32)
    return jnp.einsum("igchp,gh->igpc", blocks.reshape(SSM_BLOCKS, GROUPS_PER_BLOCK, 16, GROUPS_PER_BLOCK, 64),
                      eye).reshape(32, 64, 16)


def _block_diag_out(c):
    eye = jnp.eye(GROUPS_PER_BLOCK, dtype=F32)
    return jnp.einsum("igcp,gh->igphc", c.reshape(SSM_BLOCKS, GROUPS_PER_BLOCK, 16, 64), eye).reshape(
        SSM_BLOCKS, BLOCK_STATES, BLOCK_CH)


def _block_diag_out_t(blocks):
    eye = jnp.eye(GROUPS_PER_BLOCK, dtype=F32)
    return jnp.einsum("igphc,gh->igcp", blocks.reshape(SSM_BLOCKS, GROUPS_PER_BLOCK, 64, GROUPS_PER_BLOCK, 16),
                      eye).reshape(32, 16, 64)


SMALL_NAMES = ("g_mix", "g_q", "g_k", "lambda_re", "lambda_im", "log_dt", "b_re", "b_im", "c_re", "c_im",
               "d_skip", "g_ffn")


def _pack_small(parts):
    flat = jnp.concatenate([parts[n].reshape(-1) for n in SMALL_NAMES])
    pad = (-flat.shape[0]) % (8 * LANES * SMALL_TILES)
    return jnp.pad(flat, (0, pad)).reshape(-1, LANES)


def _unpack_small(packed, like):
    flat = packed.reshape(-1)
    out, off = {}, 0
    for n in SMALL_NAMES:
        size = like[n].size
        out[n] = flat[off:off + size].reshape(like[n].shape)
        off += size
    return out


BIG_NAMES = ("w_in", "w_attn_proj", "w_glu_a", "w_glu_b", "w_out", "w_ffn_gate", "w_ffn_up", "w_ffn_down")
BIG_SHARD_AXIS = {"w_in": 2, "w_attn_proj": 2, "w_glu_a": 2, "w_glu_b": 2, "w_out": 1,
                  "w_ffn_gate": 2, "w_ffn_up": 2, "w_ffn_down": 1}
ADAMW_ROWS = {"w_in": 256, "w_attn_proj": 512, "w_glu_a": 512, "w_glu_b": 512, "w_out": 128,
              "w_ffn_gate": 256, "w_ffn_up": 256, "w_ffn_down": 176}


def kernel(x, g_mix, w_in, g_q, g_k, w_attn_proj, lambda_re, lambda_im, log_dt, b_re, b_im, c_re, c_im, d_skip, w_glu_a, w_glu_b, w_out, g_ffn, w_ffn_gate, w_ffn_up, w_ffn_down, loss_target, m_g_mix, m_w_in, m_g_q, m_g_k, m_w_attn_proj, m_lambda_re, m_lambda_im, m_log_dt, m_b_re, m_b_im, m_c_re, m_c_im, m_d_skip, m_w_glu_a, m_w_glu_b, m_w_out, m_g_ffn, m_w_ffn_gate, m_w_ffn_up, m_w_ffn_down, v_g_mix, v_w_in, v_g_q, v_g_k, v_w_attn_proj, v_lambda_re, v_lambda_im, v_log_dt, v_b_re, v_b_im, v_c_re, v_c_im, v_d_skip, v_w_glu_a, v_w_glu_b, v_w_out, v_g_ffn, v_w_ffn_gate, v_w_ffn_up, v_w_ffn_down):
    args = dict(locals())
    weights = {n: args[n] for n in BIG_NAMES + SMALL_NAMES}
    moments_m = {n: args["m_" + n] for n in BIG_NAMES + SMALL_NAMES}
    moments_v = {n: args["v_" + n] for n in BIG_NAMES + SMALL_NAMES}
    x0 = x[0]
    target = loss_target[0]

    shards = []
    for n in BIG_NAMES:
        w = weights[n]
        rows_to, cols_to = w.shape[1], w.shape[2]
        if n in ("w_ffn_gate", "w_ffn_up"):
            cols_to = FF_SHARD_PAD
        if n == "w_ffn_down":
            rows_to = FF_SHARD_PAD
        shards.append(_prep_weight(w, rows_to, cols_to, "prep_" + n))
    full = dict(zip(BIG_NAMES, _all_gather(shards, [BIG_SHARD_AXIS[n] for n in BIG_NAMES])))

    saved = []
    xl = x0
    for l in range(DEPTH):
        abar_re, abar_im, bb_re, bb_im = _discretise(lambda_re[l], lambda_im[l], log_dt[l], b_re[l], b_im[l])
        ssm = dict(ar8=abar_re.reshape(8, 256), ai8=abar_im.reshape(8, 256),
                   bdr=_block_diag_in(bb_re).astype(BF16), bdi=_block_diag_in(bb_im).astype(BF16),
                   cdr=_block_diag_out(c_re[l]).astype(BF16), cdi=_block_diag_out(c_im[l]).astype(BF16),
                   dsk=d_skip[l][None])
        gq2 = jnp.tile(g_q[l], 2)[None]
        gk2 = jnp.tile(g_k[l], 2)[None]
        z, h = _in_proj(xl, g_mix[l][None], full["w_in"][l])
        ols = []
        for g in range(N_GROUPS):
            ols.extend(_attn_fwd(z, gq2, gk2, g))
        y, cin = _ssm_fwd(z, **ssm)
        xm = _mix_fwd(ols, y, z, xl, full["w_attn_proj"][l], full["w_glu_a"][l], full["w_glu_b"][l], full["w_out"][l])
        xo = _ffn_fwd(xm, g_ffn[l][None], full["w_ffn_gate"][l], full["w_ffn_up"][l], full["w_ffn_down"][l])
        saved.append(dict(x=xl, z=z, h=h, ols=ols, y=y, cin=cin, xm=xm, ssm=ssm, gq2=gq2, gk2=gk2))
        xl = xo

    dxo, loss_local = _loss_grad(xl, target)
    loss = lax.psum(loss_local[0, 0], MESH_AXES)
    big_grads = {n: [None] * DEPTH for n in BIG_NAMES}
    small_grads = {n: [None] * DEPTH for n in SMALL_NAMES}
    for l in reversed(range(DEPTH)):
        s = saved[l]
        dxm, h2, hid, dgate, dup, dgffn = _ffn_bwd(s["xm"], g_ffn[l][None], full["w_ffn_gate"][l],
                                                   full["w_ffn_up"][l], full["w_ffn_down"][l], dxo)
        big_grads["w_ffn_down"][l] = _tn(hid, dxo, "grad_w_ffn_down")
        big_grads["w_ffn_gate"][l] = _tn(h2, dgate, "grad_w_ffn_gate")
        big_grads["w_ffn_up"][l] = _tn(h2, dup, "grad_w_ffn_up")
        (do0, c0, do1, c1, do2, c2, dy, dga, dgs, a_b, yg_b, mix_b, dao_b, dpa_b, dpb_b) = _mix_bwd(
            dxm, s["ols"], s["y"], s["z"], full["w_attn_proj"][l], full["w_glu_a"][l], full["w_glu_b"][l],
            full["w_out"][l])
        big_grads["w_out"][l] = _tn(mix_b, dxm, "grad_w_out")
        big_grads["w_attn_proj"][l] = _tn(a_b, dao_b, "grad_w_attn_proj")
        big_grads["w_glu_a"][l] = _tn(yg_b, dpa_b, "grad_w_glu_a")
        big_grads["w_glu_b"][l] = _tn(yg_b, dpb_b, "grad_w_glu_b")
        du, da4, ddsk, dbdr, dbdi, dcdr, dcdi = _ssm_bwd(s["z"], dy, s["cin"], **s["ssm"])
        dqkv = []
        dgq = jnp.zeros((1, LANES), F32)
        dgk = jnp.zeros((1, LANES), F32)
        for g, (do_g, c_g) in enumerate(((do0, c0), (do1, c1), (do2, c2))):
            dq, dk, dv, dgq_g, dgk_g = _attn_bwd(s["z"], s["gq2"], s["gk2"], s["ols"][2 * g + 1], do_g, c_g, g)
            dqkv.append((dq, dk, dv))
            dgq, dgk = dgq + dgq_g, dgk + dgk_g
        pieces = [dqkv[g][j] for j in range(3) for g in range(N_GROUPS)] + [du, dga, dgs]
        dxo, dgmix = _in_proj_bwd(pieces, full["w_in"][l], s["x"], g_mix[l][None], dxm)
        big_grads["w_in"][l] = _tn_pieces(s["h"], pieces, "grad_w_in")
        _, disc_vjp = jax.vjp(_discretise, lambda_re[l], lambda_im[l], log_dt[l], b_re[l], b_im[l])
        dar = jnp.concatenate([da4[0], da4[1]], axis=1).reshape(32, 64)
        dai = jnp.concatenate([da4[2], da4[3]], axis=1).reshape(32, 64)
        dlr, dli, dldt, dbre, dbim = disc_vjp((dar, dai, _block_diag_in_t(dbdr), _block_diag_in_t(dbdi)))
        small_grads["g_mix"][l] = dgmix[0]
        small_grads["g_q"][l] = dgq[0, :HEAD_DIM] + dgq[0, HEAD_DIM:]
        small_grads["g_k"][l] = dgk[0, :HEAD_DIM] + dgk[0, HEAD_DIM:]
        small_grads["lambda_re"][l] = dlr
        small_grads["lambda_im"][l] = dli
        small_grads["log_dt"][l] = dldt
        small_grads["b_re"][l] = dbre
        small_grads["b_im"][l] = dbim
        small_grads["c_re"][l] = _block_diag_out_t(dcdr)
        small_grads["c_im"][l] = _block_diag_out_t(dcdi)
        small_grads["d_skip"][l] = ddsk[0]
        small_grads["g_ffn"][l] = dgffn[0]
    grad_x = dxo[None]

    small_local = {n: jnp.stack(small_grads[n]) for n in SMALL_NAMES}
    rs_axes = [BIG_SHARD_AXIS[n] - 1 for n in BIG_NAMES]
    got = _exchange_with_sibling([big_grads[n] for n in BIG_NAMES], rs_axes)
    core = lax.axis_index("c").astype(jnp.int32).reshape(1)
    sums = [[_chip_sum(big_grads[n][l], got[t], l, rs_axes[t], core, "chip_sum_" + n) for l in range(DEPTH)]
            for t, n in enumerate(BIG_NAMES)]
    recv = _exchange_chip_sums(sums, _pack_small(small_local))
    out_g, out_d, out_m, out_v = {}, {}, {}, {}
    for n, r in zip(BIG_NAMES, recv[:-1]):
        out_g[n], out_d[n], out_m[n], out_v[n] = _adamw_big(r, weights[n], moments_m[n], moments_v[n],
                                                            ADAMW_ROWS[n], "adamw_" + n)
    like = {n: weights[n] for n in SMALL_NAMES}
    packed = _adamw_small(recv[-1], _pack_small(like), _pack_small({n: moments_m[n] for n in SMALL_NAMES}),
                          _pack_small({n: moments_v[n] for n in SMALL_NAMES}))
    for dst, p in zip((out_g, out_d, out_m, out_v), packed):
        dst.update(_unpack_small(p, like))

    order = ("g_mix", "w_in", "g_q", "g_k", "w_attn_proj", "lambda_re", "lambda_im", "log_dt", "b_re", "b_im",
             "c_re", "c_im", "d_skip", "w_glu_a", "w_glu_b", "w_out", "g_ffn", "w_ffn_gate", "w_ffn_up",
             "w_ffn_down")
    return (loss, grad_x, *[out_g[n] for n in order], *[out_d[n] for n in order],
            *[out_m[n] for n in order], *[out_v[n] for n in order])
```

```python
import functools
import math

import jax
import jax.numpy as jnp
from jax import lax
from jax.experimental import pallas as pl
from jax.experimental.pallas import tpu as pltpu

F32 = jnp.float32
BF16 = jnp.bfloat16

D_MODEL = 1024
DEPTH = 4
N_DEV = 8
N_CHIPS = 4
HEAD_DIM = 64
BLK = 128
LANES = 128
ATTN_W = 512
N_GROUPS = 3
DILATIONS = (1, 4, 16)
ATTN_ROWS = 2048
SSM_W = 512
SSM_STATES = 2048
SSM_BLOCKS = 4
IN_COLS = 7168
COL_U = 9
D_FF = 2816
FF_SHARD = D_FF // N_DEV
FF_SHARD_PAD = 384
FF_PAD = FF_SHARD_PAD * N_DEV
FF_CHUNK = 512
EPS = 1e-6
SSM_TM = 512
SMALL_TILES = 4

ADAM_LR = 0.001
ADAM_B1 = 0.9
ADAM_B2 = 0.999
ADAM_EPS = 1e-08
ADAM_WD = 0.01
ADAM_STEP = 10

MESH_AXES = ("x", "y", "c")
MIB = 1024 * 1024


def _cp(sem=None, vmem_mib=None):
    kw = {}
    if sem is not None:
        kw["dimension_semantics"] = sem
    if vmem_mib is not None:
        kw["vmem_limit_bytes"] = vmem_mib * MIB
    return pltpu.CompilerParams(**kw)


def _nt(a, b):
    return lax.dot_general(a, b, (((1,), (1,)), ((), ())), preferred_element_type=F32)


def _tn_dot(a, b):
    return lax.dot_general(a, b, (((0,), (0,)), ((), ())), preferred_element_type=F32)


def _nn(a, b):
    return jnp.dot(a, b, preferred_element_type=F32)


def _sigmoid(t):
    return 0.5 * jnp.tanh(0.5 * t) + 0.5


def _prep_weight(w, rows_to, cols_to, name):
    _, k, n = w.shape

    def body(w_ref, o_ref):
        if rows_to != k or cols_to != n:
            o_ref[...] = jnp.zeros(o_ref.shape, BF16)
        o_ref[0, :k, :n] = w_ref[0].astype(BF16)

    return pl.pallas_call(
        body, grid=(DEPTH,),
        in_specs=[pl.BlockSpec((1, k, n), lambda l: (l, 0, 0))],
        out_specs=pl.BlockSpec((1, rows_to, cols_to), lambda l: (l, 0, 0)),
        out_shape=jax.ShapeDtypeStruct((DEPTH, rows_to, cols_to), BF16),
        compiler_params=_cp(("parallel",), 40), name=name)(w)


def _my_index():
    return 4 * lax.axis_index("x") + 2 * lax.axis_index("y") + lax.axis_index("c")


def _my_chip():
    return 2 * lax.axis_index("x") + lax.axis_index("y")


def _sibling():
    return (lax.axis_index("x"), lax.axis_index("y"), 1 - lax.axis_index("c"))


def _other_chip(j):
    return (jnp.bitwise_xor(lax.axis_index("x"), (j >> 1) & 1), jnp.bitwise_xor(lax.axis_index("y"), j & 1))


def _slab(ref, idx, width, axis):
    start = pl.multiple_of(idx * width, width)
    sl = [slice(None)] * len(ref.shape)
    sl[axis] = pl.ds(start, width)
    return ref.at[tuple(sl)]


def _remote(src, dst, ssem, rsem, device):
    return pltpu.make_async_remote_copy(src_ref=src, dst_ref=dst, send_sem=ssem, recv_sem=rsem,
                                        device_id=device, device_id_type=pl.DeviceIdType.MESH)


def _two_level_gather(srcs, blocks, ssem, rsem, lsem):
    nt = len(srcs)
    x, y, c = lax.axis_index("x"), lax.axis_index("y"), lax.axis_index("c")
    me = _my_index()
    local, sends = [], []
    for t in range(nt):
        mine = blocks[t](me)
        loc = pltpu.make_async_copy(srcs[t], mine, lsem.at[t])
        loc.start()
        local.append(loc)
        first = [_remote(srcs[t], mine, ssem.at[t, 0], rsem.at[t, 0], _sibling())]
        for j in range(1, N_CHIPS):
            first.append(_remote(srcs[t], mine, ssem.at[t, j], rsem.at[t, j], (*_other_chip(j), c)))
        for cp in first:
            cp.start()
        sends.extend(first)
    for t in range(nt):
        for j in range(1, N_CHIPS):
            ox, oy = _other_chip(j)
            landed = blocks[t](4 * ox + 2 * oy + c)
            _remote(landed, landed, ssem.at[t, j], rsem.at[t, j], _sibling()).wait_recv()
            fwd = _remote(landed, landed, ssem.at[t, 3 + j], rsem.at[t, 3 + j], _sibling())
            fwd.start()
            sends.append(fwd)
    for t in range(nt):
        got = blocks[t](4 * x + 2 * y + (1 - c))
        _remote(got, got, ssem.at[t, 0], rsem.at[t, 0], _sibling()).wait_recv()
        for j in range(1, N_CHIPS):
            ox, oy = _other_chip(j)
            got = blocks[t](4 * ox + 2 * oy + (1 - c))
            _remote(got, got, ssem.at[t, 3 + j], rsem.at[t, 3 + j], _sibling()).wait_recv()
    for cp in sends:
        cp.wait_send()
    for cp in local:
        cp.wait()


def _gather_sems(nt):
    return [pltpu.SemaphoreType.DMA((nt, N_DEV - 1)), pltpu.SemaphoreType.DMA((nt, N_DEV - 1)),
            pltpu.SemaphoreType.DMA((nt,))]


def _all_gather(shards, axes):
    nt = len(shards)

    def body(*refs):
        ins, outs = refs[:nt], refs[nt:2 * nt]
        ssem, rsem, lsem = refs[2 * nt:]
        blocks = [functools.partial(_slab, outs[t], width=shards[t].shape[axes[t]], axis=axes[t]) for t in range(nt)]
        _two_level_gather(ins, blocks, ssem, rsem, lsem)

    out_shape = []
    for t in range(nt):
        s = list(shards[t].shape)
        s[axes[t]] *= N_DEV
        out_shape.append(jax.ShapeDtypeStruct(tuple(s), shards[t].dtype))
    return pl.pallas_call(
        body,
        in_specs=[pl.BlockSpec(memory_space=pltpu.HBM)] * nt,
        out_specs=[pl.BlockSpec(memory_space=pltpu.HBM)] * nt,
        out_shape=out_shape, scratch_shapes=_gather_sems(nt),
        name="all_gather_weights")(*shards)


def _exchange_with_sibling(grads, axes):
    nt = len(grads)

    def body(*refs):
        ins = [refs[t * DEPTH:(t + 1) * DEPTH] for t in range(nt)]
        outs = refs[nt * DEPTH: nt * DEPTH + nt]
        ssem, rsem = refs[nt * DEPTH + nt:]
        c = lax.axis_index("c")
        for t in range(nt):
            width = grads[t][0].shape[axes[t]] // N_DEV
            for q in range(N_CHIPS):
                for l in range(DEPTH):
                    _remote(_slab(ins[t][l], 2 * q + (1 - c), width, axes[t]), outs[t].at[q, l],
                            ssem.at[t], rsem.at[t], _sibling()).start()
        for t in range(nt):
            _remote(outs[t], outs[t], ssem.at[t], rsem.at[t], _sibling()).wait()

    out_shape = []
    for t in range(nt):
        s = list(grads[t][0].shape)
        s[axes[t]] //= N_DEV
        out_shape.append(jax.ShapeDtypeStruct((N_CHIPS, DEPTH, s[0], s[1]), F32))
    flat = [g for per_type in grads for g in per_type]
    return pl.pallas_call(
        body,
        in_specs=[pl.BlockSpec(memory_space=pltpu.HBM)] * len(flat),
        out_specs=[pl.BlockSpec(memory_space=pltpu.HBM)] * nt,
        out_shape=out_shape,
        scratch_shapes=[pltpu.SemaphoreType.DMA((nt,)), pltpu.SemaphoreType.DMA((nt,))],
        name="grads_to_sibling")(*flat)


def _chip_sum(grad, got, layer, axis, core, name):
    _, _, r, c = got.shape
    tr = min(r, 512)

    def body(core_ref, g_ref, s_ref, o_ref):
        o_ref[0] = (g_ref[...] + s_ref[0, 0]).astype(BF16)

    if axis == 1:
        g_spec = pl.BlockSpec((tr, c), lambda q, i, core_ref: (i, 2 * q + core_ref[0]))
    else:
        g_spec = pl.BlockSpec((tr, c), lambda q, i, core_ref: ((2 * q + core_ref[0]) * (r // tr) + i, 0))
    return pl.pallas_call(
        body,
        grid_spec=pltpu.PrefetchScalarGridSpec(
            num_scalar_prefetch=1, grid=(N_CHIPS, r // tr),
            in_specs=[g_spec, pl.BlockSpec((1, 1, tr, c), lambda q, i, core_ref: (q, layer, i, 0))],
            out_specs=pl.BlockSpec((1, tr, c), lambda q, i, core_ref: (q, i, 0))),
        out_shape=jax.ShapeDtypeStruct((N_CHIPS, r, c), BF16),
        compiler_params=_cp(("parallel", "parallel"), 40), name=name)(core, grad, got)


def _exchange_chip_sums(sums, small):
    nt = len(sums)

    def body(*refs):
        ins = [refs[t * DEPTH:(t + 1) * DEPTH] for t in range(nt)]
        small_ref = refs[nt * DEPTH]
        outs = refs[nt * DEPTH + 1: nt * DEPTH + 1 + nt]
        small_out = refs[nt * DEPTH + 1 + nt]
        ssem, rsem, lsem, g_ssem, g_rsem, g_lsem = refs[nt * DEPTH + 2 + nt:]
        c = lax.axis_index("c")
        chip = _my_chip()
        for t in range(nt):
            for l in range(DEPTH):
                pltpu.make_async_copy(ins[t][l].at[chip], outs[t].at[chip, l], lsem.at[t]).start()
            for j in range(1, N_CHIPS):
                other = jnp.bitwise_xor(chip, j)
                for l in range(DEPTH):
                    _remote(ins[t][l].at[other], outs[t].at[chip, l], ssem.at[t, j - 1], rsem.at[t, j - 1],
                            (*_other_chip(j), c)).start()
        _two_level_gather([small_ref], [lambda idx: small_out.at[idx]], g_ssem, g_rsem, g_lsem)
        for t in range(nt):
            pltpu.make_async_copy(outs[t].at[chip], outs[t].at[chip], lsem.at[t]).wait()
            for j in range(1, N_CHIPS):
                other = jnp.bitwise_xor(chip, j)
                _remote(outs[t].at[other], outs[t].at[other], ssem.at[t, j - 1], rsem.at[t, j - 1],
                        (*_other_chip(j), c)).wait()

    out_shape = []
    for t in range(nt):
        _, r, c = sums[t][0].shape
        out_shape.append(jax.ShapeDtypeStruct((N_CHIPS, DEPTH, r, c), BF16))
    out_shape.append(jax.ShapeDtypeStruct((N_DEV,) + small.shape, F32))
    flat = [s for per_type in sums for s in per_type]
    return pl.pallas_call(
        body,
        in_specs=[pl.BlockSpec(memory_space=pltpu.HBM)] * (len(flat) + 1),
        out_specs=[pl.BlockSpec(memory_space=pltpu.HBM)] * (nt + 1),
        out_shape=out_shape,
        scratch_shapes=[pltpu.SemaphoreType.DMA((nt, N_CHIPS - 1)), pltpu.SemaphoreType.DMA((nt, N_CHIPS - 1)),
                        pltpu.SemaphoreType.DMA((nt,))] + _gather_sems(1),
        name="chip_sums_over_ici")(*flat, small)


def _adamw_math(w, g, m, v):
    m = ADAM_B1 * m + (1.0 - ADAM_B1) * g
    v = ADAM_B2 * v + (1.0 - ADAM_B2) * (g * g)
    m_hat = m / (1.0 - ADAM_B1 ** ADAM_STEP)
    v_hat = v / (1.0 - ADAM_B2 ** ADAM_STEP)
    delta = -ADAM_LR * (m_hat / (jnp.sqrt(v_hat) + ADAM_EPS) + ADAM_WD * w)
    return delta, m, v


def _adamw_big(recv, w, m, v, tk, name):
    _, k, n = w.shape
    npad = recv.shape[3]

    def body(r_ref, w_ref, m_ref, v_ref, g_out, d_out, m_out, v_out):
        g = r_ref[0, 0].astype(F32)
        for s in range(1, N_CHIPS):
            g = g + r_ref[s, 0].astype(F32)
        g = g[:, :n]
        delta, mn, vn = _adamw_math(w_ref[0], g, m_ref[0], v_ref[0])
        g_out[0] = g
        d_out[0] = delta
        m_out[0] = mn
        v_out[0] = vn

    blk = pl.BlockSpec((1, tk, n), lambda l, i: (l, i, 0))
    sds = jax.ShapeDtypeStruct(w.shape, F32)
    return pl.pallas_call(
        body, grid=(DEPTH, k // tk),
        in_specs=[pl.BlockSpec((N_CHIPS, 1, tk, npad), lambda l, i: (0, l, i, 0)), blk, blk, blk],
        out_specs=[blk, blk, blk, blk], out_shape=[sds, sds, sds, sds],
        compiler_params=_cp(("parallel", "parallel"), 48), name=name)(recv, w, m, v)


def _adamw_small(recv, w, m, v):
    rows = w.shape[0]
    tr = rows // SMALL_TILES

    def body(r_ref, w_ref, m_ref, v_ref, g_out, d_out, m_out, v_out):
        g = r_ref[0]
        for s in range(1, N_DEV):
            g = g + r_ref[s]
        delta, mn, vn = _adamw_math(w_ref[...], g, m_ref[...], v_ref[...])
        g_out[...] = g
        d_out[...] = delta
        m_out[...] = mn
        v_out[...] = vn

    blk = pl.BlockSpec((tr, LANES), lambda i: (i, 0))
    sds = jax.ShapeDtypeStruct(w.shape, F32)
    return pl.pallas_call(
        body, grid=(SMALL_TILES,),
        in_specs=[pl.BlockSpec((N_DEV, tr, LANES), lambda i: (0, i, 0)), blk, blk, blk],
        out_specs=[blk, blk, blk, blk], out_shape=[sds, sds, sds, sds],
        compiler_params=_cp(("parallel",), 40), name="adamw_small")(recv, w, m, v)


def _rms(t):
    return lax.rsqrt(jnp.mean(t * t, axis=-1, keepdims=True) + EPS)


def _rms_bwd(t, r, gain, dh, dres):
    u = dh * gain
    dt = dres + r * u - t * ((r * r * r) * (1.0 / D_MODEL) * jnp.sum(t * u, axis=-1, keepdims=True))
    return dt, dh * t * r


def _in_proj(x, gain, w):
    L = x.shape[0]
    n = w.shape[1]
    tm, halves = 512, 2
    tn = n // halves

    def body(x_ref, g_ref, w_hbm, z_ref, h_ref, w_scr):
        i, j = pl.program_id(0), pl.program_id(1)

        @pl.when((i == 0) & (j == 0))
        def _():
            pltpu.sync_copy(w_hbm, w_scr)

        @pl.when(j == 0)
        def _():
            t = x_ref[...]
            h_ref[...] = (t * _rms(t) * g_ref[...]).astype(BF16)

        for jj in range(halves):
            @pl.when(j == jj)
            def _(jj=jj):
                z_ref[...] = _nn(h_ref[...], w_scr[:, jj * tn:(jj + 1) * tn])

    return pl.pallas_call(
        body, grid=(L // tm, halves),
        in_specs=[pl.BlockSpec((tm, D_MODEL), lambda i, j: (i, 0)), pl.BlockSpec((1, D_MODEL), lambda i, j: (0, 0)),
                  pl.BlockSpec(memory_space=pltpu.HBM)],
        out_specs=[pl.BlockSpec((tm, tn), lambda i, j: (i, j)), pl.BlockSpec((tm, D_MODEL), lambda i, j: (i, 0))],
        out_shape=[jax.ShapeDtypeStruct((L, n), F32), jax.ShapeDtypeStruct((L, D_MODEL), BF16)],
        scratch_shapes=[pltpu.VMEM((D_MODEL, n), BF16)],
        compiler_params=_cp(("arbitrary", "arbitrary"), 56), name="in_proj")(x, gain, w)


PIECE_W = 512


def _piece_columns(pieces):
    cols = []
    for p, arr in enumerate(pieces):
        cols.extend((p, off) for off in range(0, arr.shape[1], PIECE_W))
    return cols


def _in_proj_bwd(pieces, w, x, gain, dres):
    L = x.shape[0]
    n = w.shape[1]
    tm = 512
    npc = len(pieces)
    cols = _piece_columns(pieces)
    per_dot = 4

    def body(*refs):
        dz_refs = refs[:npc]
        w_hbm, x_ref, g_ref, dr_ref, dx_ref, dg_ref, w_scr = refs[npc:]

        @pl.when(pl.program_id(0) == 0)
        def _():
            pltpu.sync_copy(w_hbm, w_scr)
            dg_ref[...] = jnp.zeros_like(dg_ref)

        dh = None
        for c0 in range(0, len(cols), per_dot):
            chunk = cols[c0:c0 + per_dot]
            parts = [dz_refs[p][:, off:off + PIECE_W].astype(BF16) for p, off in chunk]
            term = _nt(jnp.concatenate(parts, axis=1), w_scr[:, c0 * PIECE_W:(c0 + len(chunk)) * PIECE_W])
            dh = term if dh is None else dh + term
        t = x_ref[...]
        dt, dgt = _rms_bwd(t, _rms(t), g_ref[...], dh, dr_ref[...])
        dx_ref[...] = dt
        dg_ref[...] += jnp.sum(dgt, axis=0, keepdims=True)

    row = pl.BlockSpec((tm, D_MODEL), lambda i: (i, 0))
    vec = pl.BlockSpec((1, D_MODEL), lambda i: (0, 0))
    piece_specs = [pl.BlockSpec((tm, arr.shape[1]), lambda i: (i, 0)) for arr in pieces]
    return pl.pallas_call(
        body, grid=(L // tm,),
        in_specs=piece_specs + [pl.BlockSpec(memory_space=pltpu.HBM), row, vec, row],
        out_specs=[row, vec],
        out_shape=[jax.ShapeDtypeStruct((L, D_MODEL), F32), jax.ShapeDtypeStruct((1, D_MODEL), F32)],
        scratch_shapes=[pltpu.VMEM((D_MODEL, n), BF16)],
        compiler_params=_cp(("arbitrary",), 60), name="in_proj_bwd")(*pieces, w, x, gain, dres)


def _tn(a, b, name):
    m, na = a.shape
    nb = b.shape[1]
    ta, tb, tm = min(na, 1024), min(nb, 1024), 2048
    nm = m // tm

    def body(a_ref, b_ref, o_ref):
        @pl.when(pl.program_id(2) == 0)
        def _():
            o_ref[...] = jnp.zeros_like(o_ref)
        o_ref[...] += _tn_dot(a_ref[...].astype(BF16), b_ref[...].astype(BF16))

    return pl.pallas_call(
        body, grid=(na // ta, nb // tb, nm),
        in_specs=[pl.BlockSpec((tm, ta), lambda i, j, k: (k, i)), pl.BlockSpec((tm, tb), lambda i, j, k: (k, j))],
        out_specs=pl.BlockSpec((ta, tb), lambda i, j, k: (i, j)),
        out_shape=jax.ShapeDtypeStruct((na, nb), F32),
        compiler_params=_cp(("parallel", "parallel", "arbitrary"), 48), name=name)(a, b)


def _tn_pieces(a, pieces, name):
    m, na = a.shape
    npc = len(pieces)
    cols = _piece_columns(pieces)
    per_block = D_MODEL // PIECE_W
    nj = len(cols) // per_block
    tm = 1024
    nm = m // tm
    block_of_piece = {}
    for c, (p, _) in enumerate(cols):
        block_of_piece[p] = c // per_block

    def body(*refs):
        a_ref = refs[0]
        b_refs = refs[1:1 + npc]
        o_ref = refs[1 + npc]
        j = pl.program_id(0)

        @pl.when(pl.program_id(1) == 0)
        def _():
            o_ref[...] = jnp.zeros_like(o_ref)

        for jj in range(nj):
            @pl.when(j == jj)
            def _(jj=jj):
                parts = [b_refs[p][:, off:off + PIECE_W].astype(BF16)
                         for p, off in cols[jj * per_block:(jj + 1) * per_block]]
                o_ref[...] += _tn_dot(a_ref[...], jnp.concatenate(parts, axis=1))

    piece_specs = [pl.BlockSpec((tm, arr.shape[1]),
                                functools.partial(lambda j, k, jj: (jnp.where(j == jj, k, 0), 0), jj=block_of_piece[p]))
                   for p, arr in enumerate(pieces)]
    return pl.pallas_call(
        body, grid=(nj, nm),
        in_specs=[pl.BlockSpec((tm, na), lambda j, k: (k, 0))] + piece_specs,
        out_specs=pl.BlockSpec((na, D_MODEL), lambda j, k: (0, j)),
        out_shape=jax.ShapeDtypeStruct((na, D_MODEL * nj), F32),
        compiler_params=_cp(("parallel", "arbitrary"), 56), name=name)(a, *pieces)


def _loss_grad(xf, target):
    L = xf.shape[0]
    tm = 1024

    def body(x_ref, t_ref, dy_ref, l_ref):
        e = x_ref[...] - t_ref[...]
        dy_ref[...] = e * (1.0 / D_MODEL)

        @pl.when(pl.program_id(0) == 0)
        def _():
            l_ref[...] = jnp.zeros_like(l_ref)
        l_ref[...] += jnp.sum(jnp.sum(e * e, axis=1, keepdims=True), axis=0, keepdims=True) * (0.5 / D_MODEL)

    row = pl.BlockSpec((tm, D_MODEL), lambda i: (i, 0))
    return pl.pallas_call(
        body, grid=(L // tm,), in_specs=[row, row],
        out_specs=[row, pl.BlockSpec((1, 1), lambda i: (0, 0))],
        out_shape=[jax.ShapeDtypeStruct((L, D_MODEL), F32), jax.ShapeDtypeStruct((1, 1), F32)],
        compiler_params=_cp(("arbitrary",), 40), name="loss_grad")(xf, target)


def _ffn_fwd(x, gain, wg, wu, wd):
    L = x.shape[0]
    ff = wg.shape[1]
    tm, tf = 1024, 2 * FF_CHUNK
    nf = ff // tf

    def body(x_ref, g_ref, wg_ref, wu_ref, wd_ref, o_ref, h_scr, acc):
        c = pl.program_id(1)

        @pl.when(c == 0)
        def _():
            t = x_ref[...]
            h_scr[...] = (t * _rms(t) * g_ref[...]).astype(BF16)
            acc[...] = jnp.zeros_like(acc)

        h = h_scr[...]
        down = []
        for cols in (slice(0, FF_CHUNK), slice(FF_CHUNK, 2 * FF_CHUNK)):
            gate = _nn(h, wg_ref[:, cols])
            up = _nn(h, wu_ref[:, cols])
            hid = gate * _sigmoid(gate) * up
            down.append(_nn(hid.astype(BF16), wd_ref[cols, :]))
        acc[...] += down[0] + down[1]

        @pl.when(c == nf - 1)
        def _():
            o_ref[...] = x_ref[...] + acc[...]

    row = pl.BlockSpec((tm, D_MODEL), lambda i, c: (i, 0))
    return pl.pallas_call(
        body, grid=(L // tm, nf),
        in_specs=[row, pl.BlockSpec((1, D_MODEL), lambda i, c: (0, 0)),
                  pl.BlockSpec((D_MODEL, tf), lambda i, c: (0, c)), pl.BlockSpec((D_MODEL, tf), lambda i, c: (0, c)),
                  pl.BlockSpec((tf, D_MODEL), lambda i, c: (c, 0))],
        out_specs=row, out_shape=jax.ShapeDtypeStruct((L, D_MODEL), F32),
        scratch_shapes=[pltpu.VMEM((tm, D_MODEL), BF16), pltpu.VMEM((tm, D_MODEL), F32)],
        compiler_params=_cp(("parallel", "arbitrary"), 48), name="ffn_fwd")(x, gain, wg, wu, wd)


def _ffn_bwd(x, gain, wg, wu, wd, dxo):
    L = x.shape[0]
    ff = wg.shape[1]
    tm, tf = 512, 2 * FF_CHUNK
    nf = ff // tf

    def body(x_ref, g_ref, wg_ref, wu_ref, wd_ref, dxo_ref, dx_ref, h_ref, hid_ref, dgate_ref, dup_ref, dg_ref,
             acc, dxo_b):
        i, c = pl.program_id(0), pl.program_id(1)

        @pl.when(c == 0)
        def _():
            t = x_ref[...]
            h_ref[...] = (t * _rms(t) * g_ref[...]).astype(BF16)
            acc[...] = jnp.zeros_like(acc)
            dxo_b[...] = dxo_ref[...].astype(BF16)

        h = h_ref[...]
        back = []
        for cols in (slice(0, FF_CHUNK), slice(FF_CHUNK, 2 * FF_CHUNK)):
            gate = _nn(h, wg_ref[:, cols])
            up = _nn(h, wu_ref[:, cols])
            sg = _sigmoid(gate)
            silu = gate * sg
            hid_ref[:, cols] = (silu * up).astype(BF16)
            dhid = _nt(dxo_b[...], wd_ref[cols, :])
            dup = (dhid * silu).astype(BF16)
            dgate = (dhid * up * (sg * (1.0 + gate * (1.0 - sg)))).astype(BF16)
            dup_ref[:, cols] = dup
            dgate_ref[:, cols] = dgate
            back.append(_nt(dgate, wg_ref[:, cols]) + _nt(dup, wu_ref[:, cols]))
        acc[...] += back[0] + back[1]

        @pl.when(c == nf - 1)
        def _():
            t = x_ref[...]
            dt, dgt = _rms_bwd(t, _rms(t), g_ref[...], acc[...], dxo_ref[...])
            dx_ref[...] = dt

            @pl.when(i == 0)
            def _():
                dg_ref[...] = jnp.zeros_like(dg_ref)
            dg_ref[...] += jnp.sum(dgt, axis=0, keepdims=True)

    row = pl.BlockSpec((tm, D_MODEL), lambda i, c: (i, 0))
    vec = pl.BlockSpec((1, D_MODEL), lambda i, c: (0, 0))
    wcol = pl.BlockSpec((D_MODEL, tf), lambda i, c: (0, c))
    hcol = pl.BlockSpec((tm, tf), lambda i, c: (i, c))
    return pl.pallas_call(
        body, grid=(L // tm, nf),
        in_specs=[row, vec, wcol, wcol, pl.BlockSpec((tf, D_MODEL), lambda i, c: (c, 0)), row],
        out_specs=[row, row, hcol, hcol, hcol, vec],
        out_shape=[jax.ShapeDtypeStruct((L, D_MODEL), F32), jax.ShapeDtypeStruct((L, D_MODEL), BF16),
                   jax.ShapeDtypeStruct((L, ff), BF16), jax.ShapeDtypeStruct((L, ff), BF16),
                   jax.ShapeDtypeStruct((L, ff), BF16), jax.ShapeDtypeStruct((1, D_MODEL), F32)],
        scratch_shapes=[pltpu.VMEM((tm, D_MODEL), F32), pltpu.VMEM((tm, D_MODEL), BF16)],
        compiler_params=_cp(("arbitrary", "arbitrary"), 56), name="ffn_bwd")(x, gain, wg, wu, wd, dxo)


GELU_K = math.sqrt(2.0 / math.pi)
GELU_C = 0.044715


def _gelu(y):
    return 0.5 * y * (1.0 + jnp.tanh(GELU_K * (y + GELU_C * (y * y * y))))


def _gelu_grad(y):
    th = jnp.tanh(GELU_K * (y + GELU_C * (y * y * y)))
    return 0.5 * (1.0 + th) + 0.5 * y * (1.0 - th * th) * (GELU_K * (1.0 + 3.0 * GELU_C * (y * y)))


def _merge_groups(o_refs, l_refs):
    ls = [r[...] for r in l_refs]
    os_ = [r[...] for r in o_refs]
    lmax = jnp.maximum(jnp.maximum(ls[0], ls[1]), ls[2])
    es = [jnp.exp(l - lmax) for l in ls]
    inv = 1.0 / (es[0] + es[1] + es[2])
    ws = [e * inv for e in es]
    a = ws[0] * os_[0] + ws[1] * os_[1] + ws[2] * os_[2]
    return ws, os_, a


def _mix_fwd(ols, y, z, x, wp, wa, wb, wo):
    L = x.shape[0]
    tm = 256

    def body(o0, l0, o1, l1, o2, l2, y_ref, ga_ref, gs_ref, x_ref, wp_ref, wa_ref, wb_ref, wo_ref, out_ref):
        _, _, a = _merge_groups((o0, o1, o2), (l0, l1, l2))
        a_out = _nn(a.astype(BF16), wp_ref[...])
        yg = _gelu(y_ref[...]).astype(BF16)
        s_out = _nn(yg, wa_ref[...]) * _sigmoid(_nn(yg, wb_ref[...]))
        mix = _sigmoid(ga_ref[...]) * a_out + _sigmoid(gs_ref[...]) * s_out
        out_ref[...] = x_ref[...] + _nn(mix.astype(BF16), wo_ref[...])

    half = pl.BlockSpec((tm, ATTN_W), lambda i: (i, 0))
    row = pl.BlockSpec((tm, D_MODEL), lambda i: (i, 0))
    w512 = pl.BlockSpec((ATTN_W, D_MODEL), lambda i: (0, 0))
    return pl.pallas_call(
        body, grid=(L // tm,),
        in_specs=[half] * 7 + [pl.BlockSpec((tm, D_MODEL), lambda i: (i, 5)),
                               pl.BlockSpec((tm, D_MODEL), lambda i: (i, 6)), row, w512, w512, w512,
                               pl.BlockSpec((D_MODEL, D_MODEL), lambda i: (0, 0))],
        out_specs=row, out_shape=jax.ShapeDtypeStruct((L, D_MODEL), F32),
        compiler_params=_cp(("parallel",), 48), name="mix_fwd")(*ols, y, z, z, x, wp, wa, wb, wo)


def _mix_bwd(dxm, ols, y, z, wp, wa, wb, wo):
    L = dxm.shape[0]
    tm = 256

    def body(dx_ref, o0, l0, o1, l1, o2, l2, y_ref, ga_ref, gs_ref, wp_ref, wa_ref, wb_ref, wo_ref,
             do0, dl0, do1, dl1, do2, dl2, dy_ref, dga_ref, dgs_ref, a_ref, yg_ref, mix_ref, dao_ref, dpa_ref,
             dpb_ref):
        ws, os_, a = _merge_groups((o0, o1, o2), (l0, l1, l2))
        ab = a.astype(BF16)
        a_out = _nn(ab, wp_ref[...])
        yv = y_ref[...]
        yg = _gelu(yv).astype(BF16)
        pa = _nn(yg, wa_ref[...])
        spb = _sigmoid(_nn(yg, wb_ref[...]))
        s_out = pa * spb
        sga = _sigmoid(ga_ref[...])
        sgs = _sigmoid(gs_ref[...])
        mix = sga * a_out + sgs * s_out
        dmix = _nt(dx_ref[...].astype(BF16), wo_ref[...])
        da_out = (sga * dmix).astype(BF16)
        ds_out = sgs * dmix
        dpa = (ds_out * spb).astype(BF16)
        dpb = (ds_out * pa * spb * (1.0 - spb)).astype(BF16)
        dga_ref[...] = (dmix * a_out * sga * (1.0 - sga)).astype(BF16)
        dgs_ref[...] = (dmix * s_out * sgs * (1.0 - sgs)).astype(BF16)
        dy_ref[...] = (_nt(dpa, wa_ref[...]) + _nt(dpb, wb_ref[...])) * _gelu_grad(yv)
        da = _nt(da_out, wp_ref[...])
        for w, o, do_ref, dl_ref in zip(ws, os_, (do0, do1, do2), (dl0, dl1, dl2)):
            do_ref[...] = w * da
            dl_ref[...] = -(w * da) * a
        a_ref[...] = ab
        yg_ref[...] = yg
        mix_ref[...] = mix.astype(BF16)
        dao_ref[...] = da_out
        dpa_ref[...] = dpa
        dpb_ref[...] = dpb

    half = pl.BlockSpec((tm, ATTN_W), lambda i: (i, 0))
    row = pl.BlockSpec((tm, D_MODEL), lambda i: (i, 0))
    w512 = pl.BlockSpec((ATTN_W, D_MODEL), lambda i: (0, 0))
    hf = jax.ShapeDtypeStruct((L, ATTN_W), F32)
    hb = jax.ShapeDtypeStruct((L, ATTN_W), BF16)
    rb = jax.ShapeDtypeStruct((L, D_MODEL), BF16)
    return pl.pallas_call(
        body, grid=(L // tm,),
        in_specs=[row] + [half] * 7 + [pl.BlockSpec((tm, D_MODEL), lambda i: (i, 5)),
                                       pl.BlockSpec((tm, D_MODEL), lambda i: (i, 6)), w512, w512, w512,
                                       pl.BlockSpec((D_MODEL, D_MODEL), lambda i: (0, 0))],
        out_specs=[half] * 7 + [row, row, half, half, row, row, row, row],
        out_shape=[hf] * 7 + [rb, rb, hb, hb, rb, rb, rb, rb],
        compiler_params=_cp(("parallel",), 56), name="mix_bwd")(dxm, *ols, y, z, z, wp, wa, wb, wo)


N_ATTN_ITERS = ATTN_ROWS // BLK


def _class_rows(ref, start, d):
    if d == 1:
        return ref[pl.ds(pl.multiple_of(start, BLK), BLK), :]
    return ref[pl.ds(start, BLK, stride=d), :]


def _set_class_rows(ref, start, d, val):
    if d == 1:
        ref[pl.ds(pl.multiple_of(start, BLK), BLK), :] = val
    else:
        ref[pl.ds(start, BLK, stride=d), :] = val


def _head_masks():
    lane = lax.broadcasted_iota(jnp.int32, (1, LANES), 1)
    m0 = (lane < HEAD_DIM).astype(F32)
    return m0, 1.0 - m0


def _head_norm(t, gain2, m0, m1):
    tt = t * t
    r0 = lax.rsqrt(jnp.sum(tt * m0, axis=-1, keepdims=True) * (1.0 / HEAD_DIM) + EPS)
    r1 = lax.rsqrt(jnp.sum(tt * m1, axis=-1, keepdims=True) * (1.0 / HEAD_DIM) + EPS)
    r = m0 * r0 + m1 * r1
    return t * r * gain2, r


def _head_norm_bwd(t, r, gain2, dy, m0, m1):
    u = dy * gain2
    tu = t * u
    s = m0 * jnp.sum(tu * m0, axis=-1, keepdims=True) + m1 * jnp.sum(tu * m1, axis=-1, keepdims=True)
    return r * u - t * (r * r * r) * s * (1.0 / HEAD_DIM), jnp.sum(dy * t * r, axis=0, keepdims=True)


def _band_masks():
    qi = lax.broadcasted_iota(jnp.int32, (BLK, 2 * BLK), 0)
    ki = lax.broadcasted_iota(jnp.int32, (BLK, 2 * BLK), 1)
    dist = BLK + qi - ki
    return (dist >= 0) & (dist <= BLK), ki >= BLK


ATTN_SCALE = HEAD_DIM ** -0.5


def _attn_scores(qm, kw, ok):
    return jnp.where(ok, _nt(qm, kw), -1e30)


def _attn_probs(qm, kw, ok):
    s = _attn_scores(qm, kw, ok)
    mx = jnp.max(s, axis=-1, keepdims=True)
    p = jnp.exp(s - mx)
    den = jnp.sum(p, axis=-1, keepdims=True)
    return p, den, mx


NORM_ROWS = 256


def _norm_rows(src_ref, gain2, dst_ref, m0, m1):
    n = src_ref.shape[0]
    step = min(NORM_ROWS, n)
    for r0 in range(0, n, step):
        dst_ref[r0:r0 + step, :] = _head_norm(src_ref[r0:r0 + step, :], gain2, m0, m1)[0]


def _norm_rows_bwd(src_ref, gain2, dy_ref, dst_ref, m0, m1):
    n = src_ref.shape[0]
    step = min(NORM_ROWS, n)
    dgain = jnp.zeros((1, LANES), F32)
    for r0 in range(0, n, step):
        t = src_ref[r0:r0 + step, :]
        _, r = _head_norm(t, gain2, m0, m1)
        dt, dg = _head_norm_bwd(t, r, gain2, dy_ref[r0:r0 + step, :], m0, m1)
        dst_ref[r0:r0 + step, :] = dt.astype(dst_ref.dtype)
        dgain = dgain + dg
    return dgain


def _attn_operands(it, d, first_step, q_ref, kc_ref, kp_ref, vc_ref, vp_ref, band, is_cur):
    j = it // d
    start = (it - j * d) + (d * BLK) * j
    before = jnp.maximum(start - d * BLK, 0)
    inside = j > 0
    q2 = _class_rows(q_ref, start, d)
    kc2 = _class_rows(kc_ref, start, d)
    vc2 = _class_rows(vc_ref, start, d)
    kp2 = jnp.where(inside, _class_rows(kc_ref, before, d), _class_rows(kp_ref, it - j * d, d))
    vp2 = jnp.where(inside, _class_rows(vc_ref, before, d), _class_rows(vp_ref, it - j * d, d))
    has_prev = inside | jnp.logical_not(first_step)
    return start, q2, kp2, kc2, vp2, vc2, band & (is_cur | has_prev)


def _attn_specs(d, step_of):
    nq = N_ATTN_ITERS // d

    def cur(c):
        return pl.BlockSpec((ATTN_ROWS, LANES), lambda hp, n: (step_of(n), c + hp))

    def prev(c):
        return pl.BlockSpec((d * BLK, LANES), lambda hp, n: (jnp.maximum(step_of(n) * nq - 1, 0), c + hp))

    return cur, prev, pl.BlockSpec((1, LANES), lambda hp, n: (0, 0))


def _attn_fwd(z, gq2, gk2, group):
    L = z.shape[0]
    d = DILATIONS[group]
    nsb = L // ATTN_ROWS
    cq, ck, cv = group * 4, 12 + group * 4, 24 + group * 4

    def body(q_ref, kc_ref, kp_ref, vc_ref, vp_ref, gq_ref, gk_ref, o_ref, l_ref, qn_scr, kn_scr, kpn_scr):
        first_step = pl.program_id(1) == 0
        band, is_cur = _band_masks()
        m0, m1 = _head_masks()
        _norm_rows(q_ref, gq_ref[...], qn_scr, m0, m1)
        if d * BLK == ATTN_ROWS:
            @pl.when(first_step)
            def _():
                _norm_rows(kp_ref, gk_ref[...], kpn_scr, m0, m1)

            @pl.when(jnp.logical_not(first_step))
            def _():
                kpn_scr[...] = kn_scr[...]
        else:
            _norm_rows(kp_ref, gk_ref[...], kpn_scr, m0, m1)
        _norm_rows(kc_ref, gk_ref[...], kn_scr, m0, m1)

        def per_block(it, carry):
            start, qn, kpn, kcn, vp2, vc2, ok = _attn_operands(
                it, d, first_step, qn_scr, kn_scr, kpn_scr, vc_ref, vp_ref, band, is_cur)
            kw = jnp.concatenate([kpn, kcn], axis=0).astype(BF16)
            vw = jnp.concatenate([vp2, vc2], axis=0).astype(BF16)
            o2 = jnp.zeros((BLK, LANES), F32)
            l2 = jnp.zeros((BLK, LANES), F32)
            for mh in (m0, m1):
                p, den, mx = _attn_probs((qn * (mh * ATTN_SCALE)).astype(BF16), kw, ok)
                o2 = o2 + mh * (_nn(p.astype(BF16), vw) / den)
                l2 = l2 + mh * (mx + jnp.log(den))
            _set_class_rows(o_ref, start, d, o2)
            _set_class_rows(l_ref, start, d, l2)
            return carry

        lax.fori_loop(0, N_ATTN_ITERS // 2, lambda i, c: per_block(2 * i + 1, per_block(2 * i, c)), 0)

    cur, prev, vec = _attn_specs(d, lambda n: n)
    out = pl.BlockSpec((ATTN_ROWS, LANES), lambda hp, n: (n, hp))
    sds = jax.ShapeDtypeStruct((L, ATTN_W), F32)
    return pl.pallas_call(
        body, grid=(4, nsb),
        in_specs=[cur(cq), cur(ck), prev(ck), cur(cv), prev(cv), vec, vec],
        out_specs=[out, out], out_shape=[sds, sds],
        scratch_shapes=[pltpu.VMEM((ATTN_ROWS, LANES), F32), pltpu.VMEM((ATTN_ROWS, LANES), F32),
                        pltpu.VMEM((d * BLK, LANES), F32)],
        compiler_params=_cp(("parallel", "arbitrary"), 48), name=f"attn_fwd_g{group}")(z, z, z, z, z, gq2, gk2)


def _attn_bwd(z, gq2, gk2, lse, do, c, group):
    L = z.shape[0]
    d = DILATIONS[group]
    nsb = L // ATTN_ROWS
    cq, ck, cv = group * 4, 12 + group * 4, 24 + group * 4

    def body(q_ref, kc_ref, kp_ref, vc_ref, vp_ref, gq_ref, gk_ref, l_ref, do_ref, c_ref,
             dq_ref, dk_ref, dv_ref, dgq_ref, dgk_ref, ck_scr, cv_scr, qn_scr, kn_scr, kpn_scr, dqn_scr, dkn_scr):
        hp, n = pl.program_id(0), pl.program_id(1)
        first_step = n == nsb - 1
        band, is_cur = _band_masks()
        m0, m1 = _head_masks()
        gq, gk = gq_ref[...], gk_ref[...]
        _norm_rows(q_ref, gq, qn_scr, m0, m1)
        _norm_rows(kc_ref, gk, kn_scr, m0, m1)
        _norm_rows(kp_ref, gk, kpn_scr, m0, m1)

        @pl.when((hp == 0) & (n == 0))
        def _():
            dgq_ref[...] = jnp.zeros_like(dgq_ref)
            dgk_ref[...] = jnp.zeros_like(dgk_ref)

        @pl.when(n == 0)
        def _():
            ck_scr[...] = jnp.zeros_like(ck_scr)
            cv_scr[...] = jnp.zeros_like(cv_scr)

        def per_block(i, carry):
            it = N_ATTN_ITERS - 1 - i
            start, qn, kpn, kcn, vp2, vc2, ok = _attn_operands(
                it, d, first_step, qn_scr, kn_scr, kpn_scr, vc_ref, vp_ref, band, is_cur)
            r = it - (it // d) * d
            kw = jnp.concatenate([kpn, kcn], axis=0).astype(BF16)
            vw = jnp.concatenate([vp2, vc2], axis=0).astype(BF16)
            l2 = _class_rows(l_ref, start, d)
            c2 = _class_rows(c_ref, start, d)
            do2 = _class_rows(do_ref, start, d)
            dqn = jnp.zeros((BLK, LANES), F32)
            dkw = jnp.zeros((2 * BLK, LANES), F32)
            dvw = jnp.zeros((2 * BLK, LANES), F32)
            for mh in (m0, m1):
                qm = (qn * (mh * ATTN_SCALE)).astype(BF16)
                lse = jnp.max(jnp.where(mh > 0.5, l2, -3e38), axis=-1, keepdims=True)
                pn = jnp.exp(_attn_scores(qm, kw, ok) - lse)
                dohb = (do2 * mh).astype(BF16)
                dvw = dvw + _tn_dot(pn.astype(BF16), dohb)
                ds = (pn * (_nt(dohb, vw) + jnp.sum(c2 * mh, axis=-1, keepdims=True))).astype(BF16)
                dqn = dqn + (mh * ATTN_SCALE) * _nn(ds, kw)
                dkw = dkw + _tn_dot(ds, qm)
            _set_class_rows(dqn_scr, start, d, dqn)
            _set_class_rows(dkn_scr, start, d, ck_scr[r] + dkw[BLK:])
            _set_class_rows(dv_ref, start, d, cv_scr[r] + dvw[BLK:])
            ck_scr[r] = dkw[:BLK]
            cv_scr[r] = dvw[:BLK]
            return carry

        lax.fori_loop(0, N_ATTN_ITERS // 2, lambda i, c: per_block(2 * i + 1, per_block(2 * i, c)), 0)
        dgq_ref[...] += _norm_rows_bwd(q_ref, gq, dqn_scr, dq_ref, m0, m1)
        dgk_ref[...] += _norm_rows_bwd(kc_ref, gk, dkn_scr, dk_ref, m0, m1)

    cur, prev, vec = _attn_specs(d, lambda n: nsb - 1 - n)
    sds = jax.ShapeDtypeStruct((L, ATTN_W), F32)
    sdb = jax.ShapeDtypeStruct((L, ATTN_W), BF16)
    vsd = jax.ShapeDtypeStruct((1, LANES), F32)
    return pl.pallas_call(
        body, grid=(4, nsb),
        in_specs=[cur(cq), cur(ck), prev(ck), cur(cv), prev(cv), vec, vec, cur(0), cur(0), cur(0)],
        out_specs=[cur(0), cur(0), cur(0), vec, vec], out_shape=[sdb, sdb, sds, vsd, vsd],
        scratch_shapes=[pltpu.VMEM((d, BLK, LANES), F32), pltpu.VMEM((d, BLK, LANES), F32),
                        pltpu.VMEM((ATTN_ROWS, LANES), F32), pltpu.VMEM((ATTN_ROWS, LANES), F32),
                        pltpu.VMEM((d * BLK, LANES), F32),
                        pltpu.VMEM((ATTN_ROWS, LANES), F32), pltpu.VMEM((ATTN_ROWS, LANES), F32)],
        compiler_params=_cp(("arbitrary", "arbitrary"), 56),
        name=f"attn_bwd_g{group}")(z, z, z, z, z, gq2, gk2, lse, do, c)


BLOCK_STATES = SSM_STATES // SSM_BLOCKS
BLOCK_CH = SSM_W // SSM_BLOCKS
SLABS_PER_BLOCK = BLOCK_STATES // LANES


SCAN_STEPS = 8


def _store_block(bufs, b, val, tm):
    for s in range(SLABS_PER_BLOCK):
        k = SLABS_PER_BLOCK * b + s
        bufs[k % 2][pl.ds(8 + k // 2, tm, stride=8), :] = val[:, s * LANES:(s + 1) * LANES]


def _load_block(bufs, b, tm):
    tiles = []
    for s in range(SLABS_PER_BLOCK):
        k = SLABS_PER_BLOCK * b + s
        tiles.append(bufs[k % 2][pl.ds(8 + k // 2, tm, stride=8), :])
    return jnp.concatenate(tiles, axis=1).astype(BF16)


def _ssm_project_in(ub, bdr_ref, bdi_ref, sr, si, tm):
    for b in range(SSM_BLOCKS):
        ubb = ub[:, b * BLOCK_CH:(b + 1) * BLOCK_CH]
        _store_block(sr, b, _nn(ubb, bdr_ref[b]), tm)
        _store_block(si, b, _nn(ubb, bdi_ref[b]), tm)


def _ssm_scan(a, x0, sr, si, tm):
    ar0, ar1, ai0, ai1 = a
    sr[0][0:8, :], sr[1][0:8, :], si[0][0:8, :], si[1][0:8, :] = x0

    def steps(it, c):
        xr0, xr1, xi0, xi1 = c
        base = it * (8 * SCAN_STEPS) + 8
        for q in range(SCAN_STEPS):
            rows = pl.ds(pl.multiple_of(base + 8 * q, 8), 8)
            nr0 = ar0 * xr0 - ai0 * xi0 + sr[0][rows, :]
            ni0 = ar0 * xi0 + ai0 * xr0 + si[0][rows, :]
            nr1 = ar1 * xr1 - ai1 * xi1 + sr[1][rows, :]
            ni1 = ar1 * xi1 + ai1 * xr1 + si[1][rows, :]
            sr[0][rows, :] = nr0
            si[0][rows, :] = ni0
            sr[1][rows, :] = nr1
            si[1][rows, :] = ni1
            xr0, xr1, xi0, xi1 = nr0, nr1, ni0, ni1
        return xr0, xr1, xi0, xi1

    return lax.fori_loop(0, tm // SCAN_STEPS, steps, x0)


def _load_a(ar_ref, ai_ref):
    return ar_ref[:, :LANES], ar_ref[:, LANES:], ai_ref[:, :LANES], ai_ref[:, LANES:]


def _ssm_fwd(z, ar8, ai8, bdr, bdi, cdr, cdi, dsk):
    L = z.shape[0]
    tm = SSM_TM
    nc = L // tm

    def body(u_ref, ar_ref, ai_ref, bdr_ref, bdi_ref, cdr_ref, cdi_ref, dsk_ref, y_ref, cin_ref,
             sr0, sr1, si0, si1, car):
        sr, si = (sr0, sr1), (si0, si1)

        @pl.when(pl.program_id(0) == 0)
        def _():
            car[...] = jnp.zeros_like(car)

        u = u_ref[...]
        _ssm_project_in(u.astype(BF16), bdr_ref, bdi_ref, sr, si, tm)
        cin_ref[0] = car[...]
        xr0, xr1, xi0, xi1 = _ssm_scan(_load_a(ar_ref, ai_ref), (car[0], car[1], car[2], car[3]), sr, si, tm)
        car[0], car[1], car[2], car[3] = xr0, xr1, xi0, xi1
        for b in range(SSM_BLOCKS):
            cols = slice(b * BLOCK_CH, (b + 1) * BLOCK_CH)
            y_ref[:, cols] = (dsk_ref[:, cols] * u[:, cols] + _nn(_load_block(sr, b, tm), cdr_ref[b])
                              - _nn(_load_block(si, b, tm), cdi_ref[b]))

    def const(shape):
        return pl.BlockSpec(shape, lambda i: (0,) * len(shape))

    state = pltpu.VMEM(((tm + 1) * 8, LANES), F32)
    wb = const((SSM_BLOCKS, BLOCK_CH, BLOCK_STATES))
    wc = const((SSM_BLOCKS, BLOCK_STATES, BLOCK_CH))
    return pl.pallas_call(
        body, grid=(nc,),
        in_specs=[pl.BlockSpec((tm, SSM_W), lambda i: (i, COL_U)), const((8, 256)), const((8, 256)),
                  wb, wb, wc, wc, const((1, SSM_W))],
        out_specs=[pl.BlockSpec((tm, SSM_W), lambda i: (i, 0)), pl.BlockSpec((1, 4, 8, LANES), lambda i: (i, 0, 0, 0))],
        out_shape=[jax.ShapeDtypeStruct((L, SSM_W), F32), jax.ShapeDtypeStruct((nc, 4, 8, LANES), F32)],
        scratch_shapes=[state, state, state, state, pltpu.VMEM((4, 8, LANES), F32)],
        compiler_params=_cp(("arbitrary",), 48), name="ssm_fwd")(z, ar8, ai8, bdr, bdi, cdr, cdi, dsk)


def _ssm_bwd(z, dy, cin, ar8, ai8, bdr, bdi, cdr, cdi, dsk):
    L = z.shape[0]
    tm = SSM_TM
    nc = L // tm

    def body(u_ref, dy_ref, cin_ref, ar_ref, ai_ref, dsk_ref, bdr_ref, bdi_ref, cdr_ref, cdi_ref,
             du_ref, da_ref, dds_ref, dbdr_ref, dbdi_ref, dcdr_ref, dcdi_ref,
             sr0, sr1, si0, si1, gr0, gr1, gi0, gi1, carg):
        sr, si, gr, gi = (sr0, sr1), (si0, si1), (gr0, gr1), (gi0, gi1)

        @pl.when(pl.program_id(0) == 0)
        def _():
            carg[...] = jnp.zeros_like(carg)
            for ref in (da_ref, dds_ref, dbdr_ref, dbdi_ref, dcdr_ref, dcdi_ref):
                ref[...] = jnp.zeros_like(ref)

        u = u_ref[...]
        ub = u.astype(BF16)
        dyv = dy_ref[...]
        dyb = dyv.astype(BF16)
        a = _load_a(ar_ref, ai_ref)
        ar0, ar1, ai0, ai1 = a
        x_in = (cin_ref[0, 0], cin_ref[0, 1], cin_ref[0, 2], cin_ref[0, 3])
        _ssm_project_in(ub, bdr_ref, bdi_ref, sr, si, tm)
        _ssm_scan(a, x_in, sr, si, tm)
        for b in range(SSM_BLOCKS):
            dyb_b = dyb[:, b * BLOCK_CH:(b + 1) * BLOCK_CH]
            _store_block(gr, b, _nt(dyb_b, cdr_ref[b]), tm)
            _store_block(gi, b, -_nt(dyb_b, cdi_ref[b]), tm)

        def grad_steps(it, c):
            (nr0, nr1, ni0, ni1), (d_r0, d_r1, d_i0, d_i1) = c
            base = (tm - SCAN_STEPS * (it + 1)) * 8
            for q in reversed(range(SCAN_STEPS)):
                prev = pl.ds(pl.multiple_of(base + 8 * q, 8), 8)
                rows = pl.ds(pl.multiple_of(base + 8 * q + 8, 8), 8)
                g_r0 = gr[0][rows, :] + ar0 * nr0 + ai0 * ni0
                g_i0 = gi[0][rows, :] + ar0 * ni0 - ai0 * nr0
                g_r1 = gr[1][rows, :] + ar1 * nr1 + ai1 * ni1
                g_i1 = gi[1][rows, :] + ar1 * ni1 - ai1 * nr1
                gr[0][rows, :] = g_r0
                gi[0][rows, :] = g_i0
                gr[1][rows, :] = g_r1
                gi[1][rows, :] = g_i1
                pr0, pr1, pi0, pi1 = sr[0][prev, :], sr[1][prev, :], si[0][prev, :], si[1][prev, :]
                d_r0 = d_r0 + pr0 * g_r0 + pi0 * g_i0
                d_r1 = d_r1 + pr1 * g_r1 + pi1 * g_i1
                d_i0 = d_i0 + pr0 * g_i0 - pi0 * g_r0
                d_i1 = d_i1 + pr1 * g_i1 - pi1 * g_r1
                nr0, nr1, ni0, ni1 = g_r0, g_r1, g_i0, g_i1
            return (nr0, nr1, ni0, ni1), (d_r0, d_r1, d_i0, d_i1)

        acc0 = (da_ref[0], da_ref[1], da_ref[2], da_ref[3])
        g_first, acc = lax.fori_loop(0, tm // SCAN_STEPS, grad_steps,
                                     ((carg[0], carg[1], carg[2], carg[3]), acc0))
        carg[0], carg[1], carg[2], carg[3] = g_first
        da_ref[0], da_ref[1], da_ref[2], da_ref[3] = acc

        for b in range(SSM_BLOCKS):
            cols = slice(b * BLOCK_CH, (b + 1) * BLOCK_CH)
            grb, gib = _load_block(gr, b, tm), _load_block(gi, b, tm)
            du_ref[:, cols] = (dsk_ref[:, cols] * dyv[:, cols] + _nt(grb, bdr_ref[b])
                               + _nt(gib, bdi_ref[b])).astype(BF16)
            dbdr_ref[b] += _tn_dot(ub[:, cols], grb)
            dbdi_ref[b] += _tn_dot(ub[:, cols], gib)
            dcdr_ref[b] += _tn_dot(_load_block(sr, b, tm), dyb[:, cols])
            dcdi_ref[b] -= _tn_dot(_load_block(si, b, tm), dyb[:, cols])
        dds_ref[...] += jnp.sum(dyv * u, axis=0, keepdims=True)

    def const(shape):
        return pl.BlockSpec(shape, lambda i: (0,) * len(shape))

    state = pltpu.VMEM(((tm + 1) * 8, LANES), F32)
    wb = const((SSM_BLOCKS, BLOCK_CH, BLOCK_STATES))
    wc = const((SSM_BLOCKS, BLOCK_STATES, BLOCK_CH))
    return pl.pallas_call(
        body, grid=(nc,),
        in_specs=[pl.BlockSpec((tm, SSM_W), lambda i: (nc - 1 - i, COL_U)),
                  pl.BlockSpec((tm, SSM_W), lambda i: (nc - 1 - i, 0)),
                  pl.BlockSpec((1, 4, 8, LANES), lambda i: (nc - 1 - i, 0, 0, 0)),
                  const((8, 256)), const((8, 256)), const((1, SSM_W)), wb, wb, wc, wc],
        out_specs=[pl.BlockSpec((tm, SSM_W), lambda i: (nc - 1 - i, 0)), const((4, 8, LANES)), const((1, SSM_W)),
                   wb, wb, wc, wc],
        out_shape=[jax.ShapeDtypeStruct((L, SSM_W), BF16), jax.ShapeDtypeStruct((4, 8, LANES), F32),
                   jax.ShapeDtypeStruct((1, SSM_W), F32),
                   jax.ShapeDtypeStruct((SSM_BLOCKS, BLOCK_CH, BLOCK_STATES), F32),
                   jax.ShapeDtypeStruct((SSM_BLOCKS, BLOCK_CH, BLOCK_STATES), F32),
                   jax.ShapeDtypeStruct((SSM_BLOCKS, BLOCK_STATES, BLOCK_CH), F32),
                   jax.ShapeDtypeStruct((SSM_BLOCKS, BLOCK_STATES, BLOCK_CH), F32)],
        scratch_shapes=[state] * 8 + [pltpu.VMEM((4, 8, LANES), F32)],
        compiler_params=_cp(("arbitrary",), 56), name="ssm_bwd")(z, dy, cin, ar8, ai8, dsk, bdr, bdi, cdr, cdi)


def _discretise(lam_re, lam_im, log_dt, b_re, b_im):
    dt = jnp.exp(log_dt)[:, None]
    mag = jnp.exp(lam_re * dt)
    ang = lam_im * dt
    abar_re = mag * jnp.cos(ang)
    abar_im = mag * jnp.sin(ang)
    nr = abar_re - 1.0
    ni = abar_im
    den = lam_re * lam_re + lam_im * lam_im
    cr = ((nr * lam_re + ni * lam_im) / den)[..., None]
    ci = ((ni * lam_re - nr * lam_im) / den)[..., None]
    return abar_re, abar_im, cr * b_re - ci * b_im, cr * b_im + ci * b_re


GROUPS_PER_BLOCK = 8


def _block_diag_in(bbar):
    eye = jnp.eye(GROUPS_PER_BLOCK, dtype=F32)
    return jnp.einsum("igpc,gh->igchp", bbar.reshape(SSM_BLOCKS, GROUPS_PER_BLOCK, 64, 16), eye).reshape(
        SSM_BLOCKS, BLOCK_CH, BLOCK_STATES)


def _block_diag_in_t(blocks):
    eye = jnp.eye(GROUPS_PER_BLOCK, dtype=F32)
    return jnp.einsum("igchp,gh->igpc", blocks.reshape(SSM_BLOCKS, GROUPS_PER_BLOCK, 16, GROUPS_PER_BLOCK, 64),
                      eye).reshape(32, 64, 16)


def _block_diag_out(c):
    eye = jnp.eye(GROUPS_PER_BLOCK, dtype=F32)
    return jnp.einsum("igcp,gh->igphc", c.reshape(SSM_BLOCKS, GROUPS_PER_BLOCK, 16, 64), eye).reshape(
        SSM_BLOCKS, BLOCK_STATES, BLOCK_CH)


def _block_diag_out_t(blocks):
    eye = jnp.eye(GROUPS_PER_BLOCK, dtype=F32)
    return jnp.einsum("igphc,gh->igcp", blocks.reshape(SSM_BLOCKS, GROUPS_PER_BLOCK, 64, GROUPS_PER_BLOCK, 16),
                      eye).reshape(32, 16, 64)


SMALL_NAMES = ("g_mix", "g_q", "g_k", "lambda_re", "lambda_im", "log_dt", "b_re", "b_im", "c_re", "c_im",
               "d_skip", "g_ffn")


def _pack_small(parts):
    flat = jnp.concatenate([parts[n].reshape(-1) for n in SMALL_NAMES])
    pad = (-flat.shape[0]) % (8 * LANES * SMALL_TILES)
    return jnp.pad(flat, (0, pad)).reshape(-1, LANES)


def _unpack_small(packed, like):
    flat = packed.reshape(-1)
    out, off = {}, 0
    for n in SMALL_NAMES:
        size = like[n].size
        out[n] = flat[off:off + size].reshape(like[n].shape)
        off += size
    return out


BIG_NAMES = ("w_in", "w_attn_proj", "w_glu_a", "w_glu_b", "w_out", "w_ffn_gate", "w_ffn_up", "w_ffn_down")
BIG_SHARD_AXIS = {"w_in": 2, "w_attn_proj": 2, "w_glu_a": 2, "w_glu_b": 2, "w_out": 1,
                  "w_ffn_gate": 2, "w_ffn_up": 2, "w_ffn_down": 1}
ADAMW_ROWS = {"w_in": 256, "w_attn_proj": 512, "w_glu_a": 512, "w_glu_b": 512, "w_out": 128,
              "w_ffn_gate": 256, "w_ffn_up": 256, "w_ffn_down": 176}


def kernel(x, g_mix, w_in, g_q, g_k, w_attn_proj, lambda_re, lambda_im, log_dt, b_re, b_im, c_re, c_im, d_skip, w_glu_a, w_glu_b, w_out, g_ffn, w_ffn_gate, w_ffn_up, w_ffn_down, loss_target, m_g_mix, m_w_in, m_g_q, m_g_k, m_w_attn_proj, m_lambda_re, m_lambda_im, m_log_dt, m_b_re, m_b_im, m_c_re, m_c_im, m_d_skip, m_w_glu_a, m_w_glu_b, m_w_out, m_g_ffn, m_w_ffn_gate, m_w_ffn_up, m_w_ffn_down, v_g_mix, v_w_in, v_g_q, v_g_k, v_w_attn_proj, v_lambda_re, v_lambda_im, v_log_dt, v_b_re, v_b_im, v_c_re, v_c_im, v_d_skip, v_w_glu_a, v_w_glu_b, v_w_out, v_g_ffn, v_w_ffn_gate, v_w_ffn_up, v_w_ffn_down):
    args = dict(locals())
    weights = {n: args[n] for n in BIG_NAMES + SMALL_NAMES}
    moments_m = {n: args["m_" + n] for n in BIG_NAMES + SMALL_NAMES}
    moments_v = {n: args["v_" + n] for n in BIG_NAMES + SMALL_NAMES}
    x0 = x[0]
    target = loss_target[0]

    shards = []
    for n in BIG_NAMES:
        w = weights[n]
        rows_to, cols_to = w.shape[1], w.shape[2]
        if n in ("w_ffn_gate", "w_ffn_up"):
            cols_to = FF_SHARD_PAD
        if n == "w_ffn_down":
            rows_to = FF_SHARD_PAD
        shards.append(_prep_weight(w, rows_to, cols_to, "prep_" + n))
    full = dict(zip(BIG_NAMES, _all_gather(shards, [BIG_SHARD_AXIS[n] for n in BIG_NAMES])))

    saved = []
    xl = x0
    for l in range(DEPTH):
        abar_re, abar_im, bb_re, bb_im = _discretise(lambda_re[l], lambda_im[l], log_dt[l], b_re[l], b_im[l])
        ssm = dict(ar8=abar_re.reshape(8, 256), ai8=abar_im.reshape(8, 256),
                   bdr=_block_diag_in(bb_re).astype(BF16), bdi=_block_diag_in(bb_im).astype(BF16),
                   cdr=_block_diag_out(c_re[l]).astype(BF16), cdi=_block_diag_out(c_im[l]).astype(BF16),
                   dsk=d_skip[l][None])
        gq2 = jnp.tile(g_q[l], 2)[None]
        gk2 = jnp.tile(g_k[l], 2)[None]
        z, h = _in_proj(xl, g_mix[l][None], full["w_in"][l])
        ols = []
        for g in range(N_GROUPS):
            ols.extend(_attn_fwd(z, gq2, gk2, g))
        y, cin = _ssm_fwd(z, **ssm)
        xm = _mix_fwd(ols, y, z, xl, full["w_attn_proj"][l], full["w_glu_a"][l], full["w_glu_b"][l], full["w_out"][l])
        xo = _ffn_fwd(xm, g_ffn[l][None], full["w_ffn_gate"][l], full["w_ffn_up"][l], full["w_ffn_down"][l])
        saved.append(dict(x=xl, z=z, h=h, ols=ols, y=y, cin=cin, xm=xm, ssm=ssm, gq2=gq2, gk2=gk2))
        xl = xo

    dxo, loss_local = _loss_grad(xl, target)
    loss = lax.psum(loss_local[0, 0], MESH_AXES)
    big_grads = {n: [None] * DEPTH for n in BIG_NAMES}
    small_grads = {n: [None] * DEPTH for n in SMALL_NAMES}
    for l in reversed(range(DEPTH)):
        s = saved[l]
        dxm, h2, hid, dgate, dup, dgffn = _ffn_bwd(s["xm"], g_ffn[l][None], full["w_ffn_gate"][l],
                                                   full["w_ffn_up"][l], full["w_ffn_down"][l], dxo)
        big_grads["w_ffn_down"][l] = _tn(hid, dxo, "grad_w_ffn_down")
        big_grads["w_ffn_gate"][l] = _tn(h2, dgate, "grad_w_ffn_gate")
        big_grads["w_ffn_up"][l] = _tn(h2, dup, "grad_w_ffn_up")
        (do0, c0, do1, c1, do2, c2, dy, dga, dgs, a_b, yg_b, mix_b, dao_b, dpa_b, dpb_b) = _mix_bwd(
            dxm, s["ols"], s["y"], s["z"], full["w_attn_proj"][l], full["w_glu_a"][l], full["w_glu_b"][l],
            full["w_out"][l])
        big_grads["w_out"][l] = _tn(mix_b, dxm, "grad_w_out")
        big_grads["w_attn_proj"][l] = _tn(a_b, dao_b, "grad_w_attn_proj")
        big_grads["w_glu_a"][l] = _tn(yg_b, dpa_b, "grad_w_glu_a")
        big_grads["w_glu_b"][l] = _tn(yg_b, dpb_b, "grad_w_glu_b")
        du, da4, ddsk, dbdr, dbdi, dcdr, dcdi = _ssm_bwd(s["z"], dy, s["cin"], **s["ssm"])
        dqkv = []
        dgq = jnp.zeros((1, LANES), F32)
        dgk = jnp.zeros((1, LANES), F32)
        for g, (do_g, c_g) in enumerate(((do0, c0), (do1, c1), (do2, c2))):
            dq, dk, dv, dgq_g, dgk_g = _attn_bwd(s["z"], s["gq2"], s["gk2"], s["ols"][2 * g + 1], do_g, c_g, g)
            dqkv.append((dq, dk, dv))
            dgq, dgk = dgq + dgq_g, dgk + dgk_g
        pieces = [dqkv[g][j] for j in range(3) for g in range(N_GROUPS)] + [du, dga, dgs]
        dxo, dgmix = _in_proj_bwd(pieces, full["w_in"][l], s["x"], g_mix[l][None], dxm)
        big_grads["w_in"][l] = _tn_pieces(s["h"], pieces, "grad_w_in")
        _, disc_vjp = jax.vjp(_discretise, lambda_re[l], lambda_im[l], log_dt[l], b_re[l], b_im[l])
        dar = jnp.concatenate([da4[0], da4[1]], axis=1).reshape(32, 64)
        dai = jnp.concatenate([da4[2], da4[3]], axis=1).reshape(32, 64)
        dlr, dli, dldt, dbre, dbim = disc_vjp((dar, dai, _block_diag_in_t(dbdr), _block_diag_in_t(dbdi)))
        small_grads["g_mix"][l] = dgmix[0]
        small_grads["g_q"][l] = dgq[0, :HEAD_DIM] + dgq[0, HEAD_DIM:]
        small_grads["g_k"][l] = dgk[0, :HEAD_DIM] + dgk[0, HEAD_DIM:]
        small_grads["lambda_re"][l] = dlr
        small_grads["lambda_im"][l] = dli
        small_grads["log_dt"][l] = dldt
        small_grads["b_re"][l] = dbre
        small_grads["b_im"][l] = dbim
        small_grads["c_re"][l] = _block_diag_out_t(dcdr)
        small_grads["c_im"][l] = _block_diag_out_t(dcdi)
        small_grads["d_skip"][l] = ddsk[0]
        small_grads["g_ffn"][l] = dgffn[0]
    grad_x = dxo[None]

    small_local = {n: jnp.stack(small_grads[n]) for n in SMALL_NAMES}
    rs_axes = [BIG_SHARD_AXIS[n] - 1 for n in BIG_NAMES]
    got = _exchange_with_sibling([big_grads[n] for n in BIG_NAMES], rs_axes)
    core = lax.axis_index("c").astype(jnp.int32).reshape(1)
    sums = [[_chip_sum(big_grads[n][l], got[t], l, rs_axes[t], core, "chip_sum_" + n) for l in range(DEPTH)]
            for t, n in enumerate(BIG_NAMES)]
    recv = _exchange_chip_sums(sums, _pack_small(small_local))
    out_g, out_d, out_m, out_v = {}, {}, {}, {}
    for n, r in zip(BIG_NAMES, recv[:-1]):
        out_g[n], out_d[n], out_m[n], out_v[n] = _adamw_big(r, weights[n], moments_m[n], moments_v[n],
                                                            ADAMW_ROWS[n], "adamw_" + n)
    like = {n: weights[n] for n in SMALL_NAMES}
    packed = _adamw_small(recv[-1], _pack_small(like), _pack_small({n: moments_m[n] for n in SMALL_NAMES}),
                          _pack_small({n: moments_v[n] for n in SMALL_NAMES}))
    for dst, p in zip((out_g, out_d, out_m, out_v), packed):
        dst.update(_unpack_small(p, like))

    order = ("g_mix", "w_in", "g_q", "g_k", "w_attn_proj", "lambda_re", "lambda_im", "log_dt", "b_re", "b_im",
             "c_re", "c_im", "d_skip", "w_glu_a", "w_glu_b", "w_out", "g_ffn", "w_ffn_gate", "w_ffn_up",
             "w_ffn_down")
    return (loss, grad_x, *[out_g[n] for n in order], *[out_d[n] for n in order],
            *[out_m[n] for n in order], *[out_v[n] for n in order])
```
